```python
import math
import jax
import jax.numpy as jnp
from jax import lax

D_MODEL = 1024
BATCH = 2
SEQ = 8192
DEPTH = 1

HEAD_DIM = 64
D_MIX = D_MODEL
RWKV_WIDTH = D_MIX // 2
ATTN_WIDTH = D_MIX - RWKV_WIDTH
RWKV_HEADS = RWKV_WIDTH // HEAD_DIM
ATTN_HEADS = ATTN_WIDTH // HEAD_DIM
W_LORA = max(32, int(round(1.8 * RWKV_WIDTH ** 0.5 / 32)) * 32)
A_LORA = max(32, int(round(1.8 * RWKV_WIDTH ** 0.5 / 32)) * 32)
G_LORA = max(32, int(round(0.6 * RWKV_WIDTH ** 0.8 / 32)) * 32)
RWKV_COLS = 3 * RWKV_WIDTH + W_LORA + A_LORA + G_LORA
ATTN_COLS = 3 * ATTN_WIDTH
PROJ_COLS = RWKV_COLS + ATTN_COLS
D_FF = -(-8 * D_MODEL // (3 * 128)) * 128
CONV_WIDTH = 3
ROPE_THETA = 500000.0
ROT_DIM = HEAD_DIM // 4
DILATED_PATTERNS = ((128, 1), (512, 4), (2048, 16))
ATTN_BLOCK = 128
NORM_EPS = 1e-6
GN_EPS = 64e-5

kernel_name = 'hymba_rwkv7_dilated_convffn'


def _rmsnorm(x, gain):
    xf = x.astype(jnp.float32)
    y = xf * lax.rsqrt(jnp.mean(xf * xf, axis=-1, keepdims=True) + NORM_EPS)
    return (y * gain.astype(jnp.float32)).astype(x.dtype)


def _token_shift(t):
    return jnp.pad(t, ((0, 0), (1, 0), (0, 0)))[:, :-1]


def _partial_rotary(x, positions):
    half = ROT_DIM // 2
    inv_freq = ROPE_THETA ** (-jnp.arange(half, dtype=jnp.float32) * 2.0 / ROT_DIM)
    ang = positions.astype(jnp.float32)[:, None] * inv_freq[None, :]
    cos = jnp.cos(ang)[None, :, None, :]
    sin = jnp.sin(ang)[None, :, None, :]
    xf = x[..., :ROT_DIM].astype(jnp.float32)
    x1, x2 = xf[..., :half], xf[..., half:]
    rot = jnp.concatenate([x1 * cos - x2 * sin, x2 * cos + x1 * sin], axis=-1)
    return jnp.concatenate([rot.astype(x.dtype), x[..., ROT_DIM:]], axis=-1)


def _banded_causal_attention(q, k, v, span):
    B, H, G, L, hd = q.shape
    n_blk = -(-L // ATTN_BLOCK)
    pad = n_blk * ATTN_BLOCK - L
    cfg = ((0, 0), (0, 0), (0, 0), (0, pad), (0, 0))
    q, k, v = (jnp.pad(t, cfg).reshape(B, H, G, n_blk, ATTN_BLOCK, hd) for t in (q, k, v))

    def with_prev(t):
        prev = jnp.pad(t, ((0, 0), (0, 0), (0, 0), (1, 0), (0, 0), (0, 0)))[:, :, :, :-1]
        return jnp.concatenate([prev, t], axis=4)

    kb, vb = with_prev(k), with_prev(v)
    s = jnp.einsum('bhgnqd,bhgnkd->bhgnqk', q, kb, preferred_element_type=jnp.float32) / math.sqrt(hd)
    blk = jnp.arange(n_blk)[:, None, None] * ATTN_BLOCK
    qpos = blk + jnp.arange(ATTN_BLOCK)[None, :, None]
    kpos = blk - ATTN_BLOCK + jnp.arange(2 * ATTN_BLOCK)[None, None, :]
    dist = qpos - kpos
    valid = (dist >= 0) & (dist <= span) & (kpos >= 0)
    s = jnp.where(valid, s, -jnp.inf)
    m = jnp.max(s, axis=-1, keepdims=True)
    p = jnp.exp(s - m)
    den = jnp.sum(p, axis=-1, keepdims=True)
    o = jnp.einsum('bhgnqk,bhgnkd->bhgnqd', p, vb.astype(jnp.float32)) / den
    lse = (m + jnp.log(den))[..., 0]
    o = o.reshape(B, H, G, n_blk * ATTN_BLOCK, hd)[:, :, :, :L]
    lse = lse.reshape(B, H, G, n_blk * ATTN_BLOCK)[..., :L]
    return o, lse


def _dilated_attention(q, k, v):
    B, S, H, hd = q.shape
    q, k, v = (t.transpose(0, 2, 1, 3) for t in (q, k, v))
    outs, lses = [], []
    for window, dil in DILATED_PATTERNS:
        L = S // dil

        def by_stride(t):
            return t.reshape(B, H, L, dil, hd).transpose(0, 1, 3, 2, 4)

        o, lse = _banded_causal_attention(by_stride(q), by_stride(k), by_stride(v), window // dil)
        outs.append(o.transpose(0, 1, 3, 2, 4).reshape(B, H, S, hd))
        lses.append(lse.transpose(0, 1, 3, 2).reshape(B, H, S))
    wts = jax.nn.softmax(jnp.stack(lses), axis=0)
    o = jnp.sum(wts[..., None] * jnp.stack(outs), axis=0)
    return o.transpose(0, 2, 1, 3)


def _rwkv7_scan(r, decay, k, v, kk, a):
    def step(state, inp):
        r_t, w_t, k_t, v_t, kk_t, a_t = inp
        sa = jnp.einsum('bhvk,bhk->bhv', state, -kk_t)
        state = (state * w_t[:, :, None, :]
                 + sa[..., None] * (kk_t * a_t)[:, :, None, :]
                 + v_t[..., None] * k_t[:, :, None, :])
        y = jnp.einsum('bhvk,bhk->bhv', state, r_t)
        return state, y

    B, S, H, N = r.shape
    xs = tuple(jnp.swapaxes(t, 0, 1) for t in (r, decay, k, v, kk, a))
    init = jnp.zeros((B, H, N, N), jnp.float32)
    _, ys = lax.scan(step, init, xs)
    return jnp.swapaxes(ys, 0, 1)


def _rwkv7_mixer(p, shift_mix, w0, w_lora_up, a0, a_lora_up, g_lora_up, k_k, k_a, r_k, ln_x_w, ln_x_b):
    B, S, _ = p.shape
    f32 = jnp.float32
    p = p + (_token_shift(p) - p) * shift_mix
    idx = [RWKV_WIDTH, 2 * RWKV_WIDTH, 3 * RWKV_WIDTH, 3 * RWKV_WIDTH + W_LORA, 3 * RWKV_WIDTH + W_LORA + A_LORA]
    r, k, v, wd, ad, gd = jnp.split(p, idx, axis=-1)
    w_log = -jax.nn.softplus(-(w0 + jnp.tanh(wd) @ w_lora_up)) - 0.5
    decay = jnp.exp(-jnp.exp(w_log.astype(f32)))
    a = jax.nn.sigmoid(a0 + ad @ a_lora_up)
    g = jax.nn.sigmoid(gd) @ g_lora_up

    def heads(t):
        return t.reshape(B, S, RWKV_HEADS, HEAD_DIM).astype(f32)

    kk = heads(k * k_k)
    kk = kk / jnp.maximum(jnp.sqrt(jnp.sum(kk * kk, axis=-1, keepdims=True)), 1e-12)
    k = k * (1.0 + (a - 1.0) * k_a)
    rh, kh, vh, ah, wh = heads(r), heads(k), heads(v), heads(a), heads(decay)
    y = _rwkv7_scan(rh, wh, kh, vh, kk, ah)
    mu = jnp.mean(y, axis=-1, keepdims=True)
    var = jnp.mean(jnp.square(y - mu), axis=-1, keepdims=True)
    y = ((y - mu) * lax.rsqrt(var + GN_EPS)).reshape(B, S, RWKV_WIDTH) * ln_x_w + ln_x_b
    bonus = jnp.sum(rh * kh * r_k.astype(f32), axis=-1, keepdims=True) * vh
    y = y + bonus.reshape(B, S, RWKV_WIDTH)
    return (y * g).astype(p.dtype)


def _causal_dwconv(t, w, b):
    out = lax.conv_general_dilated(t, w[:, None, :], window_strides=(1,), padding=[(CONV_WIDTH - 1, 0)],
                                   dimension_numbers=('NWC', 'WIO', 'NWC'), feature_group_count=t.shape[-1])
    return out + b


def setup_inputs(seed: int = 0) -> dict:
    key = jax.random.key(seed)
    ks = jax.random.split(key, 24)
    f32 = jnp.float32
    L = DEPTH

    def normal(k, shape, scale):
        return jax.random.normal(k, shape, f32) * scale

    return {
        'x': normal(ks[0], (BATCH, SEQ, D_MODEL), 1.0),
        'mix_norm_gain': 1.0 + normal(ks[1], (L, D_MODEL), 0.02),
        'w_in': normal(ks[2], (L, D_MODEL, PROJ_COLS), D_MODEL ** -0.5),
        'rwkv_shift_mix': jax.random.uniform(ks[3], (L, RWKV_COLS), f32),
        'w0': jax.random.uniform(ks[4], (L, RWKV_WIDTH), f32, -4.0, 0.0),
        'w_lora_up': normal(ks[5], (L, W_LORA, RWKV_WIDTH), 0.1),
        'a0': normal(ks[6], (L, RWKV_WIDTH), 0.1),
        'a_lora_up': normal(ks[7], (L, A_LORA, RWKV_WIDTH), 0.1),
        'g_lora_up': normal(ks[8], (L, G_LORA, RWKV_WIDTH), G_LORA ** -0.5),
        'k_k': 0.85 + normal(ks[9], (L, RWKV_WIDTH), 0.02),
        'k_a': 1.0 + normal(ks[10], (L, RWKV_WIDTH), 0.02),
        'r_k': normal(ks[11], (L, RWKV_HEADS, HEAD_DIM), 0.1),
        'ln_x_w': 1.0 + normal(ks[12], (L, RWKV_WIDTH), 0.02),
        'ln_x_b': normal(ks[13], (L, RWKV_WIDTH), 0.02),
        'attn_norm_gain': 1.0 + normal(ks[14], (L, ATTN_WIDTH), 0.02),
        'w_out': normal(ks[15], (L, D_MIX, D_MODEL), D_MIX ** -0.5),
        'ffn_norm_gain': 1.0 + normal(ks[16], (L, D_MODEL), 0.02),
        'w_ffn_up': normal(ks[17], (L, D_MODEL, 2 * D_FF), D_MODEL ** -0.5),
        'ffn_conv_w': normal(ks[18], (L, CONV_WIDTH, D_FF), CONV_WIDTH ** -0.5),
        'ffn_conv_b': normal(ks[19], (L, D_FF), 0.02),
        'w_ffn_down': normal(ks[20], (L, D_FF, D_MODEL), D_FF ** -0.5),
        'final_norm_gain': 1.0 + normal(ks[21], (D_MODEL,), 0.02),
    }


def reference(x, mix_norm_gain, w_in, rwkv_shift_mix, w0, w_lora_up, a0, a_lora_up, g_lora_up,
              k_k, k_a, r_k, ln_x_w, ln_x_b, attn_norm_gain, w_out, ffn_norm_gain, w_ffn_up,
              ffn_conv_w, ffn_conv_b, w_ffn_down, final_norm_gain):
    B, S, _ = x.shape
    positions = jnp.arange(S)
    for l in range(DEPTH):
        h = _rmsnorm(x, mix_norm_gain[l])
        proj = h @ w_in[l]
        p_rwkv, p_attn = proj[..., :RWKV_COLS], proj[..., RWKV_COLS:]
        y_rwkv = _rwkv7_mixer(p_rwkv, rwkv_shift_mix[l], w0[l], w_lora_up[l], a0[l], a_lora_up[l],
                              g_lora_up[l], k_k[l], k_a[l], r_k[l], ln_x_w[l], ln_x_b[l])
        q, k, v = (t.reshape(B, S, ATTN_HEADS, HEAD_DIM) for t in jnp.split(p_attn, 3, axis=-1))
        q = _partial_rotary(q, positions)
        k = _partial_rotary(k, positions)
        o = _dilated_attention(q, k, v)
        o = _rmsnorm(o, attn_norm_gain[l].reshape(ATTN_HEADS, HEAD_DIM)).astype(x.dtype)
        y_attn = o.reshape(B, S, ATTN_WIDTH)
        x = x + jnp.concatenate([y_rwkv, y_attn], axis=-1) @ w_out[l]
        h = _rmsnorm(x, ffn_norm_gain[l])
        gate, val = jnp.split(h @ w_ffn_up[l], 2, axis=-1)
        gate = _causal_dwconv(gate, ffn_conv_w[l], ffn_conv_b[l])
        x = x + (jax.nn.silu(gate) * val) @ w_ffn_down[l]
    return _rmsnorm(x, final_norm_gain)
```

```python
import functools
import math

import jax
import jax.numpy as jnp
from jax import lax
from jax.experimental import pallas as pl
from jax.experimental.pallas import tpu as pltpu

F32 = jnp.float32
BF16 = jnp.bfloat16

LANES = 128
HEAD_DIM = 64
PAIR = 2 * HEAD_DIM
ROT_DIM = HEAD_DIM // 4
ROPE_THETA = 500000.0
NORM_EPS = 1e-6
GN_EPS = 64e-5
DILATED_PATTERNS = ((128, 1), (512, 4), (2048, 16))
ATTN_BLOCK = 128
CONV_WIDTH = 3
CHUNK = 64
NEG_BIG = -1e30
VMEM_LIMIT = 56 * 1024 * 1024


def _dot(a, b):
    return jnp.dot(a, b, preferred_element_type=F32)


def _dot_nt(a, b):
    return lax.dot_general(a, b, (((1,), (1,)), ((), ())), preferred_element_type=F32)


def _dot_tn(a, b):
    return lax.dot_general(a, b, (((0,), (0,)), ((), ())), preferred_element_type=F32)


def _rmsnorm(x, gain):
    return x * lax.rsqrt(jnp.mean(x * x, axis=-1, keepdims=True) + NORM_EPS) * gain


def _sigmoid(x):
    return 1.0 / (1.0 + jnp.exp(-x))


def _split_dot(x, w):
    hi = x.astype(BF16)
    lo = (x - hi.astype(F32)).astype(BF16)
    return _dot(hi, w) + _dot(lo, w)


def _inproj_kernel(x_ref, g_ref, w_ref, pr_ref, q_ref, k_ref, v_ref, *, rw, aw):
    h = _rmsnorm(x_ref[...], g_ref[...]).astype(BF16)
    p = _dot(h, w_ref[...])
    pr_ref[...] = p[:, :rw]
    q_ref[...] = p[:, rw:rw + aw]
    k_ref[...] = p[:, rw + aw:rw + 2 * aw]
    v_ref[...] = p[:, rw + 2 * aw:rw + 3 * aw]


def _inproj(x2d, gain, w_all, rw, aw, tm=512):
    T, D = x2d.shape
    N = w_all.shape[1]
    row = lambda i: (i, 0)
    fixed = lambda i: (0, 0)
    return pl.pallas_call(
        functools.partial(_inproj_kernel, rw=rw, aw=aw),
        out_shape=(jax.ShapeDtypeStruct((T, rw), F32),) + (jax.ShapeDtypeStruct((T, aw), F32),) * 3,
        grid=(T // tm,),
        in_specs=[pl.BlockSpec((tm, D), row), pl.BlockSpec((1, D), fixed), pl.BlockSpec((D, N), fixed)],
        out_specs=(pl.BlockSpec((tm, rw), row),) + (pl.BlockSpec((tm, aw), row),) * 3,
        compiler_params=pltpu.CompilerParams(dimension_semantics=("parallel",), vmem_limit_bytes=VMEM_LIMIT),
        name="inproj",
    )(x2d, gain, w_all)


def _rwkv_pair(r, k, v, logw, kkn, bb, s_old, tri, same_blk, strict_blk, incl_blk, lane_lo):
    L = CHUNK
    h1 = logw.astype(BF16)
    r1 = logw - h1.astype(F32)
    h2 = r1.astype(BF16)
    h3 = (r1 - h2.astype(F32)).astype(BF16)
    cum = _dot(tri, h1) + _dot(tri, h2) + _dot(tri, h3)
    cum_last = cum[L - 1:L, :]
    g_in = jnp.exp(cum)
    g_ex = jnp.exp(cum - logw)
    g_inv = jnp.exp(-cum)
    g_hat = jnp.exp(cum_last - cum)
    g_last = jnp.exp(cum_last)

    al = -kkn * g_ex
    rb = r * g_in
    bt = bb * g_inv
    kt = k * g_inv
    bh = bb * g_hat
    kh = k * g_hat

    def stack_masked(x):
        return jnp.concatenate([jnp.where(lane_lo, x, 0.0), jnp.where(lane_lo, 0.0, x)], axis=0)

    def dup(x):
        return jnp.concatenate([x, x], axis=0)

    def unstack(x):
        return x[:L] + x[L:]

    al_s = stack_masked(al)
    rb_s = stack_masked(rb)
    lhs = jnp.concatenate([al_s, rb_s], axis=0).astype(BF16)
    rhs = jnp.concatenate([dup(bt), dup(kt)], axis=0).astype(BF16)
    aq = _dot_nt(lhs, rhs)
    a_ab = jnp.where(strict_blk, aq[:2 * L, :2 * L], 0.0)
    a_ak = jnp.where(strict_blk, aq[:2 * L, 2 * L:], 0.0)
    a_rb = jnp.where(incl_blk, aq[2 * L:, :2 * L], 0.0).astype(BF16)
    a_rk = jnp.where(incl_blk, aq[2 * L:, 2 * L:], 0.0).astype(BF16)

    v_d = dup(v).astype(BF16)
    av = jnp.where(same_blk, _dot(a_ak.astype(BF16), v_d), 0.0)
    x = jnp.concatenate([al_s, av], axis=1)
    ap = a_ab
    n_lvl = int(math.log2(L))
    for lvl in range(n_lvl):
        apb = ap.astype(BF16)
        x = x + _dot(apb, x.astype(BF16))
        if lvl + 1 < n_lvl:
            ap = _dot(apb, apb)

    z = _dot(a_rb, x.astype(BF16))
    w2 = unstack(rb_s + z[:, :PAIR])
    y_loc = unstack(z[:, PAIR:] + jnp.where(same_blk, _dot(a_rk, v_d), 0.0))
    w1 = unstack(x[:, :PAIR])
    u_loc = unstack(x[:, PAIR:])

    bh_b = bh.astype(BF16)
    m_t = jnp.where(same_blk, _dot_tn(w1.astype(BF16), bh_b), 0.0)
    s_loc = jnp.where(
        same_blk,
        _dot_tn(jnp.concatenate([u_loc, v], axis=0).astype(BF16),
                jnp.concatenate([bh, kh], axis=0).astype(BF16)),
        0.0)
    s_b = s_old.astype(BF16)
    y = _dot_nt(w2.astype(BF16), s_b) + y_loc
    s_new = s_old * g_last + _dot(s_b, m_t.astype(BF16)) + s_loc
    return y, s_new


def _rwkv_kernel(p_ref, mix_ref, wc_ref, w0_ref, a0_ref, kk_ref, ka_ref, rk_ref, lnw_ref, lnb_ref,
                 o_ref, s_scr, carry_scr, *, nb, width, n_lora):
    L = CHUNK
    c = pl.program_id(0)

    @pl.when(c == 0)
    def _():
        s_scr[...] = jnp.zeros_like(s_scr)
        carry_scr[...] = jnp.zeros_like(carry_scr)

    n_pairs = width // PAIR
    row = lax.broadcasted_iota(jnp.int32, (L, 1), 0)
    tri = (lax.broadcasted_iota(jnp.int32, (L, L), 0) >= lax.broadcasted_iota(jnp.int32, (L, L), 1)).astype(BF16)
    ri = lax.broadcasted_iota(jnp.int32, (2 * L, 2 * L), 0)
    ci = lax.broadcasted_iota(jnp.int32, (2 * L, 2 * L), 1)
    same_blk = (ri >= L) == (ci >= L)
    strict_blk = same_blk & (ri > ci)
    incl_blk = same_blk & (ri >= ci)
    seg_ones = same_blk.astype(BF16)
    lane_lo = lax.broadcasted_iota(jnp.int32, (L, PAIR), 1) < HEAD_DIM
    lora_lane = lax.broadcasted_iota(jnp.int32, (L, 2 * LANES), 1)

    mix = mix_ref[...]
    for b in range(nb):
        p = p_ref[b]
        prev_last = carry_scr[b, 7:8, :]
        p_prev = jnp.where(row == 0, prev_last, pltpu.roll(p, 1, 0))
        carry_scr[b] = p[L - 8:L, :]
        pm = p + (p_prev - p) * mix

        lora = pm[:, 3 * width:]
        lora_act = jnp.where(lora_lane < n_lora[0], jnp.tanh(lora),
                             jnp.where(lora_lane < n_lora[1], lora,
                                       jnp.where(lora_lane < n_lora[2], _sigmoid(lora), 0.0)))
        up = _dot(lora_act.astype(BF16), wc_ref[...])

        for pr in range(n_pairs):
            sl = slice(pr * PAIR, (pr + 1) * PAIR)
            r = pm[:, sl]
            k_raw = pm[:, width + pr * PAIR: width + (pr + 1) * PAIR]
            v = pm[:, 2 * width + pr * PAIR: 2 * width + (pr + 1) * PAIR]
            w_pre = w0_ref[:, sl] + up[:, sl]
            w_log = -(jnp.maximum(-w_pre, 0.0) + jnp.log(1.0 + jnp.exp(-jnp.abs(w_pre)))) - 0.5
            logw = -jnp.exp(w_log)
            a = _sigmoid(a0_ref[:, sl] + up[:, width + pr * PAIR: width + (pr + 1) * PAIR])
            g = up[:, 2 * width + pr * PAIR: 2 * width + (pr + 1) * PAIR]

            kk = k_raw * kk_ref[:, sl]
            kk_norm = jnp.sqrt(_split_dot(kk * kk, seg_ones))
            kkn = kk / jnp.maximum(kk_norm, 1e-12)
            k = k_raw * (1.0 + (a - 1.0) * ka_ref[:, sl])
            bb = kkn * a

            idx = b * n_pairs + pr
            y, s_new = _rwkv_pair(r, k, v, logw, kkn, bb, s_scr[idx], tri, same_blk, strict_blk, incl_blk, lane_lo)
            s_scr[idx] = s_new

            mu = _split_dot(y, seg_ones) * (1.0 / HEAD_DIM)
            yc = y - mu
            var = _split_dot(yc * yc, seg_ones) * (1.0 / HEAD_DIM)
            yn = yc * lax.rsqrt(var + GN_EPS) * lnw_ref[:, sl] + lnb_ref[:, sl]
            bonus = _split_dot(r * k * rk_ref[:, sl], seg_ones) * v
            o_ref[b, :, sl] = ((yn + bonus) * g).astype(o_ref.dtype)


def _rwkv(p_r, mix, wc, w0, a0, k_k, k_a, r_k, ln_w, ln_b, width, n_lora):
    B, S, C = p_r.shape
    L = CHUNK
    n_state = B * (width // PAIR)
    vec = lambda n: pl.BlockSpec((1, n), lambda c: (0, 0))
    return pl.pallas_call(
        functools.partial(_rwkv_kernel, nb=B, width=width, n_lora=n_lora),
        out_shape=jax.ShapeDtypeStruct((B, S, width), BF16),
        grid=(S // L,),
        in_specs=[pl.BlockSpec((B, L, C), lambda c: (0, c, 0)), vec(C),
                  pl.BlockSpec(wc.shape, lambda c: (0, 0))] + [vec(width)] * 7,
        out_specs=pl.BlockSpec((B, L, width), lambda c: (0, c, 0)),
        scratch_shapes=[pltpu.VMEM((n_state, PAIR, PAIR), F32), pltpu.VMEM((B, 8, C), F32)],
        compiler_params=pltpu.CompilerParams(dimension_semantics=("arbitrary",), vmem_limit_bytes=VMEM_LIMIT),
        name="rwkv7",
    )(p_r, mix, wc, w0, a0, k_k, k_a, r_k, ln_w, ln_b)


def _attn_kernel(q_ref, k_ref, v_ref, cos_ref, sin_ref, gain_ref, o_ref,
                 q_scr, k_ring, v_ring, *stat_scr, sb_rows):
    sb = pl.program_id(2)
    n_pat = len(DILATED_PATTERNS)
    num_scr, m_scr, l_scr = stat_scr[:n_pat], stat_scr[n_pat:2 * n_pat], stat_scr[2 * n_pat:]
    Q = ATTN_BLOCK
    ring_rows = 2 * sb_rows
    cur_base = (sb % 2) * sb_rows

    lane = lax.broadcasted_iota(jnp.int32, (sb_rows, PAIR), 1)
    first_half = (lane % HEAD_DIM) < (ROT_DIM // 2)

    def rotary(x):
        partner = jnp.where(first_half, pltpu.roll(x, PAIR - ROT_DIM // 2, 1), pltpu.roll(x, ROT_DIM // 2, 1))
        return x * cos_ref[...] + partner * sin_ref[...]

    @pl.when(sb == 0)
    def _():
        k_ring[pl.ds(sb_rows, sb_rows), :] = jnp.zeros((sb_rows, PAIR), F32)
        v_ring[pl.ds(sb_rows, sb_rows), :] = jnp.zeros((sb_rows, PAIR), F32)

    q_scr[...] = rotary(q_ref[...]) * (1.0 / math.sqrt(HEAD_DIM))
    k_ring[pl.ds(pl.multiple_of(cur_base, sb_rows), sb_rows), :] = rotary(k_ref[...])
    v_ring[pl.ds(pl.multiple_of(cur_base, sb_rows), sb_rows), :] = v_ref[...]

    ii = lax.broadcasted_iota(jnp.int32, (Q, Q), 0)
    jj = lax.broadcasted_iota(jnp.int32, (Q, Q), 1)
    cur_ok = jj <= ii
    lane_lo = lax.broadcasted_iota(jnp.int32, (Q, PAIR), 1) < HEAD_DIM

    for pi, (window, dil) in enumerate(DILATED_PATTERNS):
        assert window // dil == Q
        span = Q * dil
        n_blk = sb_rows // Q

        def tile(g, carry, dil=dil, span=span, pi=pi):
            start = (g // dil) * span + (g % dil)
            prev_valid = (sb > 0) | (start >= span)
            k_cur = cur_base + start
            k_prev = (k_cur - span + ring_rows) % ring_rows
            rows = lambda s: pl.ds(s, Q, stride=dil) if dil > 1 else pl.ds(s, Q)
            q = q_scr[rows(start), :]
            kc = k_ring[rows(k_cur), :].astype(BF16)
            kp = k_ring[rows(k_prev), :].astype(BF16)
            vc = v_ring[rows(k_cur), :].astype(BF16)
            vp = v_ring[rows(k_prev), :].astype(BF16)
            prev_mask = jj >= ii + jnp.where(prev_valid, 0, Q)
            nums, ms, ls = [], [], []
            for h in range(2):
                qh = jnp.where(lane_lo if h == 0 else ~lane_lo, q, 0.0).astype(BF16)
                s_p = jnp.where(prev_mask, _dot_nt(qh, kp), NEG_BIG)
                s_c = jnp.where(cur_ok, _dot_nt(qh, kc), NEG_BIG)
                m = jnp.maximum(jnp.max(s_p, axis=-1, keepdims=True), jnp.max(s_c, axis=-1, keepdims=True))
                p_p = jnp.exp(s_p - m)
                p_c = jnp.exp(s_c - m)
                ls.append(jnp.sum(p_p, axis=-1, keepdims=True) + jnp.sum(p_c, axis=-1, keepdims=True))
                nums.append(_dot(p_p.astype(BF16), vp) + _dot(p_c.astype(BF16), vc))
                ms.append(m)
            num_scr[pi][rows(start), :] = jnp.where(lane_lo, nums[0], nums[1])
            m_scr[pi][rows(start), :] = jnp.where(lane_lo, ms[0], ms[1])
            l_scr[pi][rows(start), :] = jnp.where(lane_lo, ls[0], ls[1])
            return carry

        lax.fori_loop(0, n_blk, tile, 0, unroll=2)

    ri = lax.broadcasted_iota(jnp.int32, (PAIR, PAIR), 0)
    ci = lax.broadcasted_iota(jnp.int32, (PAIR, PAIR), 1)
    seg_ones = ((ri >= HEAD_DIM) == (ci >= HEAD_DIM)).astype(BF16)
    gain = gain_ref[...]

    def merge(i, carry):
        rows = pl.ds(pl.multiple_of(i * Q, Q), Q)
        ms = [m_scr[pi][rows, :] for pi in range(n_pat)]
        m_all = functools.reduce(jnp.maximum, ms)
        num = 0.0
        den = 0.0
        for pi in range(n_pat):
            wgt = jnp.exp(ms[pi] - m_all)
            num = num + wgt * num_scr[pi][rows, :]
            den = den + wgt * l_scr[pi][rows, :]
        o = num / den
        ms_o = _split_dot(o * o, seg_ones) * (1.0 / HEAD_DIM)
        o_ref[rows, :] = (o * lax.rsqrt(ms_o + NORM_EPS) * gain).astype(o_ref.dtype)
        return carry

    lax.fori_loop(0, sb_rows // Q, merge, 0)


def _attention(q, k, v, cos_t, sin_t, gain, sb_rows=2048):
    B, S, W = q.shape
    n_pairs = W // PAIR
    n_pat = len(DILATED_PATTERNS)
    blk = pl.BlockSpec((None, sb_rows, PAIR), lambda b, p, s: (b, s, p))
    tab = pl.BlockSpec((sb_rows, PAIR), lambda b, p, s: (s, 0))
    return pl.pallas_call(
        functools.partial(_attn_kernel, sb_rows=sb_rows),
        out_shape=jax.ShapeDtypeStruct((B, S, W), BF16),
        grid=(B, n_pairs, S // sb_rows),
        in_specs=[blk, blk, blk, tab, tab, pl.BlockSpec((1, PAIR), lambda b, p, s: (0, p))],
        out_specs=blk,
        scratch_shapes=[pltpu.VMEM((sb_rows, PAIR), F32),
                        pltpu.VMEM((2 * sb_rows, PAIR), F32),
                        pltpu.VMEM((2 * sb_rows, PAIR), F32)]
                       + [pltpu.VMEM((sb_rows, PAIR), F32)] * (3 * n_pat),
        compiler_params=pltpu.CompilerParams(dimension_semantics=("parallel", "parallel", "arbitrary"),
                                             vmem_limit_bytes=VMEM_LIMIT),
        name="dilated_attn",
    )(q, k, v, cos_t, sin_t, gain)


def _rotary_tables(seq):
    half = ROT_DIM // 2
    inv_freq = ROPE_THETA ** (-jnp.arange(half, dtype=F32) * 2.0 / ROT_DIM)
    ang = jnp.arange(seq).astype(F32)[:, None] * inv_freq[None, :]
    cos, sin = jnp.cos(ang), jnp.sin(ang)
    rest = HEAD_DIM - ROT_DIM
    cos_h = jnp.concatenate([cos, cos, jnp.ones((seq, rest), F32)], axis=-1)
    sin_h = jnp.concatenate([-sin, sin, jnp.zeros((seq, rest), F32)], axis=-1)
    return jnp.tile(cos_h, (1, PAIR // HEAD_DIM)), jnp.tile(sin_h, (1, PAIR // HEAD_DIM))


def _outproj_kernel(x_ref, ya_ref, yb_ref, wa_ref, wb_ref, o_ref):
    o_ref[...] = x_ref[...] + _dot(ya_ref[...], wa_ref[...]) + _dot(yb_ref[...], wb_ref[...])


def _outproj(x2d, y_a, y_b, w_a, w_b, tm=1024):
    T, D = x2d.shape
    row = lambda i: (i, 0)
    fixed = lambda i: (0, 0)
    return pl.pallas_call(
        _outproj_kernel,
        out_shape=jax.ShapeDtypeStruct((T, D), F32),
        grid=(T // tm,),
        in_specs=[pl.BlockSpec((tm, D), row), pl.BlockSpec((tm, y_a.shape[1]), row),
                  pl.BlockSpec((tm, y_b.shape[1]), row), pl.BlockSpec(w_a.shape, fixed), pl.BlockSpec(w_b.shape, fixed)],
        out_specs=pl.BlockSpec((tm, D), row),
        compiler_params=pltpu.CompilerParams(dimension_semantics=("parallel",), vmem_limit_bytes=VMEM_LIMIT),
        name="outproj",
    )(x2d, y_a, y_b, w_a, w_b)


FFN_HALO = 16


def _ffn_kernel(x_ref, xh_ref, g_ref, wg_ref, wv_ref, cw_ref, cb_ref, wd_ref, fg_ref, o_ref,
                h_scr, hh_scr, acc_scr, *, tm, seq, apply_final):
    i = pl.program_id(0)
    j = pl.program_id(1)

    @pl.when(j == 0)
    def _():
        h_scr[...] = _rmsnorm(x_ref[...], g_ref[...]).astype(BF16)
        hh_scr[...] = _rmsnorm(xh_ref[...], g_ref[...]).astype(BF16)
        acc_scr[...] = jnp.zeros_like(acc_scr)

    gate = _dot(h_scr[...], wg_ref[...])
    val = _dot(h_scr[...], wv_ref[...])
    seq_start = (i * tm) % seq == 0
    gate_h = jnp.where(seq_start, 0.0, _dot(hh_scr[...], wg_ref[...]))
    row = lax.broadcasted_iota(jnp.int32, (tm, 1), 0)
    g1 = jnp.where(row == 0, gate_h[FFN_HALO - 1:FFN_HALO, :], pltpu.roll(gate, 1, 0))
    g2 = jnp.where(row == 0, gate_h[FFN_HALO - 2:FFN_HALO - 1, :],
                   jnp.where(row == 1, gate_h[FFN_HALO - 1:FFN_HALO, :], pltpu.roll(gate, 2, 0)))
    u = cw_ref[0:1, :] * g2 + cw_ref[1:2, :] * g1 + cw_ref[2:3, :] * gate + cb_ref[...]
    act = (u * _sigmoid(u) * val).astype(BF16)
    acc_scr[...] += _dot(act, wd_ref[...])

    @pl.when(j == pl.num_programs(1) - 1)
    def _():
        y = x_ref[...] + acc_scr[...]
        if apply_final:
            y = _rmsnorm(y, fg_ref[...])
        o_ref[...] = y


def _ffn(x2d, gain, w_gate, w_val, conv_w, conv_b, w_down, final_gain, seq, apply_final, tm=1024, n_ff_tiles=2):
    T, D = x2d.shape
    F = w_gate.shape[1]
    tf = F // n_ff_tiles
    assert tf % LANES == 0 and seq % tm == 0
    halo_blocks = tm // FFN_HALO
    return pl.pallas_call(
        functools.partial(_ffn_kernel, tm=tm, seq=seq, apply_final=apply_final),
        out_shape=jax.ShapeDtypeStruct((T, D), F32),
        grid=(T // tm, n_ff_tiles),
        in_specs=[pl.BlockSpec((tm, D), lambda i, j: (i, 0)),
                  pl.BlockSpec((FFN_HALO, D), lambda i, j: (jnp.maximum(i * halo_blocks - 1, 0), 0)),
                  pl.BlockSpec((1, D), lambda i, j: (0, 0)),
                  pl.BlockSpec((D, tf), lambda i, j: (0, j)),
                  pl.BlockSpec((D, tf), lambda i, j: (0, j)),
                  pl.BlockSpec((CONV_WIDTH, tf), lambda i, j: (0, j)),
                  pl.BlockSpec((1, tf), lambda i, j: (0, j)),
                  pl.BlockSpec((tf, D), lambda i, j: (j, 0)),
                  pl.BlockSpec((1, D), lambda i, j: (0, 0))],
        out_specs=pl.BlockSpec((tm, D), lambda i, j: (i, 0)),
        scratch_shapes=[pltpu.VMEM((tm, D), BF16), pltpu.VMEM((FFN_HALO, D), BF16), pltpu.VMEM((tm, D), F32)],
        compiler_params=pltpu.CompilerParams(dimension_semantics=("parallel", "arbitrary"),
                                             vmem_limit_bytes=VMEM_LIMIT),
        name="convglu_ffn",
    )(x2d, x2d, gain, w_gate, w_val, conv_w, conv_b, w_down, final_gain)


def kernel(x, mix_norm_gain, w_in, rwkv_shift_mix, w0, w_lora_up, a0, a_lora_up, g_lora_up, k_k, k_a, r_k,
           ln_x_w, ln_x_b, attn_norm_gain, w_out, ffn_norm_gain, w_ffn_up, ffn_conv_w, ffn_conv_b,
           w_ffn_down, final_norm_gain):
    B, S, D = x.shape
    depth = w_in.shape[0]
    rw = w0.shape[1]
    aw = attn_norm_gain.shape[1]
    n_w, n_a, n_g = w_lora_up.shape[1], a_lora_up.shape[1], g_lora_up.shape[1]
    n_lora = n_w + n_a + n_g
    lora_pad = -(-n_lora // (2 * LANES)) * (2 * LANES)
    assert lora_pad == 2 * LANES and rw % PAIR == 0 and aw % PAIR == 0
    rwkv_cols = 3 * rw + n_lora
    d_ff = w_ffn_down.shape[1]
    cos_t, sin_t = _rotary_tables(S)

    x2d = x.reshape(B * S, D)
    for l in range(depth):
        pad = jnp.zeros((D, lora_pad - n_lora), F32)
        w_all = jnp.concatenate([w_in[l][:, :rwkv_cols], pad, w_in[l][:, rwkv_cols:]], axis=1).astype(BF16)
        mix = jnp.concatenate([rwkv_shift_mix[l], jnp.zeros((lora_pad - n_lora,), F32)])[None, :]
        wc = jnp.zeros((lora_pad, 3 * rw), F32)
        wc = wc.at[:n_w, :rw].set(w_lora_up[l])
        wc = wc.at[n_w:n_w + n_a, rw:2 * rw].set(a_lora_up[l])
        wc = wc.at[n_w + n_a:n_lora, 2 * rw:].set(g_lora_up[l]).astype(BF16)

        p_r, q, k, v = _inproj(x2d, mix_norm_gain[l][None, :], w_all, 3 * rw + lora_pad, aw)
        y_rwkv = _rwkv(p_r.reshape(B, S, -1), mix, wc, w0[l][None, :], a0[l][None, :], k_k[l][None, :],
                       k_a[l][None, :], r_k[l].reshape(1, rw), ln_x_w[l][None, :], ln_x_b[l][None, :],
                       rw, (n_w, n_w + n_a, n_lora))
        y_attn = _attention(q.reshape(B, S, aw), k.reshape(B, S, aw), v.reshape(B, S, aw), cos_t, sin_t,
                            attn_norm_gain[l][None, :])
        w_o = w_out[l].astype(BF16)
        x2d = _outproj(x2d, y_rwkv.reshape(B * S, rw), y_attn.reshape(B * S, aw), w_o[:rw], w_o[rw:])
        w_up = w_ffn_up[l].astype(BF16)
        x2d = _ffn(x2d, ffn_norm_gain[l][None, :], w_up[:, :d_ff], w_up[:, d_ff:], ffn_conv_w[l],
                   ffn_conv_b[l][None, :], w_ffn_down[l].astype(BF16), final_norm_gain[None, :], S,
                   apply_final=(l == depth - 1))
    return x2d.reshape(B, S, D)
```

```python
import functools
import math

import jax
import jax.numpy as jnp
from jax import lax
from jax.experimental import pallas as pl
from jax.experimental.pallas import tpu as pltpu

F32 = jnp.float32
BF16 = jnp.bfloat16

LANES = 128
HEAD_DIM = 64
PAIR = 2 * HEAD_DIM
ROT_DIM = HEAD_DIM // 4
ROPE_THETA = 500000.0
NORM_EPS = 1e-6
GN_EPS = 64e-5
DILATED_PATTERNS = ((128, 1), (512, 4), (2048, 16))
ATTN_BLOCK = 128
CONV_WIDTH = 3
CHUNK = 64
NEG_BIG = -1e30
VMEM_LIMIT = 56 * 1024 * 1024


def _dot(a, b):
    return jnp.dot(a, b, preferred_element_type=F32)


def _dot_nt(a, b):
    return lax.dot_general(a, b, (((1,), (1,)), ((), ())), preferred_element_type=F32)


def _dot_tn(a, b):
    return lax.dot_general(a, b, (((0,), (0,)), ((), ())), preferred_element_type=F32)


def _rmsnorm(x, gain):
    return x * lax.rsqrt(jnp.mean(x * x, axis=-1, keepdims=True) + NORM_EPS) * gain


def _sigmoid(x):
    return 1.0 / (1.0 + jnp.exp(-x))


def _split_dot(x, w):
    hi = x.astype(BF16)
    lo = (x - hi.astype(F32)).astype(BF16)
    return _dot(hi, w) + _dot(lo, w)


def _inproj_kernel(x_ref, g_ref, w_ref, pr_ref, q_ref, k_ref, v_ref, *, rw, aw):
    h = _rmsnorm(x_ref[...], g_ref[...]).astype(BF16)
    p = _dot(h, w_ref[...])
    pr_ref[...] = p[:, :rw]
    q_ref[...] = p[:, rw:rw + aw]
    k_ref[...] = p[:, rw + aw:rw + 2 * aw]
    v_ref[...] = p[:, rw + 2 * aw:rw + 3 * aw]


def _inproj(x2d, gain, w_all, rw, aw, tm=512):
    T, D = x2d.shape
    N = w_all.shape[1]
    row = lambda i: (i, 0)
    fixed = lambda i: (0, 0)
    return pl.pallas_call(
        functools.partial(_inproj_kernel, rw=rw, aw=aw),
        out_shape=(jax.ShapeDtypeStruct((T, rw), F32),) + (jax.ShapeDtypeStruct((T, aw), F32),) * 3,
        grid=(T // tm,),
        in_specs=[pl.BlockSpec((tm, D), row), pl.BlockSpec((1, D), fixed), pl.BlockSpec((D, N), fixed)],
        out_specs=(pl.BlockSpec((tm, rw), row),) + (pl.BlockSpec((tm, aw), row),) * 3,
        compiler_params=pltpu.CompilerParams(dimension_semantics=("parallel",), vmem_limit_bytes=VMEM_LIMIT),
        name="inproj",
    )(x2d, gain, w_all)


def _each(fn, *lists):
    return [fn(*args) for args in zip(*lists)]


def _rwkv_pairs(r, k, v, logw, kkn, bb, s_old, tri, same_blk, strict_blk, incl_blk, lane_lo):
    L = CHUNK
    bf = lambda t: t.astype(BF16)

    h1 = _each(bf, logw)
    r1 = _each(lambda t, h: t - h.astype(F32), logw, h1)
    h2 = _each(bf, r1)
    h3 = _each(lambda t, h: (t - h.astype(F32)).astype(BF16), r1, h2)
    cum = _each(lambda a, b, c: _dot(tri, a) + _dot(tri, b) + _dot(tri, c), h1, h2, h3)
    cum_last = [t[L - 1:L, :] for t in cum]
    g_in = _each(jnp.exp, cum)
    g_ex = _each(lambda t, w: jnp.exp(t - w), cum, logw)
    g_inv = _each(lambda t: jnp.exp(-t), cum)
    g_hat = _each(lambda tl, t: jnp.exp(tl - t), cum_last, cum)
    g_last = _each(jnp.exp, cum_last)

    def stack_masked(x):
        return jnp.concatenate([jnp.where(lane_lo, x, 0.0), jnp.where(lane_lo, 0.0, x)], axis=0)

    def dup(x):
        return jnp.concatenate([x, x], axis=0)

    def unstack(x):
        return x[:L] + x[L:]

    al_s = _each(lambda t, g: stack_masked(-t * g), kkn, g_ex)
    rb_s = _each(lambda t, g: stack_masked(t * g), r, g_in)
    bt = _each(jnp.multiply, bb, g_inv)
    kt = _each(jnp.multiply, k, g_inv)
    bh = _each(jnp.multiply, bb, g_hat)
    kh = _each(jnp.multiply, k, g_hat)
    lhs = _each(lambda a, b: jnp.concatenate([a, b], axis=0).astype(BF16), al_s, rb_s)
    rhs = _each(lambda a, b: jnp.concatenate([dup(a), dup(b)], axis=0).astype(BF16), bt, kt)
    aq = _each(_dot_nt, lhs, rhs)
    a_ab = [jnp.where(strict_blk, t[:2 * L, :2 * L], 0.0) for t in aq]
    a_ak = [jnp.where(strict_blk, t[:2 * L, 2 * L:], 0.0).astype(BF16) for t in aq]
    a_rb = [jnp.where(incl_blk, t[2 * L:, :2 * L], 0.0).astype(BF16) for t in aq]
    a_rk = [jnp.where(incl_blk, t[2 * L:, 2 * L:], 0.0).astype(BF16) for t in aq]

    v_d = _each(lambda t: dup(t).astype(BF16), v)
    av = _each(lambda a, t: jnp.where(same_blk, _dot(a, t), 0.0), a_ak, v_d)
    x = _each(lambda a, b: jnp.concatenate([a, b], axis=1), al_s, av)
    ap = a_ab
    n_lvl = int(math.log2(L))
    for lvl in range(n_lvl):
        apb = _each(bf, ap)
        x = _each(lambda t, a: t + _dot(a, t.astype(BF16)), x, apb)
        if lvl + 1 < n_lvl:
            ap = _each(lambda a: _dot(a, a), apb)

    z = _each(lambda a, t: _dot(a, t.astype(BF16)), a_rb, x)
    akv = _each(lambda a, t: jnp.where(same_blk, _dot(a, t), 0.0), a_rk, v_d)
    w2 = _each(lambda a, t: unstack(a + t[:, :PAIR]).astype(BF16), rb_s, z)
    y_loc = _each(lambda t, a: unstack(t[:, PAIR:] + a), z, akv)
    w1 = [unstack(t[:, :PAIR]).astype(BF16) for t in x]
    u_loc = [unstack(t[:, PAIR:]) for t in x]

    m_t = _each(lambda a, b: jnp.where(same_blk, _dot_tn(a, b.astype(BF16)), 0.0).astype(BF16), w1, bh)
    s_loc = _each(
        lambda u, vv, b, kk_: jnp.where(
            same_blk,
            _dot_tn(jnp.concatenate([u, vv], axis=0).astype(BF16), jnp.concatenate([b, kk_], axis=0).astype(BF16)),
            0.0),
        u_loc, v, bh, kh)
    s_b = _each(bf, s_old)
    y = _each(lambda a, s, yl: _dot_nt(a, s) + yl, w2, s_b, y_loc)
    s_new = _each(lambda s, g, sb, m, sl: s * g + _dot(sb, m) + sl, s_old, g_last, s_b, m_t, s_loc)
    return y, s_new


def _rwkv_kernel(p_ref, mix_ref, wc_ref, w0_ref, a0_ref, kk_ref, ka_ref, rk_ref, lnw_ref, lnb_ref,
                 o_ref, s_scr, carry_scr, *, nb, width, n_lora):
    L = CHUNK
    c = pl.program_id(0)

    @pl.when(c == 0)
    def _():
        s_scr[...] = jnp.zeros_like(s_scr)
        carry_scr[...] = jnp.zeros_like(carry_scr)

    n_pairs = width // PAIR
    row = lax.broadcasted_iota(jnp.int32, (L, 1), 0)
    tri = (lax.broadcasted_iota(jnp.int32, (L, L), 0) >= lax.broadcasted_iota(jnp.int32, (L, L), 1)).astype(BF16)
    ri = lax.broadcasted_iota(jnp.int32, (2 * L, 2 * L), 0)
    ci = lax.broadcasted_iota(jnp.int32, (2 * L, 2 * L), 1)
    same_blk = (ri >= L) == (ci >= L)
    strict_blk = same_blk & (ri > ci)
    incl_blk = same_blk & (ri >= ci)
    seg_ones = same_blk.astype(BF16)
    lane_lo = lax.broadcasted_iota(jnp.int32, (L, PAIR), 1) < HEAD_DIM
    lora_lane = lax.broadcasted_iota(jnp.int32, (L, 2 * LANES), 1)

    mix = mix_ref[...]
    pms = []
    for b in range(nb):
        p = p_ref[b]
        prev_last = carry_scr[b, 7:8, :]
        p_prev = jnp.where(row == 0, prev_last, pltpu.roll(p, 1, 0))
        carry_scr[b] = p[L - 8:L, :]
        pms.append(p + (p_prev - p) * mix)

    def lora_up(pm):
        lora = pm[:, 3 * width:]
        act = jnp.where(lora_lane < n_lora[0], jnp.tanh(lora),
                        jnp.where(lora_lane < n_lora[1], lora,
                                  jnp.where(lora_lane < n_lora[2], _sigmoid(lora), 0.0)))
        return _dot(act.astype(BF16), wc_ref[...])

    ups = _each(lora_up, pms)

    tiles = [(b, pr) for b in range(nb) for pr in range(n_pairs)]
    lanes = lambda pr, part=0: slice(part * width + pr * PAIR, part * width + (pr + 1) * PAIR)
    sls = [lanes(pr) for _, pr in tiles]
    r = [pms[b][:, lanes(pr, 0)] for b, pr in tiles]
    k_raw = [pms[b][:, lanes(pr, 1)] for b, pr in tiles]
    v = [pms[b][:, lanes(pr, 2)] for b, pr in tiles]
    w_pre = [w0_ref[:, lanes(pr)] + ups[b][:, lanes(pr, 0)] for b, pr in tiles]
    logw = _each(lambda t: -jnp.exp(-(jnp.maximum(-t, 0.0) + jnp.log(1.0 + jnp.exp(-jnp.abs(t)))) - 0.5), w_pre)
    a = [_sigmoid(a0_ref[:, lanes(pr)] + ups[b][:, lanes(pr, 1)]) for b, pr in tiles]
    g = [ups[b][:, lanes(pr, 2)] for b, pr in tiles]

    kk = _each(lambda t, sl: t * kk_ref[:, sl], k_raw, sls)
    kk_ss = _each(lambda t: _split_dot(t * t, seg_ones), kk)
    kkn = _each(lambda t, ss: t / jnp.maximum(jnp.sqrt(ss), 1e-12), kk, kk_ss)
    k = _each(lambda t, aa, sl: t * (1.0 + (aa - 1.0) * ka_ref[:, sl]), k_raw, a, sls)
    bb = _each(jnp.multiply, kkn, a)
    bonus_dot = _each(lambda rr, kk_, sl: _split_dot(rr * kk_ * rk_ref[:, sl], seg_ones), r, k, sls)

    s_old = [s_scr[i] for i in range(len(tiles))]
    y, s_new = _rwkv_pairs(r, k, v, logw, kkn, bb, s_old, tri, same_blk, strict_blk, incl_blk, lane_lo)
    for i, s in enumerate(s_new):
        s_scr[i] = s

    mu = _each(lambda t: _split_dot(t, seg_ones) * (1.0 / HEAD_DIM), y)
    yc = _each(jnp.subtract, y, mu)
    var = _each(lambda t: _split_dot(t * t, seg_ones) * (1.0 / HEAD_DIM), yc)
    for i, (b, pr) in enumerate(tiles):
        sl = sls[i]
        yn = yc[i] * lax.rsqrt(var[i] + GN_EPS) * lnw_ref[:, sl] + lnb_ref[:, sl]
        o_ref[b, :, sl] = ((yn + bonus_dot[i] * v[i]) * g[i]).astype(o_ref.dtype)


def _rwkv(p_r, mix, wc, w0, a0, k_k, k_a, r_k, ln_w, ln_b, width, n_lora):
    B, S, C = p_r.shape
    L = CHUNK
    n_state = B * (width // PAIR)
    vec = lambda n: pl.BlockSpec((1, n), lambda c: (0, 0))
    return pl.pallas_call(
        functools.partial(_rwkv_kernel, nb=B, width=width, n_lora=n_lora),
        out_shape=jax.ShapeDtypeStruct((B, S, width), BF16),
        grid=(S // L,),
        in_specs=[pl.BlockSpec((B, L, C), lambda c: (0, c, 0)), vec(C),
                  pl.BlockSpec(wc.shape, lambda c: (0, 0))] + [vec(width)] * 7,
        out_specs=pl.BlockSpec((B, L, width), lambda c: (0, c, 0)),
        scratch_shapes=[pltpu.VMEM((n_state, PAIR, PAIR), F32), pltpu.VMEM((B, 8, C), F32)],
        compiler_params=pltpu.CompilerParams(dimension_semantics=("arbitrary",), vmem_limit_bytes=VMEM_LIMIT),
        name="rwkv7",
    )(p_r, mix, wc, w0, a0, k_k, k_a, r_k, ln_w, ln_b)


def _attn_kernel(q_ref, k_ref, v_ref, cos_ref, sin_ref, gain_ref, o_ref,
                 q_scr, k_ring, v_ring, *stat_scr, sb_rows, tiles_per_iter):
    sb = pl.program_id(2)
    n_pat = len(DILATED_PATTERNS)
    num_scr, m_scr, l_scr = stat_scr[:n_pat], stat_scr[n_pat:2 * n_pat], stat_scr[2 * n_pat:]
    Q = ATTN_BLOCK
    ring_rows = 2 * sb_rows
    cur_base = (sb % 2) * sb_rows

    lane = lax.broadcasted_iota(jnp.int32, (sb_rows, PAIR), 1)
    first_half = (lane % HEAD_DIM) < (ROT_DIM // 2)

    def rotary(x):
        partner = jnp.where(first_half, pltpu.roll(x, PAIR - ROT_DIM // 2, 1), pltpu.roll(x, ROT_DIM // 2, 1))
        return x * cos_ref[...] + partner * sin_ref[...]

    @pl.when(sb == 0)
    def _():
        k_ring[pl.ds(sb_rows, sb_rows), :] = jnp.zeros((sb_rows, PAIR), F32)
        v_ring[pl.ds(sb_rows, sb_rows), :] = jnp.zeros((sb_rows, PAIR), F32)

    q_scr[...] = rotary(q_ref[...]) * (1.0 / math.sqrt(HEAD_DIM))
    k_ring[pl.ds(pl.multiple_of(cur_base, sb_rows), sb_rows), :] = rotary(k_ref[...])
    v_ring[pl.ds(pl.multiple_of(cur_base, sb_rows), sb_rows), :] = v_ref[...]

    ii = lax.broadcasted_iota(jnp.int32, (2 * Q, 2 * Q), 0) % Q
    cj = lax.broadcasted_iota(jnp.int32, (2 * Q, 2 * Q), 1)
    upper_ok = cj <= ii + Q
    lane_lo = lax.broadcasted_iota(jnp.int32, (Q, PAIR), 1) < HEAD_DIM
    ones_blk = jnp.ones((2 * Q, PAIR), BF16)

    for pi, (window, dil) in enumerate(DILATED_PATTERNS):
        assert window // dil == Q
        span = Q * dil
        n_blk = sb_rows // Q
        assert n_blk % tiles_per_iter == 0

        def tiles(it, carry, dil=dil, span=span, pi=pi):
            rows = lambda s: pl.ds(s, Q, stride=dil) if dil > 1 else pl.ds(s, Q)
            gs = [it * tiles_per_iter + t for t in range(tiles_per_iter)]
            starts = [(g // dil) * span + (g % dil) for g in gs]
            k_cur = [cur_base + s for s in starts]
            k_prev = [(s - span + ring_rows) % ring_rows for s in k_cur]
            lower = [jnp.maximum(ii, jnp.where((sb > 0) | (s >= span), 0, Q)) for s in starts]
            q2 = [q_scr[rows(s), :] for s in starts]
            q2 = [jnp.concatenate([jnp.where(lane_lo, t, 0.0), jnp.where(lane_lo, 0.0, t)], axis=0).astype(BF16)
                  for t in q2]
            kcat = _each(lambda a, b: jnp.concatenate([k_ring[rows(a), :], k_ring[rows(b), :]], axis=0).astype(BF16),
                         k_prev, k_cur)
            vcat = _each(lambda a, b: jnp.concatenate([v_ring[rows(a), :], v_ring[rows(b), :]], axis=0).astype(BF16),
                         k_prev, k_cur)
            vext = [jnp.concatenate([t, ones_blk], axis=1) for t in vcat]
            s = _each(_dot_nt, q2, kcat)
            s = _each(lambda t, lo: jnp.where(upper_ok & (cj >= lo), t, NEG_BIG), s, lower)
            m = [jnp.max(t, axis=-1, keepdims=True) for t in s]
            p = _each(lambda t, mm: jnp.exp(t - mm).astype(BF16), s, m)
            nl = _each(_dot, p, vext)
            for t in range(tiles_per_iter):
                dst = rows(starts[t])
                num_scr[pi][dst, :] = jnp.where(lane_lo, nl[t][:Q, :PAIR], nl[t][Q:, :PAIR])
                l_scr[pi][dst, :] = jnp.where(lane_lo, nl[t][:Q, PAIR:], nl[t][Q:, PAIR:])
                m_scr[pi][dst, :] = jnp.where(lane_lo, m[t][:Q], m[t][Q:])
            return carry

        lax.fori_loop(0, n_blk // tiles_per_iter, tiles, 0)

    ri = lax.broadcasted_iota(jnp.int32, (PAIR, PAIR), 0)
    ci = lax.broadcasted_iota(jnp.int32, (PAIR, PAIR), 1)
    seg_ones = ((ri >= HEAD_DIM) == (ci >= HEAD_DIM)).astype(BF16)
    gain = gain_ref[...]

    def merge(i, carry):
        rows = pl.ds(pl.multiple_of(i * Q, Q), Q)
        ms = [m_scr[pi][rows, :] for pi in range(n_pat)]
        m_all = functools.reduce(jnp.maximum, ms)
        num = 0.0
        den = 0.0
        for pi in range(n_pat):
            wgt = jnp.exp(ms[pi] - m_all)
            num = num + wgt * num_scr[pi][rows, :]
            den = den + wgt * l_scr[pi][rows, :]
        o = num / den
        ms_o = _split_dot(o * o, seg_ones) * (1.0 / HEAD_DIM)
        o_ref[rows, :] = (o * lax.rsqrt(ms_o + NORM_EPS) * gain).astype(o_ref.dtype)
        return carry

    lax.fori_loop(0, sb_rows // Q, merge, 0, unroll=4)


def _attention(q, k, v, cos_t, sin_t, gain, sb_rows=2048, tiles_per_iter=4):
    B, S, W = q.shape
    n_pairs = W // PAIR
    n_pat = len(DILATED_PATTERNS)
    blk = pl.BlockSpec((None, sb_rows, PAIR), lambda b, p, s: (b, s, p))
    tab = pl.BlockSpec((sb_rows, PAIR), lambda b, p, s: (s, 0))
    return pl.pallas_call(
        functools.partial(_attn_kernel, sb_rows=sb_rows, tiles_per_iter=tiles_per_iter),
        out_shape=jax.ShapeDtypeStruct((B, S, W), BF16),
        grid=(B, n_pairs, S // sb_rows),
        in_specs=[blk, blk, blk, tab, tab, pl.BlockSpec((1, PAIR), lambda b, p, s: (0, p))],
        out_specs=blk,
        scratch_shapes=[pltpu.VMEM((sb_rows, PAIR), F32),
                        pltpu.VMEM((2 * sb_rows, PAIR), F32),
                        pltpu.VMEM((2 * sb_rows, PAIR), F32)]
                       + [pltpu.VMEM((sb_rows, PAIR), F32)] * (3 * n_pat),
        compiler_params=pltpu.CompilerParams(dimension_semantics=("parallel", "parallel", "arbitrary"),
                                             vmem_limit_bytes=VMEM_LIMIT),
        name="dilated_attn",
    )(q, k, v, cos_t, sin_t, gain)


def _rotary_tables(seq):
    half = ROT_DIM // 2
    inv_freq = ROPE_THETA ** (-jnp.arange(half, dtype=F32) * 2.0 / ROT_DIM)
    ang = jnp.arange(seq).astype(F32)[:, None] * inv_freq[None, :]
    cos, sin = jnp.cos(ang), jnp.sin(ang)
    rest = HEAD_DIM - ROT_DIM
    cos_h = jnp.concatenate([cos, cos, jnp.ones((seq, rest), F32)], axis=-1)
    sin_h = jnp.concatenate([-sin, sin, jnp.zeros((seq, rest), F32)], axis=-1)
    return jnp.tile(cos_h, (1, PAIR // HEAD_DIM)), jnp.tile(sin_h, (1, PAIR // HEAD_DIM))


def _outproj_kernel(x_ref, ya_ref, yb_ref, wa_ref, wb_ref, o_ref):
    o_ref[...] = x_ref[...] + _dot(ya_ref[...], wa_ref[...]) + _dot(yb_ref[...], wb_ref[...])


def _outproj(x2d, y_a, y_b, w_a, w_b, tm=1024):
    T, D = x2d.shape
    row = lambda i: (i, 0)
    fixed = lambda i: (0, 0)
    return pl.pallas_call(
        _outproj_kernel,
        out_shape=jax.ShapeDtypeStruct((T, D), F32),
        grid=(T // tm,),
        in_specs=[pl.BlockSpec((tm, D), row), pl.BlockSpec((tm, y_a.shape[1]), row),
                  pl.BlockSpec((tm, y_b.shape[1]), row), pl.BlockSpec(w_a.shape, fixed), pl.BlockSpec(w_b.shape, fixed)],
        out_specs=pl.BlockSpec((tm, D), row),
        compiler_params=pltpu.CompilerParams(dimension_semantics=("parallel",), vmem_limit_bytes=VMEM_LIMIT),
        name="outproj",
    )(x2d, y_a, y_b, w_a, w_b)


FFN_HALO = 16


def _ffn_kernel(x_ref, xh_ref, g_ref, wg_ref, wv_ref, cw_ref, cb_ref, wd_ref, fg_ref, o_ref,
                h_scr, hh_scr, acc_scr, *, tm, seq, apply_final):
    i = pl.program_id(0)
    j = pl.program_id(1)

    @pl.when(j == 0)
    def _():
        h_scr[...] = _rmsnorm(x_ref[...], g_ref[...]).astype(BF16)
        hh_scr[...] = _rmsnorm(xh_ref[...], g_ref[...]).astype(BF16)
        acc_scr[...] = jnp.zeros_like(acc_scr)

    gate = _dot(h_scr[...], wg_ref[...])
    val = _dot(h_scr[...], wv_ref[...])
    seq_start = (i * tm) % seq == 0
    gate_h = jnp.where(seq_start, 0.0, _dot(hh_scr[...], wg_ref[...]))
    row = lax.broadcasted_iota(jnp.int32, (tm, 1), 0)
    g1 = jnp.where(row == 0, gate_h[FFN_HALO - 1:FFN_HALO, :], pltpu.roll(gate, 1, 0))
    g2 = jnp.where(row == 0, gate_h[FFN_HALO - 2:FFN_HALO - 1, :],
                   jnp.where(row == 1, gate_h[FFN_HALO - 1:FFN_HALO, :], pltpu.roll(gate, 2, 0)))
    u = cw_ref[0:1, :] * g2 + cw_ref[1:2, :] * g1 + cw_ref[2:3, :] * gate + cb_ref[...]
    act = (u * _sigmoid(u) * val).astype(BF16)
    acc_scr[...] += _dot(act, wd_ref[...])

    @pl.when(j == pl.num_programs(1) - 1)
    def _():
        y = x_ref[...] + acc_scr[...]
        if apply_final:
            y = _rmsnorm(y, fg_ref[...])
        o_ref[...] = y


def _ffn(x2d, gain, w_gate, w_val, conv_w, conv_b, w_down, final_gain, seq, apply_final, tm=1024, n_ff_tiles=2):
    T, D = x2d.shape
    F = w_gate.shape[1]
    tf = F // n_ff_tiles
    assert tf % LANES == 0 and seq % tm == 0
    halo_blocks = tm // FFN_HALO
    return pl.pallas_call(
        functools.partial(_ffn_kernel, tm=tm, seq=seq, apply_final=apply_final),
        out_shape=jax.ShapeDtypeStruct((T, D), F32),
        grid=(T // tm, n_ff_tiles),
        in_specs=[pl.BlockSpec((tm, D), lambda i, j: (i, 0)),
                  pl.BlockSpec((FFN_HALO, D), lambda i, j: (jnp.maximum(i * halo_blocks - 1, 0), 0)),
                  pl.BlockSpec((1, D), lambda i, j: (0, 0)),
                  pl.BlockSpec((D, tf), lambda i, j: (0, j)),
                  pl.BlockSpec((D, tf), lambda i, j: (0, j)),
                  pl.BlockSpec((CONV_WIDTH, tf), lambda i, j: (0, j)),
                  pl.BlockSpec((1, tf), lambda i, j: (0, j)),
                  pl.BlockSpec((tf, D), lambda i, j: (j, 0)),
                  pl.BlockSpec((1, D), lambda i, j: (0, 0))],
        out_specs=pl.BlockSpec((tm, D), lambda i, j: (i, 0)),
        scratch_shapes=[pltpu.VMEM((tm, D), BF16), pltpu.VMEM((FFN_HALO, D), BF16), pltpu.VMEM((tm, D), F32)],
        compiler_params=pltpu.CompilerParams(dimension_semantics=("parallel", "arbitrary"),
                                             vmem_limit_bytes=VMEM_LIMIT),
        name="convglu_ffn",
    )(x2d, x2d, gain, w_gate, w_val, conv_w, conv_b, w_down, final_gain)


def kernel(x, mix_norm_gain, w_in, rwkv_shift_mix, w0, w_lora_up, a0, a_lora_up, g_lora_up, k_k, k_a, r_k,
           ln_x_w, ln_x_b, attn_norm_gain, w_out, ffn_norm_gain, w_ffn_up, ffn_conv_w, ffn_conv_b,
           w_ffn_down, final_norm_gain):
    B, S, D = x.shape
    depth = w_in.shape[0]
    rw = w0.shape[1]
    aw = attn_norm_gain.shape[1]
    n_w, n_a, n_g = w_lora_up.shape[1], a_lora_up.shape[1], g_lora_up.shape[1]
    n_lora = n_w + n_a + n_g
    lora_pad = -(-n_lora // (2 * LANES)) * (2 * LANES)
    assert lora_pad == 2 * LANES and rw % PAIR == 0 and aw % PAIR == 0
    rwkv_cols = 3 * rw + n_lora
    d_ff = w_ffn_down.shape[1]
    cos_t, sin_t = _rotary_tables(S)

    x2d = x.reshape(B * S, D)
    for l in range(depth):
        pad = jnp.zeros((D, lora_pad - n_lora), F32)
        w_all = jnp.concatenate([w_in[l][:, :rwkv_cols], pad, w_in[l][:, rwkv_cols:]], axis=1).astype(BF16)
        mix = jnp.concatenate([rwkv_shift_mix[l], jnp.zeros((lora_pad - n_lora,), F32)])[None, :]
        wc = jnp.zeros((lora_pad, 3 * rw), F32)
        wc = wc.at[:n_w, :rw].set(w_lora_up[l])
        wc = wc.at[n_w:n_w + n_a, rw:2 * rw].set(a_lora_up[l])
        wc = wc.at[n_w + n_a:n_lora, 2 * rw:].set(g_lora_up[l]).astype(BF16)

        p_r, q, k, v = _inproj(x2d, mix_norm_gain[l][None, :], w_all, 3 * rw + lora_pad, aw)
        y_rwkv = _rwkv(p_r.reshape(B, S, -1), mix, wc, w0[l][None, :], a0[l][None, :], k_k[l][None, :],
                       k_a[l][None, :], r_k[l].reshape(1, rw), ln_x_w[l][None, :], ln_x_b[l][None, :],
                       rw, (n_w, n_w + n_a, n_lora))
        y_attn = _attention(q.reshape(B, S, aw), k.reshape(B, S, aw), v.reshape(B, S, aw), cos_t, sin_t,
                            attn_norm_gain[l][None, :])
        w_o = w_out[l].astype(BF16)
        x2d = _outproj(x2d, y_rwkv.reshape(B * S, rw), y_attn.reshape(B * S, aw), w_o[:rw], w_o[rw:])
        w_up = w_ffn_up[l].astype(BF16)
        x2d = _ffn(x2d, ffn_norm_gain[l][None, :], w_up[:, :d_ff], w_up[:, d_ff:], ffn_conv_w[l],
                   ffn_conv_b[l][None, :], w_ffn_down[l].astype(BF16), final_norm_gain[None, :], S,
                   apply_final=(l == depth - 1))
    return x2d.reshape(B, S, D)
```

```python
import functools
import math

import jax
import jax.numpy as jnp
from jax import lax
from jax.experimental import pallas as pl
from jax.experimental.pallas import tpu as pltpu

F32 = jnp.float32
BF16 = jnp.bfloat16

LANES = 128
HEAD_DIM = 64
PAIR = 2 * HEAD_DIM
ROT_DIM = HEAD_DIM // 4
ROPE_THETA = 500000.0
NORM_EPS = 1e-6
GN_EPS = 64e-5
DILATED_PATTERNS = ((128, 1), (512, 4), (2048, 16))
ATTN_BLOCK = 128
CONV_WIDTH = 3
CHUNK = 64
NEG_BIG = -1e30
VMEM_LIMIT = 56 * 1024 * 1024


def _dot(a, b):
    return jnp.dot(a, b, preferred_element_type=F32)


def _dot_nt(a, b):
    return lax.dot_general(a, b, (((1,), (1,)), ((), ())), preferred_element_type=F32)


def _dot_tn(a, b):
    return lax.dot_general(a, b, (((0,), (0,)), ((), ())), preferred_element_type=F32)


def _rmsnorm(x, gain):
    return x * lax.rsqrt(jnp.mean(x * x, axis=-1, keepdims=True) + NORM_EPS) * gain


def _sigmoid(x):
    return 1.0 / (1.0 + jnp.exp(-x))


def _split_dot(x, w):
    hi = x.astype(BF16)
    lo = (x - hi.astype(F32)).astype(BF16)
    return _dot(hi, w) + _dot(lo, w)


def _inproj_kernel(x_ref, g_ref, w_ref, pr_ref, q_ref, k_ref, v_ref, *, rw, aw):
    h = _rmsnorm(x_ref[...], g_ref[...]).astype(BF16)
    p = _dot(h, w_ref[...])
    pr_ref[...] = p[:, :rw]
    q_ref[...] = p[:, rw:rw + aw]
    k_ref[...] = p[:, rw + aw:rw + 2 * aw]
    v_ref[...] = p[:, rw + 2 * aw:rw + 3 * aw]


def _inproj(x2d, gain, w_all, rw, aw, tm=512):
    T, D = x2d.shape
    N = w_all.shape[1]
    row = lambda i: (i, 0)
    fixed = lambda i: (0, 0)
    return pl.pallas_call(
        functools.partial(_inproj_kernel, rw=rw, aw=aw),
        out_shape=(jax.ShapeDtypeStruct((T, rw), F32),) + (jax.ShapeDtypeStruct((T, aw), F32),) * 3,
        grid=(T // tm,),
        in_specs=[pl.BlockSpec((tm, D), row), pl.BlockSpec((1, D), fixed), pl.BlockSpec((D, N), fixed)],
        out_specs=(pl.BlockSpec((tm, rw), row),) + (pl.BlockSpec((tm, aw), row),) * 3,
        compiler_params=pltpu.CompilerParams(dimension_semantics=("parallel",), vmem_limit_bytes=VMEM_LIMIT),
        name="inproj",
    )(x2d, gain, w_all)


def _each(fn, *lists):
    return [fn(*args) for args in zip(*lists)]


def _rwkv_chunk_terms(r, k, v, logw, cum, kkn, bb, same_blk, strict_c, incl_c, lane_lo):
    L = CHUNK
    bf = lambda t: t.astype(BF16)
    cum_last = [t[L - 1:L, :] for t in cum]
    g_in = _each(jnp.exp, cum)
    g_ex = _each(lambda t, w: jnp.exp(t - w), cum, logw)
    g_inv = _each(lambda t: jnp.exp(-t), cum)
    g_hat = _each(lambda tl, t: jnp.exp(tl - t), cum_last, cum)
    g_last = _each(jnp.exp, cum_last)
    yield

    def by_head_rows(x):
        lo = lane_lo if x.shape[1] == PAIR else jnp.concatenate([lane_lo] * (x.shape[1] // PAIR), axis=1)
        return jnp.concatenate([jnp.where(lo, x, 0.0), jnp.where(lo, 0.0, x)], axis=0).astype(BF16)

    al = _each(lambda t, g: -t * g, kkn, g_ex)
    rb = _each(jnp.multiply, r, g_in)
    bt = _each(jnp.multiply, bb, g_inv)
    kt = _each(jnp.multiply, k, g_inv)
    bh = _each(jnp.multiply, bb, g_hat)
    kh = _each(jnp.multiply, k, g_hat)
    lhs = _each(lambda a, b: jnp.concatenate([a, b], axis=0).astype(BF16), al, rb)
    rhs = _each(lambda a, b: jnp.concatenate([by_head_rows(a), by_head_rows(b)], axis=0), bt, kt)
    yield
    aq = _each(_dot_nt, lhs, rhs)
    a_ab = [jnp.where(strict_c, t[:L, :2 * L], 0.0) for t in aq]
    a_ak = [jnp.where(strict_c, t[:L, 2 * L:], 0.0).astype(BF16) for t in aq]
    a_rb = [jnp.where(incl_c, t[L:, :2 * L], 0.0).astype(BF16) for t in aq]
    a_rk = [jnp.where(incl_c, t[L:, 2 * L:], 0.0).astype(BF16) for t in aq]
    yield

    v_s = _each(by_head_rows, v)
    av = _each(_dot, a_ak, v_s)
    x = _each(lambda a, b: jnp.concatenate([a, b], axis=1), al, av)
    yield
    ap = a_ab
    n_lvl = int(math.log2(L))
    for lvl in range(n_lvl):
        apb = _each(bf, ap)
        x = _each(lambda t, a: t + _dot(a, by_head_rows(t)), x, apb)
        if lvl + 1 < n_lvl:
            ap = _each(lambda a, ab: _dot(ab, jnp.where(same_blk, jnp.concatenate([a, a], axis=0), 0.0).astype(BF16)),
                       ap, apb)
        yield

    z = _each(lambda a, t: _dot(a, by_head_rows(t)), a_rb, x)
    akv = _each(_dot, a_rk, v_s)
    w2 = _each(lambda a, t: (a + t[:, :PAIR]).astype(BF16), rb, z)
    y_loc = _each(lambda t, a: t[:, PAIR:] + a, z, akv)
    w1 = [t[:, :PAIR].astype(BF16) for t in x]
    u_loc = [t[:, PAIR:] for t in x]
    yield

    m_t = _each(lambda a, b: jnp.where(same_blk, _dot_tn(a, b.astype(BF16)), 0.0).astype(BF16), w1, bh)
    s_loc = _each(
        lambda u, vv, b, kk_: jnp.where(
            same_blk,
            _dot_tn(jnp.concatenate([u, vv], axis=0).astype(BF16), jnp.concatenate([b, kk_], axis=0).astype(BF16)),
            0.0),
        u_loc, v, bh, kh)
    return w2, y_loc, m_t, s_loc, g_last


def _rwkv_kernel(p_ref, mix_ref, wc_ref, w0_ref, a0_ref, kk_ref, ka_ref, rk_ref, lnw_ref, lnb_ref,
                 o_ref, s_scr, carry_scr, *, nb, width, n_lora, n_chunks, n_groups, stage_offset):
    L = CHUNK
    R = n_chunks * L
    c = pl.program_id(0)

    @pl.when(c == 0)
    def _():
        s_scr[...] = jnp.zeros_like(s_scr)
        carry_scr[...] = jnp.zeros_like(carry_scr)

    n_pairs = width // PAIR
    row = lax.broadcasted_iota(jnp.int32, (R, 1), 0)
    ri = lax.broadcasted_iota(jnp.int32, (2 * L, 2 * L), 0)
    ci = lax.broadcasted_iota(jnp.int32, (2 * L, 2 * L), 1)
    same_blk = (ri >= L) == (ci >= L)
    ti = lax.broadcasted_iota(jnp.int32, (L, 2 * L), 0)
    si = lax.broadcasted_iota(jnp.int32, (L, 2 * L), 1) % L
    strict_c = si < ti
    incl_c = si <= ti
    rr = lax.broadcasted_iota(jnp.int32, (R, R), 0)
    rc = lax.broadcasted_iota(jnp.int32, (R, R), 1)
    tri = ((rr // L == rc // L) & (rr >= rc)).astype(BF16)
    lane_lo = lax.broadcasted_iota(jnp.int32, (L, PAIR), 1) < HEAD_DIM
    lane_lo_r = lax.broadcasted_iota(jnp.int32, (R, PAIR), 1) < HEAD_DIM
    lora_lane = lax.broadcasted_iota(jnp.int32, (R, 2 * LANES), 1)

    def head_sums(x):
        s0 = jnp.sum(jnp.where(lane_lo_r, x, 0.0), axis=-1, keepdims=True)
        s1 = jnp.sum(jnp.where(lane_lo_r, 0.0, x), axis=-1, keepdims=True)
        return jnp.where(lane_lo_r, s0, s1)

    mix = mix_ref[...]

    def lora_up(pm):
        lora = pm[:, 3 * width:]
        act = jnp.where(lora_lane < n_lora[0], jnp.tanh(lora),
                        jnp.where(lora_lane < n_lora[1], lora,
                                  jnp.where(lora_lane < n_lora[2], _sigmoid(lora), 0.0)))
        return _dot(act.astype(BF16), wc_ref[...])

    def log_decay(up):
        t = w0_ref[...] + up[:, :width]
        return -jnp.exp(-(jnp.maximum(-t, 0.0) + jnp.log(1.0 + jnp.exp(-jnp.abs(t)))) - 0.5)

    def running_sum(t):
        h1 = t.astype(BF16)
        r1 = t - h1.astype(F32)
        h2 = r1.astype(BF16)
        h3 = (r1 - h2.astype(F32)).astype(BF16)
        return _dot(tri, h1) + _dot(tri, h2) + _dot(tri, h3)

    tiles = [(b, pr) for b in range(nb) for pr in range(n_pairs)]
    n_t = len(tiles)
    lanes = lambda pr, part=0: slice(part * width + pr * PAIR, part * width + (pr + 1) * PAIR)
    sls = [lanes(pr) for _, pr in tiles]

    states = {0: [s_scr[i] for i in range(n_t)]}
    outs = {}

    def run_group(gi):
        g0 = gi * R
        pms = []
        for b in range(nb):
            p = p_ref[b, g0:g0 + R, :]
            prev_last = carry_scr[b, 7:8, :] if gi == 0 else p_ref[b, g0 - 1:g0, :]
            p_prev = jnp.where(row == 0, prev_last, pltpu.roll(p, 1, 0))
            pms.append(p + (p_prev - p) * mix)
        yield
        ups = _each(lora_up, pms)
        logw_b = _each(log_decay, ups)
        cum_b = _each(running_sum, logw_b)
        yield

        r = [pms[b][:, lanes(pr, 0)] for b, pr in tiles]
        k_raw = [pms[b][:, lanes(pr, 1)] for b, pr in tiles]
        v = [pms[b][:, lanes(pr, 2)] for b, pr in tiles]
        logw = [logw_b[b][:, lanes(pr)] for b, pr in tiles]
        cum = [cum_b[b][:, lanes(pr)] for b, pr in tiles]
        a = [_sigmoid(a0_ref[:, lanes(pr)] + ups[b][:, lanes(pr, 1)]) for b, pr in tiles]
        g = [ups[b][:, lanes(pr, 2)] for b, pr in tiles]

        kk = _each(lambda t, sl: t * kk_ref[:, sl], k_raw, sls)
        kk_ss = _each(lambda t: head_sums(t * t), kk)
        kkn = _each(lambda t, ss: t / jnp.maximum(jnp.sqrt(ss), 1e-12), kk, kk_ss)
        k = _each(lambda t, aa, sl: t * (1.0 + (aa - 1.0) * ka_ref[:, sl]), k_raw, a, sls)
        bb = _each(jnp.multiply, kkn, a)
        bonus_dot = _each(lambda rr_, kk_, sl: head_sums(rr_ * kk_ * rk_ref[:, sl]), r, k, sls)
        yield

        def chunks(ts):
            return [t[ch * L:(ch + 1) * L] for ch in range(n_chunks) for t in ts]

        w2, y_loc, m_t, s_loc, g_last = yield from _rwkv_chunk_terms(
            chunks(r), chunks(k), chunks(v), chunks(logw), chunks(cum), chunks(kkn), chunks(bb),
            same_blk, strict_c, incl_c, lane_lo)
        yield

        s = states[gi]
        y_parts = []
        for ch in range(n_chunks):
            sel = slice(ch * n_t, (ch + 1) * n_t)
            s_b = _each(lambda t: t.astype(BF16), s)
            y_parts.append(_each(lambda a_, sb, yl: _dot_nt(a_, sb) + yl, w2[sel], s_b, y_loc[sel]))
            s = _each(lambda s0, gl, sb, m, sl: s0 * gl + _dot(sb, m) + sl, s, g_last[sel], s_b, m_t[sel], s_loc[sel])
        states[gi + 1] = s
        y = [jnp.concatenate([y_parts[ch][i] for ch in range(n_chunks)], axis=0) for i in range(n_t)]
        yield

        mu = _each(lambda t: head_sums(t) * (1.0 / HEAD_DIM), y)
        yc = _each(jnp.subtract, y, mu)
        var = _each(lambda t: head_sums(t * t) * (1.0 / HEAD_DIM), yc)
        yield
        res = []
        for i in range(n_t):
            sl = sls[i]
            yn = yc[i] * lax.rsqrt(var[i] + GN_EPS) * lnw_ref[:, sl] + lnb_ref[:, sl]
            res.append(((yn + bonus_dot[i] * v[i]) * g[i]).astype(o_ref.dtype))
        outs[gi] = res

    gens = [run_group(gi) for gi in range(n_groups)]
    live = [True] * n_groups
    tick = 0
    while any(live):
        for gi in range(n_groups):
            if live[gi] and tick >= gi * stage_offset:
                try:
                    next(gens[gi])
                except StopIteration:
                    live[gi] = False
        tick += 1

    for gi in range(n_groups):
        for i, (b, pr) in enumerate(tiles):
            o_ref[b, gi * R:(gi + 1) * R, sls[i]] = outs[gi][i]
    for i in range(n_t):
        s_scr[i] = states[n_groups][i]
    for b in range(nb):
        carry_scr[b] = p_ref[b, n_groups * R - 8:n_groups * R, :]


def _rwkv(p_r, mix, wc, w0, a0, k_k, k_a, r_k, ln_w, ln_b, width, n_lora, n_chunks=2, n_groups=2, stage_offset=5):
    B, S, C = p_r.shape
    L = CHUNK * n_chunks * n_groups
    n_state = B * (width // PAIR)
    vec = lambda n: pl.BlockSpec((1, n), lambda c: (0, 0))
    return pl.pallas_call(
        functools.partial(_rwkv_kernel, nb=B, width=width, n_lora=n_lora, n_chunks=n_chunks, n_groups=n_groups,
                          stage_offset=stage_offset),
        out_shape=jax.ShapeDtypeStruct((B, S, width), BF16),
        grid=(S // L,),
        in_specs=[pl.BlockSpec((B, L, C), lambda c: (0, c, 0)), vec(C),
                  pl.BlockSpec(wc.shape, lambda c: (0, 0))] + [vec(width)] * 7,
        out_specs=pl.BlockSpec((B, L, width), lambda c: (0, c, 0)),
        scratch_shapes=[pltpu.VMEM((n_state, PAIR, PAIR), F32), pltpu.VMEM((B, 8, C), F32)],
        compiler_params=pltpu.CompilerParams(dimension_semantics=("arbitrary",), vmem_limit_bytes=VMEM_LIMIT),
        name="rwkv7",
    )(p_r, mix, wc, w0, a0, k_k, k_a, r_k, ln_w, ln_b)


def _attn_kernel(q_ref, k_ref, v_ref, cos_ref, sin_ref, gain_ref, o_ref,
                 q_scr, k_ring, v_ring, *stat_scr, sb_rows, tiles_per_iter):
    sb = pl.program_id(2)
    n_pat = len(DILATED_PATTERNS)
    num_scr, m_scr, l_scr = stat_scr[:n_pat], stat_scr[n_pat:2 * n_pat], stat_scr[2 * n_pat:]
    Q = ATTN_BLOCK
    ring_rows = 2 * sb_rows
    cur_base = (sb % 2) * sb_rows

    lane = lax.broadcasted_iota(jnp.int32, (sb_rows, PAIR), 1)
    first_half = (lane % HEAD_DIM) < (ROT_DIM // 2)

    def rotary(x):
        partner = jnp.where(first_half, pltpu.roll(x, PAIR - ROT_DIM // 2, 1), pltpu.roll(x, ROT_DIM // 2, 1))
        return x * cos_ref[...] + partner * sin_ref[...]

    @pl.when(sb == 0)
    def _():
        k_ring[pl.ds(sb_rows, sb_rows), :] = jnp.zeros((sb_rows, PAIR), F32)
        v_ring[pl.ds(sb_rows, sb_rows), :] = jnp.zeros((sb_rows, PAIR), F32)

    q_scr[...] = rotary(q_ref[...]) * (1.0 / math.sqrt(HEAD_DIM))
    k_ring[pl.ds(pl.multiple_of(cur_base, sb_rows), sb_rows), :] = rotary(k_ref[...])
    v_ring[pl.ds(pl.multiple_of(cur_base, sb_rows), sb_rows), :] = v_ref[...]

    ii = lax.broadcasted_iota(jnp.int32, (2 * Q, 2 * Q), 0) % Q
    cj = lax.broadcasted_iota(jnp.int32, (2 * Q, 2 * Q), 1)
    upper_ok = cj <= ii + Q
    lane_lo = lax.broadcasted_iota(jnp.int32, (Q, PAIR), 1) < HEAD_DIM
    ones_blk = jnp.ones((2 * Q, PAIR), BF16)

    for pi, (window, dil) in enumerate(DILATED_PATTERNS):
        assert window // dil == Q
        span = Q * dil
        n_blk = sb_rows // Q
        assert n_blk % tiles_per_iter == 0

        def tiles(it, carry, dil=dil, span=span, pi=pi):
            rows = lambda s: pl.ds(s, Q, stride=dil) if dil > 1 else pl.ds(s, Q)
            gs = [it * tiles_per_iter + t for t in range(tiles_per_iter)]
            starts = [(g // dil) * span + (g % dil) for g in gs]
            k_cur = [cur_base + s for s in starts]
            k_prev = [(s - span + ring_rows) % ring_rows for s in k_cur]
            lower = [jnp.maximum(ii, jnp.where((sb > 0) | (s >= span), 0, Q)) for s in starts]
            q2 = [q_scr[rows(s), :] for s in starts]
            q2 = [jnp.concatenate([jnp.where(lane_lo, t, 0.0), jnp.where(lane_lo, 0.0, t)], axis=0).astype(BF16)
                  for t in q2]
            kcat = _each(lambda a, b: jnp.concatenate([k_ring[rows(a), :], k_ring[rows(b), :]], axis=0).astype(BF16),
                         k_prev, k_cur)
            vcat = _each(lambda a, b: jnp.concatenate([v_ring[rows(a), :], v_ring[rows(b), :]], axis=0).astype(BF16),
                         k_prev, k_cur)
            vext = [jnp.concatenate([t, ones_blk], axis=1) for t in vcat]
            s = _each(_dot_nt, q2, kcat)
            s = _each(lambda t, lo: jnp.where(upper_ok & (cj >= lo), t, NEG_BIG), s, lower)
            m = [jnp.max(t, axis=-1, keepdims=True) for t in s]
            p = _each(lambda t, mm: jnp.exp(t - mm).astype(BF16), s, m)
            nl = _each(_dot, p, vext)
            for t in range(tiles_per_iter):
                dst = rows(starts[t])
                num_scr[pi][dst, :] = jnp.where(lane_lo, nl[t][:Q, :PAIR], nl[t][Q:, :PAIR])
                l_scr[pi][dst, :] = jnp.where(lane_lo, nl[t][:Q, PAIR:], nl[t][Q:, PAIR:])
                m_scr[pi][dst, :] = jnp.where(lane_lo, m[t][:Q], m[t][Q:])
            return carry

        lax.fori_loop(0, n_blk // tiles_per_iter, tiles, 0)

    ri = lax.broadcasted_iota(jnp.int32, (PAIR, PAIR), 0)
    ci = lax.broadcasted_iota(jnp.int32, (PAIR, PAIR), 1)
    seg_ones = ((ri >= HEAD_DIM) == (ci >= HEAD_DIM)).astype(BF16)
    gain = gain_ref[...]

    def merge(i, carry):
        rows = pl.ds(pl.multiple_of(i * Q, Q), Q)
        ms = [m_scr[pi][rows, :] for pi in range(n_pat)]
        m_all = functools.reduce(jnp.maximum, ms)
        num = 0.0
        den = 0.0
        for pi in range(n_pat):
            wgt = jnp.exp(ms[pi] - m_all)
            num = num + wgt * num_scr[pi][rows, :]
            den = den + wgt * l_scr[pi][rows, :]
        o = num / den
        ms_o = _split_dot(o * o, seg_ones) * (1.0 / HEAD_DIM)
        o_ref[rows, :] = (o * lax.rsqrt(ms_o + NORM_EPS) * gain).astype(o_ref.dtype)
        return carry

    lax.fori_loop(0, sb_rows // Q, merge, 0, unroll=4)


def _attention(q, k, v, cos_t, sin_t, gain, sb_rows=2048, tiles_per_iter=4):
    B, S, W = q.shape
    n_pairs = W // PAIR
    n_pat = len(DILATED_PATTERNS)
    blk = pl.BlockSpec((None, sb_rows, PAIR), lambda b, p, s: (b, s, p))
    tab = pl.BlockSpec((sb_rows, PAIR), lambda b, p, s: (s, 0))
    return pl.pallas_call(
        functools.partial(_attn_kernel, sb_rows=sb_rows, tiles_per_iter=tiles_per_iter),
        out_shape=jax.ShapeDtypeStruct((B, S, W), BF16),
        grid=(B, n_pairs, S // sb_rows),
        in_specs=[blk, blk, blk, tab, tab, pl.BlockSpec((1, PAIR), lambda b, p, s: (0, p))],
        out_specs=blk,
        scratch_shapes=[pltpu.VMEM((sb_rows, PAIR), F32),
                        pltpu.VMEM((2 * sb_rows, PAIR), F32),
                        pltpu.VMEM((2 * sb_rows, PAIR), F32)]
                       + [pltpu.VMEM((sb_rows, PAIR), F32)] * (3 * n_pat),
        compiler_params=pltpu.CompilerParams(dimension_semantics=("parallel", "parallel", "arbitrary"),
                                             vmem_limit_bytes=VMEM_LIMIT),
        name="dilated_attn",
    )(q, k, v, cos_t, sin_t, gain)


def _rotary_tables(seq):
    half = ROT_DIM // 2
    inv_freq = ROPE_THETA ** (-jnp.arange(half, dtype=F32) * 2.0 / ROT_DIM)
    ang = jnp.arange(seq).astype(F32)[:, None] * inv_freq[None, :]
    cos, sin = jnp.cos(ang), jnp.sin(ang)
    rest = HEAD_DIM - ROT_DIM
    cos_h = jnp.concatenate([cos, cos, jnp.ones((seq, rest), F32)], axis=-1)
    sin_h = jnp.concatenate([-sin, sin, jnp.zeros((seq, rest), F32)], axis=-1)
    return jnp.tile(cos_h, (1, PAIR // HEAD_DIM)), jnp.tile(sin_h, (1, PAIR // HEAD_DIM))


def _outproj_kernel(x_ref, ya_ref, yb_ref, wa_ref, wb_ref, o_ref):
    o_ref[...] = x_ref[...] + _dot(ya_ref[...], wa_ref[...]) + _dot(yb_ref[...], wb_ref[...])


def _outproj(x2d, y_a, y_b, w_a, w_b, tm=1024):
    T, D = x2d.shape
    row = lambda i: (i, 0)
    fixed = lambda i: (0, 0)
    return pl.pallas_call(
        _outproj_kernel,
        out_shape=jax.ShapeDtypeStruct((T, D), F32),
        grid=(T // tm,),
        in_specs=[pl.BlockSpec((tm, D), row), pl.BlockSpec((tm, y_a.shape[1]), row),
                  pl.BlockSpec((tm, y_b.shape[1]), row), pl.BlockSpec(w_a.shape, fixed), pl.BlockSpec(w_b.shape, fixed)],
        out_specs=pl.BlockSpec((tm, D), row),
        compiler_params=pltpu.CompilerParams(dimension_semantics=("parallel",), vmem_limit_bytes=VMEM_LIMIT),
        name="outproj",
    )(x2d, y_a, y_b, w_a, w_b)


FFN_HALO = 16


def _ffn_kernel(x_ref, xh_ref, g_ref, wg_ref, wv_ref, cw_ref, cb_ref, wd_ref, fg_ref, o_ref,
                h_scr, hh_scr, acc_scr, *, tm, seq, apply_final):
    i = pl.program_id(0)
    j = pl.program_id(1)

    @pl.when(j == 0)
    def _():
        h_scr[...] = _rmsnorm(x_ref[...], g_ref[...]).astype(BF16)
        hh_scr[...] = _rmsnorm(xh_ref[...], g_ref[...]).astype(BF16)
        acc_scr[...] = jnp.zeros_like(acc_scr)

    gate = _dot(h_scr[...], wg_ref[...])
    val = _dot(h_scr[...], wv_ref[...])
    seq_start = (i * tm) % seq == 0
    gate_h = jnp.where(seq_start, 0.0, _dot(hh_scr[...], wg_ref[...]))
    row = lax.broadcasted_iota(jnp.int32, (tm, 1), 0)
    g1 = jnp.where(row == 0, gate_h[FFN_HALO - 1:FFN_HALO, :], pltpu.roll(gate, 1, 0))
    g2 = jnp.where(row == 0, gate_h[FFN_HALO - 2:FFN_HALO - 1, :],
                   jnp.where(row == 1, gate_h[FFN_HALO - 1:FFN_HALO, :], pltpu.roll(gate, 2, 0)))
    u = cw_ref[0:1, :] * g2 + cw_ref[1:2, :] * g1 + cw_ref[2:3, :] * gate + cb_ref[...]
    act = (u * _sigmoid(u) * val).astype(BF16)
    acc_scr[...] += _dot(act, wd_ref[...])

    @pl.when(j == pl.num_programs(1) - 1)
    def _():
        y = x_ref[...] + acc_scr[...]
        if apply_final:
            y = _rmsnorm(y, fg_ref[...])
        o_ref[...] = y


def _ffn(x2d, gain, w_gate, w_val, conv_w, conv_b, w_down, final_gain, seq, apply_final, tm=1024, n_ff_tiles=2):
    T, D = x2d.shape
    F = w_gate.shape[1]
    tf = F // n_ff_tiles
    assert tf % LANES == 0 and seq % tm == 0
    halo_blocks = tm // FFN_HALO
    return pl.pallas_call(
        functools.partial(_ffn_kernel, tm=tm, seq=seq, apply_final=apply_final),
        out_shape=jax.ShapeDtypeStruct((T, D), F32),
        grid=(T // tm, n_ff_tiles),
        in_specs=[pl.BlockSpec((tm, D), lambda i, j: (i, 0)),
                  pl.BlockSpec((FFN_HALO, D), lambda i, j: (jnp.maximum(i * halo_blocks - 1, 0), 0)),
                  pl.BlockSpec((1, D), lambda i, j: (0, 0)),
                  pl.BlockSpec((D, tf), lambda i, j: (0, j)),
                  pl.BlockSpec((D, tf), lambda i, j: (0, j)),
                  pl.BlockSpec((CONV_WIDTH, tf), lambda i, j: (0, j)),
                  pl.BlockSpec((1, tf), lambda i, j: (0, j)),
                  pl.BlockSpec((tf, D), lambda i, j: (j, 0)),
                  pl.BlockSpec((1, D), lambda i, j: (0, 0))],
        out_specs=pl.BlockSpec((tm, D), lambda i, j: (i, 0)),
        scratch_shapes=[pltpu.VMEM((tm, D), BF16), pltpu.VMEM((FFN_HALO, D), BF16), pltpu.VMEM((tm, D), F32)],
        compiler_params=pltpu.CompilerParams(dimension_semantics=("parallel", "arbitrary"),
                                             vmem_limit_bytes=VMEM_LIMIT),
        name="convglu_ffn",
    )(x2d, x2d, gain, w_gate, w_val, conv_w, conv_b, w_down, final_gain)


def kernel(x, mix_norm_gain, w_in, rwkv_shift_mix, w0, w_lora_up, a0, a_lora_up, g_lora_up, k_k, k_a, r_k,
           ln_x_w, ln_x_b, attn_norm_gain, w_out, ffn_norm_gain, w_ffn_up, ffn_conv_w, ffn_conv_b,
           w_ffn_down, final_norm_gain):
    B, S, D = x.shape
    depth = w_in.shape[0]
    rw = w0.shape[1]
    aw = attn_norm_gain.shape[1]
    n_w, n_a, n_g = w_lora_up.shape[1], a_lora_up.shape[1], g_lora_up.shape[1]
    n_lora = n_w + n_a + n_g
    lora_pad = -(-n_lora // (2 * LANES)) * (2 * LANES)
    assert lora_pad == 2 * LANES and rw % PAIR == 0 and aw % PAIR == 0
    rwkv_cols = 3 * rw + n_lora
    d_ff = w_ffn_down.shape[1]
    cos_t, sin_t = _rotary_tables(S)

    x2d = x.reshape(B * S, D)
    for l in range(depth):
        pad = jnp.zeros((D, lora_pad - n_lora), F32)
        w_all = jnp.concatenate([w_in[l][:, :rwkv_cols], pad, w_in[l][:, rwkv_cols:]], axis=1).astype(BF16)
        mix = jnp.concatenate([rwkv_shift_mix[l], jnp.zeros((lora_pad - n_lora,), F32)])[None, :]
        wc = jnp.zeros((lora_pad, 3 * rw), F32)
        wc = wc.at[:n_w, :rw].set(w_lora_up[l])
        wc = wc.at[n_w:n_w + n_a, rw:2 * rw].set(a_lora_up[l])
        wc = wc.at[n_w + n_a:n_lora, 2 * rw:].set(g_lora_up[l]).astype(BF16)

        p_r, q, k, v = _inproj(x2d, mix_norm_gain[l][None, :], w_all, 3 * rw + lora_pad, aw)
        y_rwkv = _rwkv(p_r.reshape(B, S, -1), mix, wc, w0[l][None, :], a0[l][None, :], k_k[l][None, :],
                       k_a[l][None, :], r_k[l].reshape(1, rw), ln_x_w[l][None, :], ln_x_b[l][None, :],
                       rw, (n_w, n_w + n_a, n_lora))
        y_attn = _attention(q.reshape(B, S, aw), k.reshape(B, S, aw), v.reshape(B, S, aw), cos_t, sin_t,
                            attn_norm_gain[l][None, :])
        w_o = w_out[l].astype(BF16)
        x2d = _outproj(x2d, y_rwkv.reshape(B * S, rw), y_attn.reshape(B * S, aw), w_o[:rw], w_o[rw:])
        w_up = w_ffn_up[l].astype(BF16)
        x2d = _ffn(x2d, ffn_norm_gain[l][None, :], w_up[:, :d_ff], w_up[:, d_ff:], ffn_conv_w[l],
                   ffn_conv_b[l][None, :], w_ffn_down[l].astype(BF16), final_norm_gain[None, :], S,
                   apply_final=(l == depth - 1))
    return x2d.reshape(B, S, D)
```

```python
import functools
import math

import jax
import jax.numpy as jnp
from jax import lax
from jax.experimental import pallas as pl
from jax.experimental.pallas import tpu as pltpu

F32 = jnp.float32
BF16 = jnp.bfloat16

LANES = 128
HEAD_DIM = 64
PAIR = 2 * HEAD_DIM
ROT_DIM = HEAD_DIM // 4
ROPE_THETA = 500000.0
NORM_EPS = 1e-6
GN_EPS = 64e-5
DILATED_PATTERNS = ((128, 1), (512, 4), (2048, 16))
ATTN_BLOCK = 128
CONV_WIDTH = 3
CHUNK = 64
NEG_BIG = -1e30
VMEM_LIMIT = 56 * 1024 * 1024


def _dot(a, b):
    return jnp.dot(a, b, preferred_element_type=F32)


def _dot_nt(a, b):
    return lax.dot_general(a, b, (((1,), (1,)), ((), ())), preferred_element_type=F32)


def _dot_tn(a, b):
    return lax.dot_general(a, b, (((0,), (0,)), ((), ())), preferred_element_type=F32)


def _rmsnorm(x, gain):
    return x * lax.rsqrt(jnp.mean(x * x, axis=-1, keepdims=True) + NORM_EPS) * gain


def _sigmoid(x):
    return 1.0 / (1.0 + jnp.exp(-x))


def _split_dot(x, w):
    hi = x.astype(BF16)
    lo = (x - hi.astype(F32)).astype(BF16)
    return _dot(hi, w) + _dot(lo, w)


def _inproj_kernel(x_ref, g_ref, w_ref, cos_ref, sin_ref, pr_ref, q_ref, k_ref, v_ref, *, rw, aw):
    h = _rmsnorm(x_ref[...], g_ref[...]).astype(BF16)
    p = _dot(h, w_ref[...])
    pr_ref[...] = p[:, :rw]

    tm = x_ref.shape[0]
    lane = lax.broadcasted_iota(jnp.int32, (tm, PAIR), 1)
    first_half = (lane % HEAD_DIM) < (ROT_DIM // 2)
    cos, sin = cos_ref[...], sin_ref[...]

    def rotary(x):
        partner = jnp.where(first_half, pltpu.roll(x, PAIR - ROT_DIM // 2, 1), pltpu.roll(x, ROT_DIM // 2, 1))
        return x * cos + partner * sin

    for t in range(aw // PAIR):
        lo = rw + t * PAIR
        q_ref[:, t * PAIR:(t + 1) * PAIR] = (rotary(p[:, lo:lo + PAIR]) * (1.0 / math.sqrt(HEAD_DIM))).astype(BF16)
        k_ref[:, t * PAIR:(t + 1) * PAIR] = rotary(p[:, lo + aw:lo + aw + PAIR]).astype(BF16)
    v_ref[...] = p[:, rw + 2 * aw:rw + 3 * aw].astype(BF16)


def _inproj(x2d, gain, w_all, cos_t, sin_t, rw, aw, tm=512):
    T, D = x2d.shape
    N = w_all.shape[1]
    seq_tiles = cos_t.shape[0] // tm
    row = lambda i: (i, 0)
    fixed = lambda i: (0, 0)
    tab = pl.BlockSpec((tm, PAIR), lambda i: (i % seq_tiles, 0))
    return pl.pallas_call(
        functools.partial(_inproj_kernel, rw=rw, aw=aw),
        out_shape=(jax.ShapeDtypeStruct((T, rw), F32),) + (jax.ShapeDtypeStruct((T, aw), BF16),) * 3,
        grid=(T // tm,),
        in_specs=[pl.BlockSpec((tm, D), row), pl.BlockSpec((1, D), fixed), pl.BlockSpec((D, N), fixed), tab, tab],
        out_specs=(pl.BlockSpec((tm, rw), row),) + (pl.BlockSpec((tm, aw), row),) * 3,
        compiler_params=pltpu.CompilerParams(dimension_semantics=("parallel",), vmem_limit_bytes=VMEM_LIMIT),
        name="inproj",
    )(x2d, gain, w_all, cos_t, sin_t)


def _each(fn, *lists):
    return [fn(*args) for args in zip(*lists)]


def _rwkv_chunk_terms(r, k, v, logw, cum, kkn, bb, same_blk, strict_c, incl_c, lane_lo):
    L = CHUNK
    bf = lambda t: t.astype(BF16)
    cum_last = [t[L - 1:L, :] for t in cum]
    g_in = _each(jnp.exp, cum)
    g_ex = _each(lambda t, w: jnp.exp(t - w), cum, logw)
    g_inv = _each(lambda t: jnp.exp(-t), cum)
    g_hat = _each(lambda tl, t: jnp.exp(tl - t), cum_last, cum)
    g_last = _each(jnp.exp, cum_last)
    yield

    def by_head_rows(x):
        lo = lane_lo if x.shape[1] == PAIR else jnp.concatenate([lane_lo] * (x.shape[1] // PAIR), axis=1)
        return jnp.concatenate([jnp.where(lo, x, 0.0), jnp.where(lo, 0.0, x)], axis=0).astype(BF16)

    al = _each(lambda t, g: -t * g, kkn, g_ex)
    rb = _each(jnp.multiply, r, g_in)
    bt = _each(jnp.multiply, bb, g_inv)
    kt = _each(jnp.multiply, k, g_inv)
    bh = _each(jnp.multiply, bb, g_hat)
    kh = _each(jnp.multiply, k, g_hat)
    lhs = _each(lambda a, b: jnp.concatenate([a, b], axis=0).astype(BF16), al, rb)
    rhs = _each(lambda a, b: jnp.concatenate([by_head_rows(a), by_head_rows(b)], axis=0), bt, kt)
    yield
    aq = _each(_dot_nt, lhs, rhs)
    a_ab = [jnp.where(strict_c, t[:L, :2 * L], 0.0) for t in aq]
    a_ak = [jnp.where(strict_c, t[:L, 2 * L:], 0.0).astype(BF16) for t in aq]
    a_rb = [jnp.where(incl_c, t[L:, :2 * L], 0.0).astype(BF16) for t in aq]
    a_rk = [jnp.where(incl_c, t[L:, 2 * L:], 0.0).astype(BF16) for t in aq]
    yield

    v_s = _each(by_head_rows, v)
    av = _each(_dot, a_ak, v_s)
    x = _each(lambda a, b: jnp.concatenate([a, b], axis=1), al, av)
    yield
    ap = a_ab
    n_lvl = int(math.log2(L))
    for lvl in range(n_lvl):
        apb = _each(bf, ap)
        x = _each(lambda t, a: t + _dot(a, by_head_rows(t)), x, apb)
        if lvl + 1 < n_lvl:
            ap = _each(lambda a, ab: _dot(ab, jnp.where(same_blk, jnp.concatenate([a, a], axis=0), 0.0).astype(BF16)),
                       ap, apb)
        yield

    z = _each(lambda a, t: _dot(a, by_head_rows(t)), a_rb, x)
    akv = _each(_dot, a_rk, v_s)
    w2 = _each(lambda a, t: (a + t[:, :PAIR]).astype(BF16), rb, z)
    y_loc = _each(lambda t, a: t[:, PAIR:] + a, z, akv)
    w1 = [t[:, :PAIR].astype(BF16) for t in x]
    u_loc = [t[:, PAIR:] for t in x]
    yield

    m_t = _each(lambda a, b: jnp.where(same_blk, _dot_tn(a, b.astype(BF16)), 0.0).astype(BF16), w1, bh)
    s_loc = _each(
        lambda u, vv, b, kk_: jnp.where(
            same_blk,
            _dot_tn(jnp.concatenate([u, vv], axis=0).astype(BF16), jnp.concatenate([b, kk_], axis=0).astype(BF16)),
            0.0),
        u_loc, v, bh, kh)
    return w2, y_loc, m_t, s_loc, g_last


def _rwkv_kernel(p_ref, mix_ref, wc_ref, w0_ref, a0_ref, kk_ref, ka_ref, rk_ref, lnw_ref, lnb_ref,
                 o_ref, s_scr, carry_scr, *, nb, width, n_lora, n_chunks, n_groups, stage_offset):
    L = CHUNK
    R = n_chunks * L
    c = pl.program_id(0)

    @pl.when(c == 0)
    def _():
        s_scr[...] = jnp.zeros_like(s_scr)
        carry_scr[...] = jnp.zeros_like(carry_scr)

    n_pairs = width // PAIR
    row = lax.broadcasted_iota(jnp.int32, (R, 1), 0)
    ri = lax.broadcasted_iota(jnp.int32, (2 * L, 2 * L), 0)
    ci = lax.broadcasted_iota(jnp.int32, (2 * L, 2 * L), 1)
    same_blk = (ri >= L) == (ci >= L)
    ti = lax.broadcasted_iota(jnp.int32, (L, 2 * L), 0)
    si = lax.broadcasted_iota(jnp.int32, (L, 2 * L), 1) % L
    strict_c = si < ti
    incl_c = si <= ti
    rr = lax.broadcasted_iota(jnp.int32, (R, R), 0)
    rc = lax.broadcasted_iota(jnp.int32, (R, R), 1)
    tri = ((rr // L == rc // L) & (rr >= rc)).astype(BF16)
    lane_lo = lax.broadcasted_iota(jnp.int32, (L, PAIR), 1) < HEAD_DIM
    lane_lo_r = lax.broadcasted_iota(jnp.int32, (R, PAIR), 1) < HEAD_DIM
    lora_lane = lax.broadcasted_iota(jnp.int32, (R, 2 * LANES), 1)

    def head_sums(x):
        s0 = jnp.sum(jnp.where(lane_lo_r, x, 0.0), axis=-1, keepdims=True)
        s1 = jnp.sum(jnp.where(lane_lo_r, 0.0, x), axis=-1, keepdims=True)
        return jnp.where(lane_lo_r, s0, s1)

    mix = mix_ref[...]

    def lora_up(pm):
        lora = pm[:, 3 * width:]
        act = jnp.where(lora_lane < n_lora[0], jnp.tanh(lora),
                        jnp.where(lora_lane < n_lora[1], lora,
                                  jnp.where(lora_lane < n_lora[2], _sigmoid(lora), 0.0)))
        return _dot(act.astype(BF16), wc_ref[...])

    def log_decay(up):
        t = w0_ref[...] + up[:, :width]
        return -jnp.exp(-(jnp.maximum(-t, 0.0) + jnp.log(1.0 + jnp.exp(-jnp.abs(t)))) - 0.5)

    def running_sum(t):
        h1 = t.astype(BF16)
        r1 = t - h1.astype(F32)
        h2 = r1.astype(BF16)
        h3 = (r1 - h2.astype(F32)).astype(BF16)
        return _dot(tri, h1) + _dot(tri, h2) + _dot(tri, h3)

    tiles = [(b, pr) for b in range(nb) for pr in range(n_pairs)]
    n_t = len(tiles)
    lanes = lambda pr, part=0: slice(part * width + pr * PAIR, part * width + (pr + 1) * PAIR)
    sls = [lanes(pr) for _, pr in tiles]

    states = {0: [s_scr[i] for i in range(n_t)]}
    outs = {}

    def run_group(gi):
        g0 = gi * R
        pms = []
        for b in range(nb):
            p = p_ref[b, g0:g0 + R, :]
            prev_last = carry_scr[b, 7:8, :] if gi == 0 else p_ref[b, g0 - 1:g0, :]
            p_prev = jnp.where(row == 0, prev_last, pltpu.roll(p, 1, 0))
            pms.append(p + (p_prev - p) * mix)
        yield
        ups = _each(lora_up, pms)
        logw_b = _each(log_decay, ups)
        cum_b = _each(running_sum, logw_b)
        yield

        r = [pms[b][:, lanes(pr, 0)] for b, pr in tiles]
        k_raw = [pms[b][:, lanes(pr, 1)] for b, pr in tiles]
        v = [pms[b][:, lanes(pr, 2)] for b, pr in tiles]
        logw = [logw_b[b][:, lanes(pr)] for b, pr in tiles]
        cum = [cum_b[b][:, lanes(pr)] for b, pr in tiles]
        a = [_sigmoid(a0_ref[:, lanes(pr)] + ups[b][:, lanes(pr, 1)]) for b, pr in tiles]
        g = [ups[b][:, lanes(pr, 2)] for b, pr in tiles]

        kk = _each(lambda t, sl: t * kk_ref[:, sl], k_raw, sls)
        kk_ss = _each(lambda t: head_sums(t * t), kk)
        kkn = _each(lambda t, ss: t / jnp.maximum(jnp.sqrt(ss), 1e-12), kk, kk_ss)
        k = _each(lambda t, aa, sl: t * (1.0 + (aa - 1.0) * ka_ref[:, sl]), k_raw, a, sls)
        bb = _each(jnp.multiply, kkn, a)
        bonus_dot = _each(lambda rr_, kk_, sl: head_sums(rr_ * kk_ * rk_ref[:, sl]), r, k, sls)
        yield

        def chunks(ts):
            return [t[ch * L:(ch + 1) * L] for ch in range(n_chunks) for t in ts]

        w2, y_loc, m_t, s_loc, g_last = yield from _rwkv_chunk_terms(
            chunks(r), chunks(k), chunks(v), chunks(logw), chunks(cum), chunks(kkn), chunks(bb),
            same_blk, strict_c, incl_c, lane_lo)
        yield

        s = states[gi]
        y_parts = []
        for ch in range(n_chunks):
            sel = slice(ch * n_t, (ch + 1) * n_t)
            s_b = _each(lambda t: t.astype(BF16), s)
            y_parts.append(_each(lambda a_, sb, yl: _dot_nt(a_, sb) + yl, w2[sel], s_b, y_loc[sel]))
            s = _each(lambda s0, gl, sb, m, sl: s0 * gl + _dot(sb, m) + sl, s, g_last[sel], s_b, m_t[sel], s_loc[sel])
        states[gi + 1] = s
        y = [jnp.concatenate([y_parts[ch][i] for ch in range(n_chunks)], axis=0) for i in range(n_t)]
        yield

        mu = _each(lambda t: head_sums(t) * (1.0 / HEAD_DIM), y)
        yc = _each(jnp.subtract, y, mu)
        var = _each(lambda t: head_sums(t * t) * (1.0 / HEAD_DIM), yc)
        yield
        res = []
        for i in range(n_t):
            sl = sls[i]
            yn = yc[i] * lax.rsqrt(var[i] + GN_EPS) * lnw_ref[:, sl] + lnb_ref[:, sl]
            res.append(((yn + bonus_dot[i] * v[i]) * g[i]).astype(o_ref.dtype))
        outs[gi] = res

    gens = [run_group(gi) for gi in range(n_groups)]
    live = [True] * n_groups
    tick = 0
    while any(live):
        for gi in range(n_groups):
            if live[gi] and tick >= gi * stage_offset:
                try:
                    next(gens[gi])
                except StopIteration:
                    live[gi] = False
        tick += 1

    for gi in range(n_groups):
        for i, (b, pr) in enumerate(tiles):
            o_ref[b, gi * R:(gi + 1) * R, sls[i]] = outs[gi][i]
    for i in range(n_t):
        s_scr[i] = states[n_groups][i]
    for b in range(nb):
        carry_scr[b] = p_ref[b, n_groups * R - 8:n_groups * R, :]


def _rwkv(p_r, mix, wc, w0, a0, k_k, k_a, r_k, ln_w, ln_b, width, n_lora, n_chunks=2, n_groups=2, stage_offset=5):
    B, S, C = p_r.shape
    L = CHUNK * n_chunks * n_groups
    n_state = B * (width // PAIR)
    vec = lambda n: pl.BlockSpec((1, n), lambda c: (0, 0))
    return pl.pallas_call(
        functools.partial(_rwkv_kernel, nb=B, width=width, n_lora=n_lora, n_chunks=n_chunks, n_groups=n_groups,
                          stage_offset=stage_offset),
        out_shape=jax.ShapeDtypeStruct((B, S, width), BF16),
        grid=(S // L,),
        in_specs=[pl.BlockSpec((B, L, C), lambda c: (0, c, 0)), vec(C),
                  pl.BlockSpec(wc.shape, lambda c: (0, 0))] + [vec(width)] * 7,
        out_specs=pl.BlockSpec((B, L, width), lambda c: (0, c, 0)),
        scratch_shapes=[pltpu.VMEM((n_state, PAIR, PAIR), F32), pltpu.VMEM((B, 8, C), F32)],
        compiler_params=pltpu.CompilerParams(dimension_semantics=("arbitrary",), vmem_limit_bytes=VMEM_LIMIT),
        name="rwkv7",
    )(p_r, mix, wc, w0, a0, k_k, k_a, r_k, ln_w, ln_b)


def _attn_kernel(q_ref, k_ref, v_ref, gain_ref, o_ref,
                 q_scr, k_ring, v_ring, bias_scr, *stat_scr, sb_rows, tiles_per_iter):
    sb = pl.program_id(2)
    n_pat = len(DILATED_PATTERNS)
    num_scr, m_scr, l_scr = stat_scr[:n_pat], stat_scr[n_pat:2 * n_pat], stat_scr[2 * n_pat:]
    Q = ATTN_BLOCK
    ring_rows = 2 * sb_rows
    cur_base = (sb % 2) * sb_rows

    @pl.when(sb == 0)
    def _():
        k_ring[pl.ds(sb_rows, sb_rows), :] = jnp.zeros((sb_rows, PAIR), F32)
        v_ring[pl.ds(sb_rows, sb_rows), :] = jnp.zeros((sb_rows, PAIR), F32)

    q_scr[...] = q_ref[...].astype(F32)
    k_ring[pl.ds(pl.multiple_of(cur_base, sb_rows), sb_rows), :] = k_ref[...].astype(F32)
    v_ring[pl.ds(pl.multiple_of(cur_base, sb_rows), sb_rows), :] = v_ref[...].astype(F32)

    @pl.when(sb == 0)
    def _():
        ii = lax.broadcasted_iota(jnp.int32, (2 * Q, 2 * Q), 0) % Q
        cj = lax.broadcasted_iota(jnp.int32, (2 * Q, 2 * Q), 1)
        band = (cj >= ii) & (cj <= ii + Q)
        bias_scr[1] = jnp.where(band, 0.0, NEG_BIG)
        bias_scr[0] = jnp.where(band & (cj >= Q), 0.0, NEG_BIG)

    lane_lo = lax.broadcasted_iota(jnp.int32, (Q, PAIR), 1) < HEAD_DIM
    ones_blk = jnp.ones((2 * Q, PAIR), BF16)

    for pi, (window, dil) in enumerate(DILATED_PATTERNS):
        assert window // dil == Q
        span = Q * dil
        n_blk = sb_rows // Q
        n_tiles = tiles_per_iter[pi]
        assert n_blk % n_tiles == 0

        def tiles(it, carry, dil=dil, span=span, pi=pi, n_tiles=n_tiles):
            rows = lambda s: pl.ds(s, Q, stride=dil) if dil > 1 else pl.ds(s, Q)
            gs = [it * n_tiles + t for t in range(n_tiles)]
            starts = [(g // dil) * span + (g % dil) for g in gs]
            k_cur = [cur_base + s for s in starts]
            k_prev = [(s - span + ring_rows) % ring_rows for s in k_cur]
            has_prev = [jnp.where((sb > 0) | (s >= span), 1, 0) for s in starts]
            q2 = [q_scr[rows(s), :] for s in starts]
            q2 = [jnp.concatenate([jnp.where(lane_lo, t, 0.0), jnp.where(lane_lo, 0.0, t)], axis=0).astype(BF16)
                  for t in q2]
            kcat = _each(lambda a, b: jnp.concatenate([k_ring[rows(a), :], k_ring[rows(b), :]], axis=0).astype(BF16),
                         k_prev, k_cur)
            vcat = _each(lambda a, b: jnp.concatenate([v_ring[rows(a), :], v_ring[rows(b), :]], axis=0).astype(BF16),
                         k_prev, k_cur)
            vext = [jnp.concatenate([t, ones_blk], axis=1) for t in vcat]
            s = _each(_dot_nt, q2, kcat)
            s = _each(lambda t, hp: t + bias_scr[hp], s, has_prev)
            m = [jnp.max(t, axis=-1, keepdims=True) for t in s]
            p = _each(lambda t, mm: jnp.exp(t - mm).astype(BF16), s, m)
            nl = _each(_dot, p, vext)
            for t in range(n_tiles):
                dst = rows(starts[t])
                num_scr[pi][dst, :] = jnp.where(lane_lo, nl[t][:Q, :PAIR], nl[t][Q:, :PAIR])
                l_scr[pi][dst, :] = jnp.where(lane_lo, nl[t][:Q, PAIR:], nl[t][Q:, PAIR:])
                m_scr[pi][dst, :] = jnp.where(lane_lo, m[t][:Q], m[t][Q:])
            return carry

        lax.fori_loop(0, n_blk // n_tiles, tiles, 0)

    ri = lax.broadcasted_iota(jnp.int32, (PAIR, PAIR), 0)
    ci = lax.broadcasted_iota(jnp.int32, (PAIR, PAIR), 1)
    seg_ones = ((ri >= HEAD_DIM) == (ci >= HEAD_DIM)).astype(BF16)
    gain = gain_ref[...]

    def merge(i, carry):
        rows = pl.ds(pl.multiple_of(i * Q, Q), Q)
        ms = [m_scr[pi][rows, :] for pi in range(n_pat)]
        m_all = functools.reduce(jnp.maximum, ms)
        num = 0.0
        den = 0.0
        for pi in range(n_pat):
            wgt = jnp.exp(ms[pi] - m_all)
            num = num + wgt * num_scr[pi][rows, :]
            den = den + wgt * l_scr[pi][rows, :]
        o = num / den
        ms_o = _split_dot(o * o, seg_ones) * (1.0 / HEAD_DIM)
        o_ref[rows, :] = (o * lax.rsqrt(ms_o + NORM_EPS) * gain).astype(o_ref.dtype)
        return carry

    lax.fori_loop(0, sb_rows // Q, merge, 0, unroll=4)


def _attention(q, k, v, gain, sb_rows=2048, tiles_per_iter=(8, 8, 4)):
    B, S, W = q.shape
    n_pairs = W // PAIR
    n_pat = len(DILATED_PATTERNS)
    blk = pl.BlockSpec((None, sb_rows, PAIR), lambda b, p, s: (b, s, p))
    return pl.pallas_call(
        functools.partial(_attn_kernel, sb_rows=sb_rows, tiles_per_iter=tiles_per_iter),
        out_shape=jax.ShapeDtypeStruct((B, S, W), BF16),
        grid=(B, n_pairs, S // sb_rows),
        in_specs=[blk, blk, blk, pl.BlockSpec((1, PAIR), lambda b, p, s: (0, p))],
        out_specs=blk,
        scratch_shapes=[pltpu.VMEM((sb_rows, PAIR), F32),
                        pltpu.VMEM((2 * sb_rows, PAIR), F32),
                        pltpu.VMEM((2 * sb_rows, PAIR), F32),
                        pltpu.VMEM((2, 2 * ATTN_BLOCK, 2 * ATTN_BLOCK), F32)]
                       + [pltpu.VMEM((sb_rows, PAIR), F32)] * (3 * n_pat),
        compiler_params=pltpu.CompilerParams(dimension_semantics=("parallel", "parallel", "arbitrary"),
                                             vmem_limit_bytes=VMEM_LIMIT),
        name="dilated_attn",
    )(q, k, v, gain)


def _rotary_tables(seq):
    half = ROT_DIM // 2
    inv_freq = ROPE_THETA ** (-jnp.arange(half, dtype=F32) * 2.0 / ROT_DIM)
    ang = jnp.arange(seq).astype(F32)[:, None] * inv_freq[None, :]
    cos, sin = jnp.cos(ang), jnp.sin(ang)
    rest = HEAD_DIM - ROT_DIM
    cos_h = jnp.concatenate([cos, cos, jnp.ones((seq, rest), F32)], axis=-1)
    sin_h = jnp.concatenate([-sin, sin, jnp.zeros((seq, rest), F32)], axis=-1)
    return jnp.tile(cos_h, (1, PAIR // HEAD_DIM)), jnp.tile(sin_h, (1, PAIR // HEAD_DIM))


def _outproj_kernel(x_ref, ya_ref, yb_ref, wa_ref, wb_ref, o_ref):
    o_ref[...] = x_ref[...] + _dot(ya_ref[...], wa_ref[...]) + _dot(yb_ref[...], wb_ref[...])


def _outproj(x2d, y_a, y_b, w_a, w_b, tm=1024):
    T, D = x2d.shape
    row = lambda i: (i, 0)
    fixed = lambda i: (0, 0)
    return pl.pallas_call(
        _outproj_kernel,
        out_shape=jax.ShapeDtypeStruct((T, D), F32),
        grid=(T // tm,),
        in_specs=[pl.BlockSpec((tm, D), row), pl.BlockSpec((tm, y_a.shape[1]), row),
                  pl.BlockSpec((tm, y_b.shape[1]), row), pl.BlockSpec(w_a.shape, fixed), pl.BlockSpec(w_b.shape, fixed)],
        out_specs=pl.BlockSpec((tm, D), row),
        compiler_params=pltpu.CompilerParams(dimension_semantics=("parallel",), vmem_limit_bytes=VMEM_LIMIT),
        name="outproj",
    )(x2d, y_a, y_b, w_a, w_b)


FFN_HALO = 16


def _ffn_kernel(x_ref, xh_ref, g_ref, wg_ref, wv_ref, cw_ref, cb_ref, wd_ref, fg_ref, o_ref,
                h_scr, hh_scr, acc_scr, *, tm, seq, apply_final):
    i = pl.program_id(0)
    j = pl.program_id(1)

    @pl.when(j == 0)
    def _():
        h_scr[...] = _rmsnorm(x_ref[...], g_ref[...]).astype(BF16)
        hh_scr[...] = _rmsnorm(xh_ref[...], g_ref[...]).astype(BF16)
        acc_scr[...] = jnp.zeros_like(acc_scr)

    gate = _dot(h_scr[...], wg_ref[...])
    val = _dot(h_scr[...], wv_ref[...])
    seq_start = (i * tm) % seq == 0
    gate_h = jnp.where(seq_start, 0.0, _dot(hh_scr[...], wg_ref[...]))
    row = lax.broadcasted_iota(jnp.int32, (tm, 1), 0)
    g1 = jnp.where(row == 0, gate_h[FFN_HALO - 1:FFN_HALO, :], pltpu.roll(gate, 1, 0))
    g2 = jnp.where(row == 0, gate_h[FFN_HALO - 2:FFN_HALO - 1, :],
                   jnp.where(row == 1, gate_h[FFN_HALO - 1:FFN_HALO, :], pltpu.roll(gate, 2, 0)))
    u = cw_ref[0:1, :] * g2 + cw_ref[1:2, :] * g1 + cw_ref[2:3, :] * gate + cb_ref[...]
    act = (u * _sigmoid(u) * val).astype(BF16)
    acc_scr[...] += _dot(act, wd_ref[...])

    @pl.when(j == pl.num_programs(1) - 1)
    def _():
        y = x_ref[...] + acc_scr[...]
        if apply_final:
            y = _rmsnorm(y, fg_ref[...])
        o_ref[...] = y


def _ffn(x2d, gain, w_gate, w_val, conv_w, conv_b, w_down, final_gain, seq, apply_final, tm=1024, n_ff_tiles=2):
    T, D = x2d.shape
    F = w_gate.shape[1]
    tf = F // n_ff_tiles
    assert tf % LANES == 0 and seq % tm == 0
    halo_blocks = tm // FFN_HALO
    return pl.pallas_call(
        functools.partial(_ffn_kernel, tm=tm, seq=seq, apply_final=apply_final),
        out_shape=jax.ShapeDtypeStruct((T, D), F32),
        grid=(T // tm, n_ff_tiles),
        in_specs=[pl.BlockSpec((tm, D), lambda i, j: (i, 0)),
                  pl.BlockSpec((FFN_HALO, D), lambda i, j: (jnp.maximum(i * halo_blocks - 1, 0), 0)),
                  pl.BlockSpec((1, D), lambda i, j: (0, 0)),
                  pl.BlockSpec((D, tf), lambda i, j: (0, j)),
                  pl.BlockSpec((D, tf), lambda i, j: (0, j)),
                  pl.BlockSpec((CONV_WIDTH, tf), lambda i, j: (0, j)),
                  pl.BlockSpec((1, tf), lambda i, j: (0, j)),
                  pl.BlockSpec((tf, D), lambda i, j: (j, 0)),
                  pl.BlockSpec((1, D), lambda i, j: (0, 0))],
        out_specs=pl.BlockSpec((tm, D), lambda i, j: (i, 0)),
        scratch_shapes=[pltpu.VMEM((tm, D), BF16), pltpu.VMEM((FFN_HALO, D), BF16), pltpu.VMEM((tm, D), F32)],
        compiler_params=pltpu.CompilerParams(dimension_semantics=("parallel", "arbitrary"),
                                             vmem_limit_bytes=VMEM_LIMIT),
        name="convglu_ffn",
    )(x2d, x2d, gain, w_gate, w_val, conv_w, conv_b, w_down, final_gain)


def kernel(x, mix_norm_gain, w_in, rwkv_shift_mix, w0, w_lora_up, a0, a_lora_up, g_lora_up, k_k, k_a, r_k,
           ln_x_w, ln_x_b, attn_norm_gain, w_out, ffn_norm_gain, w_ffn_up, ffn_conv_w, ffn_conv_b,
           w_ffn_down, final_norm_gain):
    B, S, D = x.shape
    depth = w_in.shape[0]
    rw = w0.shape[1]
    aw = attn_norm_gain.shape[1]
    n_w, n_a, n_g = w_lora_up.shape[1], a_lora_up.shape[1], g_lora_up.shape[1]
    n_lora = n_w + n_a + n_g
    lora_pad = -(-n_lora // (2 * LANES)) * (2 * LANES)
    assert lora_pad == 2 * LANES and rw % PAIR == 0 and aw % PAIR == 0
    rwkv_cols = 3 * rw + n_lora
    d_ff = w_ffn_down.shape[1]
    cos_t, sin_t = _rotary_tables(S)

    x2d = x.reshape(B * S, D)
    for l in range(depth):
        pad = jnp.zeros((D, lora_pad - n_lora), F32)
        w_all = jnp.concatenate([w_in[l][:, :rwkv_cols], pad, w_in[l][:, rwkv_cols:]], axis=1).astype(BF16)
        mix = jnp.concatenate([rwkv_shift_mix[l], jnp.zeros((lora_pad - n_lora,), F32)])[None, :]
        wc = jnp.zeros((lora_pad, 3 * rw), F32)
        wc = wc.at[:n_w, :rw].set(w_lora_up[l])
        wc = wc.at[n_w:n_w + n_a, rw:2 * rw].set(a_lora_up[l])
        wc = wc.at[n_w + n_a:n_lora, 2 * rw:].set(g_lora_up[l]).astype(BF16)

        p_r, q, k, v = _inproj(x2d, mix_norm_gain[l][None, :], w_all, cos_t, sin_t, 3 * rw + lora_pad, aw)
        y_rwkv = _rwkv(p_r.reshape(B, S, -1), mix, wc, w0[l][None, :], a0[l][None, :], k_k[l][None, :],
                       k_a[l][None, :], r_k[l].reshape(1, rw), ln_x_w[l][None, :], ln_x_b[l][None, :],
                       rw, (n_w, n_w + n_a, n_lora))
        y_attn = _attention(q.reshape(B, S, aw), k.reshape(B, S, aw), v.reshape(B, S, aw),
                            attn_norm_gain[l][None, :])
        w_o = w_out[l].astype(BF16)
        x2d = _outproj(x2d, y_rwkv.reshape(B * S, rw), y_attn.reshape(B * S, aw), w_o[:rw], w_o[rw:])
        w_up = w_ffn_up[l].astype(BF16)
        x2d = _ffn(x2d, ffn_norm_gain[l][None, :], w_up[:, :d_ff], w_up[:, d_ff:], ffn_conv_w[l],
                   ffn_conv_b[l][None, :], w_ffn_down[l].astype(BF16), final_norm_gain[None, :], S,
                   apply_final=(l == depth - 1))
    return x2d.reshape(B, S, D)
```

```python
import functools
import math

import jax
import jax.numpy as jnp
from jax import lax
from jax.experimental import pallas as pl
from jax.experimental.pallas import tpu as pltpu

F32 = jnp.float32
BF16 = jnp.bfloat16

LANES = 128
HEAD_DIM = 64
PAIR = 2 * HEAD_DIM
ROT_DIM = HEAD_DIM // 4
ROPE_THETA = 500000.0
NORM_EPS = 1e-6
GN_EPS = 64e-5
DILATED_PATTERNS = ((128, 1), (512, 4), (2048, 16))
ATTN_BLOCK = 128
CONV_WIDTH = 3
CHUNK = 64
NEG_BIG = -1e30
VMEM_LIMIT = 56 * 1024 * 1024


def _dot(a, b):
    return jnp.dot(a, b, preferred_element_type=F32)


def _dot_nt(a, b):
    return lax.dot_general(a, b, (((1,), (1,)), ((), ())), preferred_element_type=F32)


def _dot_tn(a, b):
    return lax.dot_general(a, b, (((0,), (0,)), ((), ())), preferred_element_type=F32)


def _rmsnorm(x, gain):
    return x * lax.rsqrt(jnp.mean(x * x, axis=-1, keepdims=True) + NORM_EPS) * gain


def _sigmoid(x):
    return 1.0 / (1.0 + jnp.exp(-x))


def _split_dot(x, w):
    hi = x.astype(BF16)
    lo = (x - hi.astype(F32)).astype(BF16)
    return _dot(hi, w) + _dot(lo, w)


def _inproj_kernel(x_ref, g_ref, wr_ref, wa_ref, cos_ref, sin_ref, pr_ref, q_ref, k_ref, v_ref):
    h = _rmsnorm(x_ref[...], g_ref[...]).astype(BF16)
    pr_ref[...] = _dot(h, wr_ref[...])
    p = _dot(h, wa_ref[...])
    aw = q_ref.shape[1]

    tm = x_ref.shape[0]
    lane = lax.broadcasted_iota(jnp.int32, (tm, PAIR), 1)
    first_half = (lane % HEAD_DIM) < (ROT_DIM // 2)
    cos, sin = cos_ref[...], sin_ref[...]

    def rotary(x):
        partner = jnp.where(first_half, pltpu.roll(x, PAIR - ROT_DIM // 2, 1), pltpu.roll(x, ROT_DIM // 2, 1))
        return x * cos + partner * sin

    for t in range(aw // PAIR):
        lo = t * PAIR
        q_ref[:, lo:lo + PAIR] = (rotary(p[:, lo:lo + PAIR]) * (1.0 / math.sqrt(HEAD_DIM))).astype(BF16)
        k_ref[:, lo:lo + PAIR] = rotary(p[:, aw + lo:aw + lo + PAIR]).astype(BF16)
    v_ref[...] = p[:, 2 * aw:].astype(BF16)


def _inproj(x2d, gain, w_r, w_a, cos_t, sin_t, tm=512):
    T, D = x2d.shape
    rw, aw = w_r.shape[1], w_a.shape[1] // 3
    seq_tiles = cos_t.shape[0] // tm
    row = lambda i: (i, 0)
    fixed = lambda i: (0, 0)
    tab = pl.BlockSpec((tm, PAIR), lambda i: (i % seq_tiles, 0))
    return pl.pallas_call(
        _inproj_kernel,
        out_shape=(jax.ShapeDtypeStruct((T, rw), F32),) + (jax.ShapeDtypeStruct((T, aw), BF16),) * 3,
        grid=(T // tm,),
        in_specs=[pl.BlockSpec((tm, D), row), pl.BlockSpec((1, D), fixed), pl.BlockSpec(w_r.shape, fixed),
                  pl.BlockSpec(w_a.shape, fixed), tab, tab],
        out_specs=(pl.BlockSpec((tm, rw), row),) + (pl.BlockSpec((tm, aw), row),) * 3,
        compiler_params=pltpu.CompilerParams(dimension_semantics=("parallel",), vmem_limit_bytes=VMEM_LIMIT),
        name="inproj",
    )(x2d, gain, w_r, w_a, cos_t, sin_t)


def _each(fn, *lists):
    return [fn(*args) for args in zip(*lists)]


def _rwkv_chunk_terms(r, k, v, logw, cum, kkn, bb, same_blk, strict_c, incl_c, lane_lo):
    L = CHUNK
    bf = lambda t: t.astype(BF16)
    cum_last = [t[L - 1:L, :] for t in cum]
    g_in = _each(jnp.exp, cum)
    g_ex = _each(lambda t, w: jnp.exp(t - w), cum, logw)
    g_inv = _each(lambda t: jnp.exp(-t), cum)
    g_hat = _each(lambda tl, t: jnp.exp(tl - t), cum_last, cum)
    g_last = _each(jnp.exp, cum_last)
    yield

    def by_head_rows(x):
        lo = lane_lo if x.shape[1] == PAIR else jnp.concatenate([lane_lo] * (x.shape[1] // PAIR), axis=1)
        return jnp.concatenate([jnp.where(lo, x, 0.0), jnp.where(lo, 0.0, x)], axis=0).astype(BF16)

    al = _each(lambda t, g: -t * g, kkn, g_ex)
    rb = _each(jnp.multiply, r, g_in)
    bt = _each(jnp.multiply, bb, g_inv)
    kt = _each(jnp.multiply, k, g_inv)
    bh = _each(jnp.multiply, bb, g_hat)
    kh = _each(jnp.multiply, k, g_hat)
    lhs = _each(lambda a, b: jnp.concatenate([a, b], axis=0).astype(BF16), al, rb)
    rhs = _each(lambda a, b: jnp.concatenate([by_head_rows(a), by_head_rows(b)], axis=0), bt, kt)
    yield
    aq = _each(_dot_nt, lhs, rhs)
    a_ab = [jnp.where(strict_c, t[:L, :2 * L], 0.0) for t in aq]
    a_ak = [jnp.where(strict_c, t[:L, 2 * L:], 0.0).astype(BF16) for t in aq]
    a_rb = [jnp.where(incl_c, t[L:, :2 * L], 0.0).astype(BF16) for t in aq]
    a_rk = [jnp.where(incl_c, t[L:, 2 * L:], 0.0).astype(BF16) for t in aq]
    yield

    v_s = _each(by_head_rows, v)
    av = _each(_dot, a_ak, v_s)
    x = _each(lambda a, b: jnp.concatenate([a, b], axis=1), al, av)
    yield
    ap = a_ab
    n_lvl = int(math.log2(L))
    for lvl in range(n_lvl):
        apb = _each(bf, ap)
        x = _each(lambda t, a: t + _dot(a, by_head_rows(t)), x, apb)
        if lvl + 1 < n_lvl:
            ap = _each(lambda a, ab: _dot(ab, jnp.where(same_blk, jnp.concatenate([a, a], axis=0), 0.0).astype(BF16)),
                       ap, apb)
        yield

    z = _each(lambda a, t: _dot(a, by_head_rows(t)), a_rb, x)
    akv = _each(_dot, a_rk, v_s)
    w2 = _each(lambda a, t: (a + t[:, :PAIR]).astype(BF16), rb, z)
    y_loc = _each(lambda t, a: t[:, PAIR:] + a, z, akv)
    w1 = [t[:, :PAIR].astype(BF16) for t in x]
    u_loc = [t[:, PAIR:] for t in x]
    yield

    m_t = _each(lambda a, b: jnp.where(same_blk, _dot_tn(a, b.astype(BF16)), 0.0).astype(BF16), w1, bh)
    s_loc = _each(
        lambda u, vv, b, kk_: jnp.where(
            same_blk,
            _dot_tn(jnp.concatenate([u, vv], axis=0).astype(BF16), jnp.concatenate([b, kk_], axis=0).astype(BF16)),
            0.0),
        u_loc, v, bh, kh)
    return w2, y_loc, m_t, s_loc, g_last


def _rwkv_kernel(p_ref, mix_ref, wc_ref, w0_ref, a0_ref, kk_ref, ka_ref, rk_ref, lnw_ref, lnb_ref,
                 o_ref, s_scr, carry_scr, *, nb, width, n_lora, n_chunks, n_groups, stage_offset):
    L = CHUNK
    R = n_chunks * L
    c = pl.program_id(0)

    @pl.when(c == 0)
    def _():
        s_scr[...] = jnp.zeros_like(s_scr)
        carry_scr[...] = jnp.zeros_like(carry_scr)

    n_pairs = width // PAIR
    row = lax.broadcasted_iota(jnp.int32, (R, 1), 0)
    ri = lax.broadcasted_iota(jnp.int32, (2 * L, 2 * L), 0)
    ci = lax.broadcasted_iota(jnp.int32, (2 * L, 2 * L), 1)
    same_blk = (ri >= L) == (ci >= L)
    ti = lax.broadcasted_iota(jnp.int32, (L, 2 * L), 0)
    si = lax.broadcasted_iota(jnp.int32, (L, 2 * L), 1) % L
    strict_c = si < ti
    incl_c = si <= ti
    rr = lax.broadcasted_iota(jnp.int32, (R, R), 0)
    rc = lax.broadcasted_iota(jnp.int32, (R, R), 1)
    tri = ((rr // L == rc // L) & (rr >= rc)).astype(BF16)
    lane_lo = lax.broadcasted_iota(jnp.int32, (L, PAIR), 1) < HEAD_DIM
    lane_lo_r = lax.broadcasted_iota(jnp.int32, (R, PAIR), 1) < HEAD_DIM
    lora_lane = lax.broadcasted_iota(jnp.int32, (R, 2 * LANES), 1)

    def head_sums(x):
        s0 = jnp.sum(jnp.where(lane_lo_r, x, 0.0), axis=-1, keepdims=True)
        s1 = jnp.sum(jnp.where(lane_lo_r, 0.0, x), axis=-1, keepdims=True)
        return jnp.where(lane_lo_r, s0, s1)

    mix = mix_ref[...]

    def lora_up(pm):
        lora = pm[:, 3 * width:]
        act = jnp.where(lora_lane < n_lora[0], jnp.tanh(lora),
                        jnp.where(lora_lane < n_lora[1], lora,
                                  jnp.where(lora_lane < n_lora[2], _sigmoid(lora), 0.0)))
        return _dot(act.astype(BF16), wc_ref[...])

    def log_decay(up):
        t = w0_ref[...] + up[:, :width]
        return -jnp.exp(-(jnp.maximum(-t, 0.0) + jnp.log(1.0 + jnp.exp(-jnp.abs(t)))) - 0.5)

    def running_sum(t):
        h1 = t.astype(BF16)
        r1 = t - h1.astype(F32)
        h2 = r1.astype(BF16)
        h3 = (r1 - h2.astype(F32)).astype(BF16)
        return _dot(tri, h1) + _dot(tri, h2) + _dot(tri, h3)

    tiles = [(b, pr) for b in range(nb) for pr in range(n_pairs)]
    n_t = len(tiles)
    lanes = lambda pr, part=0: slice(part * width + pr * PAIR, part * width + (pr + 1) * PAIR)
    sls = [lanes(pr) for _, pr in tiles]

    states = {0: [s_scr[i] for i in range(n_t)]}
    outs = {}

    def run_group(gi):
        g0 = gi * R
        pms = []
        for b in range(nb):
            p = p_ref[b, g0:g0 + R, :]
            prev_last = carry_scr[b, 7:8, :] if gi == 0 else p_ref[b, g0 - 1:g0, :]
            p_prev = jnp.where(row == 0, prev_last, pltpu.roll(p, 1, 0))
            pms.append(p + (p_prev - p) * mix)
        yield
        ups = _each(lora_up, pms)
        logw_b = _each(log_decay, ups)
        cum_b = _each(running_sum, logw_b)
        yield

        r = [pms[b][:, lanes(pr, 0)] for b, pr in tiles]
        k_raw = [pms[b][:, lanes(pr, 1)] for b, pr in tiles]
        v = [pms[b][:, lanes(pr, 2)] for b, pr in tiles]
        logw = [logw_b[b][:, lanes(pr)] for b, pr in tiles]
        cum = [cum_b[b][:, lanes(pr)] for b, pr in tiles]
        a = [_sigmoid(a0_ref[:, lanes(pr)] + ups[b][:, lanes(pr, 1)]) for b, pr in tiles]
        g = [ups[b][:, lanes(pr, 2)] for b, pr in tiles]

        kk = _each(lambda t, sl: t * kk_ref[:, sl], k_raw, sls)
        kk_ss = _each(lambda t: head_sums(t * t), kk)
        kkn = _each(lambda t, ss: t / jnp.maximum(jnp.sqrt(ss), 1e-12), kk, kk_ss)
        k = _each(lambda t, aa, sl: t * (1.0 + (aa - 1.0) * ka_ref[:, sl]), k_raw, a, sls)
        bb = _each(jnp.multiply, kkn, a)
        bonus_dot = _each(lambda rr_, kk_, sl: head_sums(rr_ * kk_ * rk_ref[:, sl]), r, k, sls)
        yield

        def chunks(ts):
            return [t[ch * L:(ch + 1) * L] for ch in range(n_chunks) for t in ts]

        w2, y_loc, m_t, s_loc, g_last = yield from _rwkv_chunk_terms(
            chunks(r), chunks(k), chunks(v), chunks(logw), chunks(cum), chunks(kkn), chunks(bb),
            same_blk, strict_c, incl_c, lane_lo)
        yield

        s = states[gi]
        y_parts = []
        for ch in range(n_chunks):
            sel = slice(ch * n_t, (ch + 1) * n_t)
            s_b = _each(lambda t: t.astype(BF16), s)
            y_parts.append(_each(lambda a_, sb, yl: _dot_nt(a_, sb) + yl, w2[sel], s_b, y_loc[sel]))
            s = _each(lambda s0, gl, sb, m, sl: s0 * gl + _dot(sb, m) + sl, s, g_last[sel], s_b, m_t[sel], s_loc[sel])
        states[gi + 1] = s
        y = [jnp.concatenate([y_parts[ch][i] for ch in range(n_chunks)], axis=0) for i in range(n_t)]
        yield

        mu = _each(lambda t: head_sums(t) * (1.0 / HEAD_DIM), y)
        yc = _each(jnp.subtract, y, mu)
        var = _each(lambda t: head_sums(t * t) * (1.0 / HEAD_DIM), yc)
        yield
        res = []
        for i in range(n_t):
            sl = sls[i]
            yn = yc[i] * lax.rsqrt(var[i] + GN_EPS) * lnw_ref[:, sl] + lnb_ref[:, sl]
            res.append(((yn + bonus_dot[i] * v[i]) * g[i]).astype(o_ref.dtype))
        outs[gi] = res

    gens = [run_group(gi) for gi in range(n_groups)]
    live = [True] * n_groups
    tick = 0
    while any(live):
        for gi in range(n_groups):
            if live[gi] and tick >= gi * stage_offset:
                try:
                    next(gens[gi])
                except StopIteration:
                    live[gi] = False
        tick += 1

    for gi in range(n_groups):
        for i, (b, pr) in enumerate(tiles):
            o_ref[b, gi * R:(gi + 1) * R, sls[i]] = outs[gi][i]
    for i in range(n_t):
        s_scr[i] = states[n_groups][i]
    for b in range(nb):
        carry_scr[b] = p_ref[b, n_groups * R - 8:n_groups * R, :]


def _rwkv(p_r, mix, wc, w0, a0, k_k, k_a, r_k, ln_w, ln_b, width, n_lora, n_chunks=2, n_groups=2, stage_offset=5):
    B, S, C = p_r.shape
    L = CHUNK * n_chunks * n_groups
    n_state = B * (width // PAIR)
    vec = lambda n: pl.BlockSpec((1, n), lambda c: (0, 0))
    return pl.pallas_call(
        functools.partial(_rwkv_kernel, nb=B, width=width, n_lora=n_lora, n_chunks=n_chunks, n_groups=n_groups,
                          stage_offset=stage_offset),
        out_shape=jax.ShapeDtypeStruct((B, S, width), BF16),
        grid=(S // L,),
        in_specs=[pl.BlockSpec((B, L, C), lambda c: (0, c, 0)), vec(C),
                  pl.BlockSpec(wc.shape, lambda c: (0, 0))] + [vec(width)] * 7,
        out_specs=pl.BlockSpec((B, L, width), lambda c: (0, c, 0)),
        scratch_shapes=[pltpu.VMEM((n_state, PAIR, PAIR), F32), pltpu.VMEM((B, 8, C), F32)],
        compiler_params=pltpu.CompilerParams(dimension_semantics=("arbitrary",), vmem_limit_bytes=VMEM_LIMIT),
        name="rwkv7",
    )(p_r, mix, wc, w0, a0, k_k, k_a, r_k, ln_w, ln_b)


def _attn_kernel(q_ref, k_ref, v_ref, gain_ref, o_ref,
                 q_scr, k_ring, v_ring, bias_scr, *stat_scr, sb_rows, tiles_per_iter):
    sb = pl.program_id(2)
    n_pat = len(DILATED_PATTERNS)
    num_scr, m_scr, l_scr = stat_scr[:n_pat], stat_scr[n_pat:2 * n_pat], stat_scr[2 * n_pat:]
    Q = ATTN_BLOCK
    ring_rows = 2 * sb_rows
    cur_base = (sb % 2) * sb_rows

    @pl.when(sb == 0)
    def _():
        k_ring[pl.ds(sb_rows, sb_rows), :] = jnp.zeros((sb_rows, PAIR), F32)
        v_ring[pl.ds(sb_rows, sb_rows), :] = jnp.zeros((sb_rows, PAIR), F32)

    q_scr[...] = q_ref[...].astype(F32)
    k_ring[pl.ds(pl.multiple_of(cur_base, sb_rows), sb_rows), :] = k_ref[...].astype(F32)
    v_ring[pl.ds(pl.multiple_of(cur_base, sb_rows), sb_rows), :] = v_ref[...].astype(F32)

    @pl.when(sb == 0)
    def _():
        ii = lax.broadcasted_iota(jnp.int32, (2 * Q, 2 * Q), 0) % Q
        cj = lax.broadcasted_iota(jnp.int32, (2 * Q, 2 * Q), 1)
        band = (cj >= ii) & (cj <= ii + Q)
        bias_scr[1] = jnp.where(band, 0.0, NEG_BIG)
        bias_scr[0] = jnp.where(band & (cj >= Q), 0.0, NEG_BIG)

    lane_lo = lax.broadcasted_iota(jnp.int32, (Q, PAIR), 1) < HEAD_DIM
    ones_blk = jnp.ones((2 * Q, PAIR), BF16)

    for pi, (window, dil) in enumerate(DILATED_PATTERNS):
        assert window // dil == Q
        span = Q * dil
        n_blk = sb_rows // Q
        n_tiles = tiles_per_iter[pi]
        assert n_blk % n_tiles == 0

        def tiles(it, carry, dil=dil, span=span, pi=pi, n_tiles=n_tiles):
            rows = lambda s: pl.ds(s, Q, stride=dil) if dil > 1 else pl.ds(s, Q)
            gs = [it * n_tiles + t for t in range(n_tiles)]
            starts = [(g // dil) * span + (g % dil) for g in gs]
            k_cur = [cur_base + s for s in starts]
            k_prev = [(s - span + ring_rows) % ring_rows for s in k_cur]
            has_prev = [jnp.where((sb > 0) | (s >= span), 1, 0) for s in starts]
            q2 = [q_scr[rows(s), :] for s in starts]
            q2 = [jnp.concatenate([jnp.where(lane_lo, t, 0.0), jnp.where(lane_lo, 0.0, t)], axis=0).astype(BF16)
                  for t in q2]
            kcat = _each(lambda a, b: jnp.concatenate([k_ring[rows(a), :], k_ring[rows(b), :]], axis=0).astype(BF16),
                         k_prev, k_cur)
            vcat = _each(lambda a, b: jnp.concatenate([v_ring[rows(a), :], v_ring[rows(b), :]], axis=0).astype(BF16),
                         k_prev, k_cur)
            vext = [jnp.concatenate([t, ones_blk], axis=1) for t in vcat]
            s = _each(_dot_nt, q2, kcat)
            s = _each(lambda t, hp: t + bias_scr[hp], s, has_prev)
            m = [jnp.max(t, axis=-1, keepdims=True) for t in s]
            p = _each(lambda t, mm: jnp.exp(t - mm).astype(BF16), s, m)
            nl = _each(_dot, p, vext)
            for t in range(n_tiles):
                dst = rows(starts[t])
                num_scr[pi][dst, :] = jnp.where(lane_lo, nl[t][:Q, :PAIR], nl[t][Q:, :PAIR])
                l_scr[pi][dst, :] = jnp.where(lane_lo, nl[t][:Q, PAIR:], nl[t][Q:, PAIR:])
                m_scr[pi][dst, :] = jnp.where(lane_lo, m[t][:Q], m[t][Q:])
            return carry

        lax.fori_loop(0, n_blk // n_tiles, tiles, 0)

    ri = lax.broadcasted_iota(jnp.int32, (PAIR, PAIR), 0)
    ci = lax.broadcasted_iota(jnp.int32, (PAIR, PAIR), 1)
    seg_ones = ((ri >= HEAD_DIM) == (ci >= HEAD_DIM)).astype(BF16)
    gain = gain_ref[...]

    def merge(i, carry):
        rows = pl.ds(pl.multiple_of(i * Q, Q), Q)
        ms = [m_scr[pi][rows, :] for pi in range(n_pat)]
        m_all = functools.reduce(jnp.maximum, ms)
        num = 0.0
        den = 0.0
        for pi in range(n_pat):
            wgt = jnp.exp(ms[pi] - m_all)
            num = num + wgt * num_scr[pi][rows, :]
            den = den + wgt * l_scr[pi][rows, :]
        o = num / den
        ms_o = _split_dot(o * o, seg_ones) * (1.0 / HEAD_DIM)
        o_ref[rows, :] = (o * lax.rsqrt(ms_o + NORM_EPS) * gain).astype(o_ref.dtype)
        return carry

    lax.fori_loop(0, sb_rows // Q, merge, 0, unroll=4)


def _attention(q, k, v, gain, sb_rows=2048, tiles_per_iter=(8, 8, 4)):
    B, S, W = q.shape
    n_pairs = W // PAIR
    n_pat = len(DILATED_PATTERNS)
    blk = pl.BlockSpec((None, sb_rows, PAIR), lambda b, p, s: (b, s, p))
    return pl.pallas_call(
        functools.partial(_attn_kernel, sb_rows=sb_rows, tiles_per_iter=tiles_per_iter),
        out_shape=jax.ShapeDtypeStruct((B, S, W), BF16),
        grid=(B, n_pairs, S // sb_rows),
        in_specs=[blk, blk, blk, pl.BlockSpec((1, PAIR), lambda b, p, s: (0, p))],
        out_specs=blk,
        scratch_shapes=[pltpu.VMEM((sb_rows, PAIR), F32),
                        pltpu.VMEM((2 * sb_rows, PAIR), F32),
                        pltpu.VMEM((2 * sb_rows, PAIR), F32),
                        pltpu.VMEM((2, 2 * ATTN_BLOCK, 2 * ATTN_BLOCK), F32)]
                       + [pltpu.VMEM((sb_rows, PAIR), F32)] * (3 * n_pat),
        compiler_params=pltpu.CompilerParams(dimension_semantics=("parallel", "parallel", "arbitrary"),
                                             vmem_limit_bytes=VMEM_LIMIT),
        name="dilated_attn",
    )(q, k, v, gain)


def _rotary_tables(seq):
    half = ROT_DIM // 2
    inv_freq = ROPE_THETA ** (-jnp.arange(half, dtype=F32) * 2.0 / ROT_DIM)
    ang = jnp.arange(seq).astype(F32)[:, None] * inv_freq[None, :]
    cos, sin = jnp.cos(ang), jnp.sin(ang)
    rest = HEAD_DIM - ROT_DIM
    cos_h = jnp.concatenate([cos, cos, jnp.ones((seq, rest), F32)], axis=-1)
    sin_h = jnp.concatenate([-sin, sin, jnp.zeros((seq, rest), F32)], axis=-1)
    return jnp.tile(cos_h, (1, PAIR // HEAD_DIM)), jnp.tile(sin_h, (1, PAIR // HEAD_DIM))


FFN_HALO = 16


def _mix_ffn_kernel(x_ref, ya_ref, yb_ref, xh_ref, yah_ref, ybh_ref, wo_ref, g_ref, wg_ref, wv_ref, cw_ref,
                    cb_ref, wd_ref, fg_ref, o_ref, h_scr, hh_scr, *, tm, seq, row_chunks, apply_final):
    i = pl.program_id(0)
    j = pl.program_id(1)
    wa = ya_ref.shape[1]

    def mixed(x, ya, yb):
        return x + _dot(ya, wo_ref[:wa, :]) + _dot(yb, wo_ref[wa:, :])

    @pl.when(j == 0)
    def _():
        x1 = mixed(x_ref[...], ya_ref[...], yb_ref[...])
        o_ref[...] = x1
        h_scr[...] = _rmsnorm(x1, g_ref[...]).astype(BF16)
        hh_scr[...] = _rmsnorm(mixed(xh_ref[...], yah_ref[...], ybh_ref[...]), g_ref[...]).astype(BF16)

    seq_start = (i * tm) % seq == 0
    rc = tm // row_chunks
    row = lax.broadcasted_iota(jnp.int32, (rc, 1), 0)
    for c in range(row_chunks):
        rows = slice(c * rc, (c + 1) * rc)
        h = h_scr[rows, :]
        gate = _dot(h, wg_ref[...])
        val = _dot(h, wv_ref[...])
        if c == 0:
            gate_h = jnp.where(seq_start, 0.0, _dot(hh_scr[...], wg_ref[...]))
        else:
            gate_h = _dot(h_scr[c * rc - FFN_HALO:c * rc, :], wg_ref[...])
        g1 = jnp.where(row == 0, gate_h[FFN_HALO - 1:FFN_HALO, :], pltpu.roll(gate, 1, 0))
        g2 = jnp.where(row == 0, gate_h[FFN_HALO - 2:FFN_HALO - 1, :],
                       jnp.where(row == 1, gate_h[FFN_HALO - 1:FFN_HALO, :], pltpu.roll(gate, 2, 0)))
        u = cw_ref[0:1, :] * g2 + cw_ref[1:2, :] * g1 + cw_ref[2:3, :] * gate + cb_ref[...]
        act = (u * _sigmoid(u) * val).astype(BF16)
        o_ref[rows, :] += _dot(act, wd_ref[...])

    if apply_final:
        @pl.when(j == pl.num_programs(1) - 1)
        def _():
            o_ref[...] = _rmsnorm(o_ref[...], fg_ref[...])


def _mix_ffn(x2d, y_a, y_b, w_o, gain, w_up, conv_w, conv_b, w_down, final_gain, seq, apply_final,
             tm=1024, n_ff_tiles=2, row_chunks=2):
    T, D = x2d.shape
    F = w_down.shape[0]
    tf = F // n_ff_tiles
    assert tf % LANES == 0 and seq % tm == 0 and w_up.shape[1] == 2 * F
    halo_blocks = tm // FFN_HALO
    row = lambda i, j: (i, 0)
    halo = lambda i, j: (jnp.maximum(i * halo_blocks - 1, 0), 0)
    fixed = lambda i, j: (0, 0)
    wa, wb = y_a.shape[1], y_b.shape[1]
    return pl.pallas_call(
        functools.partial(_mix_ffn_kernel, tm=tm, seq=seq, row_chunks=row_chunks, apply_final=apply_final),
        out_shape=jax.ShapeDtypeStruct((T, D), F32),
        grid=(T // tm, n_ff_tiles),
        in_specs=[pl.BlockSpec((tm, D), row), pl.BlockSpec((tm, wa), row), pl.BlockSpec((tm, wb), row),
                  pl.BlockSpec((FFN_HALO, D), halo), pl.BlockSpec((FFN_HALO, wa), halo),
                  pl.BlockSpec((FFN_HALO, wb), halo),
                  pl.BlockSpec(w_o.shape, fixed),
                  pl.BlockSpec((1, D), fixed),
                  pl.BlockSpec((D, tf), lambda i, j: (0, j)),
                  pl.BlockSpec((D, tf), lambda i, j: (0, j + n_ff_tiles)),
                  pl.BlockSpec((CONV_WIDTH, tf), lambda i, j: (0, j)),
                  pl.BlockSpec((1, tf), lambda i, j: (0, j)),
                  pl.BlockSpec((tf, D), lambda i, j: (j, 0)),
                  pl.BlockSpec((1, D), fixed)],
        out_specs=pl.BlockSpec((tm, D), row),
        scratch_shapes=[pltpu.VMEM((tm, D), BF16), pltpu.VMEM((FFN_HALO, D), BF16)],
        compiler_params=pltpu.CompilerParams(dimension_semantics=("parallel", "arbitrary"),
                                             vmem_limit_bytes=VMEM_LIMIT),
        name="mix_convglu_ffn",
    )(x2d, y_a, y_b, x2d, y_a, y_b, w_o, gain, w_up, w_up, conv_w, conv_b, w_down, final_gain)


def kernel(x, mix_norm_gain, w_in, rwkv_shift_mix, w0, w_lora_up, a0, a_lora_up, g_lora_up, k_k, k_a, r_k,
           ln_x_w, ln_x_b, attn_norm_gain, w_out, ffn_norm_gain, w_ffn_up, ffn_conv_w, ffn_conv_b,
           w_ffn_down, final_norm_gain):
    B, S, D = x.shape
    depth = w_in.shape[0]
    rw = w0.shape[1]
    aw = attn_norm_gain.shape[1]
    n_w, n_a, n_g = w_lora_up.shape[1], a_lora_up.shape[1], g_lora_up.shape[1]
    n_lora = n_w + n_a + n_g
    lora_pad = -(-n_lora // (2 * LANES)) * (2 * LANES)
    assert lora_pad == 2 * LANES and rw % PAIR == 0 and aw % PAIR == 0
    rwkv_cols = 3 * rw + n_lora
    cos_t, sin_t = _rotary_tables(S)

    x2d = x.reshape(B * S, D)
    for l in range(depth):
        w_r = jnp.pad(w_in[l][:, :rwkv_cols].astype(BF16), ((0, 0), (0, lora_pad - n_lora)))
        w_a = w_in[l][:, rwkv_cols:].astype(BF16)
        mix =jnp.concatenate([rwkv_shift_mix[l], jnp.zeros((lora_pad - n_lora,), F32)])[None, :]
        wc = jnp.zeros((lora_pad, 3 * rw), F32)
        wc = wc.at[:n_w, :rw].set(w_lora_up[l])
        wc = wc.at[n_w:n_w + n_a, rw:2 * rw].set(a_lora_up[l])
        wc = wc.at[n_w + n_a:n_lora, 2 * rw:].set(g_lora_up[l]).astype(BF16)

        p_r, q, k, v = _inproj(x2d, mix_norm_gain[l][None, :], w_r, w_a, cos_t, sin_t)
        y_rwkv = _rwkv(p_r.reshape(B, S, -1), mix, wc, w0[l][None, :], a0[l][None, :], k_k[l][None, :],
                       k_a[l][None, :], r_k[l].reshape(1, rw), ln_x_w[l][None, :], ln_x_b[l][None, :],
                       rw, (n_w, n_w + n_a, n_lora))
        y_attn = _attention(q.reshape(B, S, aw), k.reshape(B, S, aw), v.reshape(B, S, aw),
                            attn_norm_gain[l][None, :])
        x2d = _mix_ffn(x2d, y_rwkv.reshape(B * S, rw), y_attn.reshape(B * S, aw), w_out[l].astype(BF16),
                       ffn_norm_gain[l][None, :], w_ffn_up[l].astype(BF16), ffn_conv_w[l],
                       ffn_conv_b[l][None, :], w_ffn_down[l].astype(BF16), final_norm_gain[None, :], S,
                       apply_final=(l == depth - 1))
    return x2d.reshape(B, S, D)
```

```python
import functools
import math

import jax
import jax.numpy as jnp
from jax import lax
from jax.experimental import pallas as pl
from jax.experimental.pallas import tpu as pltpu

F32 = jnp.float32
BF16 = jnp.bfloat16

LANES = 128
HEAD_DIM = 64
PAIR = 2 * HEAD_DIM
ROT_DIM = HEAD_DIM // 4
ROPE_THETA = 500000.0
NORM_EPS = 1e-6
GN_EPS = 64e-5
DILATED_PATTERNS = ((128, 1), (512, 4), (2048, 16))
ATTN_BLOCK = 128
CONV_WIDTH = 3
CHUNK = 64
NEG_BIG = -1e30
VMEM_LIMIT = 56 * 1024 * 1024


def _dot(a, b):
    return jnp.dot(a, b, preferred_element_type=F32)


def _dot_nt(a, b):
    return lax.dot_general(a, b, (((1,), (1,)), ((), ())), preferred_element_type=F32)


def _dot_tn(a, b):
    return lax.dot_general(a, b, (((0,), (0,)), ((), ())), preferred_element_type=F32)


def _rmsnorm(x, gain):
    return x * lax.rsqrt(jnp.mean(x * x, axis=-1, keepdims=True) + NORM_EPS) * gain


def _sigmoid(x):
    return 1.0 / (1.0 + jnp.exp(-x))


def _split_dot(x, w):
    hi = x.astype(BF16)
    lo = (x - hi.astype(F32)).astype(BF16)
    return _dot(hi, w) + _dot(lo, w)


def _inproj_kernel(x_ref, g_ref, wr_ref, wa_ref, cos_ref, sin_ref, pr_ref, q_ref, k_ref, v_ref):
    h = _rmsnorm(x_ref[...], g_ref[...]).astype(BF16)
    pr_ref[...] = _dot(h, wr_ref[...])
    p = _dot(h, wa_ref[...])
    aw = q_ref.shape[1]

    tm = x_ref.shape[0]
    lane = lax.broadcasted_iota(jnp.int32, (tm, PAIR), 1)
    first_half = (lane % HEAD_DIM) < (ROT_DIM // 2)
    cos, sin = cos_ref[...], sin_ref[...]

    def rotary(x):
        partner = jnp.where(first_half, pltpu.roll(x, PAIR - ROT_DIM // 2, 1), pltpu.roll(x, ROT_DIM // 2, 1))
        return x * cos + partner * sin

    for t in range(aw // PAIR):
        lo = t * PAIR
        q_ref[:, lo:lo + PAIR] = (rotary(p[:, lo:lo + PAIR]) * (1.0 / math.sqrt(HEAD_DIM))).astype(BF16)
        k_ref[:, lo:lo + PAIR] = rotary(p[:, aw + lo:aw + lo + PAIR]).astype(BF16)
    v_ref[...] = p[:, 2 * aw:].astype(BF16)


def _inproj(x2d, gain, w_r, w_a, cos_t, sin_t, tm=512):
    T, D = x2d.shape
    rw, aw = w_r.shape[1], w_a.shape[1] // 3
    seq_tiles = cos_t.shape[0] // tm
    row = lambda i: (i, 0)
    fixed = lambda i: (0, 0)
    tab = pl.BlockSpec((tm, PAIR), lambda i: (i % seq_tiles, 0))
    return pl.pallas_call(
        _inproj_kernel,
        out_shape=(jax.ShapeDtypeStruct((T, rw), F32),) + (jax.ShapeDtypeStruct((T, aw), BF16),) * 3,
        grid=(T // tm,),
        in_specs=[pl.BlockSpec((tm, D), row), pl.BlockSpec((1, D), fixed), pl.BlockSpec(w_r.shape, fixed),
                  pl.BlockSpec(w_a.shape, fixed), tab, tab],
        out_specs=(pl.BlockSpec((tm, rw), row),) + (pl.BlockSpec((tm, aw), row),) * 3,
        compiler_params=pltpu.CompilerParams(dimension_semantics=("parallel",), vmem_limit_bytes=VMEM_LIMIT),
        name="inproj",
    )(x2d, gain, w_r, w_a, cos_t, sin_t)


def _each(fn, *lists):
    return [fn(*args) for args in zip(*lists)]


def _rwkv_chunk_terms(r, k, v, logw, cum, kkn, bb, same_blk, strict_c, incl_c, lane_lo):
    L = CHUNK
    bf = lambda t: t.astype(BF16)
    cum_last = [t[L - 1:L, :] for t in cum]
    g_in = _each(jnp.exp, cum)
    g_ex = _each(lambda t, w: jnp.exp(t - w), cum, logw)
    g_inv = _each(lambda t: jnp.exp(-t), cum)
    g_hat = _each(lambda tl, t: jnp.exp(tl - t), cum_last, cum)
    g_last = _each(jnp.exp, cum_last)
    yield

    def by_head_rows(x):
        lo = lane_lo if x.shape[1] == PAIR else jnp.concatenate([lane_lo] * (x.shape[1] // PAIR), axis=1)
        return jnp.concatenate([jnp.where(lo, x, 0.0), jnp.where(lo, 0.0, x)], axis=0).astype(BF16)

    al = _each(lambda t, g: -t * g, kkn, g_ex)
    rb = _each(jnp.multiply, r, g_in)
    bt = _each(jnp.multiply, bb, g_inv)
    kt = _each(jnp.multiply, k, g_inv)
    bh = _each(jnp.multiply, bb, g_hat)
    kh = _each(jnp.multiply, k, g_hat)
    lhs = _each(lambda a, b: jnp.concatenate([a, b], axis=0).astype(BF16), al, rb)
    rhs = _each(lambda a, b: jnp.concatenate([by_head_rows(a), by_head_rows(b)], axis=0), bt, kt)
    yield
    aq = _each(_dot_nt, lhs, rhs)
    a_ab = [jnp.where(strict_c, t[:L, :2 * L], 0.0) for t in aq]
    a_ak = [jnp.where(strict_c, t[:L, 2 * L:], 0.0).astype(BF16) for t in aq]
    a_rb = [jnp.where(incl_c, t[L:, :2 * L], 0.0).astype(BF16) for t in aq]
    a_rk = [jnp.where(incl_c, t[L:, 2 * L:], 0.0).astype(BF16) for t in aq]
    yield

    v_s = _each(by_head_rows, v)
    av = _each(_dot, a_ak, v_s)
    x = _each(lambda a, b: jnp.concatenate([a, b], axis=1), al, av)
    yield
    ap = a_ab
    n_lvl = int(math.log2(L))
    for lvl in range(n_lvl):
        apb = _each(bf, ap)
        x = _each(lambda t, a: t + _dot(a, by_head_rows(t)), x, apb)
        if lvl + 1 < n_lvl:
            ap = _each(lambda a, ab: _dot(ab, jnp.where(same_blk, jnp.concatenate([a, a], axis=0), 0.0).astype(BF16)),
                       ap, apb)
        yield

    z = _each(lambda a, t: _dot(a, by_head_rows(t)), a_rb, x)
    akv = _each(_dot, a_rk, v_s)
    w2 = _each(lambda a, t: (a + t[:, :PAIR]).astype(BF16), rb, z)
    y_loc = _each(lambda t, a: t[:, PAIR:] + a, z, akv)
    w1 = [t[:, :PAIR].astype(BF16) for t in x]
    u_loc = [t[:, PAIR:] for t in x]
    yield

    m_t = _each(lambda a, b: jnp.where(same_blk, _dot_tn(a, b.astype(BF16)), 0.0).astype(BF16), w1, bh)
    s_loc = _each(
        lambda u, vv, b, kk_: jnp.where(
            same_blk,
            _dot_tn(jnp.concatenate([u, vv], axis=0).astype(BF16), jnp.concatenate([b, kk_], axis=0).astype(BF16)),
            0.0),
        u_loc, v, bh, kh)
    return w2, y_loc, m_t, s_loc, g_last


def _rwkv_kernel(p_ref, mix_ref, wc_ref, w0_ref, a0_ref, kk_ref, ka_ref, rk_ref, lnw_ref, lnb_ref,
                 o_ref, s_scr, carry_scr, *, nb, width, n_lora, n_chunks, n_groups, stage_offset):
    L = CHUNK
    R = n_chunks * L
    c = pl.program_id(0)

    @pl.when(c == 0)
    def _():
        s_scr[...] = jnp.zeros_like(s_scr)
        carry_scr[...] = jnp.zeros_like(carry_scr)

    n_pairs = width // PAIR
    row = lax.broadcasted_iota(jnp.int32, (R, 1), 0)
    ri = lax.broadcasted_iota(jnp.int32, (2 * L, 2 * L), 0)
    ci = lax.broadcasted_iota(jnp.int32, (2 * L, 2 * L), 1)
    same_blk = (ri >= L) == (ci >= L)
    ti = lax.broadcasted_iota(jnp.int32, (L, 2 * L), 0)
    si = lax.broadcasted_iota(jnp.int32, (L, 2 * L), 1) % L
    strict_c = si < ti
    incl_c = si <= ti
    rr = lax.broadcasted_iota(jnp.int32, (R, R), 0)
    rc = lax.broadcasted_iota(jnp.int32, (R, R), 1)
    tri = ((rr // L == rc // L) & (rr >= rc)).astype(BF16)
    lane_lo = lax.broadcasted_iota(jnp.int32, (L, PAIR), 1) < HEAD_DIM
    lane_lo_r = lax.broadcasted_iota(jnp.int32, (R, PAIR), 1) < HEAD_DIM
    lora_lane = lax.broadcasted_iota(jnp.int32, (R, 2 * LANES), 1)

    def head_sums(x):
        s0 = jnp.sum(jnp.where(lane_lo_r, x, 0.0), axis=-1, keepdims=True)
        s1 = jnp.sum(jnp.where(lane_lo_r, 0.0, x), axis=-1, keepdims=True)
        return jnp.where(lane_lo_r, s0, s1)

    mix = mix_ref[...]

    def lora_up(pm):
        lora = pm[:, 3 * width:]
        act = jnp.where(lora_lane < n_lora[0], jnp.tanh(lora),
                        jnp.where(lora_lane < n_lora[1], lora,
                                  jnp.where(lora_lane < n_lora[2], _sigmoid(lora), 0.0)))
        return _dot(act.astype(BF16), wc_ref[...])

    def log_decay(up):
        t = w0_ref[...] + up[:, :width]
        return -jnp.exp(-(jnp.maximum(-t, 0.0) + jnp.log(1.0 + jnp.exp(-jnp.abs(t)))) - 0.5)

    def running_sum(t):
        h1 = t.astype(BF16)
        r1 = t - h1.astype(F32)
        h2 = r1.astype(BF16)
        h3 = (r1 - h2.astype(F32)).astype(BF16)
        return _dot(tri, h1) + _dot(tri, h2) + _dot(tri, h3)

    tiles = [(b, pr) for b in range(nb) for pr in range(n_pairs)]
    n_t = len(tiles)
    lanes = lambda pr, part=0: slice(part * width + pr * PAIR, part * width + (pr + 1) * PAIR)
    sls = [lanes(pr) for _, pr in tiles]

    states = {0: [s_scr[i] for i in range(n_t)]}
    outs = {}

    def run_group(gi):
        g0 = gi * R
        pms = []
        for b in range(nb):
            p = p_ref[b, g0:g0 + R, :]
            prev_last = carry_scr[b, 7:8, :] if gi == 0 else p_ref[b, g0 - 1:g0, :]
            p_prev = jnp.where(row == 0, prev_last, pltpu.roll(p, 1, 0))
            pms.append(p + (p_prev - p) * mix)
        yield
        ups = _each(lora_up, pms)
        logw_b = _each(log_decay, ups)
        cum_b = _each(running_sum, logw_b)
        yield

        r = [pms[b][:, lanes(pr, 0)] for b, pr in tiles]
        k_raw = [pms[b][:, lanes(pr, 1)] for b, pr in tiles]
        v = [pms[b][:, lanes(pr, 2)] for b, pr in tiles]
        logw = [logw_b[b][:, lanes(pr)] for b, pr in tiles]
        cum = [cum_b[b][:, lanes(pr)] for b, pr in tiles]
        a = [_sigmoid(a0_ref[:, lanes(pr)] + ups[b][:, lanes(pr, 1)]) for b, pr in tiles]
        g = [ups[b][:, lanes(pr, 2)] for b, pr in tiles]

        kk = _each(lambda t, sl: t * kk_ref[:, sl], k_raw, sls)
        kk_ss = _each(lambda t: head_sums(t * t), kk)
        kkn = _each(lambda t, ss: t / jnp.maximum(jnp.sqrt(ss), 1e-12), kk, kk_ss)
        k = _each(lambda t, aa, sl: t * (1.0 + (aa - 1.0) * ka_ref[:, sl]), k_raw, a, sls)
        bb = _each(jnp.multiply, kkn, a)
        bonus_dot = _each(lambda rr_, kk_, sl: head_sums(rr_ * kk_ * rk_ref[:, sl]), r, k, sls)
        yield

        def chunks(ts):
            return [t[ch * L:(ch + 1) * L] for ch in range(n_chunks) for t in ts]

        w2, y_loc, m_t, s_loc, g_last = yield from _rwkv_chunk_terms(
            chunks(r), chunks(k), chunks(v), chunks(logw), chunks(cum), chunks(kkn), chunks(bb),
            same_blk, strict_c, incl_c, lane_lo)
        yield

        s = states[gi]
        y_parts = []
        for ch in range(n_chunks):
            sel = slice(ch * n_t, (ch + 1) * n_t)
            s_b = _each(lambda t: t.astype(BF16), s)
            y_parts.append(_each(lambda a_, sb, yl: _dot_nt(a_, sb) + yl, w2[sel], s_b, y_loc[sel]))
            s = _each(lambda s0, gl, sb, m, sl: s0 * gl + _dot(sb, m) + sl, s, g_last[sel], s_b, m_t[sel], s_loc[sel])
        states[gi + 1] = s
        y = [jnp.concatenate([y_parts[ch][i] for ch in range(n_chunks)], axis=0) for i in range(n_t)]
        yield

        mu = _each(lambda t: head_sums(t) * (1.0 / HEAD_DIM), y)
        yc = _each(jnp.subtract, y, mu)
        var = _each(lambda t: head_sums(t * t) * (1.0 / HEAD_DIM), yc)
        yield
        res = []
        for i in range(n_t):
            sl = sls[i]
            yn = yc[i] * lax.rsqrt(var[i] + GN_EPS) * lnw_ref[:, sl] + lnb_ref[:, sl]
            res.append(((yn + bonus_dot[i] * v[i]) * g[i]).astype(o_ref.dtype))
        outs[gi] = res

    gens = [run_group(gi) for gi in range(n_groups)]
    live = [True] * n_groups
    tick = 0
    while any(live):
        for gi in range(n_groups):
            if live[gi] and tick >= gi * stage_offset:
                try:
                    next(gens[gi])
                except StopIteration:
                    live[gi] = False
        tick += 1

    for gi in range(n_groups):
        for i, (b, pr) in enumerate(tiles):
            o_ref[b, gi * R:(gi + 1) * R, sls[i]] = outs[gi][i]
    for i in range(n_t):
        s_scr[i] = states[n_groups][i]
    for b in range(nb):
        carry_scr[b] = p_ref[b, n_groups * R - 8:n_groups * R, :]


def _rwkv(p_r, mix, wc, w0, a0, k_k, k_a, r_k, ln_w, ln_b, width, n_lora, n_chunks=2, n_groups=2, stage_offset=5):
    B, S, C = p_r.shape
    L = CHUNK * n_chunks * n_groups
    n_state = B * (width // PAIR)
    vec = lambda n: pl.BlockSpec((1, n), lambda c: (0, 0))
    return pl.pallas_call(
        functools.partial(_rwkv_kernel, nb=B, width=width, n_lora=n_lora, n_chunks=n_chunks, n_groups=n_groups,
                          stage_offset=stage_offset),
        out_shape=jax.ShapeDtypeStruct((B, S, width), BF16),
        grid=(S // L,),
        in_specs=[pl.BlockSpec((B, L, C), lambda c: (0, c, 0)), vec(C),
                  pl.BlockSpec(wc.shape, lambda c: (0, 0))] + [vec(width)] * 7,
        out_specs=pl.BlockSpec((B, L, width), lambda c: (0, c, 0)),
        scratch_shapes=[pltpu.VMEM((n_state, PAIR, PAIR), F32), pltpu.VMEM((B, 8, C), F32)],
        compiler_params=pltpu.CompilerParams(dimension_semantics=("arbitrary",), vmem_limit_bytes=VMEM_LIMIT),
        name="rwkv7",
    )(p_r, mix, wc, w0, a0, k_k, k_a, r_k, ln_w, ln_b)


SB_ROWS = 2048
SB_QUARTER = SB_ROWS // 4
N_RES = 4


def _attn_segments(pi, g):
    if pi == 0:
        return [((g // 4) * SB_QUARTER + r * ATTN_BLOCK + (ATTN_BLOCK // N_RES) * (g % 4), ATTN_BLOCK // N_RES, 1)
                for r in range(N_RES)]
    if pi == 1:
        return [((g // 4) * SB_QUARTER + (g % 4) * ATTN_BLOCK, ATTN_BLOCK, 1)]
    return [(qq * SB_QUARTER + (g % 4) * ATTN_BLOCK + g // 4, ATTN_BLOCK // 4, 4) for qq in range(4)]


def _attn_prev_tile(pi, g):
    if pi == 0:
        return (g + 15) % 16, g == 0
    if pi == 1:
        return ((g // 4 + 3) % 4) * 4 + g % 4, g < 4
    return g, True


def _attn_kernel(q_ref, k_ref, v_ref, gain_ref, o_ref,
                 nat_q, nat_k, nat_v, q_scr, k_ring, v_ring, bias_scr, out_nat, *stat_scr, tiles_per_iter):
    sb = pl.program_id(2)
    n_pat = len(DILATED_PATTERNS)
    assert DILATED_PATTERNS == ((128, 1), (512, 4), (2048, 16)) and ATTN_BLOCK == 128
    num_scr, m_scr, l_scr = stat_scr[:n_pat], stat_scr[n_pat:2 * n_pat], stat_scr[2 * n_pat:]
    Q = ATTN_BLOCK
    cur_base = (sb % 2) * SB_ROWS
    other_base = SB_ROWS - cur_base

    @pl.when(sb == 0)
    def _():
        k_ring[pl.ds(SB_ROWS, SB_ROWS), :] = jnp.zeros((SB_ROWS, PAIR), F32)
        v_ring[pl.ds(SB_ROWS, SB_ROWS), :] = jnp.zeros((SB_ROWS, PAIR), F32)

    nat_q[...] = q_ref[...].astype(F32)
    nat_k[...] = k_ref[...].astype(F32)
    nat_v[...] = v_ref[...].astype(F32)
    for qq in range(4):
        for r in range(N_RES):
            src = pl.ds(qq * SB_QUARTER + r, Q, stride=N_RES)
            dst = qq * SB_QUARTER + r * Q
            q_scr[dst:dst + Q, :] = nat_q[src, :]
            k_ring[pl.ds(cur_base + dst, Q), :] = nat_k[src, :]
            v_ring[pl.ds(cur_base + dst, Q), :] = nat_v[src, :]

    @pl.when(sb == 0)
    def _():
        ii = lax.broadcasted_iota(jnp.int32, (2 * Q, 2 * Q), 0) % Q
        cj = lax.broadcasted_iota(jnp.int32, (2 * Q, 2 * Q), 1)
        per = Q // N_RES
        for kind in range(2):
            if kind == 0:
                qi = N_RES * (ii % per) + ii // per
                kj = N_RES * ((cj % Q) % per) + (cj % Q) // per + (cj // Q) * Q
            else:
                qi, kj = ii, cj
            band = (kj >= qi) & (kj <= qi + Q)
            bias_scr[kind, 1] = jnp.where(band, 0.0, NEG_BIG)
            bias_scr[kind, 0] = jnp.where(band & (cj >= Q), 0.0, NEG_BIG)

    def load_tile(ref, base, segs):
        parts = [ref[pl.ds(base + s, n, stride=st) if st > 1 else pl.ds(base + s, n), :] for s, n, st in segs]
        return parts[0] if len(parts) == 1 else jnp.concatenate(parts, axis=0)

    def store_tile(ref, segs, val):
        off = 0
        for s, n, st in segs:
            ref[pl.ds(s, n, stride=st) if st > 1 else pl.ds(s, n), :] = val[off:off + n]
            off += n

    lane_lo = lax.broadcasted_iota(jnp.int32, (Q, PAIR), 1) < HEAD_DIM
    ones_blk = jnp.ones((2 * Q, PAIR), BF16)

    n_blk = SB_ROWS // Q
    for pi in range(n_pat):
        n_tiles = tiles_per_iter[pi]
        assert n_blk % n_tiles == 0
        kind = 0 if pi == 0 else 1

        def tiles(it, carry, pi=pi, n_tiles=n_tiles, kind=kind):
            gs = [it * n_tiles + t for t in range(n_tiles)]
            segs = [_attn_segments(pi, g) for g in gs]
            prev = [_attn_prev_tile(pi, g) for g in gs]
            prev_segs = [_attn_segments(pi, pg) for pg, _ in prev]
            prev_base = [other_base if other is True else jnp.where(other, other_base, cur_base) for _, other in prev]
            has_prev = [jnp.where(sb > 0, 1, 0) if other is True else jnp.where((sb > 0) | ~other, 1, 0)
                        for _, other in prev]
            q2 = [load_tile(q_scr, 0, sg) for sg in segs]
            q2 = [jnp.concatenate([jnp.where(lane_lo, t, 0.0), jnp.where(lane_lo, 0.0, t)], axis=0).astype(BF16)
                  for t in q2]
            kcat = [jnp.concatenate([load_tile(k_ring, pb, psg), load_tile(k_ring, cur_base, sg)], axis=0).astype(BF16)
                    for pb, psg, sg in zip(prev_base, prev_segs, segs)]
            vcat = [jnp.concatenate([load_tile(v_ring, pb, psg), load_tile(v_ring, cur_base, sg)], axis=0).astype(BF16)
                    for pb, psg, sg in zip(prev_base, prev_segs, segs)]
            vext = [jnp.concatenate([t, ones_blk], axis=1) for t in vcat]
            s = _each(_dot_nt, q2, kcat)
            s = _each(lambda t, hp: t + bias_scr[kind, hp], s, has_prev)
            m = [jnp.max(t, axis=-1, keepdims=True) for t in s]
            p = _each(lambda t, mm: jnp.exp(t - mm).astype(BF16), s, m)
            nl = _each(_dot, p, vext)
            for t in range(n_tiles):
                store_tile(num_scr[pi], segs[t], jnp.where(lane_lo, nl[t][:Q, :PAIR], nl[t][Q:, :PAIR]))
                store_tile(l_scr[pi], segs[t], jnp.where(lane_lo, nl[t][:Q, PAIR:], nl[t][Q:, PAIR:]))
                store_tile(m_scr[pi], segs[t], jnp.where(lane_lo, m[t][:Q], m[t][Q:]))
            return carry

        lax.fori_loop(0, n_blk // n_tiles, tiles, 0)

    ri = lax.broadcasted_iota(jnp.int32, (PAIR, PAIR), 0)
    ci = lax.broadcasted_iota(jnp.int32, (PAIR, PAIR), 1)
    seg_ones = ((ri >= HEAD_DIM) == (ci >= HEAD_DIM)).astype(BF16)
    gain = gain_ref[...]

    def merge(i, carry):
        rows = pl.ds(pl.multiple_of(i * Q, Q), Q)
        ms = [m_scr[pi][rows, :] for pi in range(n_pat)]
        m_all = functools.reduce(jnp.maximum, ms)
        num = 0.0
        den = 0.0
        for pi in range(n_pat):
            wgt = jnp.exp(ms[pi] - m_all)
            num = num + wgt * num_scr[pi][rows, :]
            den = den + wgt * l_scr[pi][rows, :]
        o = num / den
        ms_o = _split_dot(o * o, seg_ones) * (1.0 / HEAD_DIM)
        out_nat[pl.ds((i // N_RES) * SB_QUARTER + i % N_RES, Q, stride=N_RES), :] = o * lax.rsqrt(ms_o + NORM_EPS) * gain
        return carry

    lax.fori_loop(0, SB_ROWS // Q, merge, 0, unroll=4)
    o_ref[...] = out_nat[...].astype(o_ref.dtype)


def _attention(q, k, v, gain, tiles_per_iter=(8, 8, 8)):
    B, S, W = q.shape
    n_pairs = W // PAIR
    n_pat = len(DILATED_PATTERNS)
    blk = pl.BlockSpec((None, SB_ROWS, PAIR), lambda b, p, s: (b, s, p))
    tile = pltpu.VMEM((SB_ROWS, PAIR), F32)
    return pl.pallas_call(
        functools.partial(_attn_kernel, tiles_per_iter=tiles_per_iter),
        out_shape=jax.ShapeDtypeStruct((B, S, W), BF16),
        grid=(B, n_pairs, S // SB_ROWS),
        in_specs=[blk, blk, blk, pl.BlockSpec((1, PAIR), lambda b, p, s: (0, p))],
        out_specs=blk,
        scratch_shapes=[tile, tile, tile, tile,
                        pltpu.VMEM((2 * SB_ROWS, PAIR), F32),
                        pltpu.VMEM((2 * SB_ROWS, PAIR), F32),
                        pltpu.VMEM((2, 2, 2 * ATTN_BLOCK, 2 * ATTN_BLOCK), F32),
                        tile]
                       + [tile] * (3 * n_pat),
        compiler_params=pltpu.CompilerParams(dimension_semantics=("parallel", "parallel", "arbitrary"),
                                             vmem_limit_bytes=VMEM_LIMIT),
        name="dilated_attn",
    )(q, k, v, gain)


def _rotary_tables(seq):
    half = ROT_DIM // 2
    inv_freq = ROPE_THETA ** (-jnp.arange(half, dtype=F32) * 2.0 / ROT_DIM)
    ang = jnp.arange(seq).astype(F32)[:, None] * inv_freq[None, :]
    cos, sin = jnp.cos(ang), jnp.sin(ang)
    rest = HEAD_DIM - ROT_DIM
    cos_h = jnp.concatenate([cos, cos, jnp.ones((seq, rest), F32)], axis=-1)
    sin_h = jnp.concatenate([-sin, sin, jnp.zeros((seq, rest), F32)], axis=-1)
    return jnp.tile(cos_h, (1, PAIR // HEAD_DIM)), jnp.tile(sin_h, (1, PAIR // HEAD_DIM))


FFN_HALO = 16


def _mix_ffn_kernel(x_ref, ya_ref, yb_ref, xh_ref, yah_ref, ybh_ref, wo_ref, g_ref, wg_ref, wv_ref, cw_ref,
                    cb_ref, wd_ref, fg_ref, o_ref, h_scr, hh_scr, *, tm, seq, row_chunks, apply_final):
    i = pl.program_id(0)
    j = pl.program_id(1)
    wa = ya_ref.shape[1]

    def mixed(x, ya, yb):
        return x + _dot(ya, wo_ref[:wa, :]) + _dot(yb, wo_ref[wa:, :])

    @pl.when(j == 0)
    def _():
        x1 = mixed(x_ref[...], ya_ref[...], yb_ref[...])
        o_ref[...] = x1
        h_scr[...] = _rmsnorm(x1, g_ref[...]).astype(BF16)
        hh_scr[...] = _rmsnorm(mixed(xh_ref[...], yah_ref[...], ybh_ref[...]), g_ref[...]).astype(BF16)

    seq_start = (i * tm) % seq == 0
    rc = tm // row_chunks
    row = lax.broadcasted_iota(jnp.int32, (rc, 1), 0)
    for c in range(row_chunks):
        rows = slice(c * rc, (c + 1) * rc)
        h = h_scr[rows, :]
        gate = _dot(h, wg_ref[...])
        val = _dot(h, wv_ref[...])
        if c == 0:
            gate_h = jnp.where(seq_start, 0.0, _dot(hh_scr[...], wg_ref[...]))
        else:
            gate_h = _dot(h_scr[c * rc - FFN_HALO:c * rc, :], wg_ref[...])
        g1 = jnp.where(row == 0, gate_h[FFN_HALO - 1:FFN_HALO, :], pltpu.roll(gate, 1, 0))
        g2 = jnp.where(row == 0, gate_h[FFN_HALO - 2:FFN_HALO - 1, :],
                       jnp.where(row == 1, gate_h[FFN_HALO - 1:FFN_HALO, :], pltpu.roll(gate, 2, 0)))
        u = cw_ref[0:1, :] * g2 + cw_ref[1:2, :] * g1 + cw_ref[2:3, :] * gate + cb_ref[...]
        act = (u * _sigmoid(u) * val).astype(BF16)
        o_ref[rows, :] += _dot(act, wd_ref[...])

    if apply_final:
        @pl.when(j == pl.num_programs(1) - 1)
        def _():
            o_ref[...] = _rmsnorm(o_ref[...], fg_ref[...])


def _mix_ffn(x2d, y_a, y_b, w_o, gain, w_up, conv_w, conv_b, w_down, final_gain, seq, apply_final,
             tm=1024, n_ff_tiles=2, row_chunks=2):
    T, D = x2d.shape
    F = w_down.shape[0]
    tf = F // n_ff_tiles
    assert tf % LANES == 0 and seq % tm == 0 and w_up.shape[1] == 2 * F
    halo_blocks = tm // FFN_HALO
    row = lambda i, j: (i, 0)
    halo = lambda i, j: (jnp.maximum(i * halo_blocks - 1, 0), 0)
    fixed = lambda i, j: (0, 0)
    wa, wb = y_a.shape[1], y_b.shape[1]
    return pl.pallas_call(
        functools.partial(_mix_ffn_kernel, tm=tm, seq=seq, row_chunks=row_chunks, apply_final=apply_final),
        out_shape=jax.ShapeDtypeStruct((T, D), F32),
        grid=(T // tm, n_ff_tiles),
        in_specs=[pl.BlockSpec((tm, D), row), pl.BlockSpec((tm, wa), row), pl.BlockSpec((tm, wb), row),
                  pl.BlockSpec((FFN_HALO, D), halo), pl.BlockSpec((FFN_HALO, wa), halo),
                  pl.BlockSpec((FFN_HALO, wb), halo),
                  pl.BlockSpec(w_o.shape, fixed),
                  pl.BlockSpec((1, D), fixed),
                  pl.BlockSpec((D, tf), lambda i, j: (0, j)),
                  pl.BlockSpec((D, tf), lambda i, j: (0, j + n_ff_tiles)),
                  pl.BlockSpec((CONV_WIDTH, tf), lambda i, j: (0, j)),
                  pl.BlockSpec((1, tf), lambda i, j: (0, j)),
                  pl.BlockSpec((tf, D), lambda i, j: (j, 0)),
                  pl.BlockSpec((1, D), fixed)],
        out_specs=pl.BlockSpec((tm, D), row),
        scratch_shapes=[pltpu.VMEM((tm, D), BF16), pltpu.VMEM((FFN_HALO, D), BF16)],
        compiler_params=pltpu.CompilerParams(dimension_semantics=("parallel", "arbitrary"),
                                             vmem_limit_bytes=VMEM_LIMIT),
        name="mix_convglu_ffn",
    )(x2d, y_a, y_b, x2d, y_a, y_b, w_o, gain, w_up, w_up, conv_w, conv_b, w_down, final_gain)


def kernel(x, mix_norm_gain, w_in, rwkv_shift_mix, w0, w_lora_up, a0, a_lora_up, g_lora_up, k_k, k_a, r_k,
           ln_x_w, ln_x_b, attn_norm_gain, w_out, ffn_norm_gain, w_ffn_up, ffn_conv_w, ffn_conv_b,
           w_ffn_down, final_norm_gain):
    B, S, D = x.shape
    depth = w_in.shape[0]
    rw = w0.shape[1]
    aw = attn_norm_gain.shape[1]
    n_w, n_a, n_g = w_lora_up.shape[1], a_lora_up.shape[1], g_lora_up.shape[1]
    n_lora = n_w + n_a + n_g
    lora_pad = -(-n_lora // (2 * LANES)) * (2 * LANES)
    assert lora_pad == 2 * LANES and rw % PAIR == 0 and aw % PAIR == 0
    rwkv_cols = 3 * rw + n_lora
    cos_t, sin_t = _rotary_tables(S)

    x2d = x.reshape(B * S, D)
    for l in range(depth):
        w_r = jnp.pad(w_in[l][:, :rwkv_cols].astype(BF16), ((0, 0), (0, lora_pad - n_lora)))
        w_a = w_in[l][:, rwkv_cols:].astype(BF16)
        mix =jnp.concatenate([rwkv_shift_mix[l], jnp.zeros((lora_pad - n_lora,), F32)])[None, :]
        wc = jnp.zeros((lora_pad, 3 * rw), F32)
        wc = wc.at[:n_w, :rw].set(w_lora_up[l])
        wc = wc.at[n_w:n_w + n_a, rw:2 * rw].set(a_lora_up[l])
        wc = wc.at[n_w + n_a:n_lora, 2 * rw:].set(g_lora_up[l]).astype(BF16)

        p_r, q, k, v = _inproj(x2d, mix_norm_gain[l][None, :], w_r, w_a, cos_t, sin_t)
        y_rwkv = _rwkv(p_r.reshape(B, S, -1), mix, wc, w0[l][None, :], a0[l][None, :], k_k[l][None, :],
                       k_a[l][None, :], r_k[l].reshape(1, rw), ln_x_w[l][None, :], ln_x_b[l][None, :],
                       rw, (n_w, n_w + n_a, n_lora))
        y_attn = _attention(q.reshape(B, S, aw), k.reshape(B, S, aw), v.reshape(B, S, aw),
                            attn_norm_gain[l][None, :])
        x2d = _mix_ffn(x2d, y_rwkv.reshape(B * S, rw), y_attn.reshape(B * S, aw), w_out[l].astype(BF16),
                       ffn_norm_gain[l][None, :], w_ffn_up[l].astype(BF16), ffn_conv_w[l],
                       ffn_conv_b[l][None, :], w_ffn_down[l].astype(BF16), final_norm_gain[None, :], S,
                       apply_final=(l == depth - 1))
    return x2d.reshape(B, S, D)
```

```python
import functools
import math

import jax
import jax.numpy as jnp
from jax import lax
from jax.experimental import pallas as pl
from jax.experimental.pallas import tpu as pltpu

F32 = jnp.float32
BF16 = jnp.bfloat16

LANES = 128
HEAD_DIM = 64
PAIR = 2 * HEAD_DIM
ROT_DIM = HEAD_DIM // 4
ROPE_THETA = 500000.0
NORM_EPS = 1e-6
GN_EPS = 64e-5
DILATED_PATTERNS = ((128, 1), (512, 4), (2048, 16))
ATTN_BLOCK = 128
CONV_WIDTH = 3
CHUNK = 64
NEG_BIG = -1e30
VMEM_LIMIT = 56 * 1024 * 1024


def _dot(a, b):
    return jnp.dot(a, b, preferred_element_type=F32)


def _dot_nt(a, b):
    return lax.dot_general(a, b, (((1,), (1,)), ((), ())), preferred_element_type=F32)


def _dot_tn(a, b):
    return lax.dot_general(a, b, (((0,), (0,)), ((), ())), preferred_element_type=F32)


def _rmsnorm(x, gain):
    return x * lax.rsqrt(jnp.mean(x * x, axis=-1, keepdims=True) + NORM_EPS) * gain


def _sigmoid(x):
    return 1.0 / (1.0 + jnp.exp(-x))


def _split_dot(x, w):
    hi = x.astype(BF16)
    lo = (x - hi.astype(F32)).astype(BF16)
    return _dot(hi, w) + _dot(lo, w)


def _inproj_kernel(x_ref, g_ref, wr_ref, wa_ref, cos_ref, sin_ref, pr_ref, q_ref, k_ref, v_ref):
    h = _rmsnorm(x_ref[...], g_ref[...]).astype(BF16)
    pr_ref[...] = _dot(h, wr_ref[...])
    p = _dot(h, wa_ref[...])
    aw = q_ref.shape[1]

    tm = x_ref.shape[0]
    lane = lax.broadcasted_iota(jnp.int32, (tm, PAIR), 1)
    first_half = (lane % HEAD_DIM) < (ROT_DIM // 2)
    cos, sin = cos_ref[...], sin_ref[...]

    def rotary(x):
        partner = jnp.where(first_half, pltpu.roll(x, PAIR - ROT_DIM // 2, 1), pltpu.roll(x, ROT_DIM // 2, 1))
        return x * cos + partner * sin

    for t in range(aw // PAIR):
        lo = t * PAIR
        q_ref[:, lo:lo + PAIR] = (rotary(p[:, lo:lo + PAIR]) * (1.0 / math.sqrt(HEAD_DIM))).astype(BF16)
        k_ref[:, lo:lo + PAIR] = rotary(p[:, aw + lo:aw + lo + PAIR]).astype(BF16)
    v_ref[...] = p[:, 2 * aw:].astype(BF16)


def _inproj(x2d, gain, w_r, w_a, cos_t, sin_t, tm=512):
    T, D = x2d.shape
    rw, aw = w_r.shape[1], w_a.shape[1] // 3
    seq_tiles = cos_t.shape[0] // tm
    row = lambda i: (i, 0)
    fixed = lambda i: (0, 0)
    tab = pl.BlockSpec((tm, PAIR), lambda i: (i % seq_tiles, 0))
    return pl.pallas_call(
        _inproj_kernel,
        out_shape=(jax.ShapeDtypeStruct((T, rw), F32),) + (jax.ShapeDtypeStruct((T, aw), BF16),) * 3,
        grid=(T // tm,),
        in_specs=[pl.BlockSpec((tm, D), row), pl.BlockSpec((1, D), fixed), pl.BlockSpec(w_r.shape, fixed),
                  pl.BlockSpec(w_a.shape, fixed), tab, tab],
        out_specs=(pl.BlockSpec((tm, rw), row),) + (pl.BlockSpec((tm, aw), row),) * 3,
        compiler_params=pltpu.CompilerParams(dimension_semantics=("parallel",), vmem_limit_bytes=VMEM_LIMIT),
        name="inproj",
    )(x2d, gain, w_r, w_a, cos_t, sin_t)


def _each(fn, *lists):
    return [fn(*args) for args in zip(*lists)]


def _rwkv_chunk_terms(r, k, v, logw, cum, kkn, bb, same_blk, strict_c, incl_c, lane_lo):
    L = CHUNK
    bf = lambda t: t.astype(BF16)
    cum_last = [t[L - 1:L, :] for t in cum]
    g_in = _each(jnp.exp, cum)
    g_ex = _each(lambda t, w: jnp.exp(t - w), cum, logw)
    g_inv = _each(lambda t: jnp.exp(-t), cum)
    g_hat = _each(lambda tl, t: jnp.exp(tl - t), cum_last, cum)
    g_last = _each(jnp.exp, cum_last)
    yield

    def by_head_rows(x):
        lo = lane_lo if x.shape[1] == PAIR else jnp.concatenate([lane_lo] * (x.shape[1] // PAIR), axis=1)
        return jnp.concatenate([jnp.where(lo, x, 0.0), jnp.where(lo, 0.0, x)], axis=0).astype(BF16)

    al = _each(lambda t, g: -t * g, kkn, g_ex)
    rb = _each(jnp.multiply, r, g_in)
    bt = _each(jnp.multiply, bb, g_inv)
    kt = _each(jnp.multiply, k, g_inv)
    bh = _each(jnp.multiply, bb, g_hat)
    kh = _each(jnp.multiply, k, g_hat)
    lhs = _each(lambda a, b: jnp.concatenate([a, b], axis=0).astype(BF16), al, rb)
    rhs = _each(lambda a, b: jnp.concatenate([by_head_rows(a), by_head_rows(b)], axis=0), bt, kt)
    yield
    aq = _each(_dot_nt, lhs, rhs)
    a_ab = [jnp.where(strict_c, t[:L, :2 * L], 0.0) for t in aq]
    a_ak = [jnp.where(strict_c, t[:L, 2 * L:], 0.0).astype(BF16) for t in aq]
    a_rb = [jnp.where(incl_c, t[L:, :2 * L], 0.0).astype(BF16) for t in aq]
    a_rk = [jnp.where(incl_c, t[L:, 2 * L:], 0.0).astype(BF16) for t in aq]
    yield

    v_s = _each(by_head_rows, v)
    av = _each(_dot, a_ak, v_s)
    x = _each(lambda a, b: jnp.concatenate([a, b], axis=1), al, av)
    yield
    ap = a_ab
    n_lvl = int(math.log2(L))
    for lvl in range(n_lvl):
        apb = _each(bf, ap)
        x = _each(lambda t, a: t + _dot(a, by_head_rows(t)), x, apb)
        if lvl + 1 < n_lvl:
            ap = _each(lambda a, ab: _dot(ab, jnp.where(same_blk, jnp.concatenate([a, a], axis=0), 0.0).astype(BF16)),
                       ap, apb)
        yield

    z = _each(lambda a, t: _dot(a, by_head_rows(t)), a_rb, x)
    akv = _each(_dot, a_rk, v_s)
    w2 = _each(lambda a, t: (a + t[:, :PAIR]).astype(BF16), rb, z)
    y_loc = _each(lambda t, a: t[:, PAIR:] + a, z, akv)
    w1 = [t[:, :PAIR].astype(BF16) for t in x]
    u_loc = [t[:, PAIR:] for t in x]
    yield

    m_t = _each(lambda a, b: jnp.where(same_blk, _dot_tn(a, b.astype(BF16)), 0.0).astype(BF16), w1, bh)
    s_loc = _each(
        lambda u, vv, b, kk_: jnp.where(
            same_blk,
            _dot_tn(jnp.concatenate([u, vv], axis=0).astype(BF16), jnp.concatenate([b, kk_], axis=0).astype(BF16)),
            0.0),
        u_loc, v, bh, kh)
    return w2, y_loc, m_t, s_loc, g_last


def _rwkv_kernel(p_ref, mix_ref, wc_ref, w0_ref, a0_ref, kk_ref, ka_ref, rk_ref, lnw_ref, lnb_ref,
                 o_ref, s_scr, carry_scr, *, nb, width, n_lora, n_chunks, n_groups, stage_offset):
    L = CHUNK
    R = n_chunks * L
    c = pl.program_id(0)

    @pl.when(c == 0)
    def _():
        s_scr[...] = jnp.zeros_like(s_scr)
        carry_scr[...] = jnp.zeros_like(carry_scr)

    n_pairs = width // PAIR
    row = lax.broadcasted_iota(jnp.int32, (R, 1), 0)
    ri = lax.broadcasted_iota(jnp.int32, (2 * L, 2 * L), 0)
    ci = lax.broadcasted_iota(jnp.int32, (2 * L, 2 * L), 1)
    same_blk = (ri >= L) == (ci >= L)
    ti = lax.broadcasted_iota(jnp.int32, (L, 2 * L), 0)
    si = lax.broadcasted_iota(jnp.int32, (L, 2 * L), 1) % L
    strict_c = si < ti
    incl_c = si <= ti
    rr = lax.broadcasted_iota(jnp.int32, (R, R), 0)
    rc = lax.broadcasted_iota(jnp.int32, (R, R), 1)
    tri = ((rr // L == rc // L) & (rr >= rc)).astype(BF16)
    lane_lo = lax.broadcasted_iota(jnp.int32, (L, PAIR), 1) < HEAD_DIM
    lane_lo_r = lax.broadcasted_iota(jnp.int32, (R, PAIR), 1) < HEAD_DIM
    lora_lane = lax.broadcasted_iota(jnp.int32, (R, 2 * LANES), 1)

    def head_sums(x):
        s0 = jnp.sum(jnp.where(lane_lo_r, x, 0.0), axis=-1, keepdims=True)
        s1 = jnp.sum(jnp.where(lane_lo_r, 0.0, x), axis=-1, keepdims=True)
        return jnp.where(lane_lo_r, s0, s1)

    mix = mix_ref[...]

    def lora_up(pm):
        lora = pm[:, 3 * width:]
        act = jnp.where(lora_lane < n_lora[0], jnp.tanh(lora),
                        jnp.where(lora_lane < n_lora[1], lora,
                                  jnp.where(lora_lane < n_lora[2], _sigmoid(lora), 0.0)))
        return _dot(act.astype(BF16), wc_ref[...])

    def log_decay(up):
        t = w0_ref[...] + up[:, :width]
        return -jnp.exp(-(jnp.maximum(-t, 0.0) + jnp.log(1.0 + jnp.exp(-jnp.abs(t)))) - 0.5)

    def running_sum(t):
        h1 = t.astype(BF16)
        r1 = t - h1.astype(F32)
        h2 = r1.astype(BF16)
        h3 = (r1 - h2.astype(F32)).astype(BF16)
        return _dot(tri, h1) + _dot(tri, h2) + _dot(tri, h3)

    tiles = [(b, pr) for b in range(nb) for pr in range(n_pairs)]
    n_t = len(tiles)
    lanes = lambda pr, part=0: slice(part * width + pr * PAIR, part * width + (pr + 1) * PAIR)
    sls = [lanes(pr) for _, pr in tiles]

    states = {0: [s_scr[i] for i in range(n_t)]}
    outs = {}

    def run_group(gi):
        g0 = gi * R
        pms = []
        for b in range(nb):
            p = p_ref[b, g0:g0 + R, :]
            prev_last = carry_scr[b, 7:8, :] if gi == 0 else p_ref[b, g0 - 1:g0, :]
            p_prev = jnp.where(row == 0, prev_last, pltpu.roll(p, 1, 0))
            pms.append(p + (p_prev - p) * mix)
        yield
        ups = _each(lora_up, pms)
        logw_b = _each(log_decay, ups)
        cum_b = _each(running_sum, logw_b)
        yield

        r = [pms[b][:, lanes(pr, 0)] for b, pr in tiles]
        k_raw = [pms[b][:, lanes(pr, 1)] for b, pr in tiles]
        v = [pms[b][:, lanes(pr, 2)] for b, pr in tiles]
        logw = [logw_b[b][:, lanes(pr)] for b, pr in tiles]
        cum = [cum_b[b][:, lanes(pr)] for b, pr in tiles]
        a = [_sigmoid(a0_ref[:, lanes(pr)] + ups[b][:, lanes(pr, 1)]) for b, pr in tiles]
        g = [ups[b][:, lanes(pr, 2)] for b, pr in tiles]

        kk = _each(lambda t, sl: t * kk_ref[:, sl], k_raw, sls)
        kk_ss = _each(lambda t: head_sums(t * t), kk)
        kkn = _each(lambda t, ss: t / jnp.maximum(jnp.sqrt(ss), 1e-12), kk, kk_ss)
        k = _each(lambda t, aa, sl: t * (1.0 + (aa - 1.0) * ka_ref[:, sl]), k_raw, a, sls)
        bb = _each(jnp.multiply, kkn, a)
        bonus_dot = _each(lambda rr_, kk_, sl: head_sums(rr_ * kk_ * rk_ref[:, sl]), r, k, sls)
        yield

        def chunks(ts):
            return [t[ch * L:(ch + 1) * L] for ch in range(n_chunks) for t in ts]

        w2, y_loc, m_t, s_loc, g_last = yield from _rwkv_chunk_terms(
            chunks(r), chunks(k), chunks(v), chunks(logw), chunks(cum), chunks(kkn), chunks(bb),
            same_blk, strict_c, incl_c, lane_lo)
        yield

        s = states[gi]
        y_parts = []
        for ch in range(n_chunks):
            sel = slice(ch * n_t, (ch + 1) * n_t)
            s_b = _each(lambda t: t.astype(BF16), s)
            y_parts.append(_each(lambda a_, sb, yl: _dot_nt(a_, sb) + yl, w2[sel], s_b, y_loc[sel]))
            s = _each(lambda s0, gl, sb, m, sl: s0 * gl + _dot(sb, m) + sl, s, g_last[sel], s_b, m_t[sel], s_loc[sel])
        states[gi + 1] = s
        y = [jnp.concatenate([y_parts[ch][i] for ch in range(n_chunks)], axis=0) for i in range(n_t)]
        yield

        mu = _each(lambda t: head_sums(t) * (1.0 / HEAD_DIM), y)
        yc = _each(jnp.subtract, y, mu)
        var = _each(lambda t: head_sums(t * t) * (1.0 / HEAD_DIM), yc)
        yield
        res = []
        for i in range(n_t):
            sl = sls[i]
            yn = yc[i] * lax.rsqrt(var[i] + GN_EPS) * lnw_ref[:, sl] + lnb_ref[:, sl]
            res.append(((yn + bonus_dot[i] * v[i]) * g[i]).astype(o_ref.dtype))
        outs[gi] = res

    gens = [run_group(gi) for gi in range(n_groups)]
    live = [True] * n_groups
    tick = 0
    while any(live):
        for gi in range(n_groups):
            if live[gi] and tick >= gi * stage_offset:
                try:
                    next(gens[gi])
                except StopIteration:
                    live[gi] = False
        tick += 1

    for gi in range(n_groups):
        for i, (b, pr) in enumerate(tiles):
            o_ref[b, gi * R:(gi + 1) * R, sls[i]] = outs[gi][i]
    for i in range(n_t):
        s_scr[i] = states[n_groups][i]
    for b in range(nb):
        carry_scr[b] = p_ref[b, n_groups * R - 8:n_groups * R, :]


def _rwkv(p_r, mix, wc, w0, a0, k_k, k_a, r_k, ln_w, ln_b, width, n_lora, n_chunks=2, n_groups=2, stage_offset=5):
    B, S, C = p_r.shape
    L = CHUNK * n_chunks * n_groups
    n_state = B * (width // PAIR)
    vec = lambda n: pl.BlockSpec((1, n), lambda c: (0, 0))
    return pl.pallas_call(
        functools.partial(_rwkv_kernel, nb=B, width=width, n_lora=n_lora, n_chunks=n_chunks, n_groups=n_groups,
                          stage_offset=stage_offset),
        out_shape=jax.ShapeDtypeStruct((B, S, width), BF16),
        grid=(S // L,),
        in_specs=[pl.BlockSpec((B, L, C), lambda c: (0, c, 0)), vec(C),
                  pl.BlockSpec(wc.shape, lambda c: (0, 0))] + [vec(width)] * 7,
        out_specs=pl.BlockSpec((B, L, width), lambda c: (0, c, 0)),
        scratch_shapes=[pltpu.VMEM((n_state, PAIR, PAIR), F32), pltpu.VMEM((B, 8, C), F32)],
        compiler_params=pltpu.CompilerParams(dimension_semantics=("arbitrary",), vmem_limit_bytes=VMEM_LIMIT),
        name="rwkv7",
    )(p_r, mix, wc, w0, a0, k_k, k_a, r_k, ln_w, ln_b)


SB_ROWS = 2048
SB_QUARTER = SB_ROWS // 4
N_RES = 4


def _attn_segments(pi, g):
    if pi == 0:
        return [((g // 4) * SB_QUARTER + r * ATTN_BLOCK + (ATTN_BLOCK // N_RES) * (g % 4), ATTN_BLOCK // N_RES, 1)
                for r in range(N_RES)]
    if pi == 1:
        return [((g // 4) * SB_QUARTER + (g % 4) * ATTN_BLOCK, ATTN_BLOCK, 1)]
    return [(qq * SB_QUARTER + (g % 4) * ATTN_BLOCK + g // 4, ATTN_BLOCK // 4, 4) for qq in range(4)]


def _attn_prev_tile(pi, g):
    if pi == 0:
        return (g + 15) % 16, g == 0
    if pi == 1:
        return ((g // 4 + 3) % 4) * 4 + g % 4, g < 4
    return g, True


def _attn_kernel(q_ref, k_ref, v_ref, gain_ref, o_ref,
                 nat_q, nat_k, nat_v, q_scr, k_ring, v_ring, bias_scr, out_nat, *stat_scr, tiles_per_iter):
    sb = pl.program_id(2)
    n_pat = len(DILATED_PATTERNS)
    assert DILATED_PATTERNS == ((128, 1), (512, 4), (2048, 16)) and ATTN_BLOCK == 128
    num_scr, m_scr, l_scr = stat_scr[:n_pat], stat_scr[n_pat:2 * n_pat], stat_scr[2 * n_pat:]
    Q = ATTN_BLOCK
    cur_base = (sb % 2) * SB_ROWS
    other_base = SB_ROWS - cur_base

    @pl.when(sb == 0)
    def _():
        k_ring[pl.ds(SB_ROWS, SB_ROWS), :] = jnp.zeros((SB_ROWS, PAIR), F32)
        v_ring[pl.ds(SB_ROWS, SB_ROWS), :] = jnp.zeros((SB_ROWS, PAIR), F32)

    nat_q[...] = q_ref[...].astype(F32)
    nat_k[...] = k_ref[...].astype(F32)
    nat_v[...] = v_ref[...].astype(F32)
    for qq in range(4):
        for r in range(N_RES):
            src = pl.ds(qq * SB_QUARTER + r, Q, stride=N_RES)
            dst = qq * SB_QUARTER + r * Q
            q_scr[dst:dst + Q, :] = nat_q[src, :]
            k_ring[pl.ds(cur_base + dst, Q), :] = nat_k[src, :]
            v_ring[pl.ds(cur_base + dst, Q), :] = nat_v[src, :]

    @pl.when(sb == 0)
    def _():
        ii = lax.broadcasted_iota(jnp.int32, (2 * Q, 2 * Q), 0) % Q
        cj = lax.broadcasted_iota(jnp.int32, (2 * Q, 2 * Q), 1)
        per = Q // N_RES
        for kind in range(2):
            if kind == 0:
                qi = N_RES * (ii % per) + ii // per
                kj = N_RES * ((cj % Q) % per) + (cj % Q) // per + (cj // Q) * Q
            else:
                qi, kj = ii, cj
            band = (kj >= qi) & (kj <= qi + Q)
            bias_scr[kind, 1] = jnp.where(band, 0.0, NEG_BIG)
            bias_scr[kind, 0] = jnp.where(band & (cj >= Q), 0.0, NEG_BIG)

    def load_tile(ref, base, segs):
        parts = [ref[pl.ds(base + s, n, stride=st) if st > 1 else pl.ds(base + s, n), :] for s, n, st in segs]
        return parts[0] if len(parts) == 1 else jnp.concatenate(parts, axis=0)

    def store_tile(ref, segs, val):
        off = 0
        for s, n, st in segs:
            ref[pl.ds(s, n, stride=st) if st > 1 else pl.ds(s, n), :] = val[off:off + n]
            off += n

    lane_lo = lax.broadcasted_iota(jnp.int32, (Q, PAIR), 1) < HEAD_DIM
    ones_blk = jnp.ones((2 * Q, PAIR), BF16)

    n_blk = SB_ROWS // Q
    for pi in range(n_pat):
        n_tiles = tiles_per_iter[pi]
        assert n_blk % n_tiles == 0
        kind = 0 if pi == 0 else 1

        def tiles(it, carry, pi=pi, n_tiles=n_tiles, kind=kind):
            gs = [it * n_tiles + t for t in range(n_tiles)]
            segs = [_attn_segments(pi, g) for g in gs]
            prev = [_attn_prev_tile(pi, g) for g in gs]
            prev_segs = [_attn_segments(pi, pg) for pg, _ in prev]
            prev_base = [other_base if other is True else jnp.where(other, other_base, cur_base) for _, other in prev]
            has_prev = [jnp.where(sb > 0, 1, 0) if other is True else jnp.where((sb > 0) | ~other, 1, 0)
                        for _, other in prev]
            q2 = [load_tile(q_scr, 0, sg) for sg in segs]
            q2 = [jnp.concatenate([jnp.where(lane_lo, t, 0.0), jnp.where(lane_lo, 0.0, t)], axis=0).astype(BF16)
                  for t in q2]
            kcat = [jnp.concatenate([load_tile(k_ring, pb, psg), load_tile(k_ring, cur_base, sg)], axis=0).astype(BF16)
                    for pb, psg, sg in zip(prev_base, prev_segs, segs)]
            vcat = [jnp.concatenate([load_tile(v_ring, pb, psg), load_tile(v_ring, cur_base, sg)], axis=0).astype(BF16)
                    for pb, psg, sg in zip(prev_base, prev_segs, segs)]
            vext = [jnp.concatenate([t, ones_blk], axis=1) for t in vcat]
            s = _each(_dot_nt, q2, kcat)
            s = _each(lambda t, hp: t + bias_scr[kind, hp], s, has_prev)
            m = [jnp.max(t, axis=-1, keepdims=True) for t in s]
            p = _each(lambda t, mm: jnp.exp(t - mm).astype(BF16), s, m)
            nl = _each(_dot, p, vext)
            for t in range(n_tiles):
                store_tile(num_scr[pi], segs[t], jnp.where(lane_lo, nl[t][:Q, :PAIR], nl[t][Q:, :PAIR]))
                store_tile(l_scr[pi], segs[t], jnp.where(lane_lo, nl[t][:Q, PAIR:], nl[t][Q:, PAIR:]))
                store_tile(m_scr[pi], segs[t], jnp.where(lane_lo, m[t][:Q], m[t][Q:]))
            return carry

        lax.fori_loop(0, n_blk // n_tiles, tiles, 0)

    ri = lax.broadcasted_iota(jnp.int32, (PAIR, PAIR), 0)
    ci = lax.broadcasted_iota(jnp.int32, (PAIR, PAIR), 1)
    seg_ones = ((ri >= HEAD_DIM) == (ci >= HEAD_DIM)).astype(BF16)
    gain = gain_ref[...]

    def merge(i, carry):
        rows = pl.ds(pl.multiple_of(i * Q, Q), Q)
        ms = [m_scr[pi][rows, :] for pi in range(n_pat)]
        m_all = functools.reduce(jnp.maximum, ms)
        num = 0.0
        den = 0.0
        for pi in range(n_pat):
            wgt = jnp.exp(ms[pi] - m_all)
            num = num + wgt * num_scr[pi][rows, :]
            den = den + wgt * l_scr[pi][rows, :]
        o = num / den
        ms_o = _split_dot(o * o, seg_ones) * (1.0 / HEAD_DIM)
        out_nat[pl.ds((i // N_RES) * SB_QUARTER + i % N_RES, Q, stride=N_RES), :] = o * lax.rsqrt(ms_o + NORM_EPS) * gain
        return carry

    lax.fori_loop(0, SB_ROWS // Q, merge, 0, unroll=4)
    o_ref[...] = out_nat[...].astype(o_ref.dtype)


def _attention(q, k, v, gain, tiles_per_iter=(8, 8, 8)):
    B, S, W = q.shape
    n_pairs = W // PAIR
    n_pat = len(DILATED_PATTERNS)
    blk = pl.BlockSpec((None, SB_ROWS, PAIR), lambda b, p, s: (b, s, p))
    tile = pltpu.VMEM((SB_ROWS, PAIR), F32)
    return pl.pallas_call(
        functools.partial(_attn_kernel, tiles_per_iter=tiles_per_iter),
        out_shape=jax.ShapeDtypeStruct((B, S, W), BF16),
        grid=(B, n_pairs, S // SB_ROWS),
        in_specs=[blk, blk, blk, pl.BlockSpec((1, PAIR), lambda b, p, s: (0, p))],
        out_specs=blk,
        scratch_shapes=[tile, tile, tile, tile,
                        pltpu.VMEM((2 * SB_ROWS, PAIR), F32),
                        pltpu.VMEM((2 * SB_ROWS, PAIR), F32),
                        pltpu.VMEM((2, 2, 2 * ATTN_BLOCK, 2 * ATTN_BLOCK), F32),
                        tile]
                       + [tile] * (3 * n_pat),
        compiler_params=pltpu.CompilerParams(dimension_semantics=("parallel", "parallel", "arbitrary"),
                                             vmem_limit_bytes=VMEM_LIMIT),
        name="dilated_attn",
    )(q, k, v, gain)


def _rotary_tables(seq):
    half = ROT_DIM // 2
    inv_freq = ROPE_THETA ** (-jnp.arange(half, dtype=F32) * 2.0 / ROT_DIM)
    ang = jnp.arange(seq).astype(F32)[:, None] * inv_freq[None, :]
    cos, sin = jnp.cos(ang), jnp.sin(ang)
    rest = HEAD_DIM - ROT_DIM
    cos_h = jnp.concatenate([cos, cos, jnp.ones((seq, rest), F32)], axis=-1)
    sin_h = jnp.concatenate([-sin, sin, jnp.zeros((seq, rest), F32)], axis=-1)
    return jnp.tile(cos_h, (1, PAIR // HEAD_DIM)), jnp.tile(sin_h, (1, PAIR // HEAD_DIM))


FFN_HALO = 16


def _mix_ffn_kernel(x_ref, ya_ref, yb_ref, xh_ref, yah_ref, ybh_ref, wo_ref, g_ref, wu_ref, cw_ref,
                    cb_ref, wd_ref, fg_ref, o_ref, h_scr, hh_scr, *, tm, seq, row_chunks, apply_final):
    tf = wd_ref.shape[0]
    i = pl.program_id(0)
    j = pl.program_id(1)
    wa = ya_ref.shape[1]

    def mixed(x, ya, yb):
        return x + _dot(ya, wo_ref[:wa, :]) + _dot(yb, wo_ref[wa:, :])

    @pl.when(j == 0)
    def _():
        x1 = mixed(x_ref[...], ya_ref[...], yb_ref[...])
        o_ref[...] = x1
        h_scr[...] = _rmsnorm(x1, g_ref[...]).astype(BF16)
        hh_scr[...] = _rmsnorm(mixed(xh_ref[...], yah_ref[...], ybh_ref[...]), g_ref[...]).astype(BF16)

    seq_start = (i * tm) % seq == 0
    rc = tm // row_chunks
    row = lax.broadcasted_iota(jnp.int32, (rc, 1), 0)
    for c in range(row_chunks):
        rows = slice(c * rc, (c + 1) * rc)
        h = h_scr[rows, :]
        gv = _dot(h, wu_ref[...])
        gate, val = gv[:, :tf], gv[:, tf:]
        if c == 0:
            gate_h = jnp.where(seq_start, 0.0, _dot(hh_scr[...], wu_ref[:, :tf]))
        else:
            gate_h = _dot(h_scr[c * rc - FFN_HALO:c * rc, :], wu_ref[:, :tf])
        g1 = jnp.where(row == 0, gate_h[FFN_HALO - 1:FFN_HALO, :], pltpu.roll(gate, 1, 0))
        g2 = jnp.where(row == 0, gate_h[FFN_HALO - 2:FFN_HALO - 1, :],
                       jnp.where(row == 1, gate_h[FFN_HALO - 1:FFN_HALO, :], pltpu.roll(gate, 2, 0)))
        u = cw_ref[0:1, :] * g2 + cw_ref[1:2, :] * g1 + cw_ref[2:3, :] * gate + cb_ref[...]
        act = (u * _sigmoid(u) * val).astype(BF16)
        o_ref[rows, :] += _dot(act, wd_ref[...])

    if apply_final:
        @pl.when(j == pl.num_programs(1) - 1)
        def _():
            o_ref[...] = _rmsnorm(o_ref[...], fg_ref[...])


def _mix_ffn(x2d, y_a, y_b, w_o, gain, w_up, conv_w, conv_b, w_down, final_gain, seq, apply_final,
             tm=1024, n_ff_tiles=2, row_chunks=2):
    T, D = x2d.shape
    F = w_down.shape[0]
    tf = F // n_ff_tiles
    assert tf % LANES == 0 and seq % tm == 0 and w_up.shape[1] == 2 * F
    w_up = jnp.concatenate([w_up[:, half * F + j * tf: half * F + (j + 1) * tf]
                            for j in range(n_ff_tiles) for half in range(2)], axis=1)
    halo_blocks = tm // FFN_HALO
    row = lambda i, j: (i, 0)
    halo = lambda i, j: (jnp.maximum(i * halo_blocks - 1, 0), 0)
    fixed = lambda i, j: (0, 0)
    wa, wb = y_a.shape[1], y_b.shape[1]
    return pl.pallas_call(
        functools.partial(_mix_ffn_kernel, tm=tm, seq=seq, row_chunks=row_chunks, apply_final=apply_final),
        out_shape=jax.ShapeDtypeStruct((T, D), F32),
        grid=(T // tm, n_ff_tiles),
        in_specs=[pl.BlockSpec((tm, D), row), pl.BlockSpec((tm, wa), row), pl.BlockSpec((tm, wb), row),
                  pl.BlockSpec((FFN_HALO, D), halo), pl.BlockSpec((FFN_HALO, wa), halo),
                  pl.BlockSpec((FFN_HALO, wb), halo),
                  pl.BlockSpec(w_o.shape, fixed),
                  pl.BlockSpec((1, D), fixed),
                  pl.BlockSpec((D, 2 * tf), lambda i, j: (0, j)),
                  pl.BlockSpec((CONV_WIDTH, tf), lambda i, j: (0, j)),
                  pl.BlockSpec((1, tf), lambda i, j: (0, j)),
                  pl.BlockSpec((tf, D), lambda i, j: (j, 0)),
                  pl.BlockSpec((1, D), fixed)],
        out_specs=pl.BlockSpec((tm, D), row),
        scratch_shapes=[pltpu.VMEM((tm, D), BF16), pltpu.VMEM((FFN_HALO, D), BF16)],
        compiler_params=pltpu.CompilerParams(dimension_semantics=("parallel", "arbitrary"),
                                             vmem_limit_bytes=VMEM_LIMIT),
        name="mix_convglu_ffn",
    )(x2d, y_a, y_b, x2d, y_a, y_b, w_o, gain, w_up, conv_w, conv_b, w_down, final_gain)


def kernel(x, mix_norm_gain, w_in, rwkv_shift_mix, w0, w_lora_up, a0, a_lora_up, g_lora_up, k_k, k_a, r_k,
           ln_x_w, ln_x_b, attn_norm_gain, w_out, ffn_norm_gain, w_ffn_up, ffn_conv_w, ffn_conv_b,
           w_ffn_down, final_norm_gain):
    B, S, D = x.shape
    depth = w_in.shape[0]
    rw = w0.shape[1]
    aw = attn_norm_gain.shape[1]
    n_w, n_a, n_g = w_lora_up.shape[1], a_lora_up.shape[1], g_lora_up.shape[1]
    n_lora = n_w + n_a + n_g
    lora_pad = -(-n_lora // (2 * LANES)) * (2 * LANES)
    assert lora_pad == 2 * LANES and rw % PAIR == 0 and aw % PAIR == 0
    rwkv_cols = 3 * rw + n_lora
    cos_t, sin_t = _rotary_tables(S)

    x2d = x.reshape(B * S, D)
    for l in range(depth):
        w_r = w_in[l][:, :3 * rw + lora_pad].astype(BF16)
        w_a = w_in[l][:, rwkv_cols:].astype(BF16)
        mix =jnp.concatenate([rwkv_shift_mix[l], jnp.zeros((lora_pad - n_lora,), F32)])[None, :]
        wc = jnp.zeros((lora_pad, 3 * rw), F32)
        wc = wc.at[:n_w, :rw].set(w_lora_up[l])
        wc = wc.at[n_w:n_w + n_a, rw:2 * rw].set(a_lora_up[l])
        wc = wc.at[n_w + n_a:n_lora, 2 * rw:].set(g_lora_up[l]).astype(BF16)

        p_r, q, k, v = _inproj(x2d, mix_norm_gain[l][None, :], w_r, w_a, cos_t, sin_t)
        y_rwkv = _rwkv(p_r.reshape(B, S, -1), mix, wc, w0[l][None, :], a0[l][None, :], k_k[l][None, :],
                       k_a[l][None, :], r_k[l].reshape(1, rw), ln_x_w[l][None, :], ln_x_b[l][None, :],
                       rw, (n_w, n_w + n_a, n_lora))
        y_attn = _attention(q.reshape(B, S, aw), k.reshape(B, S, aw), v.reshape(B, S, aw),
                            attn_norm_gain[l][None, :])
        x2d = _mix_ffn(x2d, y_rwkv.reshape(B * S, rw), y_attn.reshape(B * S, aw), w_out[l].astype(BF16),
                       ffn_norm_gain[l][None, :], w_ffn_up[l].astype(BF16), ffn_conv_w[l],
                       ffn_conv_b[l][None, :], w_ffn_down[l].astype(BF16), final_norm_gain[None, :], S,
                       apply_final=(l == depth - 1))
    return x2d.reshape(B, S, D)
```

```python
import functools
import math

import jax
import jax.numpy as jnp
from jax import lax
from jax.experimental import pallas as pl
from jax.experimental.pallas import tpu as pltpu

F32 = jnp.float32
BF16 = jnp.bfloat16

LANES = 128
HEAD_DIM = 64
PAIR = 2 * HEAD_DIM
ROT_DIM = HEAD_DIM // 4
ROPE_THETA = 500000.0
NORM_EPS = 1e-6
GN_EPS = 64e-5
DILATED_PATTERNS = ((128, 1), (512, 4), (2048, 16))
ATTN_BLOCK = 128
CONV_WIDTH = 3
CHUNK = 64
NEG_BIG = -1e30
VMEM_LIMIT = 56 * 1024 * 1024


def _dot(a, b):
    return jnp.dot(a, b, preferred_element_type=F32)


def _dot_nt(a, b):
    return lax.dot_general(a, b, (((1,), (1,)), ((), ())), preferred_element_type=F32)


def _dot_tn(a, b):
    return lax.dot_general(a, b, (((0,), (0,)), ((), ())), preferred_element_type=F32)


def _rmsnorm(x, gain):
    return x * lax.rsqrt(jnp.mean(x * x, axis=-1, keepdims=True) + NORM_EPS) * gain


def _sigmoid(x):
    return 1.0 / (1.0 + jnp.exp(-x))


def _split_dot(x, w):
    hi = x.astype(BF16)
    lo = (x - hi.astype(F32)).astype(BF16)
    return _dot(hi, w) + _dot(lo, w)


def _inproj_kernel(x_ref, g_ref, wr_ref, wa_ref, cos_ref, sin_ref, pr_ref, q_ref, k_ref, v_ref):
    h = _rmsnorm(x_ref[...], g_ref[...]).astype(BF16)
    pr_ref[...] = _dot(h, wr_ref[...])
    p = _dot(h, wa_ref[...])
    aw = q_ref.shape[1]

    tm = x_ref.shape[0]
    lane = lax.broadcasted_iota(jnp.int32, (tm, PAIR), 1)
    first_half = (lane % HEAD_DIM) < (ROT_DIM // 2)
    cos, sin = cos_ref[...], sin_ref[...]

    def rotary(x):
        partner = jnp.where(first_half, pltpu.roll(x, PAIR - ROT_DIM // 2, 1), pltpu.roll(x, ROT_DIM // 2, 1))
        return x * cos + partner * sin

    for t in range(aw // PAIR):
        lo = t * PAIR
        q_ref[:, lo:lo + PAIR] = (rotary(p[:, lo:lo + PAIR]) * (1.0 / math.sqrt(HEAD_DIM))).astype(BF16)
        k_ref[:, lo:lo + PAIR] = rotary(p[:, aw + lo:aw + lo + PAIR]).astype(BF16)
    v_ref[...] = p[:, 2 * aw:].astype(BF16)


def _inproj(x2d, gain, w_r, w_a, cos_t, sin_t, tm=512):
    T, D = x2d.shape
    rw, aw = w_r.shape[1], w_a.shape[1] // 3
    seq_tiles = cos_t.shape[0] // tm
    row = lambda i: (i, 0)
    fixed = lambda i: (0, 0)
    tab = pl.BlockSpec((tm, PAIR), lambda i: (i % seq_tiles, 0))
    return pl.pallas_call(
        _inproj_kernel,
        out_shape=(jax.ShapeDtypeStruct((T, rw), F32),) + (jax.ShapeDtypeStruct((T, aw), BF16),) * 3,
        grid=(T // tm,),
        in_specs=[pl.BlockSpec((tm, D), row), pl.BlockSpec((1, D), fixed), pl.BlockSpec(w_r.shape, fixed),
                  pl.BlockSpec(w_a.shape, fixed), tab, tab],
        out_specs=(pl.BlockSpec((tm, rw), row),) + (pl.BlockSpec((tm, aw), row),) * 3,
        compiler_params=pltpu.CompilerParams(dimension_semantics=("parallel",), vmem_limit_bytes=VMEM_LIMIT),
        name="inproj",
    )(x2d, gain, w_r, w_a, cos_t, sin_t)


def _each(fn, *lists):
    return [fn(*args) for args in zip(*lists)]


def _rwkv_chunk_terms(r, k, v, logw, cum, kkn, bb, same_blk, strict_c, incl_c, lane_lo):
    L = CHUNK
    bf = lambda t: t.astype(BF16)
    cum_last = [t[L - 1:L, :] for t in cum]
    g_in = _each(jnp.exp, cum)
    g_ex = _each(lambda t, w: jnp.exp(t - w), cum, logw)
    g_inv = _each(lambda t: jnp.exp(-t), cum)
    g_hat = _each(lambda tl, t: jnp.exp(tl - t), cum_last, cum)
    g_last = _each(jnp.exp, cum_last)
    yield

    def by_head_rows(x):
        lo = lane_lo if x.shape[1] == PAIR else jnp.concatenate([lane_lo] * (x.shape[1] // PAIR), axis=1)
        return jnp.concatenate([jnp.where(lo, x, 0.0), jnp.where(lo, 0.0, x)], axis=0).astype(BF16)

    al = _each(lambda t, g: -t * g, kkn, g_ex)
    rb = _each(jnp.multiply, r, g_in)
    bt = _each(jnp.multiply, bb, g_inv)
    kt = _each(jnp.multiply, k, g_inv)
    bh = _each(jnp.multiply, bb, g_hat)
    kh = _each(jnp.multiply, k, g_hat)
    lhs = _each(lambda a, b: jnp.concatenate([a, b], axis=0).astype(BF16), al, rb)
    rhs = _each(lambda a, b: jnp.concatenate([by_head_rows(a), by_head_rows(b)], axis=0), bt, kt)
    yield
    aq = _each(_dot_nt, lhs, rhs)
    a_ab = [jnp.where(strict_c, t[:L, :2 * L], 0.0) for t in aq]
    a_ak = [jnp.where(strict_c, t[:L, 2 * L:], 0.0).astype(BF16) for t in aq]
    a_rb = [jnp.where(incl_c, t[L:, :2 * L], 0.0).astype(BF16) for t in aq]
    a_rk = [jnp.where(incl_c, t[L:, 2 * L:], 0.0).astype(BF16) for t in aq]
    yield

    v_s = _each(by_head_rows, v)
    av = _each(_dot, a_ak, v_s)
    x = _each(lambda a, b: jnp.concatenate([a, b], axis=1), al, av)
    yield
    ap = a_ab
    n_lvl = int(math.log2(L))
    for lvl in range(n_lvl):
        apb = _each(bf, ap)
        x = _each(lambda t, a: t + _dot(a, by_head_rows(t)), x, apb)
        if lvl + 1 < n_lvl:
            ap = _each(lambda a, ab: _dot(ab, jnp.where(same_blk, jnp.concatenate([a, a], axis=0), 0.0).astype(BF16)),
                       ap, apb)
        yield

    z = _each(lambda a, t: _dot(a, by_head_rows(t)), a_rb, x)
    akv = _each(_dot, a_rk, v_s)
    w2 = _each(lambda a, t: (a + t[:, :PAIR]).astype(BF16), rb, z)
    y_loc = _each(lambda t, a: t[:, PAIR:] + a, z, akv)
    w1 = [t[:, :PAIR].astype(BF16) for t in x]
    u_loc = [t[:, PAIR:] for t in x]
    yield

    m_t = _each(lambda a, b: jnp.where(same_blk, _dot_tn(a, b.astype(BF16)), 0.0).astype(BF16), w1, bh)
    s_loc = _each(
        lambda u, vv, b, kk_: jnp.where(
            same_blk,
            _dot_tn(jnp.concatenate([u, vv], axis=0).astype(BF16), jnp.concatenate([b, kk_], axis=0).astype(BF16)),
            0.0),
        u_loc, v, bh, kh)
    return w2, y_loc, m_t, s_loc, g_last


def _rwkv_kernel(p_ref, mix_ref, wc_ref, w0_ref, a0_ref, kk_ref, ka_ref, rk_ref, lnw_ref, lnb_ref,
                 o_ref, s_scr, carry_scr, *, nb, width, n_lora, n_chunks, n_groups, stage_offset):
    L = CHUNK
    R = n_chunks * L
    c = pl.program_id(0)

    @pl.when(c == 0)
    def _():
        s_scr[...] = jnp.zeros_like(s_scr)
        carry_scr[...] = jnp.zeros_like(carry_scr)

    n_pairs = width // PAIR
    row = lax.broadcasted_iota(jnp.int32, (R, 1), 0)
    ri = lax.broadcasted_iota(jnp.int32, (2 * L, 2 * L), 0)
    ci = lax.broadcasted_iota(jnp.int32, (2 * L, 2 * L), 1)
    same_blk = (ri >= L) == (ci >= L)
    ti = lax.broadcasted_iota(jnp.int32, (L, 2 * L), 0)
    si = lax.broadcasted_iota(jnp.int32, (L, 2 * L), 1) % L
    strict_c = si < ti
    incl_c = si <= ti
    rr = lax.broadcasted_iota(jnp.int32, (R, R), 0)
    rc = lax.broadcasted_iota(jnp.int32, (R, R), 1)
    tri = ((rr // L == rc // L) & (rr >= rc)).astype(BF16)
    lane_lo = lax.broadcasted_iota(jnp.int32, (L, PAIR), 1) < HEAD_DIM
    lane_lo_r = lax.broadcasted_iota(jnp.int32, (R, PAIR), 1) < HEAD_DIM
    lora_lane = lax.broadcasted_iota(jnp.int32, (R, 2 * LANES), 1)

    def head_sums(x):
        s0 = jnp.sum(jnp.where(lane_lo_r, x, 0.0), axis=-1, keepdims=True)
        s1 = jnp.sum(jnp.where(lane_lo_r, 0.0, x), axis=-1, keepdims=True)
        return jnp.where(lane_lo_r, s0, s1)

    mix = mix_ref[...]

    def lora_up(pm):
        lora = pm[:, 3 * width:]
        act = jnp.where(lora_lane < n_lora[0], jnp.tanh(lora),
                        jnp.where(lora_lane < n_lora[1], lora,
                                  jnp.where(lora_lane < n_lora[2], _sigmoid(lora), 0.0)))
        return _dot(act.astype(BF16), wc_ref[...])

    def log_decay(up):
        t = w0_ref[...] + up[:, :width]
        return -jnp.exp(-(jnp.maximum(-t, 0.0) + jnp.log(1.0 + jnp.exp(-jnp.abs(t)))) - 0.5)

    def running_sum(t):
        h1 = t.astype(BF16)
        r1 = t - h1.astype(F32)
        h2 = r1.astype(BF16)
        h3 = (r1 - h2.astype(F32)).astype(BF16)
        return _dot(tri, h1) + _dot(tri, h2) + _dot(tri, h3)

    tiles = [(b, pr) for b in range(nb) for pr in range(n_pairs)]
    n_t = len(tiles)
    lanes = lambda pr, part=0: slice(part * width + pr * PAIR, part * width + (pr + 1) * PAIR)
    sls = [lanes(pr) for _, pr in tiles]

    states = {0: [s_scr[i] for i in range(n_t)]}
    outs = {}

    def run_group(gi):
        g0 = gi * R
        pms = []
        for b in range(nb):
            p = p_ref[b, g0:g0 + R, :]
            prev_last = carry_scr[b, 7:8, :] if gi == 0 else p_ref[b, g0 - 1:g0, :]
            p_prev = jnp.where(row == 0, prev_last, pltpu.roll(p, 1, 0))
            pms.append(p + (p_prev - p) * mix)
        yield
        ups = _each(lora_up, pms)
        logw_b = _each(log_decay, ups)
        cum_b = _each(running_sum, logw_b)
        yield

        r = [pms[b][:, lanes(pr, 0)] for b, pr in tiles]
        k_raw = [pms[b][:, lanes(pr, 1)] for b, pr in tiles]
        v = [pms[b][:, lanes(pr, 2)] for b, pr in tiles]
        logw = [logw_b[b][:, lanes(pr)] for b, pr in tiles]
        cum = [cum_b[b][:, lanes(pr)] for b, pr in tiles]
        a = [_sigmoid(a0_ref[:, lanes(pr)] + ups[b][:, lanes(pr, 1)]) for b, pr in tiles]
        g = [ups[b][:, lanes(pr, 2)] for b, pr in tiles]

        kk = _each(lambda t, sl: t * kk_ref[:, sl], k_raw, sls)
        kk_ss = _each(lambda t: head_sums(t * t), kk)
        kkn = _each(lambda t, ss: t / jnp.maximum(jnp.sqrt(ss), 1e-12), kk, kk_ss)
        k = _each(lambda t, aa, sl: t * (1.0 + (aa - 1.0) * ka_ref[:, sl]), k_raw, a, sls)
        bb = _each(jnp.multiply, kkn, a)
        bonus_dot = _each(lambda rr_, kk_, sl: head_sums(rr_ * kk_ * rk_ref[:, sl]), r, k, sls)
        yield

        def chunks(ts):
            return [t[ch * L:(ch + 1) * L] for ch in range(n_chunks) for t in ts]

        w2, y_loc, m_t, s_loc, g_last = yield from _rwkv_chunk_terms(
            chunks(r), chunks(k), chunks(v), chunks(logw), chunks(cum), chunks(kkn), chunks(bb),
            same_blk, strict_c, incl_c, lane_lo)
        yield

        s = states[gi]
        y_parts = []
        for ch in range(n_chunks):
            sel = slice(ch * n_t, (ch + 1) * n_t)
            s_b = _each(lambda t: t.astype(BF16), s)
            y_parts.append(_each(lambda a_, sb, yl: _dot_nt(a_, sb) + yl, w2[sel], s_b, y_loc[sel]))
            s = _each(lambda s0, gl, sb, m, sl: s0 * gl + _dot(sb, m) + sl, s, g_last[sel], s_b, m_t[sel], s_loc[sel])
        states[gi + 1] = s
        y = [jnp.concatenate([y_parts[ch][i] for ch in range(n_chunks)], axis=0) for i in range(n_t)]
        yield

        mu = _each(lambda t: head_sums(t) * (1.0 / HEAD_DIM), y)
        yc = _each(jnp.subtract, y, mu)
        var = _each(lambda t: head_sums(t * t) * (1.0 / HEAD_DIM), yc)
        yield
        res = []
        for i in range(n_t):
            sl = sls[i]
            yn = yc[i] * lax.rsqrt(var[i] + GN_EPS) * lnw_ref[:, sl] + lnb_ref[:, sl]
            res.append(((yn + bonus_dot[i] * v[i]) * g[i]).astype(o_ref.dtype))
        outs[gi] = res

    gens = [run_group(gi) for gi in range(n_groups)]
    live = [True] * n_groups
    tick = 0
    while any(live):
        for gi in range(n_groups):
            if live[gi] and tick >= gi * stage_offset:
                try:
                    next(gens[gi])
                except StopIteration:
                    live[gi] = False
        tick += 1

    for gi in range(n_groups):
        for i, (b, pr) in enumerate(tiles):
            o_ref[b, gi * R:(gi + 1) * R, sls[i]] = outs[gi][i]
    for i in range(n_t):
        s_scr[i] = states[n_groups][i]
    for b in range(nb):
        carry_scr[b] = p_ref[b, n_groups * R - 8:n_groups * R, :]


def _rwkv(p_r, mix, wc, w0, a0, k_k, k_a, r_k, ln_w, ln_b, width, n_lora, n_chunks=2, n_groups=2, stage_offset=5):
    B, S, C = p_r.shape
    L = CHUNK * n_chunks * n_groups
    n_state = B * (width // PAIR)
    vec = lambda n: pl.BlockSpec((1, n), lambda c: (0, 0))
    return pl.pallas_call(
        functools.partial(_rwkv_kernel, nb=B, width=width, n_lora=n_lora, n_chunks=n_chunks, n_groups=n_groups,
                          stage_offset=stage_offset),
        out_shape=jax.ShapeDtypeStruct((B, S, width), BF16),
        grid=(S // L,),
        in_specs=[pl.BlockSpec((B, L, C), lambda c: (0, c, 0)), vec(C),
                  pl.BlockSpec(wc.shape, lambda c: (0, 0))] + [vec(width)] * 7,
        out_specs=pl.BlockSpec((B, L, width), lambda c: (0, c, 0)),
        scratch_shapes=[pltpu.VMEM((n_state, PAIR, PAIR), F32), pltpu.VMEM((B, 8, C), F32)],
        compiler_params=pltpu.CompilerParams(dimension_semantics=("arbitrary",), vmem_limit_bytes=VMEM_LIMIT),
        name="rwkv7",
    )(p_r, mix, wc, w0, a0, k_k, k_a, r_k, ln_w, ln_b)


SB_ROWS = 2048
SB_QUARTER = SB_ROWS // 4
N_RES = 4


def _attn_segments(pi, g):
    if pi == 0:
        return [((g // 4) * SB_QUARTER + r * ATTN_BLOCK + (ATTN_BLOCK // N_RES) * (g % 4), ATTN_BLOCK // N_RES, 1)
                for r in range(N_RES)]
    if pi == 1:
        return [((g // 4) * SB_QUARTER + (g % 4) * ATTN_BLOCK, ATTN_BLOCK, 1)]
    return [(qq * SB_QUARTER + (g % 4) * ATTN_BLOCK + g // 4, ATTN_BLOCK // 4, 4) for qq in range(4)]


def _attn_prev_tile(pi, g):
    if pi == 0:
        return (g + 15) % 16, g == 0
    if pi == 1:
        return ((g // 4 + 3) % 4) * 4 + g % 4, g < 4
    return g, True


def _attn_kernel(q_ref, k_ref, v_ref, gain_ref, o_ref,
                 nat_q, nat_k, nat_v, q_scr, k_ring, v_ring, bias_scr, out_nat, *stat_scr, tiles_per_group):
    sb = pl.program_id(2)
    n_pat = len(DILATED_PATTERNS)
    assert DILATED_PATTERNS == ((128, 1), (512, 4), (2048, 16)) and ATTN_BLOCK == 128
    num_scr, m_scr, l_scr = stat_scr[:n_pat], stat_scr[n_pat:2 * n_pat], stat_scr[2 * n_pat:]
    Q = ATTN_BLOCK
    cur_base = (sb % 2) * SB_ROWS
    other_base = SB_ROWS - cur_base

    @pl.when(sb == 0)
    def _():
        k_ring[pl.ds(SB_ROWS, SB_ROWS), :] = jnp.zeros((SB_ROWS, PAIR), F32)
        v_ring[pl.ds(SB_ROWS, SB_ROWS), :] = jnp.zeros((SB_ROWS, PAIR), F32)

    nat_q[...] = q_ref[...].astype(F32)
    nat_k[...] = k_ref[...].astype(F32)
    nat_v[...] = v_ref[...].astype(F32)
    for qq in range(4):
        for r in range(N_RES):
            src = pl.ds(qq * SB_QUARTER + r, Q, stride=N_RES)
            dst = qq * SB_QUARTER + r * Q
            q_scr[dst:dst + Q, :] = nat_q[src, :]
            k_ring[pl.ds(cur_base + dst, Q), :] = nat_k[src, :]
            v_ring[pl.ds(cur_base + dst, Q), :] = nat_v[src, :]

    @pl.when(sb == 0)
    def _():
        ii = lax.broadcasted_iota(jnp.int32, (2 * Q, 2 * Q), 0) % Q
        cj = lax.broadcasted_iota(jnp.int32, (2 * Q, 2 * Q), 1)
        per = Q // N_RES
        for kind in range(2):
            if kind == 0:
                qi = N_RES * (ii % per) + ii // per
                kj = N_RES * ((cj % Q) % per) + (cj % Q) // per + (cj // Q) * Q
            else:
                qi, kj = ii, cj
            band = (kj >= qi) & (kj <= qi + Q)
            bias_scr[kind, 1] = jnp.where(band, 0.0, NEG_BIG)
            bias_scr[kind, 0] = jnp.where(band & (cj >= Q), 0.0, NEG_BIG)

    def load_tile(ref, base, segs):
        parts = [ref[pl.ds(base + s, n, stride=st) if st > 1 else pl.ds(base + s, n), :] for s, n, st in segs]
        return parts[0] if len(parts) == 1 else jnp.concatenate(parts, axis=0)

    def store_tile(ref, segs, val):
        off = 0
        for s, n, st in segs:
            ref[pl.ds(s, n, stride=st) if st > 1 else pl.ds(s, n), :] = val[off:off + n]
            off += n

    lane_lo = lax.broadcasted_iota(jnp.int32, (Q, PAIR), 1) < HEAD_DIM
    ones_blk = jnp.ones((2 * Q, PAIR), BF16)

    n_blk = SB_ROWS // Q
    has_prev_sb = jnp.where(sb > 0, 1, 0)

    def scores(pi, gs):
        segs = [_attn_segments(pi, g) for g in gs]
        prev = [_attn_prev_tile(pi, g) for g in gs]
        prev_segs = [_attn_segments(pi, pg) for pg, _ in prev]
        prev_base = [other_base if other else cur_base for _, other in prev]
        q2 = [load_tile(q_scr, 0, sg) for sg in segs]
        q2 = [jnp.concatenate([jnp.where(lane_lo, t, 0.0), jnp.where(lane_lo, 0.0, t)], axis=0).astype(BF16)
              for t in q2]
        kcat = [jnp.concatenate([load_tile(k_ring, pb, psg), load_tile(k_ring, cur_base, sg)], axis=0).astype(BF16)
                for pb, psg, sg in zip(prev_base, prev_segs, segs)]
        vcat = [jnp.concatenate([load_tile(v_ring, pb, psg), load_tile(v_ring, cur_base, sg)], axis=0).astype(BF16)
                for pb, psg, sg in zip(prev_base, prev_segs, segs)]
        vext = [jnp.concatenate([t, ones_blk], axis=1) for t in vcat]
        has_prev = [has_prev_sb if other else 1 for _, other in prev]
        return dict(pi=pi, segs=segs, s=_each(_dot_nt, q2, kcat), vext=vext, has_prev=has_prev)

    def softmax(c):
        kind = 0 if c["pi"] == 0 else 1
        s = _each(lambda t, hp: t + bias_scr[kind, hp], c["s"], c["has_prev"])
        m = [jnp.max(t, axis=-1, keepdims=True) for t in s]
        p = _each(lambda t, mm: jnp.exp(t - mm).astype(BF16), s, m)
        return dict(pi=c["pi"], segs=c["segs"], vext=c["vext"], m=m, p=p)

    def weighted_values(c):
        pi, segs, m = c["pi"], c["segs"], c["m"]
        nl = _each(_dot, c["p"], c["vext"])
        for t in range(len(segs)):
            store_tile(num_scr[pi], segs[t], jnp.where(lane_lo, nl[t][:Q, :PAIR], nl[t][Q:, :PAIR]))
            store_tile(l_scr[pi], segs[t], jnp.where(lane_lo, nl[t][:Q, PAIR:], nl[t][Q:, PAIR:]))
            store_tile(m_scr[pi], segs[t], jnp.where(lane_lo, m[t][:Q], m[t][Q:]))

    groups = [(pi, list(range(g0, g0 + tiles_per_group)))
              for pi in range(n_pat) for g0 in range(0, n_blk, tiles_per_group)]
    n_grp = len(groups)
    sc = {0: scores(*groups[0])}
    if n_grp > 1:
        sc[1] = scores(*groups[1])
    sm = {0: softmax(sc.pop(0))}
    for k in range(n_grp):
        if k + 2 < n_grp:
            sc[k + 2] = scores(*groups[k + 2])
        if k + 1 < n_grp:
            sm[k + 1] = softmax(sc.pop(k + 1))
        weighted_values(sm.pop(k))

    ri = lax.broadcasted_iota(jnp.int32, (PAIR, PAIR), 0)
    ci = lax.broadcasted_iota(jnp.int32, (PAIR, PAIR), 1)
    seg_ones = ((ri >= HEAD_DIM) == (ci >= HEAD_DIM)).astype(BF16)
    gain = gain_ref[...]

    def merge(i, carry):
        rows = pl.ds(pl.multiple_of(i * Q, Q), Q)
        ms = [m_scr[pi][rows, :] for pi in range(n_pat)]
        m_all = functools.reduce(jnp.maximum, ms)
        num = 0.0
        den = 0.0
        for pi in range(n_pat):
            wgt = jnp.exp(ms[pi] - m_all)
            num = num + wgt * num_scr[pi][rows, :]
            den = den + wgt * l_scr[pi][rows, :]
        o = num / den
        ms_o = _split_dot(o * o, seg_ones) * (1.0 / HEAD_DIM)
        out_nat[pl.ds((i // N_RES) * SB_QUARTER + i % N_RES, Q, stride=N_RES), :] = o * lax.rsqrt(ms_o + NORM_EPS) * gain
        return carry

    lax.fori_loop(0, SB_ROWS // Q, merge, 0, unroll=4)
    o_ref[...] = out_nat[...].astype(o_ref.dtype)


def _attention(q, k, v, gain, tiles_per_group=4):
    B, S, W = q.shape
    n_pairs = W // PAIR
    n_pat = len(DILATED_PATTERNS)
    blk = pl.BlockSpec((None, SB_ROWS, PAIR), lambda b, p, s: (b, s, p))
    tile = pltpu.VMEM((SB_ROWS, PAIR), F32)
    return pl.pallas_call(
        functools.partial(_attn_kernel, tiles_per_group=tiles_per_group),
        out_shape=jax.ShapeDtypeStruct((B, S, W), BF16),
        grid=(B, n_pairs, S // SB_ROWS),
        in_specs=[blk, blk, blk, pl.BlockSpec((1, PAIR), lambda b, p, s: (0, p))],
        out_specs=blk,
        scratch_shapes=[tile, tile, tile, tile,
                        pltpu.VMEM((2 * SB_ROWS, PAIR), F32),
                        pltpu.VMEM((2 * SB_ROWS, PAIR), F32),
                        pltpu.VMEM((2, 2, 2 * ATTN_BLOCK, 2 * ATTN_BLOCK), F32),
                        tile]
                       + [tile] * (3 * n_pat),
        compiler_params=pltpu.CompilerParams(dimension_semantics=("parallel", "parallel", "arbitrary"),
                                             vmem_limit_bytes=VMEM_LIMIT),
        name="dilated_attn",
    )(q, k, v, gain)


def _rotary_tables(seq):
    half = ROT_DIM // 2
    inv_freq = ROPE_THETA ** (-jnp.arange(half, dtype=F32) * 2.0 / ROT_DIM)
    ang = jnp.arange(seq).astype(F32)[:, None] * inv_freq[None, :]
    cos, sin = jnp.cos(ang), jnp.sin(ang)
    rest = HEAD_DIM - ROT_DIM
    cos_h = jnp.concatenate([cos, cos, jnp.ones((seq, rest), F32)], axis=-1)
    sin_h = jnp.concatenate([-sin, sin, jnp.zeros((seq, rest), F32)], axis=-1)
    return jnp.tile(cos_h, (1, PAIR // HEAD_DIM)), jnp.tile(sin_h, (1, PAIR // HEAD_DIM))


FFN_HALO = 16


def _mix_ffn_kernel(x_ref, ya_ref, yb_ref, xh_ref, yah_ref, ybh_ref, wo_ref, g_ref, wu_ref, cw_ref,
                    cb_ref, wd_ref, fg_ref, o_ref, h_scr, hh_scr, *, tm, seq, row_chunks, apply_final):
    tf = wd_ref.shape[0]
    i = pl.program_id(0)
    j = pl.program_id(1)
    wa = ya_ref.shape[1]

    def mixed(x, ya, yb):
        return x + _dot(ya, wo_ref[:wa, :]) + _dot(yb, wo_ref[wa:, :])

    @pl.when(j == 0)
    def _():
        x1 = mixed(x_ref[...], ya_ref[...], yb_ref[...])
        o_ref[...] = x1
        h_scr[...] = _rmsnorm(x1, g_ref[...]).astype(BF16)
        hh_scr[...] = _rmsnorm(mixed(xh_ref[...], yah_ref[...], ybh_ref[...]), g_ref[...]).astype(BF16)

    seq_start = (i * tm) % seq == 0
    rc = tm // row_chunks
    row = lax.broadcasted_iota(jnp.int32, (rc, 1), 0)
    for c in range(row_chunks):
        rows = slice(c * rc, (c + 1) * rc)
        h = h_scr[rows, :]
        gv = _dot(h, wu_ref[...])
        gate, val = gv[:, :tf], gv[:, tf:]
        if c == 0:
            gate_h = jnp.where(seq_start, 0.0, _dot(hh_scr[...], wu_ref[:, :tf]))
        else:
            gate_h = _dot(h_scr[c * rc - FFN_HALO:c * rc, :], wu_ref[:, :tf])
        g1 = jnp.where(row == 0, gate_h[FFN_HALO - 1:FFN_HALO, :], pltpu.roll(gate, 1, 0))
        g2 = jnp.where(row == 0, gate_h[FFN_HALO - 2:FFN_HALO - 1, :],
                       jnp.where(row == 1, gate_h[FFN_HALO - 1:FFN_HALO, :], pltpu.roll(gate, 2, 0)))
        u = cw_ref[0:1, :] * g2 + cw_ref[1:2, :] * g1 + cw_ref[2:3, :] * gate + cb_ref[...]
        act = (u * _sigmoid(u) * val).astype(BF16)
        o_ref[rows, :] += _dot(act, wd_ref[...])

    if apply_final:
        @pl.when(j == pl.num_programs(1) - 1)
        def _():
            o_ref[...] = _rmsnorm(o_ref[...], fg_ref[...])


def _mix_ffn(x2d, y_a, y_b, w_o, gain, w_up, conv_w, conv_b, w_down, final_gain, seq, apply_final,
             tm=1024, n_ff_tiles=2, row_chunks=2):
    T, D = x2d.shape
    F = w_down.shape[0]
    tf = F // n_ff_tiles
    assert tf % LANES == 0 and seq % tm == 0 and w_up.shape[1] == 2 * F
    w_up = jnp.concatenate([w_up[:, half * F + j * tf: half * F + (j + 1) * tf]
                            for j in range(n_ff_tiles) for half in range(2)], axis=1)
    halo_blocks = tm // FFN_HALO
    row = lambda i, j: (i, 0)
    halo = lambda i, j: (jnp.maximum(i * halo_blocks - 1, 0), 0)
    fixed = lambda i, j: (0, 0)
    wa, wb = y_a.shape[1], y_b.shape[1]
    return pl.pallas_call(
        functools.partial(_mix_ffn_kernel, tm=tm, seq=seq, row_chunks=row_chunks, apply_final=apply_final),
        out_shape=jax.ShapeDtypeStruct((T, D), F32),
        grid=(T // tm, n_ff_tiles),
        in_specs=[pl.BlockSpec((tm, D), row), pl.BlockSpec((tm, wa), row), pl.BlockSpec((tm, wb), row),
                  pl.BlockSpec((FFN_HALO, D), halo), pl.BlockSpec((FFN_HALO, wa), halo),
                  pl.BlockSpec((FFN_HALO, wb), halo),
                  pl.BlockSpec(w_o.shape, fixed),
                  pl.BlockSpec((1, D), fixed),
                  pl.BlockSpec((D, 2 * tf), lambda i, j: (0, j)),
                  pl.BlockSpec((CONV_WIDTH, tf), lambda i, j: (0, j)),
                  pl.BlockSpec((1, tf), lambda i, j: (0, j)),
                  pl.BlockSpec((tf, D), lambda i, j: (j, 0)),
                  pl.BlockSpec((1, D), fixed)],
        out_specs=pl.BlockSpec((tm, D), row),
        scratch_shapes=[pltpu.VMEM((tm, D), BF16), pltpu.VMEM((FFN_HALO, D), BF16)],
        compiler_params=pltpu.CompilerParams(dimension_semantics=("parallel", "arbitrary"),
                                             vmem_limit_bytes=VMEM_LIMIT),
        name="mix_convglu_ffn",
    )(x2d, y_a, y_b, x2d, y_a, y_b, w_o, gain, w_up, conv_w, conv_b, w_down, final_gain)


def kernel(x, mix_norm_gain, w_in, rwkv_shift_mix, w0, w_lora_up, a0, a_lora_up, g_lora_up, k_k, k_a, r_k,
           ln_x_w, ln_x_b, attn_norm_gain, w_out, ffn_norm_gain, w_ffn_up, ffn_conv_w, ffn_conv_b,
           w_ffn_down, final_norm_gain):
    B, S, D = x.shape
    depth = w_in.shape[0]
    rw = w0.shape[1]
    aw = attn_norm_gain.shape[1]
    n_w, n_a, n_g = w_lora_up.shape[1], a_lora_up.shape[1], g_lora_up.shape[1]
    n_lora = n_w + n_a + n_g
    lora_pad = -(-n_lora // (2 * LANES)) * (2 * LANES)
    assert lora_pad == 2 * LANES and rw % PAIR == 0 and aw % PAIR == 0
    rwkv_cols = 3 * rw + n_lora
    cos_t, sin_t = _rotary_tables(S)

    x2d = x.reshape(B * S, D)
    for l in range(depth):
        w_r = w_in[l][:, :3 * rw + lora_pad].astype(BF16)
        w_a = w_in[l][:, rwkv_cols:].astype(BF16)
        mix =jnp.concatenate([rwkv_shift_mix[l], jnp.zeros((lora_pad - n_lora,), F32)])[None, :]
        wc = jnp.zeros((lora_pad, 3 * rw), F32)
        wc = wc.at[:n_w, :rw].set(w_lora_up[l])
        wc = wc.at[n_w:n_w + n_a, rw:2 * rw].set(a_lora_up[l])
        wc = wc.at[n_w + n_a:n_lora, 2 * rw:].set(g_lora_up[l]).astype(BF16)

        p_r, q, k, v = _inproj(x2d, mix_norm_gain[l][None, :], w_r, w_a, cos_t, sin_t)
        y_rwkv = _rwkv(p_r.reshape(B, S, -1), mix, wc, w0[l][None, :], a0[l][None, :], k_k[l][None, :],
                       k_a[l][None, :], r_k[l].reshape(1, rw), ln_x_w[l][None, :], ln_x_b[l][None, :],
                       rw, (n_w, n_w + n_a, n_lora))
        y_attn = _attention(q.reshape(B, S, aw), k.reshape(B, S, aw), v.reshape(B, S, aw),
                            attn_norm_gain[l][None, :])
        x2d = _mix_ffn(x2d, y_rwkv.reshape(B * S, rw), y_attn.reshape(B * S, aw), w_out[l].astype(BF16),
                       ffn_norm_gain[l][None, :], w_ffn_up[l].astype(BF16), ffn_conv_w[l],
                       ffn_conv_b[l][None, :], w_ffn_down[l].astype(BF16), final_norm_gain[None, :], S,
                       apply_final=(l == depth - 1))
    return x2d.reshape(B, S, D)
```

```python
import functools
import math

import jax
import jax.numpy as jnp
from jax import lax
from jax.experimental import pallas as pl
from jax.experimental.pallas import tpu as pltpu

F32 = jnp.float32
BF16 = jnp.bfloat16

LANES = 128
HEAD_DIM = 64
PAIR = 2 * HEAD_DIM
ROT_DIM = HEAD_DIM // 4
ROPE_THETA = 500000.0
NORM_EPS = 1e-6
GN_EPS = 64e-5
DILATED_PATTERNS = ((128, 1), (512, 4), (2048, 16))
ATTN_BLOCK = 128
CONV_WIDTH = 3
CHUNK = 64
SOLVE_BLOCK = 16
NEG_BIG = -1e30
VMEM_LIMIT = 56 * 1024 * 1024


def _dot(a, b):
    return jnp.dot(a, b, preferred_element_type=F32)


def _dot_nt(a, b):
    return lax.dot_general(a, b, (((1,), (1,)), ((), ())), preferred_element_type=F32)


def _dot_tn(a, b):
    return lax.dot_general(a, b, (((0,), (0,)), ((), ())), preferred_element_type=F32)


def _rmsnorm(x, gain):
    return x * lax.rsqrt(jnp.mean(x * x, axis=-1, keepdims=True) + NORM_EPS) * gain


def _sigmoid(x):
    return 1.0 / (1.0 + jnp.exp(-x))


def _split_dot(x, w):
    hi = x.astype(BF16)
    lo = (x - hi.astype(F32)).astype(BF16)
    return _dot(hi, w) + _dot(lo, w)


def _inproj_kernel(x_ref, g_ref, wr_ref, wa_ref, cos_ref, sin_ref, pr_ref, q_ref, k_ref, v_ref):
    h = _rmsnorm(x_ref[...], g_ref[...]).astype(BF16)
    pr_ref[...] = _dot(h, wr_ref[...])
    p = _dot(h, wa_ref[...])
    aw = q_ref.shape[1]

    tm = x_ref.shape[0]
    lane = lax.broadcasted_iota(jnp.int32, (tm, PAIR), 1)
    first_half = (lane % HEAD_DIM) < (ROT_DIM // 2)
    cos, sin = cos_ref[...], sin_ref[...]

    def rotary(x):
        partner = jnp.where(first_half, pltpu.roll(x, PAIR - ROT_DIM // 2, 1), pltpu.roll(x, ROT_DIM // 2, 1))
        return x * cos + partner * sin

    for t in range(aw // PAIR):
        lo = t * PAIR
        q_ref[:, lo:lo + PAIR] = (rotary(p[:, lo:lo + PAIR]) * (1.0 / math.sqrt(HEAD_DIM))).astype(BF16)
        k_ref[:, lo:lo + PAIR] = rotary(p[:, aw + lo:aw + lo + PAIR]).astype(BF16)
    v_ref[...] = p[:, 2 * aw:].astype(BF16)


def _inproj(x2d, gain, w_r, w_a, cos_t, sin_t, tm=512):
    T, D = x2d.shape
    rw, aw = w_r.shape[1], w_a.shape[1] // 3
    seq_tiles = cos_t.shape[0] // tm
    row = lambda i: (i, 0)
    fixed = lambda i: (0, 0)
    tab = pl.BlockSpec((tm, PAIR), lambda i: (i % seq_tiles, 0))
    return pl.pallas_call(
        _inproj_kernel,
        out_shape=(jax.ShapeDtypeStruct((T, rw), F32),) + (jax.ShapeDtypeStruct((T, aw), BF16),) * 3,
        grid=(T // tm,),
        in_specs=[pl.BlockSpec((tm, D), row), pl.BlockSpec((1, D), fixed), pl.BlockSpec(w_r.shape, fixed),
                  pl.BlockSpec(w_a.shape, fixed), tab, tab],
        out_specs=(pl.BlockSpec((tm, rw), row),) + (pl.BlockSpec((tm, aw), row),) * 3,
        compiler_params=pltpu.CompilerParams(dimension_semantics=("parallel",), vmem_limit_bytes=VMEM_LIMIT),
        name="inproj",
    )(x2d, gain, w_r, w_a, cos_t, sin_t)


def _each(fn, *lists):
    return [fn(*args) for args in zip(*lists)]


def _rwkv_chunk_terms(r, k, v, logw, cum, kkn, bb, same_blk, strict_c, incl_c, lane_lo):
    L = CHUNK
    bf = lambda t: t.astype(BF16)
    cum_last = [t[L - 1:L, :] for t in cum]
    g_in = _each(jnp.exp, cum)
    g_ex = _each(lambda t, w: jnp.exp(t - w), cum, logw)
    g_inv = _each(lambda t: jnp.exp(-t), cum)
    g_hat = _each(lambda tl, t: jnp.exp(tl - t), cum_last, cum)
    g_last = _each(jnp.exp, cum_last)
    yield

    def by_head_rows(x):
        lo = lane_lo if x.shape[1] == PAIR else jnp.concatenate([lane_lo] * (x.shape[1] // PAIR), axis=1)
        return jnp.concatenate([jnp.where(lo, x, 0.0), jnp.where(lo, 0.0, x)], axis=0).astype(BF16)

    al = _each(lambda t, g: -t * g, kkn, g_ex)
    rb = _each(jnp.multiply, r, g_in)
    bt = _each(jnp.multiply, bb, g_inv)
    kt = _each(jnp.multiply, k, g_inv)
    bh = _each(jnp.multiply, bb, g_hat)
    kh = _each(jnp.multiply, k, g_hat)
    lhs = _each(lambda a, b: jnp.concatenate([a, b], axis=0).astype(BF16), al, rb)
    rhs = _each(lambda a, b: jnp.concatenate([by_head_rows(a), by_head_rows(b)], axis=0), bt, kt)
    yield
    aq = _each(_dot_nt, lhs, rhs)
    a_ab = [jnp.where(strict_c, t[:L, :2 * L], 0.0) for t in aq]
    a_ak = [jnp.where(strict_c, t[:L, 2 * L:], 0.0).astype(BF16) for t in aq]
    a_rb = [jnp.where(incl_c, t[L:, :2 * L], 0.0).astype(BF16) for t in aq]
    a_rk = [jnp.where(incl_c, t[L:, 2 * L:], 0.0).astype(BF16) for t in aq]
    yield

    v_s = _each(by_head_rows, v)
    av = _each(_dot, a_ak, v_s)
    BS = SOLVE_BLOCK
    n_blk = L // BS
    rhs0 = _each(lambda a, b: jnp.concatenate([a, b], axis=1), al, av)
    lo2 = lax.broadcasted_iota(jnp.int32, (BS, 2 * PAIR), 1) % PAIR < HEAD_DIM
    lane = lax.broadcasted_iota(jnp.int32, (BS, PAIR), 1)
    br =lax.broadcasted_iota(jnp.int32, (PAIR, PAIR), 0)
    bc = lax.broadcasted_iota(jnp.int32, (PAIR, PAIR), 1)
    blk_diag = (br // BS) == (bc // BS)

    def expand(pack):
        return jnp.where(blk_diag, jnp.concatenate([pack] * (PAIR // BS), axis=0), 0.0).astype(BF16)

    d_k = [functools.reduce(jnp.add, [jnp.where((lane % HEAD_DIM) // BS == b, t[b * BS:(b + 1) * BS, :], 0.0)
                                      for b in range(n_blk)]) for t in a_ab]
    t_m = d_k
    yield
    n_sq = int(math.log2(BS)) - 1
    d_k = _each(lambda d: _dot(d.astype(BF16), expand(d)), d_k)
    yield
    for lvl in range(n_sq):
        d_e = _each(expand, d_k)
        t_m = _each(lambda t, d, de: t + d + _dot(t.astype(BF16), de), t_m, d_k, d_e)
        if lvl + 1 < n_sq:
            d_k = _each(lambda d, de: _dot(d.astype(BF16), de), d_k, d_e)
        yield
    t_mb = _each(bf, t_m)

    zero_blk = jnp.zeros((BS, 2 * PAIR), BF16)
    x_blocks = [[] for _ in a_ab]
    xs_lo = [[] for _ in a_ab]
    xs_hi = [[] for _ in a_ab]

    def stacked(lo_parts, hi_parts):
        rows = [lo_parts.get(c, zero_blk) for c in range(n_blk)] + [hi_parts.get(c, zero_blk) for c in range(n_blk)]
        return jnp.concatenate(rows, axis=0)

    for b in range(n_blk):
        r_b = [t[b * BS:(b + 1) * BS, :] for t in rhs0]
        if b > 0:
            r_b = _each(lambda rr, a, lo_p, hi_p: rr + _dot(a[b * BS:(b + 1) * BS, :].astype(BF16),
                                                            stacked(dict(enumerate(lo_p)), dict(enumerate(hi_p)))),
                        r_b, a_ab, xs_lo, xs_hi)
            yield
        x_b = _each(lambda rr, tm: rr + _dot(tm, stacked({b: jnp.where(lo2, rr, 0.0).astype(BF16)},
                                                        {b: jnp.where(lo2, 0.0, rr).astype(BF16)})),
                    r_b, t_mb)
        for i, xb in enumerate(x_b):
            x_blocks[i].append(xb)
            xs_lo[i].append(jnp.where(lo2, xb, 0.0).astype(BF16))
            xs_hi[i].append(jnp.where(lo2, 0.0, xb).astype(BF16))
        yield
    x = [jnp.concatenate(blks, axis=0) for blks in x_blocks]
    x_s = _each(lambda lo_p, hi_p: jnp.concatenate(lo_p + hi_p, axis=0), xs_lo, xs_hi)

    z = _each(_dot, a_rb, x_s)
    akv = _each(_dot, a_rk, v_s)
    w2 = _each(lambda a, t: (a + t[:, :PAIR]).astype(BF16), rb, z)
    y_loc = _each(lambda t, a: t[:, PAIR:] + a, z, akv)
    w1 = [t[:, :PAIR].astype(BF16) for t in x]
    u_loc = [t[:, PAIR:] for t in x]
    yield

    m_t = _each(lambda a, b: jnp.where(same_blk, _dot_tn(a, b.astype(BF16)), 0.0).astype(BF16), w1, bh)
    s_loc = _each(
        lambda u, vv, b, kk_: jnp.where(
            same_blk,
            _dot_tn(jnp.concatenate([u, vv], axis=0).astype(BF16), jnp.concatenate([b, kk_], axis=0).astype(BF16)),
            0.0),
        u_loc, v, bh, kh)
    return w2, y_loc, m_t, s_loc, g_last


def _rwkv_kernel(p_ref, mix_ref, wc_ref, w0_ref, a0_ref, kk_ref, ka_ref, rk_ref, lnw_ref, lnb_ref,
                 o_ref, s_scr, carry_scr, *, nb, width, n_lora, n_chunks, n_groups, stage_offset):
    L = CHUNK
    R = n_chunks * L
    c = pl.program_id(0)

    @pl.when(c == 0)
    def _():
        s_scr[...] = jnp.zeros_like(s_scr)
        carry_scr[...] = jnp.zeros_like(carry_scr)

    n_pairs = width // PAIR
    row = lax.broadcasted_iota(jnp.int32, (R, 1), 0)
    ri = lax.broadcasted_iota(jnp.int32, (2 * L, 2 * L), 0)
    ci = lax.broadcasted_iota(jnp.int32, (2 * L, 2 * L), 1)
    same_blk = (ri >= L) == (ci >= L)
    ti = lax.broadcasted_iota(jnp.int32, (L, 2 * L), 0)
    si = lax.broadcasted_iota(jnp.int32, (L, 2 * L), 1) % L
    strict_c = si < ti
    incl_c = si <= ti
    rr = lax.broadcasted_iota(jnp.int32, (R, R), 0)
    rc = lax.broadcasted_iota(jnp.int32, (R, R), 1)
    tri = ((rr // L == rc // L) & (rr >= rc)).astype(BF16)
    lane_lo = lax.broadcasted_iota(jnp.int32, (L, PAIR), 1) < HEAD_DIM
    lane_lo_r = lax.broadcasted_iota(jnp.int32, (R, PAIR), 1) < HEAD_DIM
    lora_lane = lax.broadcasted_iota(jnp.int32, (R, 2 * LANES), 1)

    def head_sums(x):
        s0 = jnp.sum(jnp.where(lane_lo_r, x, 0.0), axis=-1, keepdims=True)
        s1 = jnp.sum(jnp.where(lane_lo_r, 0.0, x), axis=-1, keepdims=True)
        return jnp.where(lane_lo_r, s0, s1)

    mix = mix_ref[...]

    def lora_up(pm):
        lora = pm[:, 3 * width:]
        act = jnp.where(lora_lane < n_lora[0], jnp.tanh(lora),
                        jnp.where(lora_lane < n_lora[1], lora,
                                  jnp.where(lora_lane < n_lora[2], _sigmoid(lora), 0.0)))
        return _dot(act.astype(BF16), wc_ref[...])

    def log_decay(up):
        t = w0_ref[...] + up[:, :width]
        return -jnp.exp(-(jnp.maximum(-t, 0.0) + jnp.log(1.0 + jnp.exp(-jnp.abs(t)))) - 0.5)

    def running_sum(t):
        h1 = t.astype(BF16)
        r1 = t - h1.astype(F32)
        h2 = r1.astype(BF16)
        h3 = (r1 - h2.astype(F32)).astype(BF16)
        return _dot(tri, h1) + _dot(tri, h2) + _dot(tri, h3)

    tiles = [(b, pr) for b in range(nb) for pr in range(n_pairs)]
    n_t = len(tiles)
    lanes = lambda pr, part=0: slice(part * width + pr * PAIR, part * width + (pr + 1) * PAIR)
    sls = [lanes(pr) for _, pr in tiles]

    states = {0: [s_scr[i] for i in range(n_t)]}
    outs = {}

    def run_group(gi):
        g0 = gi * R
        pms = []
        for b in range(nb):
            p = p_ref[b, g0:g0 + R, :]
            prev_last = carry_scr[b, 7:8, :] if gi == 0 else p_ref[b, g0 - 1:g0, :]
            p_prev = jnp.where(row == 0, prev_last, pltpu.roll(p, 1, 0))
            pms.append(p + (p_prev - p) * mix)
        yield
        ups = _each(lora_up, pms)
        logw_b = _each(log_decay, ups)
        cum_b = _each(running_sum, logw_b)
        yield

        r = [pms[b][:, lanes(pr, 0)] for b, pr in tiles]
        k_raw = [pms[b][:, lanes(pr, 1)] for b, pr in tiles]
        v = [pms[b][:, lanes(pr, 2)] for b, pr in tiles]
        logw = [logw_b[b][:, lanes(pr)] for b, pr in tiles]
        cum = [cum_b[b][:, lanes(pr)] for b, pr in tiles]
        a = [_sigmoid(a0_ref[:, lanes(pr)] + ups[b][:, lanes(pr, 1)]) for b, pr in tiles]
        g = [ups[b][:, lanes(pr, 2)] for b, pr in tiles]

        kk = _each(lambda t, sl: t * kk_ref[:, sl], k_raw, sls)
        kk_ss = _each(lambda t: head_sums(t * t), kk)
        kkn = _each(lambda t, ss: t / jnp.maximum(jnp.sqrt(ss), 1e-12), kk, kk_ss)
        k = _each(lambda t, aa, sl: t * (1.0 + (aa - 1.0) * ka_ref[:, sl]), k_raw, a, sls)
        bb = _each(jnp.multiply, kkn, a)
        bonus_dot = _each(lambda rr_, kk_, sl: head_sums(rr_ * kk_ * rk_ref[:, sl]), r, k, sls)
        yield

        def chunks(ts):
            return [t[ch * L:(ch + 1) * L] for ch in range(n_chunks) for t in ts]

        w2, y_loc, m_t, s_loc, g_last = yield from _rwkv_chunk_terms(
            chunks(r), chunks(k), chunks(v), chunks(logw), chunks(cum), chunks(kkn), chunks(bb),
            same_blk, strict_c, incl_c, lane_lo)
        yield

        s = states[gi]
        y_parts = []
        for ch in range(n_chunks):
            sel = slice(ch * n_t, (ch + 1) * n_t)
            s_b = _each(lambda t: t.astype(BF16), s)
            y_parts.append(_each(lambda a_, sb, yl: _dot_nt(a_, sb) + yl, w2[sel], s_b, y_loc[sel]))
            s = _each(lambda s0, gl, sb, m, sl: s0 * gl + _dot(sb, m) + sl, s, g_last[sel], s_b, m_t[sel], s_loc[sel])
        states[gi + 1] = s
        y = [jnp.concatenate([y_parts[ch][i] for ch in range(n_chunks)], axis=0) for i in range(n_t)]
        yield

        mu = _each(lambda t: head_sums(t) * (1.0 / HEAD_DIM), y)
        yc = _each(jnp.subtract, y, mu)
        var = _each(lambda t: head_sums(t * t) * (1.0 / HEAD_DIM), yc)
        yield
        res = []
        for i in range(n_t):
            sl = sls[i]
            yn = yc[i] * lax.rsqrt(var[i] + GN_EPS) * lnw_ref[:, sl] + lnb_ref[:, sl]
            res.append(((yn + bonus_dot[i] * v[i]) * g[i]).astype(o_ref.dtype))
        outs[gi] = res

    gens = [run_group(gi) for gi in range(n_groups)]
    live = [True] * n_groups
    tick = 0
    while any(live):
        for gi in range(n_groups):
            if live[gi] and tick >= gi * stage_offset:
                try:
                    next(gens[gi])
                except StopIteration:
                    live[gi] = False
        tick += 1

    for gi in range(n_groups):
        for i, (b, pr) in enumerate(tiles):
            o_ref[b, gi * R:(gi + 1) * R, sls[i]] = outs[gi][i]
    for i in range(n_t):
        s_scr[i] = states[n_groups][i]
    for b in range(nb):
        carry_scr[b] = p_ref[b, n_groups * R - 8:n_groups * R, :]


def _rwkv(p_r, mix, wc, w0, a0, k_k, k_a, r_k, ln_w, ln_b, width, n_lora, n_chunks=2, n_groups=4, stage_offset=5):
    B, S, C = p_r.shape
    L = CHUNK * n_chunks * n_groups
    n_state = B * (width // PAIR)
    vec = lambda n: pl.BlockSpec((1, n), lambda c: (0, 0))
    return pl.pallas_call(
        functools.partial(_rwkv_kernel, nb=B, width=width, n_lora=n_lora, n_chunks=n_chunks, n_groups=n_groups,
                          stage_offset=stage_offset),
        out_shape=jax.ShapeDtypeStruct((B, S, width), BF16),
        grid=(S // L,),
        in_specs=[pl.BlockSpec((B, L, C), lambda c: (0, c, 0)), vec(C),
                  pl.BlockSpec(wc.shape, lambda c: (0, 0))] + [vec(width)] * 7,
        out_specs=pl.BlockSpec((B, L, width), lambda c: (0, c, 0)),
        scratch_shapes=[pltpu.VMEM((n_state, PAIR, PAIR), F32), pltpu.VMEM((B, 8, C), F32)],
        compiler_params=pltpu.CompilerParams(dimension_semantics=("arbitrary",), vmem_limit_bytes=VMEM_LIMIT),
        name="rwkv7",
    )(p_r, mix, wc, w0, a0, k_k, k_a, r_k, ln_w, ln_b)


SB_ROWS = 2048
SB_QUARTER = SB_ROWS // 4
N_RES = 4


def _attn_segments(pi, g):
    if pi == 0:
        return [((g // 4) * SB_QUARTER + r * ATTN_BLOCK + (ATTN_BLOCK // N_RES) * (g % 4), ATTN_BLOCK // N_RES, 1)
                for r in range(N_RES)]
    if pi == 1:
        return [((g // 4) * SB_QUARTER + (g % 4) * ATTN_BLOCK, ATTN_BLOCK, 1)]
    return [(qq * SB_QUARTER + (g % 4) * ATTN_BLOCK + g // 4, ATTN_BLOCK // 4, 4) for qq in range(4)]


def _attn_prev_tile(pi, g):
    if pi == 0:
        return (g + 15) % 16, g == 0
    if pi == 1:
        return ((g // 4 + 3) % 4) * 4 + g % 4, g < 4
    return g, True


def _attn_kernel(q_ref, k_ref, v_ref, gain_ref, o_ref,
                 nat_q, nat_k, nat_v, q_scr, k_ring, v_ring, bias_scr, out_nat, *stat_scr, tiles_per_group):
    sb = pl.program_id(2)
    n_pat = len(DILATED_PATTERNS)
    assert DILATED_PATTERNS == ((128, 1), (512, 4), (2048, 16)) and ATTN_BLOCK == 128
    num_scr, m_scr, l_scr = stat_scr[:n_pat], stat_scr[n_pat:2 * n_pat], stat_scr[2 * n_pat:]
    Q = ATTN_BLOCK
    cur_base = (sb % 2) * SB_ROWS
    other_base = SB_ROWS - cur_base

    @pl.when(sb == 0)
    def _():
        k_ring[pl.ds(SB_ROWS, SB_ROWS), :] = jnp.zeros((SB_ROWS, PAIR), F32)
        v_ring[pl.ds(SB_ROWS, SB_ROWS), :] = jnp.zeros((SB_ROWS, PAIR), F32)

    nat_q[...] = q_ref[...].astype(F32)
    nat_k[...] = k_ref[...].astype(F32)
    nat_v[...] = v_ref[...].astype(F32)
    for qq in range(4):
        for r in range(N_RES):
            src = pl.ds(qq * SB_QUARTER + r, Q, stride=N_RES)
            dst = qq * SB_QUARTER + r * Q
            q_scr[dst:dst + Q, :] = nat_q[src, :]
            k_ring[pl.ds(cur_base + dst, Q), :] = nat_k[src, :]
            v_ring[pl.ds(cur_base + dst, Q), :] = nat_v[src, :]

    @pl.when(sb == 0)
    def _():
        ii = lax.broadcasted_iota(jnp.int32, (2 * Q, 2 * Q), 0) % Q
        cj = lax.broadcasted_iota(jnp.int32, (2 * Q, 2 * Q), 1)
        per = Q // N_RES
        for kind in range(2):
            if kind == 0:
                qi = N_RES * (ii % per) + ii // per
                kj = N_RES * ((cj % Q) % per) + (cj % Q) // per + (cj // Q) * Q
            else:
                qi, kj = ii, cj
            band = (kj >= qi) & (kj <= qi + Q)
            bias_scr[kind, 1] = jnp.where(band, 0.0, NEG_BIG)
            bias_scr[kind, 0] = jnp.where(band & (cj >= Q), 0.0, NEG_BIG)

    def load_tile(ref, base, segs):
        parts = [ref[pl.ds(base + s, n, stride=st) if st > 1 else pl.ds(base + s, n), :] for s, n, st in segs]
        return parts[0] if len(parts) == 1 else jnp.concatenate(parts, axis=0)

    def store_tile(ref, segs, val):
        off = 0
        for s, n, st in segs:
            ref[pl.ds(s, n, stride=st) if st > 1 else pl.ds(s, n), :] = val[off:off + n]
            off += n

    lane_lo = lax.broadcasted_iota(jnp.int32, (Q, PAIR), 1) < HEAD_DIM
    ones_blk = jnp.ones((2 * Q, PAIR), BF16)

    n_blk = SB_ROWS // Q
    has_prev_sb = jnp.where(sb > 0, 1, 0)

    def scores(pi, gs):
        segs = [_attn_segments(pi, g) for g in gs]
        prev = [_attn_prev_tile(pi, g) for g in gs]
        prev_segs = [_attn_segments(pi, pg) for pg, _ in prev]
        prev_base = [other_base if other else cur_base for _, other in prev]
        q2 = [load_tile(q_scr, 0, sg) for sg in segs]
        q2 = [jnp.concatenate([jnp.where(lane_lo, t, 0.0), jnp.where(lane_lo, 0.0, t)], axis=0).astype(BF16)
              for t in q2]
        kcat = [jnp.concatenate([load_tile(k_ring, pb, psg), load_tile(k_ring, cur_base, sg)], axis=0).astype(BF16)
                for pb, psg, sg in zip(prev_base, prev_segs, segs)]
        vcat = [jnp.concatenate([load_tile(v_ring, pb, psg), load_tile(v_ring, cur_base, sg)], axis=0).astype(BF16)
                for pb, psg, sg in zip(prev_base, prev_segs, segs)]
        vext = [jnp.concatenate([t, ones_blk], axis=1) for t in vcat]
        has_prev = [has_prev_sb if other else 1 for _, other in prev]
        return dict(pi=pi, segs=segs, s=_each(_dot_nt, q2, kcat), vext=vext, has_prev=has_prev)

    def softmax(c):
        kind = 0 if c["pi"] == 0 else 1
        s = _each(lambda t, hp: t + bias_scr[kind, hp], c["s"], c["has_prev"])
        m = [jnp.max(t, axis=-1, keepdims=True) for t in s]
        p = _each(lambda t, mm: jnp.exp(t - mm).astype(BF16), s, m)
        return dict(pi=c["pi"], segs=c["segs"], vext=c["vext"], m=m, p=p)

    def weighted_values(c):
        pi, segs, m = c["pi"], c["segs"], c["m"]
        nl = _each(_dot, c["p"], c["vext"])
        for t in range(len(segs)):
            store_tile(num_scr[pi], segs[t], jnp.where(lane_lo, nl[t][:Q, :PAIR], nl[t][Q:, :PAIR]))
            store_tile(l_scr[pi], segs[t], jnp.where(lane_lo, nl[t][:Q, PAIR:], nl[t][Q:, PAIR:]))
            store_tile(m_scr[pi], segs[t], jnp.where(lane_lo, m[t][:Q], m[t][Q:]))

    groups = [(pi, list(range(g0, g0 + tiles_per_group)))
              for pi in range(n_pat) for g0 in range(0, n_blk, tiles_per_group)]
    n_grp = len(groups)
    sc = {0: scores(*groups[0])}
    if n_grp > 1:
        sc[1] = scores(*groups[1])
    sm = {0: softmax(sc.pop(0))}
    for k in range(n_grp):
        if k + 2 < n_grp:
            sc[k + 2] = scores(*groups[k + 2])
        if k + 1 < n_grp:
            sm[k + 1] = softmax(sc.pop(k + 1))
        weighted_values(sm.pop(k))

    ri = lax.broadcasted_iota(jnp.int32, (PAIR, PAIR), 0)
    ci = lax.broadcasted_iota(jnp.int32, (PAIR, PAIR), 1)
    seg_ones = ((ri >= HEAD_DIM) == (ci >= HEAD_DIM)).astype(BF16)
    gain = gain_ref[...]

    def merge(i, carry):
        rows = pl.ds(pl.multiple_of(i * Q, Q), Q)
        ms = [m_scr[pi][rows, :] for pi in range(n_pat)]
        m_all = functools.reduce(jnp.maximum, ms)
        num = 0.0
        den = 0.0
        for pi in range(n_pat):
            wgt = jnp.exp(ms[pi] - m_all)
            num = num + wgt * num_scr[pi][rows, :]
            den = den + wgt * l_scr[pi][rows, :]
        o = num / den
        ms_o = _split_dot(o * o, seg_ones) * (1.0 / HEAD_DIM)
        out_nat[pl.ds((i // N_RES) * SB_QUARTER + i % N_RES, Q, stride=N_RES), :] = o * lax.rsqrt(ms_o + NORM_EPS) * gain
        return carry

    lax.fori_loop(0, SB_ROWS // Q, merge, 0, unroll=4)
    o_ref[...] = out_nat[...].astype(o_ref.dtype)


def _attention(q, k, v, gain, tiles_per_group=4):
    B, S, W = q.shape
    n_pairs = W // PAIR
    n_pat = len(DILATED_PATTERNS)
    blk = pl.BlockSpec((None, SB_ROWS, PAIR), lambda b, p, s: (b, s, p))
    tile = pltpu.VMEM((SB_ROWS, PAIR), F32)
    return pl.pallas_call(
        functools.partial(_attn_kernel, tiles_per_group=tiles_per_group),
        out_shape=jax.ShapeDtypeStruct((B, S, W), BF16),
        grid=(B, n_pairs, S // SB_ROWS),
        in_specs=[blk, blk, blk, pl.BlockSpec((1, PAIR), lambda b, p, s: (0, p))],
        out_specs=blk,
        scratch_shapes=[tile, tile, tile, tile,
                        pltpu.VMEM((2 * SB_ROWS, PAIR), F32),
                        pltpu.VMEM((2 * SB_ROWS, PAIR), F32),
                        pltpu.VMEM((2, 2, 2 * ATTN_BLOCK, 2 * ATTN_BLOCK), F32),
                        tile]
                       + [tile] * (3 * n_pat),
        compiler_params=pltpu.CompilerParams(dimension_semantics=("parallel", "parallel", "arbitrary"),
                                             vmem_limit_bytes=VMEM_LIMIT),
        name="dilated_attn",
    )(q, k, v, gain)


def _rotary_tables(seq):
    half = ROT_DIM // 2
    inv_freq = ROPE_THETA ** (-jnp.arange(half, dtype=F32) * 2.0 / ROT_DIM)
    ang = jnp.arange(seq).astype(F32)[:, None] * inv_freq[None, :]
    cos, sin = jnp.cos(ang), jnp.sin(ang)
    rest = HEAD_DIM - ROT_DIM
    cos_h = jnp.concatenate([cos, cos, jnp.ones((seq, rest), F32)], axis=-1)
    sin_h = jnp.concatenate([-sin, sin, jnp.zeros((seq, rest), F32)], axis=-1)
    return jnp.tile(cos_h, (1, PAIR // HEAD_DIM)), jnp.tile(sin_h, (1, PAIR // HEAD_DIM))


FFN_HALO = 16


def _mix_ffn_kernel(x_ref, ya_ref, yb_ref, xh_ref, yah_ref, ybh_ref, wo_ref, g_ref, wu_ref, cw_ref,
                    cb_ref, wd_ref, fg_ref, o_ref, h_scr, hh_scr, *, tm, seq, row_chunks, apply_final):
    tf = wd_ref.shape[0]
    i = pl.program_id(0)
    j = pl.program_id(1)
    wa = ya_ref.shape[1]

    def mixed(x, ya, yb):
        return x + _dot(ya, wo_ref[:wa, :]) + _dot(yb, wo_ref[wa:, :])

    @pl.when(j == 0)
    def _():
        x1 = mixed(x_ref[...], ya_ref[...], yb_ref[...])
        o_ref[...] = x1
        h_scr[...] = _rmsnorm(x1, g_ref[...]).astype(BF16)
        hh_scr[...] = _rmsnorm(mixed(xh_ref[...], yah_ref[...], ybh_ref[...]), g_ref[...]).astype(BF16)

    seq_start = (i * tm) % seq == 0
    rc = tm // row_chunks
    row = lax.broadcasted_iota(jnp.int32, (rc, 1), 0)
    for c in range(row_chunks):
        rows = slice(c * rc, (c + 1) * rc)
        h = h_scr[rows, :]
        gv = _dot(h, wu_ref[...])
        gate, val = gv[:, :tf], gv[:, tf:]
        if c == 0:
            gate_h = jnp.where(seq_start, 0.0, _dot(hh_scr[...], wu_ref[:, :tf]))
        else:
            gate_h = _dot(h_scr[c * rc - FFN_HALO:c * rc, :], wu_ref[:, :tf])
        g1 = jnp.where(row == 0, gate_h[FFN_HALO - 1:FFN_HALO, :], pltpu.roll(gate, 1, 0))
        g2 = jnp.where(row == 0, gate_h[FFN_HALO - 2:FFN_HALO - 1, :],
                       jnp.where(row == 1, gate_h[FFN_HALO - 1:FFN_HALO, :], pltpu.roll(gate, 2, 0)))
        u = cw_ref[0:1, :] * g2 + cw_ref[1:2, :] * g1 + cw_ref[2:3, :] * gate + cb_ref[...]
        act = (u * _sigmoid(u) * val).astype(BF16)
        o_ref[rows, :] += _dot(act, wd_ref[...])

    if apply_final:
        @pl.when(j == pl.num_programs(1) - 1)
        def _():
            o_ref[...] = _rmsnorm(o_ref[...], fg_ref[...])


def _mix_ffn(x2d, y_a, y_b, w_o, gain, w_up, conv_w, conv_b, w_down, final_gain, seq, apply_final,
             tm=1024, n_ff_tiles=2, row_chunks=2):
    T, D = x2d.shape
    F = w_down.shape[0]
    tf = F // n_ff_tiles
    assert tf % LANES == 0 and seq % tm == 0 and w_up.shape[1] == 2 * F
    w_up = jnp.concatenate([w_up[:, half * F + j * tf: half * F + (j + 1) * tf]
                            for j in range(n_ff_tiles) for half in range(2)], axis=1)
    halo_blocks = tm // FFN_HALO
    row = lambda i, j: (i, 0)
    halo = lambda i, j: (jnp.maximum(i * halo_blocks - 1, 0), 0)
    fixed = lambda i, j: (0, 0)
    wa, wb = y_a.shape[1], y_b.shape[1]
    return pl.pallas_call(
        functools.partial(_mix_ffn_kernel, tm=tm, seq=seq, row_chunks=row_chunks, apply_final=apply_final),
        out_shape=jax.ShapeDtypeStruct((T, D), F32),
        grid=(T // tm, n_ff_tiles),
        in_specs=[pl.BlockSpec((tm, D), row), pl.BlockSpec((tm, wa), row), pl.BlockSpec((tm, wb), row),
                  pl.BlockSpec((FFN_HALO, D), halo), pl.BlockSpec((FFN_HALO, wa), halo),
                  pl.BlockSpec((FFN_HALO, wb), halo),
                  pl.BlockSpec(w_o.shape, fixed),
                  pl.BlockSpec((1, D), fixed),
                  pl.BlockSpec((D, 2 * tf), lambda i, j: (0, j)),
                  pl.BlockSpec((CONV_WIDTH, tf), lambda i, j: (0, j)),
                  pl.BlockSpec((1, tf), lambda i, j: (0, j)),
                  pl.BlockSpec((tf, D), lambda i, j: (j, 0)),
                  pl.BlockSpec((1, D), fixed)],
        out_specs=pl.BlockSpec((tm, D), row),
        scratch_shapes=[pltpu.VMEM((tm, D), BF16), pltpu.VMEM((FFN_HALO, D), BF16)],
        compiler_params=pltpu.CompilerParams(dimension_semantics=("parallel", "arbitrary"),
                                             vmem_limit_bytes=VMEM_LIMIT),
        name="mix_convglu_ffn",
    )(x2d, y_a, y_b, x2d, y_a, y_b, w_o, gain, w_up, conv_w, conv_b, w_down, final_gain)


def kernel(x, mix_norm_gain, w_in, rwkv_shift_mix, w0, w_lora_up, a0, a_lora_up, g_lora_up, k_k, k_a, r_k,
           ln_x_w, ln_x_b, attn_norm_gain, w_out, ffn_norm_gain, w_ffn_up, ffn_conv_w, ffn_conv_b,
           w_ffn_down, final_norm_gain):
    B, S, D = x.shape
    depth = w_in.shape[0]
    rw = w0.shape[1]
    aw = attn_norm_gain.shape[1]
    n_w, n_a, n_g = w_lora_up.shape[1], a_lora_up.shape[1], g_lora_up.shape[1]
    n_lora = n_w + n_a + n_g
    lora_pad = -(-n_lora // (2 * LANES)) * (2 * LANES)
    assert lora_pad == 2 * LANES and rw % PAIR == 0 and aw % PAIR == 0
    rwkv_cols = 3 * rw + n_lora
    cos_t, sin_t = _rotary_tables(S)

    x2d = x.reshape(B * S, D)
    for l in range(depth):
        w_r = w_in[l][:, :3 * rw + lora_pad].astype(BF16)
        w_a = w_in[l][:, rwkv_cols:].astype(BF16)
        mix =jnp.concatenate([rwkv_shift_mix[l], jnp.zeros((lora_pad - n_lora,), F32)])[None, :]
        wc = jnp.zeros((lora_pad, 3 * rw), F32)
        wc = wc.at[:n_w, :rw].set(w_lora_up[l])
        wc = wc.at[n_w:n_w + n_a, rw:2 * rw].set(a_lora_up[l])
        wc = wc.at[n_w + n_a:n_lora, 2 * rw:].set(g_lora_up[l]).astype(BF16)

        p_r, q, k, v = _inproj(x2d, mix_norm_gain[l][None, :], w_r, w_a, cos_t, sin_t)
        y_rwkv = _rwkv(p_r.reshape(B, S, -1), mix, wc, w0[l][None, :], a0[l][None, :], k_k[l][None, :],
                       k_a[l][None, :], r_k[l].reshape(1, rw), ln_x_w[l][None, :], ln_x_b[l][None, :],
                       rw, (n_w, n_w + n_a, n_lora))
        y_attn = _attention(q.reshape(B, S, aw), k.reshape(B, S, aw), v.reshape(B, S, aw),
                            attn_norm_gain[l][None, :])
        x2d = _mix_ffn(x2d, y_rwkv.reshape(B * S, rw), y_attn.reshape(B * S, aw), w_out[l].astype(BF16),
                       ffn_norm_gain[l][None, :], w_ffn_up[l].astype(BF16), ffn_conv_w[l],
                       ffn_conv_b[l][None, :], w_ffn_down[l].astype(BF16), final_norm_gain[None, :], S,
                       apply_final=(l == depth - 1))
    return x2d.reshape(B, S, D)
```

```python
import functools
import math

import jax
import jax.numpy as jnp
from jax import lax
from jax.experimental import pallas as pl
from jax.experimental.pallas import tpu as pltpu

F32 = jnp.float32
BF16 = jnp.bfloat16

LANES = 128
HEAD_DIM = 64
PAIR = 2 * HEAD_DIM
ROT_DIM = HEAD_DIM // 4
ROPE_THETA = 500000.0
NORM_EPS = 1e-6
GN_EPS = 64e-5
DILATED_PATTERNS = ((128, 1), (512, 4), (2048, 16))
ATTN_BLOCK = 128
CONV_WIDTH = 3
CHUNK = 64
SOLVE_BLOCK = 16
NEG_BIG = -1e30
VMEM_LIMIT = 56 * 1024 * 1024


def _dot(a, b):
    return jnp.dot(a, b, preferred_element_type=F32)


def _dot_nt(a, b):
    return lax.dot_general(a, b, (((1,), (1,)), ((), ())), preferred_element_type=F32)


def _dot_tn(a, b):
    return lax.dot_general(a, b, (((0,), (0,)), ((), ())), preferred_element_type=F32)


def _rmsnorm(x, gain):
    return x * lax.rsqrt(jnp.mean(x * x, axis=-1, keepdims=True) + NORM_EPS) * gain


def _sigmoid(x):
    return 1.0 / (1.0 + jnp.exp(-x))


def _split_dot(x, w):
    hi = x.astype(BF16)
    lo = (x - hi.astype(F32)).astype(BF16)
    return _dot(hi, w) + _dot(lo, w)


def _inproj_kernel(x_ref, g_ref, wr_ref, wa_ref, cos_ref, sin_ref, pr_ref, q_ref, k_ref, v_ref):
    h = _rmsnorm(x_ref[...], g_ref[...]).astype(BF16)
    pr_ref[...] = _dot(h, wr_ref[...])
    p = _dot(h, wa_ref[...])
    aw = q_ref.shape[1]

    tm = x_ref.shape[0]
    lane = lax.broadcasted_iota(jnp.int32, (tm, PAIR), 1)
    first_half = (lane % HEAD_DIM) < (ROT_DIM // 2)
    cos, sin = cos_ref[...], sin_ref[...]

    def rotary(x):
        partner = jnp.where(first_half, pltpu.roll(x, PAIR - ROT_DIM // 2, 1), pltpu.roll(x, ROT_DIM // 2, 1))
        return x * cos + partner * sin

    for t in range(aw // PAIR):
        lo = t * PAIR
        q_ref[:, lo:lo + PAIR] = (rotary(p[:, lo:lo + PAIR]) * (1.0 / math.sqrt(HEAD_DIM))).astype(BF16)
        k_ref[:, lo:lo + PAIR] = rotary(p[:, aw + lo:aw + lo + PAIR]).astype(BF16)
    v_ref[...] = p[:, 2 * aw:].astype(BF16)


def _inproj(x2d, gain, w_r, w_a, cos_t, sin_t, tm=1024):
    T, D = x2d.shape
    rw, aw = w_r.shape[1], w_a.shape[1] // 3
    seq_tiles = cos_t.shape[0] // tm
    row = lambda i: (i, 0)
    fixed = lambda i: (0, 0)
    tab = pl.BlockSpec((tm, PAIR), lambda i: (i % seq_tiles, 0))
    return pl.pallas_call(
        _inproj_kernel,
        out_shape=(jax.ShapeDtypeStruct((T, rw), F32),) + (jax.ShapeDtypeStruct((T, aw), BF16),) * 3,
        grid=(T // tm,),
        in_specs=[pl.BlockSpec((tm, D), row), pl.BlockSpec((1, D), fixed), pl.BlockSpec(w_r.shape, fixed),
                  pl.BlockSpec(w_a.shape, fixed), tab, tab],
        out_specs=(pl.BlockSpec((tm, rw), row),) + (pl.BlockSpec((tm, aw), row),) * 3,
        compiler_params=pltpu.CompilerParams(dimension_semantics=("parallel",), vmem_limit_bytes=VMEM_LIMIT),
        name="inproj",
    )(x2d, gain, w_r, w_a, cos_t, sin_t)


def _each(fn, *lists):
    return [fn(*args) for args in zip(*lists)]


def _rwkv_chunk_terms(r, k, v, logw, cum, kkn, bb, same_blk, strict_c, incl_c, lane_lo):
    L = CHUNK
    bf = lambda t: t.astype(BF16)
    cum_last = [t[L - 1:L, :] for t in cum]
    g_in = _each(jnp.exp, cum)
    g_ex = _each(lambda t, w: jnp.exp(t - w), cum, logw)
    g_inv = _each(lambda t: jnp.exp(-t), cum)
    g_hat = _each(lambda tl, t: jnp.exp(tl - t), cum_last, cum)
    g_last = _each(jnp.exp, cum_last)
    yield

    def by_head_rows(x):
        lo = lane_lo if x.shape[1] == PAIR else jnp.concatenate([lane_lo] * (x.shape[1] // PAIR), axis=1)
        return jnp.concatenate([jnp.where(lo, x, 0.0), jnp.where(lo, 0.0, x)], axis=0).astype(BF16)

    al = _each(lambda t, g: -t * g, kkn, g_ex)
    rb = _each(jnp.multiply, r, g_in)
    bt = _each(jnp.multiply, bb, g_inv)
    kt = _each(jnp.multiply, k, g_inv)
    bh = _each(jnp.multiply, bb, g_hat)
    kh = _each(jnp.multiply, k, g_hat)
    lhs = _each(lambda a, b: jnp.concatenate([a, b], axis=0).astype(BF16), al, rb)
    rhs = _each(lambda a, b: jnp.concatenate([by_head_rows(a), by_head_rows(b)], axis=0), bt, kt)
    yield
    aq = _each(_dot_nt, lhs, rhs)
    a_ab = [jnp.where(strict_c, t[:L, :2 * L], 0.0) for t in aq]
    a_ak = [jnp.where(strict_c, t[:L, 2 * L:], 0.0).astype(BF16) for t in aq]
    a_rb = [jnp.where(incl_c, t[L:, :2 * L], 0.0).astype(BF16) for t in aq]
    a_rk = [jnp.where(incl_c, t[L:, 2 * L:], 0.0).astype(BF16) for t in aq]
    yield

    v_s = _each(by_head_rows, v)
    av = _each(_dot, a_ak, v_s)
    BS = SOLVE_BLOCK
    n_blk = L // BS
    rhs0 = _each(lambda a, b: jnp.concatenate([a, b], axis=1), al, av)
    lo2 = lax.broadcasted_iota(jnp.int32, (BS, 2 * PAIR), 1) % PAIR < HEAD_DIM
    lane = lax.broadcasted_iota(jnp.int32, (BS, PAIR), 1)
    br =lax.broadcasted_iota(jnp.int32, (PAIR, PAIR), 0)
    bc = lax.broadcasted_iota(jnp.int32, (PAIR, PAIR), 1)
    blk_diag = (br // BS) == (bc // BS)

    def expand(pack):
        return jnp.where(blk_diag, jnp.concatenate([pack] * (PAIR // BS), axis=0), 0.0).astype(BF16)

    d_k = [functools.reduce(jnp.add, [jnp.where((lane % HEAD_DIM) // BS == b, t[b * BS:(b + 1) * BS, :], 0.0)
                                      for b in range(n_blk)]) for t in a_ab]
    t_m = d_k
    yield
    n_sq = int(math.log2(BS)) - 1
    d_k = _each(lambda d: _dot(d.astype(BF16), expand(d)), d_k)
    yield
    for lvl in range(n_sq):
        d_e = _each(expand, d_k)
        t_m = _each(lambda t, d, de: t + d + _dot(t.astype(BF16), de), t_m, d_k, d_e)
        if lvl + 1 < n_sq:
            d_k = _each(lambda d, de: _dot(d.astype(BF16), de), d_k, d_e)
        yield
    t_mb = _each(bf, t_m)

    zero_blk = jnp.zeros((BS, 2 * PAIR), BF16)
    x_blocks = [[] for _ in a_ab]
    xs_lo = [[] for _ in a_ab]
    xs_hi = [[] for _ in a_ab]

    def stacked(lo_parts, hi_parts):
        rows = [lo_parts.get(c, zero_blk) for c in range(n_blk)] + [hi_parts.get(c, zero_blk) for c in range(n_blk)]
        return jnp.concatenate(rows, axis=0)

    for b in range(n_blk):
        r_b = [t[b * BS:(b + 1) * BS, :] for t in rhs0]
        if b > 0:
            r_b = _each(lambda rr, a, lo_p, hi_p: rr + _dot(a[b * BS:(b + 1) * BS, :].astype(BF16),
                                                            stacked(dict(enumerate(lo_p)), dict(enumerate(hi_p)))),
                        r_b, a_ab, xs_lo, xs_hi)
            yield
        x_b = _each(lambda rr, tm: rr + _dot(tm, stacked({b: jnp.where(lo2, rr, 0.0).astype(BF16)},
                                                        {b: jnp.where(lo2, 0.0, rr).astype(BF16)})),
                    r_b, t_mb)
        for i, xb in enumerate(x_b):
            x_blocks[i].append(xb)
            xs_lo[i].append(jnp.where(lo2, xb, 0.0).astype(BF16))
            xs_hi[i].append(jnp.where(lo2, 0.0, xb).astype(BF16))
        yield
    x = [jnp.concatenate(blks, axis=0) for blks in x_blocks]
    x_s = _each(lambda lo_p, hi_p: jnp.concatenate(lo_p + hi_p, axis=0), xs_lo, xs_hi)

    z = _each(_dot, a_rb, x_s)
    akv = _each(_dot, a_rk, v_s)
    w2 = _each(lambda a, t: (a + t[:, :PAIR]).astype(BF16), rb, z)
    y_loc = _each(lambda t, a: t[:, PAIR:] + a, z, akv)
    w1 = [t[:, :PAIR].astype(BF16) for t in x]
    u_loc = [t[:, PAIR:] for t in x]
    yield

    m_t = _each(lambda a, b: jnp.where(same_blk, _dot_tn(a, b.astype(BF16)), 0.0).astype(BF16), w1, bh)
    s_loc = _each(
        lambda u, vv, b, kk_: jnp.where(
            same_blk,
            _dot_tn(jnp.concatenate([u, vv], axis=0).astype(BF16), jnp.concatenate([b, kk_], axis=0).astype(BF16)),
            0.0),
        u_loc, v, bh, kh)
    return w2, y_loc, m_t, s_loc, g_last


def _rwkv_kernel(p_ref, mix_ref, wc_ref, w0_ref, a0_ref, kk_ref, ka_ref, rk_ref, lnw_ref, lnb_ref,
                 o_ref, s_scr, carry_scr, *, nb, width, n_lora, n_chunks, n_groups, stage_offset):
    L = CHUNK
    R = n_chunks * L
    c = pl.program_id(0)

    @pl.when(c == 0)
    def _():
        s_scr[...] = jnp.zeros_like(s_scr)
        carry_scr[...] = jnp.zeros_like(carry_scr)

    n_pairs = width // PAIR
    row = lax.broadcasted_iota(jnp.int32, (R, 1), 0)
    ri = lax.broadcasted_iota(jnp.int32, (2 * L, 2 * L), 0)
    ci = lax.broadcasted_iota(jnp.int32, (2 * L, 2 * L), 1)
    same_blk = (ri >= L) == (ci >= L)
    ti = lax.broadcasted_iota(jnp.int32, (L, 2 * L), 0)
    si = lax.broadcasted_iota(jnp.int32, (L, 2 * L), 1) % L
    strict_c = si < ti
    incl_c = si <= ti
    rr = lax.broadcasted_iota(jnp.int32, (R, R), 0)
    rc = lax.broadcasted_iota(jnp.int32, (R, R), 1)
    tri = ((rr // L == rc // L) & (rr >= rc)).astype(BF16)
    lane_lo = lax.broadcasted_iota(jnp.int32, (L, PAIR), 1) < HEAD_DIM
    lane_lo_r = lax.broadcasted_iota(jnp.int32, (R, PAIR), 1) < HEAD_DIM
    lora_lane = lax.broadcasted_iota(jnp.int32, (R, 2 * LANES), 1)

    def head_sums(x):
        s0 = jnp.sum(jnp.where(lane_lo_r, x, 0.0), axis=-1, keepdims=True)
        s1 = jnp.sum(jnp.where(lane_lo_r, 0.0, x), axis=-1, keepdims=True)
        return jnp.where(lane_lo_r, s0, s1)

    mix = mix_ref[...]

    def lora_up(pm):
        lora = pm[:, 3 * width:]
        act = jnp.where(lora_lane < n_lora[0], jnp.tanh(lora),
                        jnp.where(lora_lane < n_lora[1], lora,
                                  jnp.where(lora_lane < n_lora[2], _sigmoid(lora), 0.0)))
        return _dot(act.astype(BF16), wc_ref[...])

    def log_decay(up):
        t = w0_ref[...] + up[:, :width]
        return -jnp.exp(-(jnp.maximum(-t, 0.0) + jnp.log(1.0 + jnp.exp(-jnp.abs(t)))) - 0.5)

    def running_sum(t):
        h1 = t.astype(BF16)
        r1 = t - h1.astype(F32)
        h2 = r1.astype(BF16)
        h3 = (r1 - h2.astype(F32)).astype(BF16)
        return _dot(tri, h1) + _dot(tri, h2) + _dot(tri, h3)

    tiles = [(b, pr) for b in range(nb) for pr in range(n_pairs)]
    n_t = len(tiles)
    lanes = lambda pr, part=0: slice(part * width + pr * PAIR, part * width + (pr + 1) * PAIR)
    sls = [lanes(pr) for _, pr in tiles]

    states = {0: [s_scr[i] for i in range(n_t)]}
    outs = {}

    def run_group(gi):
        g0 = gi * R
        pms = []
        for b in range(nb):
            p = p_ref[b, g0:g0 + R, :]
            prev_last = carry_scr[b, 7:8, :] if gi == 0 else p_ref[b, g0 - 1:g0, :]
            p_prev = jnp.where(row == 0, prev_last, pltpu.roll(p, 1, 0))
            pms.append(p + (p_prev - p) * mix)
        yield
        ups = _each(lora_up, pms)
        logw_b = _each(log_decay, ups)
        cum_b = _each(running_sum, logw_b)
        yield

        r = [pms[b][:, lanes(pr, 0)] for b, pr in tiles]
        k_raw = [pms[b][:, lanes(pr, 1)] for b, pr in tiles]
        v = [pms[b][:, lanes(pr, 2)] for b, pr in tiles]
        logw = [logw_b[b][:, lanes(pr)] for b, pr in tiles]
        cum = [cum_b[b][:, lanes(pr)] for b, pr in tiles]
        a = [_sigmoid(a0_ref[:, lanes(pr)] + ups[b][:, lanes(pr, 1)]) for b, pr in tiles]
        g = [ups[b][:, lanes(pr, 2)] for b, pr in tiles]

        kk = _each(lambda t, sl: t * kk_ref[:, sl], k_raw, sls)
        kk_ss = _each(lambda t: head_sums(t * t), kk)
        kkn = _each(lambda t, ss: t / jnp.maximum(jnp.sqrt(ss), 1e-12), kk, kk_ss)
        k = _each(lambda t, aa, sl: t * (1.0 + (aa - 1.0) * ka_ref[:, sl]), k_raw, a, sls)
        bb = _each(jnp.multiply, kkn, a)
        bonus_dot = _each(lambda rr_, kk_, sl: head_sums(rr_ * kk_ * rk_ref[:, sl]), r, k, sls)
        yield

        def chunks(ts):
            return [t[ch * L:(ch + 1) * L] for ch in range(n_chunks) for t in ts]

        w2, y_loc, m_t, s_loc, g_last = yield from _rwkv_chunk_terms(
            chunks(r), chunks(k), chunks(v), chunks(logw), chunks(cum), chunks(kkn), chunks(bb),
            same_blk, strict_c, incl_c, lane_lo)
        yield

        s = states[gi]
        y_parts = []
        for ch in range(n_chunks):
            sel = slice(ch * n_t, (ch + 1) * n_t)
            s_b = _each(lambda t: t.astype(BF16), s)
            y_parts.append(_each(lambda a_, sb, yl: _dot_nt(a_, sb) + yl, w2[sel], s_b, y_loc[sel]))
            s = _each(lambda s0, gl, sb, m, sl: s0 * gl + _dot(sb, m) + sl, s, g_last[sel], s_b, m_t[sel], s_loc[sel])
        states[gi + 1] = s
        y = [jnp.concatenate([y_parts[ch][i] for ch in range(n_chunks)], axis=0) for i in range(n_t)]
        yield

        mu = _each(lambda t: head_sums(t) * (1.0 / HEAD_DIM), y)
        yc = _each(jnp.subtract, y, mu)
        var = _each(lambda t: head_sums(t * t) * (1.0 / HEAD_DIM), yc)
        yield
        res = []
        for i in range(n_t):
            sl = sls[i]
            yn = yc[i] * lax.rsqrt(var[i] + GN_EPS) * lnw_ref[:, sl] + lnb_ref[:, sl]
            res.append(((yn + bonus_dot[i] * v[i]) * g[i]).astype(o_ref.dtype))
        outs[gi] = res

    gens = [run_group(gi) for gi in range(n_groups)]
    live = [True] * n_groups
    tick = 0
    while any(live):
        for gi in range(n_groups):
            if live[gi] and tick >= gi * stage_offset:
                try:
                    next(gens[gi])
                except StopIteration:
                    live[gi] = False
        tick += 1

    for gi in range(n_groups):
        for i, (b, pr) in enumerate(tiles):
            o_ref[b, gi * R:(gi + 1) * R, sls[i]] = outs[gi][i]
    for i in range(n_t):
        s_scr[i] = states[n_groups][i]
    for b in range(nb):
        carry_scr[b] = p_ref[b, n_groups * R - 8:n_groups * R, :]


def _rwkv(p_r, mix, wc, w0, a0, k_k, k_a, r_k, ln_w, ln_b, width, n_lora, n_chunks=2, n_groups=4, stage_offset=5):
    B, S, C = p_r.shape
    L = CHUNK * n_chunks * n_groups
    n_state = B * (width // PAIR)
    vec = lambda n: pl.BlockSpec((1, n), lambda c: (0, 0))
    return pl.pallas_call(
        functools.partial(_rwkv_kernel, nb=B, width=width, n_lora=n_lora, n_chunks=n_chunks, n_groups=n_groups,
                          stage_offset=stage_offset),
        out_shape=jax.ShapeDtypeStruct((B, S, width), BF16),
        grid=(S // L,),
        in_specs=[pl.BlockSpec((B, L, C), lambda c: (0, c, 0)), vec(C),
                  pl.BlockSpec(wc.shape, lambda c: (0, 0))] + [vec(width)] * 7,
        out_specs=pl.BlockSpec((B, L, width), lambda c: (0, c, 0)),
        scratch_shapes=[pltpu.VMEM((n_state, PAIR, PAIR), F32), pltpu.VMEM((B, 8, C), F32)],
        compiler_params=pltpu.CompilerParams(dimension_semantics=("arbitrary",), vmem_limit_bytes=VMEM_LIMIT),
        name="rwkv7",
    )(p_r, mix, wc, w0, a0, k_k, k_a, r_k, ln_w, ln_b)


SB_ROWS = 2048
SB_QUARTER = SB_ROWS // 4
N_RES = 4


def _attn_segments(pi, g):
    if pi == 0:
        return [((g // 4) * SB_QUARTER + r * ATTN_BLOCK + (ATTN_BLOCK // N_RES) * (g % 4), ATTN_BLOCK // N_RES, 1)
                for r in range(N_RES)]
    if pi == 1:
        return [((g // 4) * SB_QUARTER + (g % 4) * ATTN_BLOCK, ATTN_BLOCK, 1)]
    return [(qq * SB_QUARTER + (g % 4) * ATTN_BLOCK + g // 4, ATTN_BLOCK // 4, 4) for qq in range(4)]


def _attn_prev_tile(pi, g):
    if pi == 0:
        return (g + 15) % 16, g == 0
    if pi == 1:
        return ((g // 4 + 3) % 4) * 4 + g % 4, g < 4
    return g, True


def _attn_kernel(q_ref, k_ref, v_ref, gain_ref, o_ref,
                 nat_q, nat_k, nat_v, q_scr, k_ring, v_ring, bias_scr, out_nat, *stat_scr, tiles_per_group):
    sb = pl.program_id(2)
    n_pat = len(DILATED_PATTERNS)
    assert DILATED_PATTERNS == ((128, 1), (512, 4), (2048, 16)) and ATTN_BLOCK == 128
    num_scr, m_scr, l_scr = stat_scr[:n_pat], stat_scr[n_pat:2 * n_pat], stat_scr[2 * n_pat:]
    Q = ATTN_BLOCK
    cur_base = (sb % 2) * SB_ROWS
    other_base = SB_ROWS - cur_base

    @pl.when(sb == 0)
    def _():
        k_ring[pl.ds(SB_ROWS, SB_ROWS), :] = jnp.zeros((SB_ROWS, PAIR), F32)
        v_ring[pl.ds(SB_ROWS, SB_ROWS), :] = jnp.zeros((SB_ROWS, PAIR), F32)

    nat_q[...] = q_ref[...].astype(F32)
    nat_k[...] = k_ref[...].astype(F32)
    nat_v[...] = v_ref[...].astype(F32)
    for qq in range(4):
        for r in range(N_RES):
            src = pl.ds(qq * SB_QUARTER + r, Q, stride=N_RES)
            dst = qq * SB_QUARTER + r * Q
            q_scr[dst:dst + Q, :] = nat_q[src, :]
            k_ring[pl.ds(cur_base + dst, Q), :] = nat_k[src, :]
            v_ring[pl.ds(cur_base + dst, Q), :] = nat_v[src, :]

    @pl.when(sb == 0)
    def _():
        ii = lax.broadcasted_iota(jnp.int32, (2 * Q, 2 * Q), 0) % Q
        cj = lax.broadcasted_iota(jnp.int32, (2 * Q, 2 * Q), 1)
        per = Q // N_RES
        for kind in range(2):
            if kind == 0:
                qi = N_RES * (ii % per) + ii // per
                kj = N_RES * ((cj % Q) % per) + (cj % Q) // per + (cj // Q) * Q
            else:
                qi, kj = ii, cj
            band = (kj >= qi) & (kj <= qi + Q)
            bias_scr[kind, 1] = jnp.where(band, 0.0, NEG_BIG)
            bias_scr[kind, 0] = jnp.where(band & (cj >= Q), 0.0, NEG_BIG)

    def load_tile(ref, base, segs):
        parts = [ref[pl.ds(base + s, n, stride=st) if st > 1 else pl.ds(base + s, n), :] for s, n, st in segs]
        return parts[0] if len(parts) == 1 else jnp.concatenate(parts, axis=0)

    def store_tile(ref, segs, val):
        off = 0
        for s, n, st in segs:
            ref[pl.ds(s, n, stride=st) if st > 1 else pl.ds(s, n), :] = val[off:off + n]
            off += n

    lane_lo = lax.broadcasted_iota(jnp.int32, (Q, PAIR), 1) < HEAD_DIM
    ones_blk = jnp.ones((2 * Q, PAIR), BF16)

    n_blk = SB_ROWS // Q
    has_prev_sb = jnp.where(sb > 0, 1, 0)

    def scores(pi, gs):
        segs = [_attn_segments(pi, g) for g in gs]
        prev = [_attn_prev_tile(pi, g) for g in gs]
        prev_segs = [_attn_segments(pi, pg) for pg, _ in prev]
        prev_base = [other_base if other else cur_base for _, other in prev]
        q2 = [load_tile(q_scr, 0, sg) for sg in segs]
        q2 = [jnp.concatenate([jnp.where(lane_lo, t, 0.0), jnp.where(lane_lo, 0.0, t)], axis=0).astype(BF16)
              for t in q2]
        kcat = [jnp.concatenate([load_tile(k_ring, pb, psg), load_tile(k_ring, cur_base, sg)], axis=0).astype(BF16)
                for pb, psg, sg in zip(prev_base, prev_segs, segs)]
        vcat = [jnp.concatenate([load_tile(v_ring, pb, psg), load_tile(v_ring, cur_base, sg)], axis=0).astype(BF16)
                for pb, psg, sg in zip(prev_base, prev_segs, segs)]
        vext = [jnp.concatenate([t, ones_blk], axis=1) for t in vcat]
        has_prev = [has_prev_sb if other else 1 for _, other in prev]
        return dict(pi=pi, segs=segs, s=_each(_dot_nt, q2, kcat), vext=vext, has_prev=has_prev)

    def softmax(c):
        kind = 0 if c["pi"] == 0 else 1
        s = _each(lambda t, hp: t + bias_scr[kind, hp], c["s"], c["has_prev"])
        m = [jnp.max(t, axis=-1, keepdims=True) for t in s]
        p = _each(lambda t, mm: jnp.exp(t - mm).astype(BF16), s, m)
        return dict(pi=c["pi"], segs=c["segs"], vext=c["vext"], m=m, p=p)

    def weighted_values(c):
        pi, segs, m = c["pi"], c["segs"], c["m"]
        nl = _each(_dot, c["p"], c["vext"])
        for t in range(len(segs)):
            store_tile(num_scr[pi], segs[t], jnp.where(lane_lo, nl[t][:Q, :PAIR], nl[t][Q:, :PAIR]))
            store_tile(l_scr[pi], segs[t], jnp.where(lane_lo, nl[t][:Q, PAIR:], nl[t][Q:, PAIR:]))
            store_tile(m_scr[pi], segs[t], jnp.where(lane_lo, m[t][:Q], m[t][Q:]))

    groups = [(pi, list(range(g0, g0 + tiles_per_group)))
              for pi in range(n_pat) for g0 in range(0, n_blk, tiles_per_group)]
    n_grp = len(groups)
    sc = {0: scores(*groups[0])}
    if n_grp > 1:
        sc[1] = scores(*groups[1])
    sm = {0: softmax(sc.pop(0))}
    for k in range(n_grp):
        if k + 2 < n_grp:
            sc[k + 2] = scores(*groups[k + 2])
        if k + 1 < n_grp:
            sm[k + 1] = softmax(sc.pop(k + 1))
        weighted_values(sm.pop(k))

    ri = lax.broadcasted_iota(jnp.int32, (PAIR, PAIR), 0)
    ci = lax.broadcasted_iota(jnp.int32, (PAIR, PAIR), 1)
    seg_ones = ((ri >= HEAD_DIM) == (ci >= HEAD_DIM)).astype(BF16)
    gain = gain_ref[...]

    def merge(i, carry):
        rows = pl.ds(pl.multiple_of(i * Q, Q), Q)
        ms = [m_scr[pi][rows, :] for pi in range(n_pat)]
        m_all = functools.reduce(jnp.maximum, ms)
        num = 0.0
        den = 0.0
        for pi in range(n_pat):
            wgt = jnp.exp(ms[pi] - m_all)
            num = num + wgt * num_scr[pi][rows, :]
            den = den + wgt * l_scr[pi][rows, :]
        o = num / den
        ms_o = _split_dot(o * o, seg_ones) * (1.0 / HEAD_DIM)
        out_nat[pl.ds((i // N_RES) * SB_QUARTER + i % N_RES, Q, stride=N_RES), :] = o * lax.rsqrt(ms_o + NORM_EPS) * gain
        return carry

    lax.fori_loop(0, SB_ROWS // Q, merge, 0, unroll=4)
    o_ref[...] = out_nat[...].astype(o_ref.dtype)


def _attention(q, k, v, gain, tiles_per_group=4):
    B, S, W = q.shape
    n_pairs = W // PAIR
    n_pat = len(DILATED_PATTERNS)
    blk = pl.BlockSpec((None, SB_ROWS, PAIR), lambda b, p, s: (b, s, p))
    tile = pltpu.VMEM((SB_ROWS, PAIR), F32)
    return pl.pallas_call(
        functools.partial(_attn_kernel, tiles_per_group=tiles_per_group),
        out_shape=jax.ShapeDtypeStruct((B, S, W), BF16),
        grid=(B, n_pairs, S // SB_ROWS),
        in_specs=[blk, blk, blk, pl.BlockSpec((1, PAIR), lambda b, p, s: (0, p))],
        out_specs=blk,
        scratch_shapes=[tile, tile, tile, tile,
                        pltpu.VMEM((2 * SB_ROWS, PAIR), F32),
                        pltpu.VMEM((2 * SB_ROWS, PAIR), F32),
                        pltpu.VMEM((2, 2, 2 * ATTN_BLOCK, 2 * ATTN_BLOCK), F32),
                        tile]
                       + [tile] * (3 * n_pat),
        compiler_params=pltpu.CompilerParams(dimension_semantics=("parallel", "parallel", "arbitrary"),
                                             vmem_limit_bytes=VMEM_LIMIT),
        name="dilated_attn",
    )(q, k, v, gain)


def _rotary_tables(seq):
    half = ROT_DIM // 2
    inv_freq = ROPE_THETA ** (-jnp.arange(half, dtype=F32) * 2.0 / ROT_DIM)
    ang = jnp.arange(seq).astype(F32)[:, None] * inv_freq[None, :]
    cos, sin = jnp.cos(ang), jnp.sin(ang)
    rest = HEAD_DIM - ROT_DIM
    cos_h = jnp.concatenate([cos, cos, jnp.ones((seq, rest), F32)], axis=-1)
    sin_h = jnp.concatenate([-sin, sin, jnp.zeros((seq, rest), F32)], axis=-1)
    return jnp.tile(cos_h, (1, PAIR // HEAD_DIM)), jnp.tile(sin_h, (1, PAIR // HEAD_DIM))


FFN_HALO = 16


def _mix_ffn_kernel(x_ref, ya_ref, yb_ref, xh_ref, yah_ref, ybh_ref, wo_ref, g_ref, wu_ref, cw_ref,
                    cb_ref, wd_ref, fg_ref, o_ref, h_scr, hh_scr, *, tm, seq, row_chunks, apply_final):
    tf = wd_ref.shape[0]
    i = pl.program_id(0)
    j = pl.program_id(1)
    wa = ya_ref.shape[1]

    def mixed(x, ya, yb):
        return x + _dot(ya, wo_ref[:wa, :]) + _dot(yb, wo_ref[wa:, :])

    @pl.when(j == 0)
    def _():
        x1 = mixed(x_ref[...], ya_ref[...], yb_ref[...])
        o_ref[...] = x1
        h_scr[...] = _rmsnorm(x1, g_ref[...]).astype(BF16)
        hh_scr[...] = _rmsnorm(mixed(xh_ref[...], yah_ref[...], ybh_ref[...]), g_ref[...]).astype(BF16)

    seq_start = (i * tm) % seq == 0
    rc = tm // row_chunks
    row = lax.broadcasted_iota(jnp.int32, (rc, 1), 0)
    for c in range(row_chunks):
        rows = slice(c * rc, (c + 1) * rc)
        gv = _dot(h_scr[rows, :], wu_ref[...])
        gate, val = gv[:, :tf], gv[:, tf:]
        if c == 0:
            gate_h = jnp.where(seq_start, 0.0, _dot(hh_scr[...], wu_ref[:, :tf]))
        else:
            gate_h = _dot(h_scr[c * rc - FFN_HALO:c * rc, :], wu_ref[:, :tf])
        g1 = jnp.where(row == 0, gate_h[FFN_HALO - 1:FFN_HALO, :], pltpu.roll(gate, 1, 0))
        g2 = jnp.where(row == 0, gate_h[FFN_HALO - 2:FFN_HALO - 1, :],
                       jnp.where(row == 1, gate_h[FFN_HALO - 1:FFN_HALO, :], pltpu.roll(gate, 2, 0)))
        u = cw_ref[0:1, :] * g2 + cw_ref[1:2, :] * g1 + cw_ref[2:3, :] * gate + cb_ref[...]
        act = (u * _sigmoid(u) * val).astype(BF16)
        o_ref[rows, :] += _dot(act, wd_ref[...])

    if apply_final:
        @pl.when(j == pl.num_programs(1) - 1)
        def _():
            o_ref[...] = _rmsnorm(o_ref[...], fg_ref[...])


def _mix_ffn(x2d, y_a, y_b, w_o, gain, w_up, conv_w, conv_b, w_down, final_gain, seq, apply_final,
             tm=1024, n_ff_tiles=2, row_chunks=1):
    T, D = x2d.shape
    F = w_down.shape[0]
    tf = F // n_ff_tiles
    assert tf % LANES == 0 and seq % tm == 0 and w_up.shape[1] == 2 * F
    w_up = jnp.concatenate([w_up[:, half * F + j * tf: half * F + (j + 1) * tf]
                            for j in range(n_ff_tiles) for half in range(2)], axis=1)
    halo_blocks = tm // FFN_HALO
    row = lambda i, j: (i, 0)
    halo = lambda i, j: (jnp.maximum(i * halo_blocks - 1, 0), 0)
    fixed = lambda i, j: (0, 0)
    wa, wb = y_a.shape[1], y_b.shape[1]
    return pl.pallas_call(
        functools.partial(_mix_ffn_kernel, tm=tm, seq=seq, row_chunks=row_chunks, apply_final=apply_final),
        out_shape=jax.ShapeDtypeStruct((T, D), F32),
        grid=(T // tm, n_ff_tiles),
        in_specs=[pl.BlockSpec((tm, D), row), pl.BlockSpec((tm, wa), row), pl.BlockSpec((tm, wb), row),
                  pl.BlockSpec((FFN_HALO, D), halo), pl.BlockSpec((FFN_HALO, wa), halo),
                  pl.BlockSpec((FFN_HALO, wb), halo),
                  pl.BlockSpec(w_o.shape, fixed),
                  pl.BlockSpec((1, D), fixed),
                  pl.BlockSpec((D, 2 * tf), lambda i, j: (0, j)),
                  pl.BlockSpec((CONV_WIDTH, tf), lambda i, j: (0, j)),
                  pl.BlockSpec((1, tf), lambda i, j: (0, j)),
                  pl.BlockSpec((tf, D), lambda i, j: (j, 0)),
                  pl.BlockSpec((1, D), fixed)],
        out_specs=pl.BlockSpec((tm, D), row),
        scratch_shapes=[pltpu.VMEM((tm, D), BF16), pltpu.VMEM((FFN_HALO, D), BF16)],
        compiler_params=pltpu.CompilerParams(dimension_semantics=("parallel", "arbitrary"),
                                             vmem_limit_bytes=VMEM_LIMIT),
        name="mix_convglu_ffn",
    )(x2d, y_a, y_b, x2d, y_a, y_b, w_o, gain, w_up, conv_w, conv_b, w_down, final_gain)


def kernel(x, mix_norm_gain, w_in, rwkv_shift_mix, w0, w_lora_up, a0, a_lora_up, g_lora_up, k_k, k_a, r_k,
           ln_x_w, ln_x_b, attn_norm_gain, w_out, ffn_norm_gain, w_ffn_up, ffn_conv_w, ffn_conv_b,
           w_ffn_down, final_norm_gain):
    B, S, D = x.shape
    depth = w_in.shape[0]
    rw = w0.shape[1]
    aw = attn_norm_gain.shape[1]
    n_w, n_a, n_g = w_lora_up.shape[1], a_lora_up.shape[1], g_lora_up.shape[1]
    n_lora = n_w + n_a + n_g
    lora_pad = -(-n_lora // (2 * LANES)) * (2 * LANES)
    assert lora_pad == 2 * LANES and rw % PAIR == 0 and aw % PAIR == 0
    rwkv_cols = 3 * rw + n_lora
    cos_t, sin_t = _rotary_tables(S)

    x2d = x.reshape(B * S, D)
    for l in range(depth):
        w_r = w_in[l][:, :3 * rw + lora_pad].astype(BF16)
        w_a = w_in[l][:, rwkv_cols:].astype(BF16)
        mix =jnp.concatenate([rwkv_shift_mix[l], jnp.zeros((lora_pad - n_lora,), F32)])[None, :]
        wc = jnp.zeros((lora_pad, 3 * rw), F32)
        wc = wc.at[:n_w, :rw].set(w_lora_up[l])
        wc = wc.at[n_w:n_w + n_a, rw:2 * rw].set(a_lora_up[l])
        wc = wc.at[n_w + n_a:n_lora, 2 * rw:].set(g_lora_up[l]).astype(BF16)

        p_r, q, k, v = _inproj(x2d, mix_norm_gain[l][None, :], w_r, w_a, cos_t, sin_t)
        y_rwkv = _rwkv(p_r.reshape(B, S, -1), mix, wc, w0[l][None, :], a0[l][None, :], k_k[l][None, :],
                       k_a[l][None, :], r_k[l].reshape(1, rw), ln_x_w[l][None, :], ln_x_b[l][None, :],
                       rw, (n_w, n_w + n_a, n_lora))
        y_attn = _attention(q.reshape(B, S, aw), k.reshape(B, S, aw), v.reshape(B, S, aw),
                            attn_norm_gain[l][None, :])
        x2d = _mix_ffn(x2d, y_rwkv.reshape(B * S, rw), y_attn.reshape(B * S, aw), w_out[l].astype(BF16),
                       ffn_norm_gain[l][None, :], w_ffn_up[l].astype(BF16), ffn_conv_w[l],
                       ffn_conv_b[l][None, :], w_ffn_down[l].astype(BF16), final_norm_gain[None, :], S,
                       apply_final=(l == depth - 1))
    return x2d.reshape(B, S, D)
```

```python
import functools
import math

import jax
import jax.numpy as jnp
from jax import lax
from jax.experimental import pallas as pl
from jax.experimental.pallas import tpu as pltpu

F32 = jnp.float32
BF16 = jnp.bfloat16

LANES = 128
HEAD_DIM = 64
PAIR = 2 * HEAD_DIM
ROT_DIM = HEAD_DIM // 4
ROPE_THETA = 500000.0
NORM_EPS = 1e-6
GN_EPS = 64e-5
DILATED_PATTERNS = ((128, 1), (512, 4), (2048, 16))
ATTN_BLOCK = 128
CONV_WIDTH = 3
CHUNK = 64
SOLVE_BLOCK = 16
NEG_BIG = -1e30
VMEM_LIMIT = 56 * 1024 * 1024


def _dot(a, b):
    return jnp.dot(a, b, preferred_element_type=F32)


def _dot_nt(a, b):
    return lax.dot_general(a, b, (((1,), (1,)), ((), ())), preferred_element_type=F32)


def _dot_tn(a, b):
    return lax.dot_general(a, b, (((0,), (0,)), ((), ())), preferred_element_type=F32)


def _rmsnorm(x, gain):
    return x * lax.rsqrt(jnp.mean(x * x, axis=-1, keepdims=True) + NORM_EPS) * gain


def _sigmoid(x):
    return 1.0 / (1.0 + jnp.exp(-x))


def _split_dot(x, w):
    hi = x.astype(BF16)
    lo = (x - hi.astype(F32)).astype(BF16)
    return _dot(hi, w) + _dot(lo, w)


def _inproj_kernel(x_ref, g_ref, wr_ref, wa_ref, cos_ref, sin_ref, pr_ref, q_ref, k_ref, v_ref):
    h = _rmsnorm(x_ref[...], g_ref[...]).astype(BF16)
    pr_ref[...] = _dot(h, wr_ref[...])
    p = _dot(h, wa_ref[...])
    aw = q_ref.shape[1]

    tm = x_ref.shape[0]
    lane = lax.broadcasted_iota(jnp.int32, (tm, PAIR), 1)
    first_half = (lane % HEAD_DIM) < (ROT_DIM // 2)
    cos, sin = cos_ref[...], sin_ref[...]

    def rotary(x):
        partner = jnp.where(first_half, pltpu.roll(x, PAIR - ROT_DIM // 2, 1), pltpu.roll(x, ROT_DIM // 2, 1))
        return x * cos + partner * sin

    for t in range(aw // PAIR):
        lo = t * PAIR
        q_ref[:, lo:lo + PAIR] = (rotary(p[:, lo:lo + PAIR]) * (1.0 / math.sqrt(HEAD_DIM))).astype(BF16)
        k_ref[:, lo:lo + PAIR] = rotary(p[:, aw + lo:aw + lo + PAIR]).astype(BF16)
    v_ref[...] = p[:, 2 * aw:].astype(BF16)


def _inproj(x2d, gain, w_r, w_a, cos_t, sin_t, tm=1024):
    T, D = x2d.shape
    rw, aw = w_r.shape[1], w_a.shape[1] // 3
    seq_tiles = cos_t.shape[0] // tm
    row = lambda i: (i, 0)
    fixed = lambda i: (0, 0)
    tab = pl.BlockSpec((tm, PAIR), lambda i: (i % seq_tiles, 0))
    return pl.pallas_call(
        _inproj_kernel,
        out_shape=(jax.ShapeDtypeStruct((T, rw), F32),) + (jax.ShapeDtypeStruct((T, aw), BF16),) * 3,
        grid=(T // tm,),
        in_specs=[pl.BlockSpec((tm, D), row), pl.BlockSpec((1, D), fixed), pl.BlockSpec(w_r.shape, fixed),
                  pl.BlockSpec(w_a.shape, fixed), tab, tab],
        out_specs=(pl.BlockSpec((tm, rw), row),) + (pl.BlockSpec((tm, aw), row),) * 3,
        compiler_params=pltpu.CompilerParams(dimension_semantics=("parallel",), vmem_limit_bytes=VMEM_LIMIT),
        name="inproj",
    )(x2d, gain, w_r, w_a, cos_t, sin_t)


def _each(fn, *lists):
    return [fn(*args) for args in zip(*lists)]


def _rwkv_chunk_terms(r, k, v, logw, cum, kkn, bb, same_blk, strict_c, incl_c, lane_lo):
    L = CHUNK
    bf = lambda t: t.astype(BF16)
    cum_last = [t[L - 1:L, :] for t in cum]
    g_in = _each(jnp.exp, cum)
    g_ex = _each(lambda t, w: jnp.exp(t - w), cum, logw)
    g_inv = _each(lambda t: jnp.exp(-t), cum)
    g_hat = _each(lambda tl, t: jnp.exp(tl - t), cum_last, cum)
    g_last = _each(jnp.exp, cum_last)
    yield

    def by_head_rows(x):
        lo = lane_lo if x.shape[1] == PAIR else jnp.concatenate([lane_lo] * (x.shape[1] // PAIR), axis=1)
        return jnp.concatenate([jnp.where(lo, x, 0.0), jnp.where(lo, 0.0, x)], axis=0).astype(BF16)

    al = _each(lambda t, g: -t * g, kkn, g_ex)
    rb = _each(jnp.multiply, r, g_in)
    bt = _each(jnp.multiply, bb, g_inv)
    kt = _each(jnp.multiply, k, g_inv)
    bh = _each(jnp.multiply, bb, g_hat)
    kh = _each(jnp.multiply, k, g_hat)
    lhs = _each(lambda a, b: jnp.concatenate([a, b], axis=0).astype(BF16), al, rb)
    rhs = _each(lambda a, b: jnp.concatenate([by_head_rows(a), by_head_rows(b)], axis=0), bt, kt)
    yield
    aq = _each(_dot_nt, lhs, rhs)
    a_ab = [jnp.where(strict_c, t[:L, :2 * L], 0.0) for t in aq]
    a_ak = [jnp.where(strict_c, t[:L, 2 * L:], 0.0).astype(BF16) for t in aq]
    a_rb = [jnp.where(incl_c, t[L:, :2 * L], 0.0).astype(BF16) for t in aq]
    a_rk = [jnp.where(incl_c, t[L:, 2 * L:], 0.0).astype(BF16) for t in aq]
    yield

    v_s = _each(by_head_rows, v)
    av = _each(_dot, a_ak, v_s)
    BS = SOLVE_BLOCK
    n_blk = L // BS
    rhs0 = _each(lambda a, b: jnp.concatenate([a, b], axis=1), al, av)
    lo2 = lax.broadcasted_iota(jnp.int32, (BS, 2 * PAIR), 1) % PAIR < HEAD_DIM
    lane = lax.broadcasted_iota(jnp.int32, (BS, PAIR), 1)
    br =lax.broadcasted_iota(jnp.int32, (PAIR, PAIR), 0)
    bc = lax.broadcasted_iota(jnp.int32, (PAIR, PAIR), 1)
    blk_diag = (br // BS) == (bc // BS)

    def expand(pack):
        return jnp.where(blk_diag, jnp.concatenate([pack] * (PAIR // BS), axis=0), 0.0).astype(BF16)

    d_k = [functools.reduce(jnp.add, [jnp.where((lane % HEAD_DIM) // BS == b, t[b * BS:(b + 1) * BS, :], 0.0)
                                      for b in range(n_blk)]) for t in a_ab]
    t_m = d_k
    yield
    n_sq = int(math.log2(BS)) - 1
    d_k = _each(lambda d: _dot(d.astype(BF16), expand(d)), d_k)
    yield
    for lvl in range(n_sq):
        d_e = _each(expand, d_k)
        t_m = _each(lambda t, d, de: t + d + _dot(t.astype(BF16), de), t_m, d_k, d_e)
        if lvl + 1 < n_sq:
            d_k = _each(lambda d, de: _dot(d.astype(BF16), de), d_k, d_e)
        yield
    t_mb = _each(bf, t_m)

    zero_blk = jnp.zeros((BS, 2 * PAIR), BF16)
    x_blocks = [[] for _ in a_ab]
    xs_lo = [[] for _ in a_ab]
    xs_hi = [[] for _ in a_ab]

    def stacked(lo_parts, hi_parts):
        rows = [lo_parts.get(c, zero_blk) for c in range(n_blk)] + [hi_parts.get(c, zero_blk) for c in range(n_blk)]
        return jnp.concatenate(rows, axis=0)

    for b in range(n_blk):
        r_b = [t[b * BS:(b + 1) * BS, :] for t in rhs0]
        if b > 0:
            r_b = _each(lambda rr, a, lo_p, hi_p: rr + _dot(a[b * BS:(b + 1) * BS, :].astype(BF16),
                                                            stacked(dict(enumerate(lo_p)), dict(enumerate(hi_p)))),
                        r_b, a_ab, xs_lo, xs_hi)
            yield
        x_b = _each(lambda rr, tm: rr + _dot(tm, stacked({b: jnp.where(lo2, rr, 0.0).astype(BF16)},
                                                        {b: jnp.where(lo2, 0.0, rr).astype(BF16)})),
                    r_b, t_mb)
        for i, xb in enumerate(x_b):
            x_blocks[i].append(xb)
            xs_lo[i].append(jnp.where(lo2, xb, 0.0).astype(BF16))
            xs_hi[i].append(jnp.where(lo2, 0.0, xb).astype(BF16))
        yield
    x = [jnp.concatenate(blks, axis=0) for blks in x_blocks]
    x_s = _each(lambda lo_p, hi_p: jnp.concatenate(lo_p + hi_p, axis=0), xs_lo, xs_hi)

    z = _each(_dot, a_rb, x_s)
    akv = _each(_dot, a_rk, v_s)
    w2 = _each(lambda a, t: (a + t[:, :PAIR]).astype(BF16), rb, z)
    y_loc = _each(lambda t, a: t[:, PAIR:] + a, z, akv)
    w1 = [t[:, :PAIR].astype(BF16) for t in x]
    u_loc = [t[:, PAIR:] for t in x]
    yield

    m_t = _each(lambda a, b: jnp.where(same_blk, _dot_tn(a, b.astype(BF16)), 0.0).astype(BF16), w1, bh)
    s_loc = _each(
        lambda u, vv, b, kk_: jnp.where(
            same_blk,
            _dot_tn(jnp.concatenate([u, vv], axis=0).astype(BF16), jnp.concatenate([b, kk_], axis=0).astype(BF16)),
            0.0),
        u_loc, v, bh, kh)
    return w2, y_loc, m_t, s_loc, g_last


def _rwkv_kernel(p_ref, mix_ref, wc_ref, w0_ref, a0_ref, kk_ref, ka_ref, rk_ref, lnw_ref, lnb_ref,
                 o_ref, s_scr, carry_scr, *, nb, width, n_lora, n_chunks, n_groups, stage_offset):
    L = CHUNK
    R = n_chunks * L
    c = pl.program_id(0)

    @pl.when(c == 0)
    def _():
        s_scr[...] = jnp.zeros_like(s_scr)
        carry_scr[...] = jnp.zeros_like(carry_scr)

    n_pairs = width // PAIR
    row = lax.broadcasted_iota(jnp.int32, (R, 1), 0)
    ri = lax.broadcasted_iota(jnp.int32, (2 * L, 2 * L), 0)
    ci = lax.broadcasted_iota(jnp.int32, (2 * L, 2 * L), 1)
    same_blk = (ri >= L) == (ci >= L)
    ti = lax.broadcasted_iota(jnp.int32, (L, 2 * L), 0)
    si = lax.broadcasted_iota(jnp.int32, (L, 2 * L), 1) % L
    strict_c = si < ti
    incl_c = si <= ti
    rr = lax.broadcasted_iota(jnp.int32, (R, R), 0)
    rc = lax.broadcasted_iota(jnp.int32, (R, R), 1)
    tri = ((rr // L == rc // L) & (rr >= rc)).astype(BF16)
    lane_lo = lax.broadcasted_iota(jnp.int32, (L, PAIR), 1) < HEAD_DIM
    lane_lo_r = lax.broadcasted_iota(jnp.int32, (R, PAIR), 1) < HEAD_DIM
    lora_lane = lax.broadcasted_iota(jnp.int32, (R, 2 * LANES), 1)

    def head_sums(x):
        s0 = jnp.sum(jnp.where(lane_lo_r, x, 0.0), axis=-1, keepdims=True)
        s1 = jnp.sum(jnp.where(lane_lo_r, 0.0, x), axis=-1, keepdims=True)
        return jnp.where(lane_lo_r, s0, s1)

    mix = mix_ref[...]

    def lora_up(pm):
        lora = pm[:, 3 * width:]
        act = jnp.where(lora_lane < n_lora[0], jnp.tanh(lora),
                        jnp.where(lora_lane < n_lora[1], lora,
                                  jnp.where(lora_lane < n_lora[2], _sigmoid(lora), 0.0)))
        return _dot(act.astype(BF16), wc_ref[...])

    def log_decay(up):
        t = w0_ref[...] + up[:, :width]
        return -jnp.exp(-(jnp.maximum(-t, 0.0) + jnp.log(1.0 + jnp.exp(-jnp.abs(t)))) - 0.5)

    def running_sum(t):
        h1 = t.astype(BF16)
        r1 = t - h1.astype(F32)
        h2 = r1.astype(BF16)
        h3 = (r1 - h2.astype(F32)).astype(BF16)
        return _dot(tri, h1) + _dot(tri, h2) + _dot(tri, h3)

    tiles = [(b, pr) for b in range(nb) for pr in range(n_pairs)]
    n_t = len(tiles)
    lanes = lambda pr, part=0: slice(part * width + pr * PAIR, part * width + (pr + 1) * PAIR)
    sls = [lanes(pr) for _, pr in tiles]

    states = {0: [s_scr[i] for i in range(n_t)]}
    outs = {}

    def run_group(gi):
        g0 = gi * R
        pms = []
        for b in range(nb):
            p = p_ref[b, g0:g0 + R, :]
            prev_last = carry_scr[b, 7:8, :] if gi == 0 else p_ref[b, g0 - 1:g0, :]
            p_prev = jnp.where(row == 0, prev_last, pltpu.roll(p, 1, 0))
            pms.append(p + (p_prev - p) * mix)
        yield
        ups = _each(lora_up, pms)
        logw_b = _each(log_decay, ups)
        cum_b = _each(running_sum, logw_b)
        yield

        r = [pms[b][:, lanes(pr, 0)] for b, pr in tiles]
        k_raw = [pms[b][:, lanes(pr, 1)] for b, pr in tiles]
        v = [pms[b][:, lanes(pr, 2)] for b, pr in tiles]
        logw = [logw_b[b][:, lanes(pr)] for b, pr in tiles]
        cum = [cum_b[b][:, lanes(pr)] for b, pr in tiles]
        a = [_sigmoid(a0_ref[:, lanes(pr)] + ups[b][:, lanes(pr, 1)]) for b, pr in tiles]
        g = [ups[b][:, lanes(pr, 2)] for b, pr in tiles]

        kk = _each(lambda t, sl: t * kk_ref[:, sl], k_raw, sls)
        kk_ss = _each(lambda t: head_sums(t * t), kk)
        kkn = _each(lambda t, ss: t / jnp.maximum(jnp.sqrt(ss), 1e-12), kk, kk_ss)
        k = _each(lambda t, aa, sl: t * (1.0 + (aa - 1.0) * ka_ref[:, sl]), k_raw, a, sls)
        bb = _each(jnp.multiply, kkn, a)
        bonus_dot = _each(lambda rr_, kk_, sl: head_sums(rr_ * kk_ * rk_ref[:, sl]), r, k, sls)
        yield

        def chunks(ts):
            return [t[ch * L:(ch + 1) * L] for ch in range(n_chunks) for t in ts]

        w2, y_loc, m_t, s_loc, g_last = yield from _rwkv_chunk_terms(
            chunks(r), chunks(k), chunks(v), chunks(logw), chunks(cum), chunks(kkn), chunks(bb),
            same_blk, strict_c, incl_c, lane_lo)
        yield

        s = states[gi]
        y_parts = []
        for ch in range(n_chunks):
            sel = slice(ch * n_t, (ch + 1) * n_t)
            s_b = _each(lambda t: t.astype(BF16), s)
            y_parts.append(_each(lambda a_, sb, yl: _dot_nt(a_, sb) + yl, w2[sel], s_b, y_loc[sel]))
            s = _each(lambda s0, gl, sb, m, sl: s0 * gl + _dot(sb, m) + sl, s, g_last[sel], s_b, m_t[sel], s_loc[sel])
        states[gi + 1] = s
        y = [jnp.concatenate([y_parts[ch][i] for ch in range(n_chunks)], axis=0) for i in range(n_t)]
        yield

        mu = _each(lambda t: head_sums(t) * (1.0 / HEAD_DIM), y)
        yc = _each(jnp.subtract, y, mu)
        var = _each(lambda t: head_sums(t * t) * (1.0 / HEAD_DIM), yc)
        yield
        res = []
        for i in range(n_t):
            sl = sls[i]
            yn = yc[i] * lax.rsqrt(var[i] + GN_EPS) * lnw_ref[:, sl] + lnb_ref[:, sl]
            res.append(((yn + bonus_dot[i] * v[i]) * g[i]).astype(o_ref.dtype))
        outs[gi] = res

    gens = [run_group(gi) for gi in range(n_groups)]
    live = [True] * n_groups
    tick = 0
    while any(live):
        for gi in range(n_groups):
            if live[gi] and tick >= gi * stage_offset:
                try:
                    next(gens[gi])
                except StopIteration:
                    live[gi] = False
        tick += 1

    for gi in range(n_groups):
        for i, (b, pr) in enumerate(tiles):
            o_ref[b, gi * R:(gi + 1) * R, sls[i]] = outs[gi][i]
    for i in range(n_t):
        s_scr[i] = states[n_groups][i]
    for b in range(nb):
        carry_scr[b] = p_ref[b, n_groups * R - 8:n_groups * R, :]


def _rwkv(p_r, mix, wc, w0, a0, k_k, k_a, r_k, ln_w, ln_b, width, n_lora, n_chunks=2, n_groups=4, stage_offset=5):
    B, S, C = p_r.shape
    L = CHUNK * n_chunks * n_groups
    n_state = B * (width // PAIR)
    vec = lambda n: pl.BlockSpec((1, n), lambda c: (0, 0))
    return pl.pallas_call(
        functools.partial(_rwkv_kernel, nb=B, width=width, n_lora=n_lora, n_chunks=n_chunks, n_groups=n_groups,
                          stage_offset=stage_offset),
        out_shape=jax.ShapeDtypeStruct((B, S, width), BF16),
        grid=(S // L,),
        in_specs=[pl.BlockSpec((B, L, C), lambda c: (0, c, 0)), vec(C),
                  pl.BlockSpec(wc.shape, lambda c: (0, 0))] + [vec(width)] * 7,
        out_specs=pl.BlockSpec((B, L, width), lambda c: (0, c, 0)),
        scratch_shapes=[pltpu.VMEM((n_state, PAIR, PAIR), F32), pltpu.VMEM((B, 8, C), F32)],
        compiler_params=pltpu.CompilerParams(dimension_semantics=("arbitrary",), vmem_limit_bytes=VMEM_LIMIT),
        name="rwkv7",
    )(p_r, mix, wc, w0, a0, k_k, k_a, r_k, ln_w, ln_b)


SB_ROWS = 2048
SB_QUARTER = SB_ROWS // 4
N_RES = 4


def _attn_segments(pi, g):
    if pi == 0:
        return [((g // 4) * SB_QUARTER + r * ATTN_BLOCK + (ATTN_BLOCK // N_RES) * (g % 4), ATTN_BLOCK // N_RES, 1)
                for r in range(N_RES)]
    if pi == 1:
        return [((g // 4) * SB_QUARTER + (g % 4) * ATTN_BLOCK, ATTN_BLOCK, 1)]
    return [(qq * SB_QUARTER + (g % 4) * ATTN_BLOCK + g // 4, ATTN_BLOCK // 4, 4) for qq in range(4)]


def _attn_prev_tile(pi, g):
    if pi == 0:
        return (g + 15) % 16, g == 0
    if pi == 1:
        return ((g // 4 + 3) % 4) * 4 + g % 4, g < 4
    return g, True


def _attn_kernel(q_ref, k_ref, v_ref, gain_ref, o_ref,
                 nat_q, nat_k, nat_v, q_scr, k_ring, v_ring, bias_scr, out_nat, *stat_scr, tiles_per_group):
    sb = pl.program_id(2)
    n_pat = len(DILATED_PATTERNS)
    assert DILATED_PATTERNS == ((128, 1), (512, 4), (2048, 16)) and ATTN_BLOCK == 128
    num_scr, m_scr, l_scr = stat_scr[:n_pat], stat_scr[n_pat:2 * n_pat], stat_scr[2 * n_pat:]
    Q = ATTN_BLOCK
    cur_base = (sb % 2) * SB_ROWS
    other_base = SB_ROWS - cur_base

    @pl.when(sb == 0)
    def _():
        k_ring[pl.ds(SB_ROWS, SB_ROWS), :] = jnp.zeros((SB_ROWS, PAIR), F32)
        v_ring[pl.ds(SB_ROWS, SB_ROWS), :] = jnp.zeros((SB_ROWS, PAIR), F32)

    nat_q[...] = q_ref[...].astype(F32)
    nat_k[...] = k_ref[...].astype(F32)
    nat_v[...] = v_ref[...].astype(F32)
    for qq in range(4):
        for r in range(N_RES):
            src = pl.ds(qq * SB_QUARTER + r, Q, stride=N_RES)
            dst = qq * SB_QUARTER + r * Q
            q_scr[dst:dst + Q, :] = nat_q[src, :]
            k_ring[pl.ds(cur_base + dst, Q), :] = nat_k[src, :]
            v_ring[pl.ds(cur_base + dst, Q), :] = nat_v[src, :]

    @pl.when(sb == 0)
    def _():
        ii = lax.broadcasted_iota(jnp.int32, (2 * Q, 2 * Q), 0) % Q
        cj = lax.broadcasted_iota(jnp.int32, (2 * Q, 2 * Q), 1)
        per = Q // N_RES
        for kind in range(2):
            if kind == 0:
                qi = N_RES * (ii % per) + ii // per
                kj = N_RES * ((cj % Q) % per) + (cj % Q) // per + (cj // Q) * Q
            else:
                qi, kj = ii, cj
            band = (kj >= qi) & (kj <= qi + Q)
            bias_scr[kind, 1] = jnp.where(band, 0.0, NEG_BIG)
            bias_scr[kind, 0] = jnp.where(band & (cj >= Q), 0.0, NEG_BIG)

    def load_tile(ref, base, segs):
        parts = [ref[pl.ds(base + s, n, stride=st) if st > 1 else pl.ds(base + s, n), :] for s, n, st in segs]
        return parts[0] if len(parts) == 1 else jnp.concatenate(parts, axis=0)

    def store_tile(ref, segs, val):
        off = 0
        for s, n, st in segs:
            ref[pl.ds(s, n, stride=st) if st > 1 else pl.ds(s, n), :] = val[off:off + n]
            off += n

    lane_lo = lax.broadcasted_iota(jnp.int32, (Q, PAIR), 1) < HEAD_DIM
    ones_blk = jnp.ones((2 * Q, PAIR), BF16)

    n_blk = SB_ROWS // Q
    has_prev_sb = jnp.where(sb > 0, 1, 0)

    def scores(pi, gs):
        segs = [_attn_segments(pi, g) for g in gs]
        prev = [_attn_prev_tile(pi, g) for g in gs]
        prev_segs = [_attn_segments(pi, pg) for pg, _ in prev]
        prev_base = [other_base if other else cur_base for _, other in prev]
        q2 = [load_tile(q_scr, 0, sg) for sg in segs]
        q2 = [jnp.concatenate([jnp.where(lane_lo, t, 0.0), jnp.where(lane_lo, 0.0, t)], axis=0).astype(BF16)
              for t in q2]
        kcat = [jnp.concatenate([load_tile(k_ring, pb, psg), load_tile(k_ring, cur_base, sg)], axis=0).astype(BF16)
                for pb, psg, sg in zip(prev_base, prev_segs, segs)]
        vcat = [jnp.concatenate([load_tile(v_ring, pb, psg), load_tile(v_ring, cur_base, sg)], axis=0).astype(BF16)
                for pb, psg, sg in zip(prev_base, prev_segs, segs)]
        vext = [jnp.concatenate([t, ones_blk], axis=1) for t in vcat]
        has_prev = [has_prev_sb if other else 1 for _, other in prev]
        return dict(pi=pi, segs=segs, s=_each(_dot_nt, q2, kcat), vext=vext, has_prev=has_prev)

    def softmax(c):
        kind = 0 if c["pi"] == 0 else 1
        s = _each(lambda t, hp: t + bias_scr[kind, hp], c["s"], c["has_prev"])
        m = [jnp.max(t, axis=-1, keepdims=True) for t in s]
        p = _each(lambda t, mm: jnp.exp(t - mm).astype(BF16), s, m)
        return dict(pi=c["pi"], segs=c["segs"], vext=c["vext"], m=m, p=p)

    def weighted_values(c):
        pi, segs, m = c["pi"], c["segs"], c["m"]
        nl = _each(_dot, c["p"], c["vext"])
        for t in range(len(segs)):
            store_tile(num_scr[pi], segs[t], jnp.where(lane_lo, nl[t][:Q, :PAIR], nl[t][Q:, :PAIR]))
            store_tile(l_scr[pi], segs[t], jnp.where(lane_lo, nl[t][:Q, PAIR:], nl[t][Q:, PAIR:]))
            store_tile(m_scr[pi], segs[t], jnp.where(lane_lo, m[t][:Q], m[t][Q:]))

    per_q = n_blk // 4
    assert tiles_per_group == per_q
    groups = [(2, list(range(g0, g0 + per_q))) for g0 in range(0, n_blk, per_q)]
    for qq in range(4):
        groups += [(1, list(range(qq * per_q, (qq + 1) * per_q))), (0, list(range(qq * per_q, (qq + 1) * per_q)))]
    n_grp = len(groups)

    ri = lax.broadcasted_iota(jnp.int32, (PAIR, PAIR), 0)
    ci = lax.broadcasted_iota(jnp.int32, (PAIR, PAIR), 1)
    seg_ones = ((ri >= HEAD_DIM) == (ci >= HEAD_DIM)).astype(BF16)
    gain = gain_ref[...]

    def merge(i):
        rows = slice(i * Q, (i + 1) * Q)
        ms = [m_scr[pi][rows, :] for pi in range(n_pat)]
        m_all = functools.reduce(jnp.maximum, ms)
        num = 0.0
        den = 0.0
        for pi in range(n_pat):
            wgt = jnp.exp(ms[pi] - m_all)
            num = num + wgt * num_scr[pi][rows, :]
            den = den + wgt * l_scr[pi][rows, :]
        o = num / den
        ms_o = _split_dot(o * o, seg_ones) * (1.0 / HEAD_DIM)
        out_nat[pl.ds((i // N_RES) * SB_QUARTER + i % N_RES, Q, stride=N_RES), :] = o * lax.rsqrt(ms_o + NORM_EPS) * gain

    sc = {0: scores(*groups[0]), 1: scores(*groups[1])}
    sm = {0: softmax(sc.pop(0))}
    for k in range(n_grp):
        if k + 2 < n_grp:
            sc[k + 2] = scores(*groups[k + 2])
        if k + 1 < n_grp:
            sm[k + 1] = softmax(sc.pop(k + 1))
        weighted_values(sm.pop(k))
        pi, gs = groups[k]
        if pi == 0:
            for i in gs:
                merge(i)
            quarter = slice(gs[0] * Q, (gs[-1] + 1) * Q)
            o_ref[quarter, :] = out_nat[quarter, :].astype(o_ref.dtype)


def _attention(q, k, v, gain, tiles_per_group=4):
    B, S, W = q.shape
    n_pairs = W // PAIR
    n_pat = len(DILATED_PATTERNS)
    blk = pl.BlockSpec((None, SB_ROWS, PAIR), lambda b, p, s: (b, s, p))
    tile = pltpu.VMEM((SB_ROWS, PAIR), F32)
    return pl.pallas_call(
        functools.partial(_attn_kernel, tiles_per_group=tiles_per_group),
        out_shape=jax.ShapeDtypeStruct((B, S, W), BF16),
        grid=(B, n_pairs, S // SB_ROWS),
        in_specs=[blk, blk, blk, pl.BlockSpec((1, PAIR), lambda b, p, s: (0, p))],
        out_specs=blk,
        scratch_shapes=[tile, tile, tile, tile,
                        pltpu.VMEM((2 * SB_ROWS, PAIR), F32),
                        pltpu.VMEM((2 * SB_ROWS, PAIR), F32),
                        pltpu.VMEM((2, 2, 2 * ATTN_BLOCK, 2 * ATTN_BLOCK), F32),
                        tile]
                       + [tile] * (3 * n_pat),
        compiler_params=pltpu.CompilerParams(dimension_semantics=("parallel", "parallel", "arbitrary"),
                                             vmem_limit_bytes=VMEM_LIMIT),
        name="dilated_attn",
    )(q, k, v, gain)


def _rotary_tables(seq):
    half = ROT_DIM // 2
    inv_freq = ROPE_THETA ** (-jnp.arange(half, dtype=F32) * 2.0 / ROT_DIM)
    ang = jnp.arange(seq).astype(F32)[:, None] * inv_freq[None, :]
    cos, sin = jnp.cos(ang), jnp.sin(ang)
    rest = HEAD_DIM - ROT_DIM
    cos_h = jnp.concatenate([cos, cos, jnp.ones((seq, rest), F32)], axis=-1)
    sin_h = jnp.concatenate([-sin, sin, jnp.zeros((seq, rest), F32)], axis=-1)
    return jnp.tile(cos_h, (1, PAIR // HEAD_DIM)), jnp.tile(sin_h, (1, PAIR // HEAD_DIM))


FFN_HALO = 16
MIX_PIECES = 4


def _mix_ffn_kernel(x_ref, ya_ref, yb_ref, xh_ref, yah_ref, ybh_ref, wo_ref, g_ref, wu_ref, cw_ref,
                    cb_ref, wd_ref, fg_ref, o_ref, h_scr, hh_scr, *, tm, seq, row_chunks, apply_final):
    tf = wd_ref.shape[0]
    i = pl.program_id(0)
    j = pl.program_id(1)
    wa = ya_ref.shape[1]

    def mixed(x, ya, yb):
        return x + _dot(ya, wo_ref[:wa, :]) + _dot(yb, wo_ref[wa:, :])

    @pl.when(j == 0)
    def _():
        hh_scr[...] = _rmsnorm(mixed(xh_ref[...], yah_ref[...], ybh_ref[...]), g_ref[...]).astype(BF16)
        piece = tm // MIX_PIECES
        for c in range(MIX_PIECES):
            rows = slice(c * piece, (c + 1) * piece)
            x1 = mixed(x_ref[rows, :], ya_ref[rows, :], yb_ref[rows, :])
            o_ref[rows, :] = x1
            h_scr[rows, :] = _rmsnorm(x1, g_ref[...]).astype(BF16)

    seq_start = (i * tm) % seq == 0
    rc = tm // row_chunks
    row = lax.broadcasted_iota(jnp.int32, (rc, 1), 0)
    for c in range(row_chunks):
        rows = slice(c * rc, (c + 1) * rc)
        gv = _dot(h_scr[rows, :], wu_ref[...])
        gate, val = gv[:, :tf], gv[:, tf:]
        if c == 0:
            gate_h = jnp.where(seq_start, 0.0, _dot(hh_scr[...], wu_ref[:, :tf]))
        else:
            gate_h = _dot(h_scr[c * rc - FFN_HALO:c * rc, :], wu_ref[:, :tf])
        g1 = jnp.where(row == 0, gate_h[FFN_HALO - 1:FFN_HALO, :], pltpu.roll(gate, 1, 0))
        g2 = jnp.where(row == 0, gate_h[FFN_HALO - 2:FFN_HALO - 1, :],
                       jnp.where(row == 1, gate_h[FFN_HALO - 1:FFN_HALO, :], pltpu.roll(gate, 2, 0)))
        u = cw_ref[0:1, :] * g2 + cw_ref[1:2, :] * g1 + cw_ref[2:3, :] * gate + cb_ref[...]
        act = (u * _sigmoid(u) * val).astype(BF16)
        o_ref[rows, :] += _dot(act, wd_ref[...])

    if apply_final:
        @pl.when(j == pl.num_programs(1) - 1)
        def _():
            o_ref[...] = _rmsnorm(o_ref[...], fg_ref[...])


def _mix_ffn(x2d, y_a, y_b, w_o, gain, w_up, conv_w, conv_b, w_down, final_gain, seq, apply_final,
             tm=1024, n_ff_tiles=2, row_chunks=1):
    T, D = x2d.shape
    F = w_down.shape[0]
    tf = F // n_ff_tiles
    assert tf % LANES == 0 and seq % tm == 0 and w_up.shape[1] == 2 * F
    w_up = jnp.concatenate([w_up[:, half * F + j * tf: half * F + (j + 1) * tf]
                            for j in range(n_ff_tiles) for half in range(2)], axis=1)
    halo_blocks = tm // FFN_HALO
    row = lambda i, j: (i, 0)
    halo = lambda i, j: (jnp.maximum(i * halo_blocks - 1, 0), 0)
    fixed = lambda i, j: (0, 0)
    wa, wb = y_a.shape[1], y_b.shape[1]
    return pl.pallas_call(
        functools.partial(_mix_ffn_kernel, tm=tm, seq=seq, row_chunks=row_chunks, apply_final=apply_final),
        out_shape=jax.ShapeDtypeStruct((T, D), F32),
        grid=(T // tm, n_ff_tiles),
        in_specs=[pl.BlockSpec((tm, D), row), pl.BlockSpec((tm, wa), row), pl.BlockSpec((tm, wb), row),
                  pl.BlockSpec((FFN_HALO, D), halo), pl.BlockSpec((FFN_HALO, wa), halo),
                  pl.BlockSpec((FFN_HALO, wb), halo),
                  pl.BlockSpec(w_o.shape, fixed),
                  pl.BlockSpec((1, D), fixed),
                  pl.BlockSpec((D, 2 * tf), lambda i, j: (0, j)),
                  pl.BlockSpec((CONV_WIDTH, tf), lambda i, j: (0, j)),
                  pl.BlockSpec((1, tf), lambda i, j: (0, j)),
                  pl.BlockSpec((tf, D), lambda i, j: (j, 0)),
                  pl.BlockSpec((1, D), fixed)],
        out_specs=pl.BlockSpec((tm, D), row),
        scratch_shapes=[pltpu.VMEM((tm, D), BF16), pltpu.VMEM((FFN_HALO, D), BF16)],
        compiler_params=pltpu.CompilerParams(dimension_semantics=("parallel", "arbitrary"),
                                             vmem_limit_bytes=VMEM_LIMIT),
        name="mix_convglu_ffn",
    )(x2d, y_a, y_b, x2d, y_a, y_b, w_o, gain, w_up, conv_w, conv_b, w_down, final_gain)


def kernel(x, mix_norm_gain, w_in, rwkv_shift_mix, w0, w_lora_up, a0, a_lora_up, g_lora_up, k_k, k_a, r_k,
           ln_x_w, ln_x_b, attn_norm_gain, w_out, ffn_norm_gain, w_ffn_up, ffn_conv_w, ffn_conv_b,
           w_ffn_down, final_norm_gain):
    B, S, D = x.shape
    depth = w_in.shape[0]
    rw = w0.shape[1]
    aw = attn_norm_gain.shape[1]
    n_w, n_a, n_g = w_lora_up.shape[1], a_lora_up.shape[1], g_lora_up.shape[1]
    n_lora = n_w + n_a + n_g
    lora_pad = -(-n_lora // (2 * LANES)) * (2 * LANES)
    assert lora_pad == 2 * LANES and rw % PAIR == 0 and aw % PAIR == 0
    rwkv_cols = 3 * rw + n_lora
    cos_t, sin_t = _rotary_tables(S)

    x2d = x.reshape(B * S, D)
    for l in range(depth):
        w_r = w_in[l][:, :3 * rw + lora_pad].astype(BF16)
        w_a = w_in[l][:, rwkv_cols:].astype(BF16)
        mix =jnp.concatenate([rwkv_shift_mix[l], jnp.zeros((lora_pad - n_lora,), F32)])[None, :]
        wc = jnp.zeros((lora_pad, 3 * rw), F32)
        wc = wc.at[:n_w, :rw].set(w_lora_up[l])
        wc = wc.at[n_w:n_w + n_a, rw:2 * rw].set(a_lora_up[l])
        wc = wc.at[n_w + n_a:n_lora, 2 * rw:].set(g_lora_up[l]).astype(BF16)

        p_r, q, k, v = _inproj(x2d, mix_norm_gain[l][None, :], w_r, w_a, cos_t, sin_t)
        y_rwkv = _rwkv(p_r.reshape(B, S, -1), mix, wc, w0[l][None, :], a0[l][None, :], k_k[l][None, :],
                       k_a[l][None, :], r_k[l].reshape(1, rw), ln_x_w[l][None, :], ln_x_b[l][None, :],
                       rw, (n_w, n_w + n_a, n_lora))
        y_attn = _attention(q.reshape(B, S, aw), k.reshape(B, S, aw), v.reshape(B, S, aw),
                            attn_norm_gain[l][None, :])
        x2d = _mix_ffn(x2d, y_rwkv.reshape(B * S, rw), y_attn.reshape(B * S, aw), w_out[l].astype(BF16),
                       ffn_norm_gain[l][None, :], w_ffn_up[l].astype(BF16), ffn_conv_w[l],
                       ffn_conv_b[l][None, :], w_ffn_down[l].astype(BF16), final_norm_gain[None, :], S,
                       apply_final=(l == depth - 1))
    return x2d.reshape(B, S, D)
```

```python
import functools
import math

import jax
import jax.numpy as jnp
from jax import lax
from jax.experimental import pallas as pl
from jax.experimental.pallas import tpu as pltpu

F32 = jnp.float32
BF16 = jnp.bfloat16

LANES = 128
HEAD_DIM = 64
PAIR = 2 * HEAD_DIM
ROT_DIM = HEAD_DIM // 4
ROPE_THETA = 500000.0
NORM_EPS = 1e-6
GN_EPS = 64e-5
DILATED_PATTERNS = ((128, 1), (512, 4), (2048, 16))
ATTN_BLOCK = 128
CONV_WIDTH = 3
CHUNK = 64
SOLVE_BLOCK = 16
NEG_BIG = -1e30
VMEM_LIMIT = 56 * 1024 * 1024


def _dot(a, b):
    return jnp.dot(a, b, preferred_element_type=F32)


def _dot_nt(a, b):
    return lax.dot_general(a, b, (((1,), (1,)), ((), ())), preferred_element_type=F32)


def _dot_tn(a, b):
    return lax.dot_general(a, b, (((0,), (0,)), ((), ())), preferred_element_type=F32)


def _rmsnorm(x, gain):
    return x * lax.rsqrt(jnp.mean(x * x, axis=-1, keepdims=True) + NORM_EPS) * gain


def _sigmoid(x):
    return 1.0 / (1.0 + jnp.exp(-x))


def _split_dot(x, w):
    hi = x.astype(BF16)
    lo = (x - hi.astype(F32)).astype(BF16)
    return _dot(hi, w) + _dot(lo, w)


def _inproj_kernel(x_ref, g_ref, wr_ref, wa_ref, cos_ref, sin_ref, pr_ref, q_ref, k_ref, v_ref):
    h = _rmsnorm(x_ref[...], g_ref[...]).astype(BF16)
    pr_ref[...] = _dot(h, wr_ref[...])
    p = _dot(h, wa_ref[...])
    aw = q_ref.shape[1]

    tm = x_ref.shape[0]
    lane = lax.broadcasted_iota(jnp.int32, (tm, PAIR), 1)
    first_half = (lane % HEAD_DIM) < (ROT_DIM // 2)
    cos, sin = cos_ref[...], sin_ref[...]

    def rotary(x):
        partner = jnp.where(first_half, pltpu.roll(x, PAIR - ROT_DIM // 2, 1), pltpu.roll(x, ROT_DIM // 2, 1))
        return x * cos + partner * sin

    for t in range(aw // PAIR):
        lo = t * PAIR
        q_ref[:, lo:lo + PAIR] = (rotary(p[:, lo:lo + PAIR]) * (1.0 / math.sqrt(HEAD_DIM))).astype(BF16)
        k_ref[:, lo:lo + PAIR] = rotary(p[:, aw + lo:aw + lo + PAIR]).astype(BF16)
    v_ref[...] = p[:, 2 * aw:].astype(BF16)


def _inproj(x2d, gain, w_r, w_a, cos_t, sin_t, tm=1024):
    T, D = x2d.shape
    rw, aw = w_r.shape[1], w_a.shape[1] // 3
    seq_tiles = cos_t.shape[0] // tm
    row = lambda i: (i, 0)
    fixed = lambda i: (0, 0)
    tab = pl.BlockSpec((tm, PAIR), lambda i: (i % seq_tiles, 0))
    return pl.pallas_call(
        _inproj_kernel,
        out_shape=(jax.ShapeDtypeStruct((T, rw), F32),) + (jax.ShapeDtypeStruct((T, aw), BF16),) * 3,
        grid=(T // tm,),
        in_specs=[pl.BlockSpec((tm, D), row), pl.BlockSpec((1, D), fixed), pl.BlockSpec(w_r.shape, fixed),
                  pl.BlockSpec(w_a.shape, fixed), tab, tab],
        out_specs=(pl.BlockSpec((tm, rw), row),) + (pl.BlockSpec((tm, aw), row),) * 3,
        compiler_params=pltpu.CompilerParams(dimension_semantics=("parallel",), vmem_limit_bytes=VMEM_LIMIT),
        name="inproj",
    )(x2d, gain, w_r, w_a, cos_t, sin_t)


def _each(fn, *lists):
    return [fn(*args) for args in zip(*lists)]


def _rwkv_chunk_terms(r, k, v, logw, cum, kkn, bb, same_blk, strict_c, incl_c, lane_lo):
    L = CHUNK
    bf = lambda t: t.astype(BF16)
    cum_last = [t[L - 1:L, :] for t in cum]
    g_in = _each(jnp.exp, cum)
    g_ex = _each(lambda t, w: jnp.exp(t - w), cum, logw)
    g_inv = _each(lambda t: jnp.exp(-t), cum)
    g_hat = _each(lambda tl, t: jnp.exp(tl - t), cum_last, cum)
    g_last = _each(jnp.exp, cum_last)
    yield

    def by_head_rows(x):
        lo = lane_lo if x.shape[1] == PAIR else jnp.concatenate([lane_lo] * (x.shape[1] // PAIR), axis=1)
        return jnp.concatenate([jnp.where(lo, x, 0.0), jnp.where(lo, 0.0, x)], axis=0).astype(BF16)

    al = _each(lambda t, g: -t * g, kkn, g_ex)
    rb = _each(jnp.multiply, r, g_in)
    bt = _each(jnp.multiply, bb, g_inv)
    kt = _each(jnp.multiply, k, g_inv)
    bh = _each(jnp.multiply, bb, g_hat)
    kh = _each(jnp.multiply, k, g_hat)
    lhs = _each(lambda a, b: jnp.concatenate([a, b], axis=0).astype(BF16), al, rb)
    rhs = _each(lambda a, b: jnp.concatenate([by_head_rows(a), by_head_rows(b)], axis=0), bt, kt)
    yield
    aq = _each(_dot_nt, lhs, rhs)
    a_ab = [jnp.where(strict_c, t[:L, :2 * L], 0.0) for t in aq]
    a_ak = [jnp.where(strict_c, t[:L, 2 * L:], 0.0).astype(BF16) for t in aq]
    a_rb = [jnp.where(incl_c, t[L:, :2 * L], 0.0).astype(BF16) for t in aq]
    a_rk = [jnp.where(incl_c, t[L:, 2 * L:], 0.0).astype(BF16) for t in aq]
    yield

    v_s = _each(by_head_rows, v)
    av = _each(_dot, a_ak, v_s)
    BS = SOLVE_BLOCK
    n_blk = L // BS
    rhs0 = _each(lambda a, b: jnp.concatenate([a, b], axis=1), al, av)
    lo2 = lax.broadcasted_iota(jnp.int32, (BS, 2 * PAIR), 1) % PAIR < HEAD_DIM
    lane = lax.broadcasted_iota(jnp.int32, (BS, PAIR), 1)
    br =lax.broadcasted_iota(jnp.int32, (PAIR, PAIR), 0)
    bc = lax.broadcasted_iota(jnp.int32, (PAIR, PAIR), 1)
    blk_diag = (br // BS) == (bc // BS)

    def expand(pack):
        return jnp.where(blk_diag, jnp.concatenate([pack] * (PAIR // BS), axis=0), 0.0).astype(BF16)

    d_k = [functools.reduce(jnp.add, [jnp.where((lane % HEAD_DIM) // BS == b, t[b * BS:(b + 1) * BS, :], 0.0)
                                      for b in range(n_blk)]) for t in a_ab]
    t_m = d_k
    yield
    n_sq = int(math.log2(BS)) - 1
    d_k = _each(lambda d: _dot(d.astype(BF16), expand(d)), d_k)
    yield
    for lvl in range(n_sq):
        d_e = _each(expand, d_k)
        t_m = _each(lambda t, d, de: t + d + _dot(t.astype(BF16), de), t_m, d_k, d_e)
        if lvl + 1 < n_sq:
            d_k = _each(lambda d, de: _dot(d.astype(BF16), de), d_k, d_e)
        yield
    t_mb = _each(bf, t_m)

    zero_blk = jnp.zeros((BS, 2 * PAIR), BF16)
    x_blocks = [[] for _ in a_ab]
    xs_lo = [[] for _ in a_ab]
    xs_hi = [[] for _ in a_ab]

    def stacked(lo_parts, hi_parts):
        rows = [lo_parts.get(c, zero_blk) for c in range(n_blk)] + [hi_parts.get(c, zero_blk) for c in range(n_blk)]
        return jnp.concatenate(rows, axis=0)

    for b in range(n_blk):
        r_b = [t[b * BS:(b + 1) * BS, :] for t in rhs0]
        if b > 0:
            r_b = _each(lambda rr, a, lo_p, hi_p: rr + _dot(a[b * BS:(b + 1) * BS, :].astype(BF16),
                                                            stacked(dict(enumerate(lo_p)), dict(enumerate(hi_p)))),
                        r_b, a_ab, xs_lo, xs_hi)
            yield
        x_b = _each(lambda rr, tm: rr + _dot(tm, stacked({b: jnp.where(lo2, rr, 0.0).astype(BF16)},
                                                        {b: jnp.where(lo2, 0.0, rr).astype(BF16)})),
                    r_b, t_mb)
        for i, xb in enumerate(x_b):
            x_blocks[i].append(xb)
            xs_lo[i].append(jnp.where(lo2, xb, 0.0).astype(BF16))
            xs_hi[i].append(jnp.where(lo2, 0.0, xb).astype(BF16))
        yield
    x = [jnp.concatenate(blks, axis=0) for blks in x_blocks]
    x_s = _each(lambda lo_p, hi_p: jnp.concatenate(lo_p + hi_p, axis=0), xs_lo, xs_hi)

    z = _each(_dot, a_rb, x_s)
    akv = _each(_dot, a_rk, v_s)
    w2 = _each(lambda a, t: (a + t[:, :PAIR]).astype(BF16), rb, z)
    y_loc = _each(lambda t, a: t[:, PAIR:] + a, z, akv)
    w1 = [t[:, :PAIR].astype(BF16) for t in x]
    u_loc = [t[:, PAIR:] for t in x]
    yield

    m_t = _each(lambda a, b: jnp.where(same_blk, _dot_tn(a, b.astype(BF16)), 0.0).astype(BF16), w1, bh)
    s_loc = _each(
        lambda u, vv, b, kk_: jnp.where(
            same_blk,
            _dot_tn(jnp.concatenate([u, vv], axis=0).astype(BF16), jnp.concatenate([b, kk_], axis=0).astype(BF16)),
            0.0),
        u_loc, v, bh, kh)
    return w2, y_loc, m_t, s_loc, g_last


def _rwkv_kernel(p_ref, mix_ref, wc_ref, w0_ref, a0_ref, kk_ref, ka_ref, rk_ref, lnw_ref, lnb_ref,
                 o_ref, s_scr, carry_scr, *, nb, width, n_lora, n_chunks, n_groups, stage_offset):
    L = CHUNK
    R = n_chunks * L
    c = pl.program_id(0)

    @pl.when(c == 0)
    def _():
        s_scr[...] = jnp.zeros_like(s_scr)
        carry_scr[...] = jnp.zeros_like(carry_scr)

    n_pairs = width // PAIR
    row = lax.broadcasted_iota(jnp.int32, (R, 1), 0)
    ri = lax.broadcasted_iota(jnp.int32, (2 * L, 2 * L), 0)
    ci = lax.broadcasted_iota(jnp.int32, (2 * L, 2 * L), 1)
    same_blk = (ri >= L) == (ci >= L)
    ti = lax.broadcasted_iota(jnp.int32, (L, 2 * L), 0)
    si = lax.broadcasted_iota(jnp.int32, (L, 2 * L), 1) % L
    strict_c = si < ti
    incl_c = si <= ti
    rr = lax.broadcasted_iota(jnp.int32, (R, R), 0)
    rc = lax.broadcasted_iota(jnp.int32, (R, R), 1)
    tri = ((rr // L == rc // L) & (rr >= rc)).astype(BF16)
    lane_lo = lax.broadcasted_iota(jnp.int32, (L, PAIR), 1) < HEAD_DIM
    lane_lo_r = lax.broadcasted_iota(jnp.int32, (R, PAIR), 1) < HEAD_DIM
    lora_lane = lax.broadcasted_iota(jnp.int32, (R, 2 * LANES), 1)

    def head_sums(x):
        s0 = jnp.sum(jnp.where(lane_lo_r, x, 0.0), axis=-1, keepdims=True)
        s1 = jnp.sum(jnp.where(lane_lo_r, 0.0, x), axis=-1, keepdims=True)
        return jnp.where(lane_lo_r, s0, s1)

    mix = mix_ref[...]

    def lora_up(pm):
        lora = pm[:, 3 * width:]
        act = jnp.where(lora_lane < n_lora[0], jnp.tanh(lora),
                        jnp.where(lora_lane < n_lora[1], lora,
                                  jnp.where(lora_lane < n_lora[2], _sigmoid(lora), 0.0)))
        return _dot(act.astype(BF16), wc_ref[...])

    def log_decay(up):
        t = w0_ref[...] + up[:, :width]
        return (-math.exp(-0.5)) / (1.0 + jnp.exp(-t))

    def running_sum(t):
        h1 = t.astype(BF16)
        r1 = t - h1.astype(F32)
        h2 = r1.astype(BF16)
        h3 = (r1 - h2.astype(F32)).astype(BF16)
        return _dot(tri, h1) + _dot(tri, h2) + _dot(tri, h3)

    tiles = [(b, pr) for b in range(nb) for pr in range(n_pairs)]
    n_t = len(tiles)
    lanes = lambda pr, part=0: slice(part * width + pr * PAIR, part * width + (pr + 1) * PAIR)
    sls = [lanes(pr) for _, pr in tiles]

    states = {0: [s_scr[i] for i in range(n_t)]}
    outs = {}

    def run_group(gi):
        g0 = gi * R
        pms = []
        for b in range(nb):
            p = p_ref[b, g0:g0 + R, :]
            prev_last = carry_scr[b, 7:8, :] if gi == 0 else p_ref[b, g0 - 1:g0, :]
            p_prev = jnp.where(row == 0, prev_last, pltpu.roll(p, 1, 0))
            pms.append(p + (p_prev - p) * mix)
        yield
        ups = _each(lora_up, pms)
        logw_b = _each(log_decay, ups)
        cum_b = _each(running_sum, logw_b)
        yield

        r = [pms[b][:, lanes(pr, 0)] for b, pr in tiles]
        k_raw = [pms[b][:, lanes(pr, 1)] for b, pr in tiles]
        v = [pms[b][:, lanes(pr, 2)] for b, pr in tiles]
        logw = [logw_b[b][:, lanes(pr)] for b, pr in tiles]
        cum = [cum_b[b][:, lanes(pr)] for b, pr in tiles]
        a = [_sigmoid(a0_ref[:, lanes(pr)] + ups[b][:, lanes(pr, 1)]) for b, pr in tiles]
        g = [ups[b][:, lanes(pr, 2)] for b, pr in tiles]

        kk = _each(lambda t, sl: t * kk_ref[:, sl], k_raw, sls)
        kk_ss = _each(lambda t: head_sums(t * t), kk)
        kkn = _each(lambda t, ss: t * lax.rsqrt(jnp.maximum(ss, 1e-24)), kk, kk_ss)
        k = _each(lambda t, aa, sl: t * (1.0 + (aa - 1.0) * ka_ref[:, sl]), k_raw, a, sls)
        bb = _each(jnp.multiply, kkn, a)
        bonus_dot = _each(lambda rr_, kk_, sl: head_sums(rr_ * kk_ * rk_ref[:, sl]), r, k, sls)
        yield

        def chunks(ts):
            return [t[ch * L:(ch + 1) * L] for ch in range(n_chunks) for t in ts]

        w2, y_loc, m_t, s_loc, g_last = yield from _rwkv_chunk_terms(
            chunks(r), chunks(k), chunks(v), chunks(logw), chunks(cum), chunks(kkn), chunks(bb),
            same_blk, strict_c, incl_c, lane_lo)
        yield

        s = states[gi]
        y_parts = []
        for ch in range(n_chunks):
            sel = slice(ch * n_t, (ch + 1) * n_t)
            s_b = _each(lambda t: t.astype(BF16), s)
            y_parts.append(_each(lambda a_, sb, yl: _dot_nt(a_, sb) + yl, w2[sel], s_b, y_loc[sel]))
            s = _each(lambda s0, gl, sb, m, sl: s0 * gl + _dot(sb, m) + sl, s, g_last[sel], s_b, m_t[sel], s_loc[sel])
        states[gi + 1] = s
        y = [jnp.concatenate([y_parts[ch][i] for ch in range(n_chunks)], axis=0) for i in range(n_t)]
        yield

        mu = _each(lambda t: head_sums(t) * (1.0 / HEAD_DIM), y)
        yc = _each(jnp.subtract, y, mu)
        var = _each(lambda t: head_sums(t * t) * (1.0 / HEAD_DIM), yc)
        yield
        res = []
        for i in range(n_t):
            sl = sls[i]
            yn = yc[i] * lax.rsqrt(var[i] + GN_EPS) * lnw_ref[:, sl] + lnb_ref[:, sl]
            res.append(((yn + bonus_dot[i] * v[i]) * g[i]).astype(o_ref.dtype))
        outs[gi] = res

    gens = [run_group(gi) for gi in range(n_groups)]
    live = [True] * n_groups
    tick = 0
    while any(live):
        for gi in range(n_groups):
            if live[gi] and tick >= gi * stage_offset:
                try:
                    next(gens[gi])
                except StopIteration:
                    live[gi] = False
        tick += 1

    for gi in range(n_groups):
        for i, (b, pr) in enumerate(tiles):
            o_ref[b, gi * R:(gi + 1) * R, sls[i]] = outs[gi][i]
    for i in range(n_t):
        s_scr[i] = states[n_groups][i]
    for b in range(nb):
        carry_scr[b] = p_ref[b, n_groups * R - 8:n_groups * R, :]


def _rwkv(p_r, mix, wc, w0, a0, k_k, k_a, r_k, ln_w, ln_b, width, n_lora, n_chunks=2, n_groups=4, stage_offset=5):
    B, S, C = p_r.shape
    L = CHUNK * n_chunks * n_groups
    n_state = B * (width // PAIR)
    vec = lambda n: pl.BlockSpec((1, n), lambda c: (0, 0))
    return pl.pallas_call(
        functools.partial(_rwkv_kernel, nb=B, width=width, n_lora=n_lora, n_chunks=n_chunks, n_groups=n_groups,
                          stage_offset=stage_offset),
        out_shape=jax.ShapeDtypeStruct((B, S, width), BF16),
        grid=(S // L,),
        in_specs=[pl.BlockSpec((B, L, C), lambda c: (0, c, 0)), vec(C),
                  pl.BlockSpec(wc.shape, lambda c: (0, 0))] + [vec(width)] * 7,
        out_specs=pl.BlockSpec((B, L, width), lambda c: (0, c, 0)),
        scratch_shapes=[pltpu.VMEM((n_state, PAIR, PAIR), F32), pltpu.VMEM((B, 8, C), F32)],
        compiler_params=pltpu.CompilerParams(dimension_semantics=("arbitrary",), vmem_limit_bytes=VMEM_LIMIT),
        name="rwkv7",
    )(p_r, mix, wc, w0, a0, k_k, k_a, r_k, ln_w, ln_b)


SB_ROWS = 2048
SB_QUARTER = SB_ROWS // 4
N_RES = 4


def _attn_segments(pi, g):
    if pi == 0:
        return [((g // 4) * SB_QUARTER + r * ATTN_BLOCK + (ATTN_BLOCK // N_RES) * (g % 4), ATTN_BLOCK // N_RES, 1)
                for r in range(N_RES)]
    if pi == 1:
        return [((g // 4) * SB_QUARTER + (g % 4) * ATTN_BLOCK, ATTN_BLOCK, 1)]
    return [(qq * SB_QUARTER + (g % 4) * ATTN_BLOCK + g // 4, ATTN_BLOCK // 4, 4) for qq in range(4)]


def _attn_prev_tile(pi, g):
    if pi == 0:
        return (g + 15) % 16, g == 0
    if pi == 1:
        return ((g // 4 + 3) % 4) * 4 + g % 4, g < 4
    return g, True


def _attn_kernel(q_ref, k_ref, v_ref, gain_ref, o_ref,
                 nat_q, nat_k, nat_v, q_scr, k_ring, v_ring, bias_scr, out_nat, *stat_scr, tiles_per_group):
    sb = pl.program_id(2)
    n_pat = len(DILATED_PATTERNS)
    assert DILATED_PATTERNS == ((128, 1), (512, 4), (2048, 16)) and ATTN_BLOCK == 128
    num_scr, m_scr, l_scr = stat_scr[:n_pat], stat_scr[n_pat:2 * n_pat], stat_scr[2 * n_pat:]
    Q = ATTN_BLOCK
    cur_base = (sb % 2) * SB_ROWS
    other_base = SB_ROWS - cur_base

    @pl.when(sb == 0)
    def _():
        k_ring[pl.ds(SB_ROWS, SB_ROWS), :] = jnp.zeros((SB_ROWS, PAIR), F32)
        v_ring[pl.ds(SB_ROWS, SB_ROWS), :] = jnp.zeros((SB_ROWS, PAIR), F32)

    nat_q[...] = q_ref[...].astype(F32)
    nat_k[...] = k_ref[...].astype(F32)
    nat_v[...] = v_ref[...].astype(F32)
    for qq in range(4):
        for r in range(N_RES):
            src = pl.ds(qq * SB_QUARTER + r, Q, stride=N_RES)
            dst = qq * SB_QUARTER + r * Q
            q_scr[dst:dst + Q, :] = nat_q[src, :]
            k_ring[pl.ds(cur_base + dst, Q), :] = nat_k[src, :]
            v_ring[pl.ds(cur_base + dst, Q), :] = nat_v[src, :]

    @pl.when(sb == 0)
    def _():
        ii = lax.broadcasted_iota(jnp.int32, (2 * Q, 2 * Q), 0) % Q
        cj = lax.broadcasted_iota(jnp.int32, (2 * Q, 2 * Q), 1)
        per = Q // N_RES
        for kind in range(2):
            if kind == 0:
                qi = N_RES * (ii % per) + ii // per
                kj = N_RES * ((cj % Q) % per) + (cj % Q) // per + (cj // Q) * Q
            else:
                qi, kj = ii, cj
            band = (kj >= qi) & (kj <= qi + Q)
            bias_scr[kind, 1] = jnp.where(band, 0.0, NEG_BIG)
            bias_scr[kind, 0] = jnp.where(band & (cj >= Q), 0.0, NEG_BIG)

    def load_tile(ref, base, segs):
        parts = [ref[pl.ds(base + s, n, stride=st) if st > 1 else pl.ds(base + s, n), :] for s, n, st in segs]
        return parts[0] if len(parts) == 1 else jnp.concatenate(parts, axis=0)

    def store_tile(ref, segs, val):
        off = 0
        for s, n, st in segs:
            ref[pl.ds(s, n, stride=st) if st > 1 else pl.ds(s, n), :] = val[off:off + n]
            off += n

    lane_lo = lax.broadcasted_iota(jnp.int32, (Q, PAIR), 1) < HEAD_DIM
    ones_blk = jnp.ones((2 * Q, PAIR), BF16)

    n_blk = SB_ROWS // Q
    has_prev_sb = jnp.where(sb > 0, 1, 0)

    def scores(pi, gs):
        segs = [_attn_segments(pi, g) for g in gs]
        prev = [_attn_prev_tile(pi, g) for g in gs]
        prev_segs = [_attn_segments(pi, pg) for pg, _ in prev]
        prev_base = [other_base if other else cur_base for _, other in prev]
        q2 = [load_tile(q_scr, 0, sg) for sg in segs]
        q2 = [jnp.concatenate([jnp.where(lane_lo, t, 0.0), jnp.where(lane_lo, 0.0, t)], axis=0).astype(BF16)
              for t in q2]
        kcat = [jnp.concatenate([load_tile(k_ring, pb, psg), load_tile(k_ring, cur_base, sg)], axis=0).astype(BF16)
                for pb, psg, sg in zip(prev_base, prev_segs, segs)]
        vcat = [jnp.concatenate([load_tile(v_ring, pb, psg), load_tile(v_ring, cur_base, sg)], axis=0).astype(BF16)
                for pb, psg, sg in zip(prev_base, prev_segs, segs)]
        vext = [jnp.concatenate([t, ones_blk], axis=1) for t in vcat]
        has_prev = [has_prev_sb if other else 1 for _, other in prev]
        return dict(pi=pi, segs=segs, s=_each(_dot_nt, q2, kcat), vext=vext, has_prev=has_prev)

    def softmax(c):
        kind = 0 if c["pi"] == 0 else 1
        s = _each(lambda t, hp: t + bias_scr[kind, hp], c["s"], c["has_prev"])
        m = [jnp.max(t, axis=-1, keepdims=True) for t in s]
        p = _each(lambda t, mm: jnp.exp(t - mm).astype(BF16), s, m)
        return dict(pi=c["pi"], segs=c["segs"], vext=c["vext"], m=m, p=p)

    def weighted_values(c):
        pi, segs, m = c["pi"], c["segs"], c["m"]
        nl = _each(_dot, c["p"], c["vext"])
        for t in range(len(segs)):
            store_tile(num_scr[pi], segs[t], jnp.where(lane_lo, nl[t][:Q, :PAIR], nl[t][Q:, :PAIR]))
            store_tile(l_scr[pi], segs[t], jnp.where(lane_lo, nl[t][:Q, PAIR:], nl[t][Q:, PAIR:]))
            store_tile(m_scr[pi], segs[t], jnp.where(lane_lo, m[t][:Q], m[t][Q:]))

    groups = [(pi, list(range(g0, g0 + tiles_per_group)))
              for pi in range(n_pat) for g0 in range(0, n_blk, tiles_per_group)]
    n_grp = len(groups)
    sc = {0: scores(*groups[0])}
    if n_grp > 1:
        sc[1] = scores(*groups[1])
    sm = {0: softmax(sc.pop(0))}
    for k in range(n_grp):
        if k + 2 < n_grp:
            sc[k + 2] = scores(*groups[k + 2])
        if k + 1 < n_grp:
            sm[k + 1] = softmax(sc.pop(k + 1))
        weighted_values(sm.pop(k))

    ri = lax.broadcasted_iota(jnp.int32, (PAIR, PAIR), 0)
    ci = lax.broadcasted_iota(jnp.int32, (PAIR, PAIR), 1)
    seg_ones = ((ri >= HEAD_DIM) == (ci >= HEAD_DIM)).astype(BF16)
    gain = gain_ref[...]

    def merge(i, carry):
        rows = pl.ds(pl.multiple_of(i * Q, Q), Q)
        ms = [m_scr[pi][rows, :] for pi in range(n_pat)]
        m_all = functools.reduce(jnp.maximum, ms)
        num = 0.0
        den = 0.0
        for pi in range(n_pat):
            wgt = jnp.exp(ms[pi] - m_all)
            num = num + wgt * num_scr[pi][rows, :]
            den = den + wgt * l_scr[pi][rows, :]
        o = num / den
        ms_o = _split_dot(o * o, seg_ones) * (1.0 / HEAD_DIM)
        out_nat[pl.ds((i // N_RES) * SB_QUARTER + i % N_RES, Q, stride=N_RES), :] = o * lax.rsqrt(ms_o + NORM_EPS) * gain
        return carry

    lax.fori_loop(0, SB_ROWS // Q, merge, 0, unroll=4)
    o_ref[...] = out_nat[...].astype(o_ref.dtype)


def _attention(q, k, v, gain, tiles_per_group=4):
    B, S, W = q.shape
    n_pairs = W // PAIR
    n_pat = len(DILATED_PATTERNS)
    blk = pl.BlockSpec((None, SB_ROWS, PAIR), lambda b, p, s: (b, s, p))
    tile = pltpu.VMEM((SB_ROWS, PAIR), F32)
    return pl.pallas_call(
        functools.partial(_attn_kernel, tiles_per_group=tiles_per_group),
        out_shape=jax.ShapeDtypeStruct((B, S, W), BF16),
        grid=(B, n_pairs, S // SB_ROWS),
        in_specs=[blk, blk, blk, pl.BlockSpec((1, PAIR), lambda b, p, s: (0, p))],
        out_specs=blk,
        scratch_shapes=[tile, tile, tile, tile,
                        pltpu.VMEM((2 * SB_ROWS, PAIR), F32),
                        pltpu.VMEM((2 * SB_ROWS, PAIR), F32),
                        pltpu.VMEM((2, 2, 2 * ATTN_BLOCK, 2 * ATTN_BLOCK), F32),
                        tile]
                       + [tile] * (3 * n_pat),
        compiler_params=pltpu.CompilerParams(dimension_semantics=("parallel", "parallel", "arbitrary"),
                                             vmem_limit_bytes=VMEM_LIMIT),
        name="dilated_attn",
    )(q, k, v, gain)


def _rotary_tables(seq):
    half = ROT_DIM // 2
    inv_freq = ROPE_THETA ** (-jnp.arange(half, dtype=F32) * 2.0 / ROT_DIM)
    ang = jnp.arange(seq).astype(F32)[:, None] * inv_freq[None, :]
    cos, sin = jnp.cos(ang), jnp.sin(ang)
    rest = HEAD_DIM - ROT_DIM
    cos_h = jnp.concatenate([cos, cos, jnp.ones((seq, rest), F32)], axis=-1)
    sin_h = jnp.concatenate([-sin, sin, jnp.zeros((seq, rest), F32)], axis=-1)
    return jnp.tile(cos_h, (1, PAIR // HEAD_DIM)), jnp.tile(sin_h, (1, PAIR // HEAD_DIM))


FFN_HALO = 16
MIX_PIECES = 4


def _mix_ffn_kernel(x_ref, ya_ref, yb_ref, xh_ref, yah_ref, ybh_ref, wo_ref, g_ref, wu_ref, cw_ref,
                    cb_ref, wd_ref, fg_ref, o_ref, h_scr, hh_scr, *, tm, seq, row_chunks, apply_final):
    tf = wd_ref.shape[0]
    i = pl.program_id(0)
    j = pl.program_id(1)
    wa = ya_ref.shape[1]

    def mixed(x, ya, yb):
        return x + _dot(ya, wo_ref[:wa, :]) + _dot(yb, wo_ref[wa:, :])

    @pl.when(j == 0)
    def _():
        hh_scr[...] = _rmsnorm(mixed(xh_ref[...], yah_ref[...], ybh_ref[...]), g_ref[...]).astype(BF16)
        piece = tm // MIX_PIECES
        for c in range(MIX_PIECES):
            rows = slice(c * piece, (c + 1) * piece)
            x1 = mixed(x_ref[rows, :], ya_ref[rows, :], yb_ref[rows, :])
            o_ref[rows, :] = x1
            h_scr[rows, :] = _rmsnorm(x1, g_ref[...]).astype(BF16)

    seq_start = (i * tm) % seq == 0
    rc = tm // row_chunks
    row = lax.broadcasted_iota(jnp.int32, (rc, 1), 0)
    for c in range(row_chunks):
        rows = slice(c * rc, (c + 1) * rc)
        gv = _dot(h_scr[rows, :], wu_ref[...])
        gate, val = gv[:, :tf], gv[:, tf:]
        if c == 0:
            gate_h = jnp.where(seq_start, 0.0, _dot(hh_scr[...], wu_ref[:, :tf]))
        else:
            gate_h = _dot(h_scr[c * rc - FFN_HALO:c * rc, :], wu_ref[:, :tf])
        g1 = jnp.where(row == 0, gate_h[FFN_HALO - 1:FFN_HALO, :], pltpu.roll(gate, 1, 0))
        g2 = jnp.where(row == 0, gate_h[FFN_HALO - 2:FFN_HALO - 1, :],
                       jnp.where(row == 1, gate_h[FFN_HALO - 1:FFN_HALO, :], pltpu.roll(gate, 2, 0)))
        u = cw_ref[0:1, :] * g2 + cw_ref[1:2, :] * g1 + cw_ref[2:3, :] * gate + cb_ref[...]
        act = (u * _sigmoid(u) * val).astype(BF16)
        o_ref[rows, :] += _dot(act, wd_ref[...])

    if apply_final:
        @pl.when(j == pl.num_programs(1) - 1)
        def _():
            o_ref[...] = _rmsnorm(o_ref[...], fg_ref[...])


def _mix_ffn(x2d, y_a, y_b, w_o, gain, w_up, conv_w, conv_b, w_down, final_gain, seq, apply_final,
             tm=1024, n_ff_tiles=2, row_chunks=1):
    T, D = x2d.shape
    F = w_down.shape[0]
    tf = F // n_ff_tiles
    assert tf % LANES == 0 and seq % tm == 0 and w_up.shape[1] == 2 * F
    w_up = jnp.concatenate([w_up[:, half * F + j * tf: half * F + (j + 1) * tf]
                            for j in range(n_ff_tiles) for half in range(2)], axis=1)
    halo_blocks = tm // FFN_HALO
    row = lambda i, j: (i, 0)
    halo = lambda i, j: (jnp.maximum(i * halo_blocks - 1, 0), 0)
    fixed = lambda i, j: (0, 0)
    wa, wb = y_a.shape[1], y_b.shape[1]
    return pl.pallas_call(
        functools.partial(_mix_ffn_kernel, tm=tm, seq=seq, row_chunks=row_chunks, apply_final=apply_final),
        out_shape=jax.ShapeDtypeStruct((T, D), F32),
        grid=(T // tm, n_ff_tiles),
        in_specs=[pl.BlockSpec((tm, D), row), pl.BlockSpec((tm, wa), row), pl.BlockSpec((tm, wb), row),
                  pl.BlockSpec((FFN_HALO, D), halo), pl.BlockSpec((FFN_HALO, wa), halo),
                  pl.BlockSpec((FFN_HALO, wb), halo),
                  pl.BlockSpec(w_o.shape, fixed),
                  pl.BlockSpec((1, D), fixed),
                  pl.BlockSpec((D, 2 * tf), lambda i, j: (0, j)),
                  pl.BlockSpec((CONV_WIDTH, tf), lambda i, j: (0, j)),
                  pl.BlockSpec((1, tf), lambda i, j: (0, j)),
                  pl.BlockSpec((tf, D), lambda i, j: (j, 0)),
                  pl.BlockSpec((1, D), fixed)],
        out_specs=pl.BlockSpec((tm, D), row),
        scratch_shapes=[pltpu.VMEM((tm, D), BF16), pltpu.VMEM((FFN_HALO, D), BF16)],
        compiler_params=pltpu.CompilerParams(dimension_semantics=("parallel", "arbitrary"),
                                             vmem_limit_bytes=VMEM_LIMIT),
        name="mix_convglu_ffn",
    )(x2d, y_a, y_b, x2d, y_a, y_b, w_o, gain, w_up, conv_w, conv_b, w_down, final_gain)


def kernel(x, mix_norm_gain, w_in, rwkv_shift_mix, w0, w_lora_up, a0, a_lora_up, g_lora_up, k_k, k_a, r_k,
           ln_x_w, ln_x_b, attn_norm_gain, w_out, ffn_norm_gain, w_ffn_up, ffn_conv_w, ffn_conv_b,
           w_ffn_down, final_norm_gain):
    B, S, D = x.shape
    depth = w_in.shape[0]
    rw = w0.shape[1]
    aw = attn_norm_gain.shape[1]
    n_w, n_a, n_g = w_lora_up.shape[1], a_lora_up.shape[1], g_lora_up.shape[1]
    n_lora = n_w + n_a + n_g
    lora_pad = -(-n_lora // (2 * LANES)) * (2 * LANES)
    assert lora_pad == 2 * LANES and rw % PAIR == 0 and aw % PAIR == 0
    rwkv_cols = 3 * rw + n_lora
    cos_t, sin_t = _rotary_tables(S)

    x2d = x.reshape(B * S, D)
    for l in range(depth):
        w_r = w_in[l][:, :3 * rw + lora_pad].astype(BF16)
        w_a = w_in[l][:, rwkv_cols:].astype(BF16)
        mix =jnp.concatenate([rwkv_shift_mix[l], jnp.zeros((lora_pad - n_lora,), F32)])[None, :]
        wc = jnp.zeros((lora_pad, 3 * rw), F32)
        wc = wc.at[:n_w, :rw].set(w_lora_up[l])
        wc = wc.at[n_w:n_w + n_a, rw:2 * rw].set(a_lora_up[l])
        wc = wc.at[n_w + n_a:n_lora, 2 * rw:].set(g_lora_up[l]).astype(BF16)

        p_r, q, k, v = _inproj(x2d, mix_norm_gain[l][None, :], w_r, w_a, cos_t, sin_t)
        y_rwkv = _rwkv(p_r.reshape(B, S, -1), mix, wc, w0[l][None, :], a0[l][None, :], k_k[l][None, :],
                       k_a[l][None, :], r_k[l].reshape(1, rw), ln_x_w[l][None, :], ln_x_b[l][None, :],
                       rw, (n_w, n_w + n_a, n_lora))
        y_attn = _attention(q.reshape(B, S, aw), k.reshape(B, S, aw), v.reshape(B, S, aw),
                            attn_norm_gain[l][None, :])
        x2d = _mix_ffn(x2d, y_rwkv.reshape(B * S, rw), y_attn.reshape(B * S, aw), w_out[l].astype(BF16),
                       ffn_norm_gain[l][None, :], w_ffn_up[l].astype(BF16), ffn_conv_w[l],
                       ffn_conv_b[l][None, :], w_ffn_down[l].astype(BF16), final_norm_gain[None, :], S,
                       apply_final=(l == depth - 1))
    return x2d.reshape(B, S, D)
```

```python
import functools
import math

import jax
import jax.numpy as jnp
from jax import lax
from jax.experimental import pallas as pl
from jax.experimental.pallas import tpu as pltpu

F32 = jnp.float32
BF16 = jnp.bfloat16

LANES = 128
HEAD_DIM = 64
PAIR = 2 * HEAD_DIM
ROT_DIM = HEAD_DIM // 4
ROPE_THETA = 500000.0
NORM_EPS = 1e-6
GN_EPS = 64e-5
DILATED_PATTERNS = ((128, 1), (512, 4), (2048, 16))
ATTN_BLOCK = 128
CONV_WIDTH = 3
CHUNK = 64
SOLVE_BLOCK = 16
NEG_BIG = -1e30
VMEM_LIMIT = 56 * 1024 * 1024


def _dot(a, b):
    return jnp.dot(a, b, preferred_element_type=F32)


def _dot_nt(a, b):
    return lax.dot_general(a, b, (((1,), (1,)), ((), ())), preferred_element_type=F32)


def _dot_tn(a, b):
    return lax.dot_general(a, b, (((0,), (0,)), ((), ())), preferred_element_type=F32)


def _rmsnorm(x, gain):
    return x * lax.rsqrt(jnp.mean(x * x, axis=-1, keepdims=True) + NORM_EPS) * gain


def _sigmoid(x):
    return 1.0 / (1.0 + jnp.exp(-x))


def _split_dot(x, w):
    hi = x.astype(BF16)
    lo = (x - hi.astype(F32)).astype(BF16)
    return _dot(hi, w) + _dot(lo, w)


def _inproj_kernel(x_ref, g_ref, wr_ref, wa_ref, mix_ref, cos_ref, sin_ref, pr_ref, q_ref, k_ref, v_ref, carry_scr,
                   *, seq_tiles):
    tm = x_ref.shape[0]
    h = _rmsnorm(x_ref[...], g_ref[...]).astype(BF16)

    @pl.when(pl.program_id(0) == 0)
    def _():
        carry_scr[...] = jnp.zeros_like(carry_scr)

    pr = _dot(h, wr_ref[...])
    p = _dot(h, wa_ref[...])
    row = lax.broadcasted_iota(jnp.int32, (tm, 1), 0)
    prev_last = jnp.where(pl.program_id(0) % seq_tiles == 0, 0.0, carry_scr[7:8, :])
    pr_prev = jnp.where(row == 0, prev_last, pltpu.roll(pr, 1, 0))
    carry_scr[...] = pr[tm - 8:tm, :]
    pr_ref[...] = pr + (pr_prev - pr) * mix_ref[...]
    aw = q_ref.shape[1]

    lane = lax.broadcasted_iota(jnp.int32, (tm, PAIR), 1)
    first_half = (lane % HEAD_DIM) < (ROT_DIM // 2)
    cos, sin = cos_ref[...], sin_ref[...]

    def rotary(x):
        partner = jnp.where(first_half, pltpu.roll(x, PAIR - ROT_DIM // 2, 1), pltpu.roll(x, ROT_DIM // 2, 1))
        return x * cos + partner * sin

    for t in range(aw // PAIR):
        lo = t * PAIR
        q_ref[:, lo:lo + PAIR] = (rotary(p[:, lo:lo + PAIR]) * (1.0 / math.sqrt(HEAD_DIM))).astype(BF16)
        k_ref[:, lo:lo + PAIR] = rotary(p[:, aw + lo:aw + lo + PAIR]).astype(BF16)
    v_ref[...] = p[:, 2 * aw:].astype(BF16)


def _inproj(x2d, gain, w_r, w_a, mix, cos_t, sin_t, tm=1024):
    T, D = x2d.shape
    rw, aw = w_r.shape[1], w_a.shape[1] // 3
    seq_tiles = cos_t.shape[0] // tm
    row = lambda i: (i, 0)
    fixed = lambda i: (0, 0)
    tab = pl.BlockSpec((tm, PAIR), lambda i: (i % seq_tiles, 0))
    return pl.pallas_call(
        functools.partial(_inproj_kernel, seq_tiles=seq_tiles),
        out_shape=(jax.ShapeDtypeStruct((T, rw), F32),) + (jax.ShapeDtypeStruct((T, aw), BF16),) * 3,
        grid=(T // tm,),
        in_specs=[pl.BlockSpec((tm, D), row), pl.BlockSpec((1, D), fixed), pl.BlockSpec(w_r.shape, fixed),
                  pl.BlockSpec(w_a.shape, fixed), pl.BlockSpec((1, rw), fixed), tab, tab],
        out_specs=(pl.BlockSpec((tm, rw), row),) + (pl.BlockSpec((tm, aw), row),) * 3,
        scratch_shapes=[pltpu.VMEM((8, rw), F32)],
        compiler_params=pltpu.CompilerParams(dimension_semantics=("arbitrary",), vmem_limit_bytes=VMEM_LIMIT),
        name="inproj",
    )(x2d, gain, w_r, w_a, mix, cos_t, sin_t)


def _each(fn, *lists):
    return [fn(*args) for args in zip(*lists)]


def _rwkv_chunk_terms(r, k, v, logw, cum, kkn, bb, same_blk, strict_c, incl_c, lane_lo):
    L = CHUNK
    bf = lambda t: t.astype(BF16)
    cum_last = [t[L - 1:L, :] for t in cum]
    g_in = _each(jnp.exp2, cum)
    g_ex = _each(lambda t, w: jnp.exp2(t - w), cum, logw)
    g_inv = _each(lambda t: jnp.exp2(-t), cum)
    g_hat = _each(lambda tl, t: jnp.exp2(tl - t), cum_last, cum)
    g_last = _each(jnp.exp2, cum_last)
    yield

    def by_head_rows(x):
        lo = lane_lo if x.shape[1] == PAIR else jnp.concatenate([lane_lo] * (x.shape[1] // PAIR), axis=1)
        return jnp.concatenate([jnp.where(lo, x, 0.0), jnp.where(lo, 0.0, x)], axis=0).astype(BF16)

    al = _each(lambda t, g: -t * g, kkn, g_ex)
    rb = _each(jnp.multiply, r, g_in)
    bt = _each(jnp.multiply, bb, g_inv)
    kt = _each(jnp.multiply, k, g_inv)
    bh = _each(jnp.multiply, bb, g_hat)
    kh = _each(jnp.multiply, k, g_hat)
    lhs = _each(lambda a, b: jnp.concatenate([a, b], axis=0).astype(BF16), al, rb)
    rhs = _each(lambda a, b: jnp.concatenate([by_head_rows(a), by_head_rows(b)], axis=0), bt, kt)
    yield
    aq = _each(_dot_nt, lhs, rhs)
    a_ab = [jnp.where(strict_c, t[:L, :2 * L], 0.0) for t in aq]
    a_ak = [jnp.where(strict_c, t[:L, 2 * L:], 0.0).astype(BF16) for t in aq]
    a_rb = [jnp.where(incl_c, t[L:, :2 * L], 0.0).astype(BF16) for t in aq]
    a_rk = [jnp.where(incl_c, t[L:, 2 * L:], 0.0).astype(BF16) for t in aq]
    yield

    v_s = _each(by_head_rows, v)
    av = _each(_dot, a_ak, v_s)
    BS = SOLVE_BLOCK
    n_blk = L // BS
    rhs0 = _each(lambda a, b: jnp.concatenate([a, b], axis=1), al, av)
    lo2 = lax.broadcasted_iota(jnp.int32, (BS, 2 * PAIR), 1) % PAIR < HEAD_DIM
    lane = lax.broadcasted_iota(jnp.int32, (BS, PAIR), 1)
    br =lax.broadcasted_iota(jnp.int32, (PAIR, PAIR), 0)
    bc = lax.broadcasted_iota(jnp.int32, (PAIR, PAIR), 1)
    blk_diag = (br // BS) == (bc // BS)

    def expand(pack):
        return jnp.where(blk_diag, jnp.concatenate([pack] * (PAIR // BS), axis=0), 0.0).astype(BF16)

    d_k = [functools.reduce(jnp.add, [jnp.where((lane % HEAD_DIM) // BS == b, t[b * BS:(b + 1) * BS, :], 0.0)
                                      for b in range(n_blk)]) for t in a_ab]
    t_m = d_k
    yield
    n_sq = int(math.log2(BS)) - 1
    d_k = _each(lambda d: _dot(d.astype(BF16), expand(d)), d_k)
    yield
    for lvl in range(n_sq):
        d_e = _each(expand, d_k)
        t_m = _each(lambda t, d, de: t + d + _dot(t.astype(BF16), de), t_m, d_k, d_e)
        if lvl + 1 < n_sq:
            d_k = _each(lambda d, de: _dot(d.astype(BF16), de), d_k, d_e)
        yield
    t_mb = _each(bf, t_m)

    zero_blk = jnp.zeros((BS, 2 * PAIR), BF16)
    x_blocks = [[] for _ in a_ab]
    xs_lo = [[] for _ in a_ab]
    xs_hi = [[] for _ in a_ab]

    def stacked(lo_parts, hi_parts):
        rows = [lo_parts.get(c, zero_blk) for c in range(n_blk)] + [hi_parts.get(c, zero_blk) for c in range(n_blk)]
        return jnp.concatenate(rows, axis=0)

    for b in range(n_blk):
        r_b = [t[b * BS:(b + 1) * BS, :] for t in rhs0]
        if b > 0:
            r_b = _each(lambda rr, a, lo_p, hi_p: rr + _dot(a[b * BS:(b + 1) * BS, :].astype(BF16),
                                                            stacked(dict(enumerate(lo_p)), dict(enumerate(hi_p)))),
                        r_b, a_ab, xs_lo, xs_hi)
            yield
        x_b = _each(lambda rr, tm: rr + _dot(tm, stacked({b: jnp.where(lo2, rr, 0.0).astype(BF16)},
                                                        {b: jnp.where(lo2, 0.0, rr).astype(BF16)})),
                    r_b, t_mb)
        for i, xb in enumerate(x_b):
            x_blocks[i].append(xb)
            xs_lo[i].append(jnp.where(lo2, xb, 0.0).astype(BF16))
            xs_hi[i].append(jnp.where(lo2, 0.0, xb).astype(BF16))
        yield
    x = [jnp.concatenate(blks, axis=0) for blks in x_blocks]
    x_s = _each(lambda lo_p, hi_p: jnp.concatenate(lo_p + hi_p, axis=0), xs_lo, xs_hi)

    z = _each(_dot, a_rb, x_s)
    akv = _each(_dot, a_rk, v_s)
    w2 = _each(lambda a, t: (a + t[:, :PAIR]).astype(BF16), rb, z)
    y_loc = _each(lambda t, a: t[:, PAIR:] + a, z, akv)
    w1 = [t[:, :PAIR].astype(BF16) for t in x]
    u_loc = [t[:, PAIR:] for t in x]
    yield

    m_t = _each(lambda a, b: jnp.where(same_blk, _dot_tn(a, b.astype(BF16)), 0.0).astype(BF16), w1, bh)
    s_loc = _each(
        lambda u, vv, b, kk_: jnp.where(
            same_blk,
            _dot_tn(jnp.concatenate([u, vv], axis=0).astype(BF16), jnp.concatenate([b, kk_], axis=0).astype(BF16)),
            0.0),
        u_loc, v, bh, kh)
    return w2, y_loc, m_t, s_loc, g_last


def _rwkv_kernel(p_ref, wc_ref, w0_ref, a0_ref, kk_ref, ka_ref, rk_ref, lnw_ref, lnb_ref,
                 o_ref, s_scr, *, nb, width, n_lora, n_chunks, n_groups, stage_offset):
    L = CHUNK
    R = n_chunks * L
    c = pl.program_id(0)

    @pl.when(c == 0)
    def _():
        s_scr[...] = jnp.zeros_like(s_scr)

    n_pairs = width // PAIR
    ri = lax.broadcasted_iota(jnp.int32, (2 * L, 2 * L), 0)
    ci = lax.broadcasted_iota(jnp.int32, (2 * L, 2 * L), 1)
    same_blk = (ri >= L) == (ci >= L)
    ti = lax.broadcasted_iota(jnp.int32, (L, 2 * L), 0)
    si = lax.broadcasted_iota(jnp.int32, (L, 2 * L), 1) % L
    strict_c = si < ti
    incl_c = si <= ti
    rr = lax.broadcasted_iota(jnp.int32, (R, R), 0)
    rc = lax.broadcasted_iota(jnp.int32, (R, R), 1)
    tri = ((rr // L == rc // L) & (rr >= rc)).astype(BF16)
    lane_lo = lax.broadcasted_iota(jnp.int32, (L, PAIR), 1) < HEAD_DIM
    lane_lo_r = lax.broadcasted_iota(jnp.int32, (R, PAIR), 1) < HEAD_DIM
    lora_lane = lax.broadcasted_iota(jnp.int32, (R, 2 * LANES), 1)

    def head_sums(x):
        s0 = jnp.sum(jnp.where(lane_lo_r, x, 0.0), axis=-1, keepdims=True)
        s1 = jnp.sum(jnp.where(lane_lo_r, 0.0, x), axis=-1, keepdims=True)
        return jnp.where(lane_lo_r, s0, s1)

    def lora_up(pm):
        lora = pm[:, 3 * width:]
        act = jnp.where(lora_lane < n_lora[0], jnp.tanh(lora),
                        jnp.where(lora_lane < n_lora[1], lora,
                                  jnp.where(lora_lane < n_lora[2], _sigmoid(lora), 0.0)))
        return _dot(act.astype(BF16), wc_ref[...])

    def log_decay(up):
        t = w0_ref[...] + up[:, :width]
        return (-math.exp(-0.5) * math.log2(math.e)) / (1.0 + jnp.exp(-t))

    def running_sum(t):
        h1 = t.astype(BF16)
        r1 = t - h1.astype(F32)
        h2 = r1.astype(BF16)
        h3 = (r1 - h2.astype(F32)).astype(BF16)
        return _dot(tri, h1) + _dot(tri, h2) + _dot(tri, h3)

    tiles = [(b, pr) for b in range(nb) for pr in range(n_pairs)]
    n_t = len(tiles)
    lanes = lambda pr, part=0: slice(part * width + pr * PAIR, part * width + (pr + 1) * PAIR)
    sls = [lanes(pr) for _, pr in tiles]

    states = {0: [s_scr[i] for i in range(n_t)]}
    outs = {}

    def run_group(gi):
        g0 = gi * R
        pms = [p_ref[b, g0:g0 + R, :] for b in range(nb)]
        yield
        ups = _each(lora_up, pms)
        logw_b = _each(log_decay, ups)
        cum_b = _each(running_sum, logw_b)
        yield

        r = [pms[b][:, lanes(pr, 0)] for b, pr in tiles]
        k_raw = [pms[b][:, lanes(pr, 1)] for b, pr in tiles]
        v = [pms[b][:, lanes(pr, 2)] for b, pr in tiles]
        logw = [logw_b[b][:, lanes(pr)] for b, pr in tiles]
        cum = [cum_b[b][:, lanes(pr)] for b, pr in tiles]
        a = [_sigmoid(a0_ref[:, lanes(pr)] + ups[b][:, lanes(pr, 1)]) for b, pr in tiles]
        g = [ups[b][:, lanes(pr, 2)] for b, pr in tiles]

        kk = _each(lambda t, sl: t * kk_ref[:, sl], k_raw, sls)
        kk_ss = _each(lambda t: head_sums(t * t), kk)
        kkn = _each(lambda t, ss: t * lax.rsqrt(jnp.maximum(ss, 1e-24)), kk, kk_ss)
        k = _each(lambda t, aa, sl: t * (1.0 + (aa - 1.0) * ka_ref[:, sl]), k_raw, a, sls)
        bb = _each(jnp.multiply, kkn, a)
        bonus_dot = _each(lambda rr_, kk_, sl: head_sums(rr_ * kk_ * rk_ref[:, sl]), r, k, sls)
        yield

        def chunks(ts):
            return [t[ch * L:(ch + 1) * L] for ch in range(n_chunks) for t in ts]

        w2, y_loc, m_t, s_loc, g_last = yield from _rwkv_chunk_terms(
            chunks(r), chunks(k), chunks(v), chunks(logw), chunks(cum), chunks(kkn), chunks(bb),
            same_blk, strict_c, incl_c, lane_lo)
        yield

        s = states[gi]
        y_parts = []
        for ch in range(n_chunks):
            sel = slice(ch * n_t, (ch + 1) * n_t)
            s_b = _each(lambda t: t.astype(BF16), s)
            y_parts.append(_each(lambda a_, sb, yl: _dot_nt(a_, sb) + yl, w2[sel], s_b, y_loc[sel]))
            s = _each(lambda s0, gl, sb, m, sl: s0 * gl + _dot(sb, m) + sl, s, g_last[sel], s_b, m_t[sel], s_loc[sel])
        states[gi + 1] = s
        y = [jnp.concatenate([y_parts[ch][i] for ch in range(n_chunks)], axis=0) for i in range(n_t)]
        yield

        mu = _each(lambda t: head_sums(t) * (1.0 / HEAD_DIM), y)
        yc = _each(jnp.subtract, y, mu)
        var = _each(lambda t: head_sums(t * t) * (1.0 / HEAD_DIM), yc)
        yield
        res = []
        for i in range(n_t):
            sl = sls[i]
            yn = yc[i] * lax.rsqrt(var[i] + GN_EPS) * lnw_ref[:, sl] + lnb_ref[:, sl]
            res.append(((yn + bonus_dot[i] * v[i]) * g[i]).astype(o_ref.dtype))
        outs[gi] = res

    gens = [run_group(gi) for gi in range(n_groups)]
    live = [True] * n_groups
    tick = 0
    while any(live):
        for gi in range(n_groups):
            if live[gi] and tick >= gi * stage_offset:
                try:
                    next(gens[gi])
                except StopIteration:
                    live[gi] = False
        tick += 1

    for gi in range(n_groups):
        for i, (b, pr) in enumerate(tiles):
            o_ref[b, gi * R:(gi + 1) * R, sls[i]] = outs[gi][i]
    for i in range(n_t):
        s_scr[i] = states[n_groups][i]


def _rwkv(p_r, wc, w0, a0, k_k, k_a, r_k, ln_w, ln_b, width, n_lora, n_chunks=2, n_groups=4, stage_offset=5):
    B, S, C = p_r.shape
    L = CHUNK * n_chunks * n_groups
    n_state = B * (width // PAIR)
    vec = lambda n: pl.BlockSpec((1, n), lambda c: (0, 0))
    return pl.pallas_call(
        functools.partial(_rwkv_kernel, nb=B, width=width, n_lora=n_lora, n_chunks=n_chunks, n_groups=n_groups,
                          stage_offset=stage_offset),
        out_shape=jax.ShapeDtypeStruct((B, S, width), BF16),
        grid=(S // L,),
        in_specs=[pl.BlockSpec((B, L, C), lambda c: (0, c, 0)),
                  pl.BlockSpec(wc.shape, lambda c: (0, 0))] + [vec(width)] * 7,
        out_specs=pl.BlockSpec((B, L, width), lambda c: (0, c, 0)),
        scratch_shapes=[pltpu.VMEM((n_state, PAIR, PAIR), F32)],
        compiler_params=pltpu.CompilerParams(dimension_semantics=("arbitrary",), vmem_limit_bytes=VMEM_LIMIT),
        name="rwkv7",
    )(p_r, wc, w0, a0, k_k, k_a, r_k, ln_w, ln_b)


SB_ROWS = 2048
SB_QUARTER = SB_ROWS // 4
N_RES = 4


def _attn_segments(pi, g):
    if pi == 0:
        return [((g // 4) * SB_QUARTER + r * ATTN_BLOCK + (ATTN_BLOCK // N_RES) * (g % 4), ATTN_BLOCK // N_RES, 1)
                for r in range(N_RES)]
    if pi == 1:
        return [((g // 4) * SB_QUARTER + (g % 4) * ATTN_BLOCK, ATTN_BLOCK, 1)]
    return [(qq * SB_QUARTER + (g % 4) * ATTN_BLOCK + g // 4, ATTN_BLOCK // 4, 4) for qq in range(4)]


def _attn_prev_tile(pi, g):
    if pi == 0:
        return (g + 15) % 16, g == 0
    if pi == 1:
        return ((g // 4 + 3) % 4) * 4 + g % 4, g < 4
    return g, True


def _attn_kernel(q_ref, k_ref, v_ref, gain_ref, o_ref,
                 nat_q, nat_k, nat_v, q_scr, k_ring, v_ring, bias_scr, out_nat, *stat_scr, tiles_per_group):
    sb = pl.program_id(2)
    n_pat = len(DILATED_PATTERNS)
    assert DILATED_PATTERNS == ((128, 1), (512, 4), (2048, 16)) and ATTN_BLOCK == 128
    num_scr, m_scr, l_scr = stat_scr[:n_pat], stat_scr[n_pat:2 * n_pat], stat_scr[2 * n_pat:]
    Q = ATTN_BLOCK
    cur_base = (sb % 2) * SB_ROWS
    other_base = SB_ROWS - cur_base

    @pl.when(sb == 0)
    def _():
        k_ring[pl.ds(SB_ROWS, SB_ROWS), :] = jnp.zeros((SB_ROWS, PAIR), F32)
        v_ring[pl.ds(SB_ROWS, SB_ROWS), :] = jnp.zeros((SB_ROWS, PAIR), F32)

    nat_q[...] = q_ref[...].astype(F32)
    nat_k[...] = k_ref[...].astype(F32)
    nat_v[...] = v_ref[...].astype(F32)
    for qq in range(4):
        for r in range(N_RES):
            src = pl.ds(qq * SB_QUARTER + r, Q, stride=N_RES)
            dst = qq * SB_QUARTER + r * Q
            q_scr[dst:dst + Q, :] = nat_q[src, :]
            k_ring[pl.ds(cur_base + dst, Q), :] = nat_k[src, :]
            v_ring[pl.ds(cur_base + dst, Q), :] = nat_v[src, :]

    @pl.when(sb == 0)
    def _():
        ii = lax.broadcasted_iota(jnp.int32, (2 * Q, 2 * Q), 0) % Q
        cj = lax.broadcasted_iota(jnp.int32, (2 * Q, 2 * Q), 1)
        per = Q // N_RES
        for kind in range(2):
            if kind == 0:
                qi = N_RES * (ii % per) + ii // per
                kj = N_RES * ((cj % Q) % per) + (cj % Q) // per + (cj // Q) * Q
            else:
                qi, kj = ii, cj
            band = (kj >= qi) & (kj <= qi + Q)
            bias_scr[kind, 1] = jnp.where(band, 0.0, NEG_BIG)
            bias_scr[kind, 0] = jnp.where(band & (cj >= Q), 0.0, NEG_BIG)

    def load_tile(ref, base, segs):
        parts = [ref[pl.ds(base + s, n, stride=st) if st > 1 else pl.ds(base + s, n), :] for s, n, st in segs]
        return parts[0] if len(parts) == 1 else jnp.concatenate(parts, axis=0)

    def store_tile(ref, segs, val):
        off = 0
        for s, n, st in segs:
            ref[pl.ds(s, n, stride=st) if st > 1 else pl.ds(s, n), :] = val[off:off + n]
            off += n

    lane_lo = lax.broadcasted_iota(jnp.int32, (Q, PAIR), 1) < HEAD_DIM
    ones_blk = jnp.ones((2 * Q, PAIR), BF16)

    n_blk = SB_ROWS // Q
    has_prev_sb = jnp.where(sb > 0, 1, 0)

    def scores(pi, gs):
        segs = [_attn_segments(pi, g) for g in gs]
        prev = [_attn_prev_tile(pi, g) for g in gs]
        prev_segs = [_attn_segments(pi, pg) for pg, _ in prev]
        prev_base = [other_base if other else cur_base for _, other in prev]
        q2 = [load_tile(q_scr, 0, sg) for sg in segs]
        q2 = [jnp.concatenate([jnp.where(lane_lo, t, 0.0), jnp.where(lane_lo, 0.0, t)], axis=0).astype(BF16)
              for t in q2]
        kcat = [jnp.concatenate([load_tile(k_ring, pb, psg), load_tile(k_ring, cur_base, sg)], axis=0).astype(BF16)
                for pb, psg, sg in zip(prev_base, prev_segs, segs)]
        vcat = [jnp.concatenate([load_tile(v_ring, pb, psg), load_tile(v_ring, cur_base, sg)], axis=0).astype(BF16)
                for pb, psg, sg in zip(prev_base, prev_segs, segs)]
        vext = [jnp.concatenate([t, ones_blk], axis=1) for t in vcat]
        has_prev = [has_prev_sb if other else 1 for _, other in prev]
        return dict(pi=pi, segs=segs, s=_each(_dot_nt, q2, kcat), vext=vext, has_prev=has_prev)

    def softmax(c):
        kind = 0 if c["pi"] == 0 else 1
        s = _each(lambda t, hp: t + bias_scr[kind, hp], c["s"], c["has_prev"])
        m = [jnp.max(t, axis=-1, keepdims=True) for t in s]
        p = _each(lambda t, mm: jnp.exp(t - mm).astype(BF16), s, m)
        return dict(pi=c["pi"], segs=c["segs"], vext=c["vext"], m=m, p=p)

    def weighted_values(c):
        pi, segs, m = c["pi"], c["segs"], c["m"]
        nl = _each(_dot, c["p"], c["vext"])
        for t in range(len(segs)):
            store_tile(num_scr[pi], segs[t], jnp.where(lane_lo, nl[t][:Q, :PAIR], nl[t][Q:, :PAIR]))
            store_tile(l_scr[pi], segs[t], jnp.where(lane_lo, nl[t][:Q, PAIR:], nl[t][Q:, PAIR:]))
            store_tile(m_scr[pi], segs[t], jnp.where(lane_lo, m[t][:Q], m[t][Q:]))

    groups = [(pi, list(range(g0, g0 + tiles_per_group)))
              for pi in range(n_pat) for g0 in range(0, n_blk, tiles_per_group)]
    n_grp = len(groups)
    sc = {0: scores(*groups[0])}
    if n_grp > 1:
        sc[1] = scores(*groups[1])
    sm = {0: softmax(sc.pop(0))}
    for k in range(n_grp):
        if k + 2 < n_grp:
            sc[k + 2] = scores(*groups[k + 2])
        if k + 1 < n_grp:
            sm[k + 1] = softmax(sc.pop(k + 1))
        weighted_values(sm.pop(k))

    ri = lax.broadcasted_iota(jnp.int32, (PAIR, PAIR), 0)
    ci = lax.broadcasted_iota(jnp.int32, (PAIR, PAIR), 1)
    seg_ones = ((ri >= HEAD_DIM) == (ci >= HEAD_DIM)).astype(BF16)
    gain = gain_ref[...]

    def merge(i, carry):
        rows = pl.ds(pl.multiple_of(i * Q, Q), Q)
        ms = [m_scr[pi][rows, :] for pi in range(n_pat)]
        m_all = functools.reduce(jnp.maximum, ms)
        num = 0.0
        den = 0.0
        for pi in range(n_pat):
            wgt = jnp.exp(ms[pi] - m_all)
            num = num + wgt * num_scr[pi][rows, :]
            den = den + wgt * l_scr[pi][rows, :]
        o = num / den
        ms_o = _split_dot(o * o, seg_ones) * (1.0 / HEAD_DIM)
        out_nat[pl.ds((i // N_RES) * SB_QUARTER + i % N_RES, Q, stride=N_RES), :] = o * lax.rsqrt(ms_o + NORM_EPS) * gain
        return carry

    lax.fori_loop(0, SB_ROWS // Q, merge, 0, unroll=4)
    o_ref[...] = out_nat[...].astype(o_ref.dtype)


def _attention(q, k, v, gain, tiles_per_group=4):
    B, S, W = q.shape
    n_pairs = W // PAIR
    n_pat = len(DILATED_PATTERNS)
    blk = pl.BlockSpec((None, SB_ROWS, PAIR), lambda b, p, s: (b, s, p))
    tile = pltpu.VMEM((SB_ROWS, PAIR), F32)
    return pl.pallas_call(
        functools.partial(_attn_kernel, tiles_per_group=tiles_per_group),
        out_shape=jax.ShapeDtypeStruct((B, S, W), BF16),
        grid=(B, n_pairs, S // SB_ROWS),
        in_specs=[blk, blk, blk, pl.BlockSpec((1, PAIR), lambda b, p, s: (0, p))],
        out_specs=blk,
        scratch_shapes=[tile, tile, tile, tile,
                        pltpu.VMEM((2 * SB_ROWS, PAIR), F32),
                        pltpu.VMEM((2 * SB_ROWS, PAIR), F32),
                        pltpu.VMEM((2, 2, 2 * ATTN_BLOCK, 2 * ATTN_BLOCK), F32),
                        tile]
                       + [tile] * (3 * n_pat),
        compiler_params=pltpu.CompilerParams(dimension_semantics=("parallel", "parallel", "arbitrary"),
                                             vmem_limit_bytes=VMEM_LIMIT),
        name="dilated_attn",
    )(q, k, v, gain)


def _rotary_tables(seq):
    half = ROT_DIM // 2
    inv_freq = ROPE_THETA ** (-jnp.arange(half, dtype=F32) * 2.0 / ROT_DIM)
    ang = jnp.arange(seq).astype(F32)[:, None] * inv_freq[None, :]
    cos, sin = jnp.cos(ang), jnp.sin(ang)
    rest = HEAD_DIM - ROT_DIM
    cos_h = jnp.concatenate([cos, cos, jnp.ones((seq, rest), F32)], axis=-1)
    sin_h = jnp.concatenate([-sin, sin, jnp.zeros((seq, rest), F32)], axis=-1)
    return jnp.tile(cos_h, (1, PAIR // HEAD_DIM)), jnp.tile(sin_h, (1, PAIR // HEAD_DIM))


FFN_HALO = 16
MIX_PIECES = 4


def _mix_ffn_kernel(x_ref, ya_ref, yb_ref, xh_ref, yah_ref, ybh_ref, wo_ref, g_ref, wu_ref, cw_ref,
                    cb_ref, wd_ref, fg_ref, o_ref, h_scr, hh_scr, *, tm, seq, row_chunks, apply_final):
    tf = wd_ref.shape[0]
    i = pl.program_id(0)
    j = pl.program_id(1)
    wa = ya_ref.shape[1]

    def mixed(x, ya, yb):
        return x + _dot(ya, wo_ref[:wa, :]) + _dot(yb, wo_ref[wa:, :])

    @pl.when(j == 0)
    def _():
        hh_scr[...] = _rmsnorm(mixed(xh_ref[...], yah_ref[...], ybh_ref[...]), g_ref[...]).astype(BF16)
        piece = tm // MIX_PIECES
        for c in range(MIX_PIECES):
            rows = slice(c * piece, (c + 1) * piece)
            x1 = mixed(x_ref[rows, :], ya_ref[rows, :], yb_ref[rows, :])
            o_ref[rows, :] = x1
            h_scr[rows, :] = _rmsnorm(x1, g_ref[...]).astype(BF16)

    seq_start = (i * tm) % seq == 0
    rc = tm // row_chunks
    row = lax.broadcasted_iota(jnp.int32, (rc, 1), 0)
    for c in range(row_chunks):
        rows = slice(c * rc, (c + 1) * rc)
        gv = _dot(h_scr[rows, :], wu_ref[...])
        gate, val = gv[:, :tf], gv[:, tf:]
        if c == 0:
            gate_h = jnp.where(seq_start, 0.0, _dot(hh_scr[...], wu_ref[:, :tf]))
        else:
            gate_h = _dot(h_scr[c * rc - FFN_HALO:c * rc, :], wu_ref[:, :tf])
        g1 = jnp.where(row == 0, gate_h[FFN_HALO - 1:FFN_HALO, :], pltpu.roll(gate, 1, 0))
        g2 = jnp.where(row == 0, gate_h[FFN_HALO - 2:FFN_HALO - 1, :],
                       jnp.where(row == 1, gate_h[FFN_HALO - 1:FFN_HALO, :], pltpu.roll(gate, 2, 0)))
        u = cw_ref[0:1, :] * g2 + cw_ref[1:2, :] * g1 + cw_ref[2:3, :] * gate + cb_ref[...]
        act = (u * _sigmoid(u) * val).astype(BF16)
        o_ref[rows, :] += _dot(act, wd_ref[...])

    if apply_final:
        @pl.when(j == pl.num_programs(1) - 1)
        def _():
            o_ref[...] = _rmsnorm(o_ref[...], fg_ref[...])


def _mix_ffn(x2d, y_a, y_b, w_o, gain, w_up, conv_w, conv_b, w_down, final_gain, seq, apply_final,
             tm=1024, n_ff_tiles=2, row_chunks=1):
    T, D = x2d.shape
    F = w_down.shape[0]
    tf = F // n_ff_tiles
    assert tf % LANES == 0 and seq % tm == 0 and w_up.shape[1] == 2 * F
    w_up = jnp.concatenate([w_up[:, half * F + j * tf: half * F + (j + 1) * tf]
                            for j in range(n_ff_tiles) for half in range(2)], axis=1)
    halo_blocks = tm // FFN_HALO
    row = lambda i, j: (i, 0)
    halo = lambda i, j: (jnp.maximum(i * halo_blocks - 1, 0), 0)
    fixed = lambda i, j: (0, 0)
    wa, wb = y_a.shape[1], y_b.shape[1]
    return pl.pallas_call(
        functools.partial(_mix_ffn_kernel, tm=tm, seq=seq, row_chunks=row_chunks, apply_final=apply_final),
        out_shape=jax.ShapeDtypeStruct((T, D), F32),
        grid=(T // tm, n_ff_tiles),
        in_specs=[pl.BlockSpec((tm, D), row), pl.BlockSpec((tm, wa), row), pl.BlockSpec((tm, wb), row),
                  pl.BlockSpec((FFN_HALO, D), halo), pl.BlockSpec((FFN_HALO, wa), halo),
                  pl.BlockSpec((FFN_HALO, wb), halo),
                  pl.BlockSpec(w_o.shape, fixed),
                  pl.BlockSpec((1, D), fixed),
                  pl.BlockSpec((D, 2 * tf), lambda i, j: (0, j)),
                  pl.BlockSpec((CONV_WIDTH, tf), lambda i, j: (0, j)),
                  pl.BlockSpec((1, tf), lambda i, j: (0, j)),
                  pl.BlockSpec((tf, D), lambda i, j: (j, 0)),
                  pl.BlockSpec((1, D), fixed)],
        out_specs=pl.BlockSpec((tm, D), row),
        scratch_shapes=[pltpu.VMEM((tm, D), BF16), pltpu.VMEM((FFN_HALO, D), BF16)],
        compiler_params=pltpu.CompilerParams(dimension_semantics=("parallel", "arbitrary"),
                                             vmem_limit_bytes=VMEM_LIMIT),
        name="mix_convglu_ffn",
    )(x2d, y_a, y_b, x2d, y_a, y_b, w_o, gain, w_up, conv_w, conv_b, w_down, final_gain)


def kernel(x, mix_norm_gain, w_in, rwkv_shift_mix, w0, w_lora_up, a0, a_lora_up, g_lora_up, k_k, k_a, r_k,
           ln_x_w, ln_x_b, attn_norm_gain, w_out, ffn_norm_gain, w_ffn_up, ffn_conv_w, ffn_conv_b,
           w_ffn_down, final_norm_gain):
    B, S, D = x.shape
    depth = w_in.shape[0]
    rw = w0.shape[1]
    aw = attn_norm_gain.shape[1]
    n_w, n_a, n_g = w_lora_up.shape[1], a_lora_up.shape[1], g_lora_up.shape[1]
    n_lora = n_w + n_a + n_g
    lora_pad = -(-n_lora // (2 * LANES)) * (2 * LANES)
    assert lora_pad == 2 * LANES and rw % PAIR == 0 and aw % PAIR == 0
    rwkv_cols = 3 * rw + n_lora
    cos_t, sin_t = _rotary_tables(S)

    x2d = x.reshape(B * S, D)
    for l in range(depth):
        w_r = w_in[l][:, :3 * rw + lora_pad].astype(BF16)
        w_a = w_in[l][:, rwkv_cols:].astype(BF16)
        mix =jnp.concatenate([rwkv_shift_mix[l], jnp.zeros((lora_pad - n_lora,), F32)])[None, :]
        wc = jnp.zeros((lora_pad, 3 * rw), F32)
        wc = wc.at[:n_w, :rw].set(w_lora_up[l])
        wc = wc.at[n_w:n_w + n_a, rw:2 * rw].set(a_lora_up[l])
        wc = wc.at[n_w + n_a:n_lora, 2 * rw:].set(g_lora_up[l]).astype(BF16)

        p_r, q, k, v = _inproj(x2d, mix_norm_gain[l][None, :], w_r, w_a, mix, cos_t, sin_t)
        y_rwkv = _rwkv(p_r.reshape(B, S, -1), wc, w0[l][None, :], a0[l][None, :], k_k[l][None, :],
                       k_a[l][None, :], r_k[l].reshape(1, rw), ln_x_w[l][None, :], ln_x_b[l][None, :],
                       rw, (n_w, n_w + n_a, n_lora))
        y_attn = _attention(q.reshape(B, S, aw), k.reshape(B, S, aw), v.reshape(B, S, aw),
                            attn_norm_gain[l][None, :])
        x2d = _mix_ffn(x2d, y_rwkv.reshape(B * S, rw), y_attn.reshape(B * S, aw), w_out[l].astype(BF16),
                       ffn_norm_gain[l][None, :], w_ffn_up[l].astype(BF16), ffn_conv_w[l],
                       ffn_conv_b[l][None, :], w_ffn_down[l].astype(BF16), final_norm_gain[None, :], S,
                       apply_final=(l == depth - 1))
    return x2d.reshape(B, S, D)
```

```python
import functools
import math

import jax
import jax.numpy as jnp
from jax import lax
from jax.experimental import pallas as pl
from jax.experimental.pallas import tpu as pltpu

F32 = jnp.float32
BF16 = jnp.bfloat16

LANES = 128
HEAD_DIM = 64
PAIR = 2 * HEAD_DIM
ROT_DIM = HEAD_DIM // 4
ROPE_THETA = 500000.0
NORM_EPS = 1e-6
GN_EPS = 64e-5
DILATED_PATTERNS = ((128, 1), (512, 4), (2048, 16))
ATTN_BLOCK = 128
CONV_WIDTH = 3
CHUNK = 64
SOLVE_BLOCK = 16
NEG_BIG = -1e30
VMEM_LIMIT = 56 * 1024 * 1024


def _dot(a, b):
    return jnp.dot(a, b, preferred_element_type=F32)


def _dot_nt(a, b):
    return lax.dot_general(a, b, (((1,), (1,)), ((), ())), preferred_element_type=F32)


def _dot_tn(a, b):
    return lax.dot_general(a, b, (((0,), (0,)), ((), ())), preferred_element_type=F32)


def _rmsnorm(x, gain):
    return x * lax.rsqrt(jnp.mean(x * x, axis=-1, keepdims=True) + NORM_EPS) * gain


def _sigmoid(x):
    return 1.0 / (1.0 + jnp.exp(-x))


def _split_dot(x, w):
    hi = x.astype(BF16)
    lo = (x - hi.astype(F32)).astype(BF16)
    return _dot(hi, w) + _dot(lo, w)


def _inproj_kernel(x_ref, g_ref, wr_ref, wa_ref, mix_ref, cos_ref, sin_ref, pr_ref, q_ref, k_ref, v_ref, carry_scr,
                   *, seq_tiles):
    tm = x_ref.shape[0]
    h = _rmsnorm(x_ref[...], g_ref[...]).astype(BF16)

    @pl.when(pl.program_id(0) == 0)
    def _():
        carry_scr[...] = jnp.zeros_like(carry_scr)

    pr = _dot(h, wr_ref[...])
    p = _dot(h, wa_ref[...])
    row = lax.broadcasted_iota(jnp.int32, (tm, 1), 0)
    prev_last = jnp.where(pl.program_id(0) % seq_tiles == 0, 0.0, carry_scr[7:8, :])
    pr_prev = jnp.where(row == 0, prev_last, pltpu.roll(pr, 1, 0))
    carry_scr[...] = pr[tm - 8:tm, :]
    pr_ref[...] = pr + (pr_prev - pr) * mix_ref[...]
    aw = q_ref.shape[1]

    lane = lax.broadcasted_iota(jnp.int32, (tm, PAIR), 1)
    first_half = (lane % HEAD_DIM) < (ROT_DIM // 2)
    cos, sin = cos_ref[...], sin_ref[...]

    def rotary(x):
        partner = jnp.where(first_half, pltpu.roll(x, PAIR - ROT_DIM // 2, 1), pltpu.roll(x, ROT_DIM // 2, 1))
        return x * cos + partner * sin

    for t in range(aw // PAIR):
        lo = t * PAIR
        q_ref[:, lo:lo + PAIR] = (rotary(p[:, lo:lo + PAIR]) * (1.0 / math.sqrt(HEAD_DIM))).astype(BF16)
        k_ref[:, lo:lo + PAIR] = rotary(p[:, aw + lo:aw + lo + PAIR]).astype(BF16)
    v_ref[...] = p[:, 2 * aw:].astype(BF16)


def _inproj(x2d, gain, w_r, w_a, mix, cos_t, sin_t, tm=1024):
    T, D = x2d.shape
    rw, aw = w_r.shape[1], w_a.shape[1] // 3
    seq_tiles = cos_t.shape[0] // tm
    row = lambda i: (i, 0)
    fixed = lambda i: (0, 0)
    tab = pl.BlockSpec((tm, PAIR), lambda i: (i % seq_tiles, 0))
    return pl.pallas_call(
        functools.partial(_inproj_kernel, seq_tiles=seq_tiles),
        out_shape=(jax.ShapeDtypeStruct((T, rw), F32),) + (jax.ShapeDtypeStruct((T, aw), BF16),) * 3,
        grid=(T // tm,),
        in_specs=[pl.BlockSpec((tm, D), row), pl.BlockSpec((1, D), fixed), pl.BlockSpec(w_r.shape, fixed),
                  pl.BlockSpec(w_a.shape, fixed), pl.BlockSpec((1, rw), fixed), tab, tab],
        out_specs=(pl.BlockSpec((tm, rw), row),) + (pl.BlockSpec((tm, aw), row),) * 3,
        scratch_shapes=[pltpu.VMEM((8, rw), F32)],
        compiler_params=pltpu.CompilerParams(dimension_semantics=("arbitrary",), vmem_limit_bytes=VMEM_LIMIT),
        name="inproj",
    )(x2d, gain, w_r, w_a, mix, cos_t, sin_t)


def _each(fn, *lists):
    return [fn(*args) for args in zip(*lists)]


def _rwkv_chunk_terms(r, k, v, logw, cum, kkn, bb, same_blk, strict_c, incl_c, lane_lo):
    L = CHUNK
    bf = lambda t: t.astype(BF16)
    cum_last = [t[L - 1:L, :] for t in cum]
    g_in = _each(jnp.exp2, cum)
    g_ex = _each(lambda t, w: jnp.exp2(t - w), cum, logw)
    g_inv = _each(lambda t: jnp.exp2(-t), cum)
    g_hat = _each(lambda tl, t: jnp.exp2(tl - t), cum_last, cum)
    g_last = _each(jnp.exp2, cum_last)
    yield

    def by_head_rows(x):
        lo = lane_lo if x.shape[1] == PAIR else jnp.concatenate([lane_lo] * (x.shape[1] // PAIR), axis=1)
        return jnp.concatenate([jnp.where(lo, x, 0.0), jnp.where(lo, 0.0, x)], axis=0).astype(BF16)

    al = _each(lambda t, g: -t * g, kkn, g_ex)
    rb = _each(jnp.multiply, r, g_in)
    bt = _each(jnp.multiply, bb, g_inv)
    kt = _each(jnp.multiply, k, g_inv)
    bh = _each(jnp.multiply, bb, g_hat)
    kh = _each(jnp.multiply, k, g_hat)
    lhs = _each(lambda a, b: jnp.concatenate([a, b], axis=0).astype(BF16), al, rb)
    rhs = _each(lambda a, b: jnp.concatenate([by_head_rows(a), by_head_rows(b)], axis=0), bt, kt)
    yield
    aq = _each(_dot_nt, lhs, rhs)
    a_ab = [jnp.where(strict_c, t[:L, :2 * L], 0.0) for t in aq]
    a_ak = [jnp.where(strict_c, t[:L, 2 * L:], 0.0).astype(BF16) for t in aq]
    a_rb = [jnp.where(incl_c, t[L:, :2 * L], 0.0).astype(BF16) for t in aq]
    a_rk = [jnp.where(incl_c, t[L:, 2 * L:], 0.0).astype(BF16) for t in aq]
    yield

    v_s = _each(by_head_rows, v)
    av = _each(_dot, a_ak, v_s)
    BS = SOLVE_BLOCK
    n_blk = L // BS
    rhs0 = _each(lambda a, b: jnp.concatenate([a, b], axis=1), al, av)
    lo2 = lax.broadcasted_iota(jnp.int32, (BS, 2 * PAIR), 1) % PAIR < HEAD_DIM
    lane = lax.broadcasted_iota(jnp.int32, (BS, PAIR), 1)
    br =lax.broadcasted_iota(jnp.int32, (PAIR, PAIR), 0)
    bc = lax.broadcasted_iota(jnp.int32, (PAIR, PAIR), 1)
    blk_diag = (br // BS) == (bc // BS)

    def expand(pack):
        return jnp.where(blk_diag, jnp.concatenate([pack] * (PAIR // BS), axis=0), 0.0).astype(BF16)

    d_k = [functools.reduce(jnp.add, [jnp.where((lane % HEAD_DIM) // BS == b, t[b * BS:(b + 1) * BS, :], 0.0)
                                      for b in range(n_blk)]) for t in a_ab]
    t_m = d_k
    yield
    n_sq = int(math.log2(BS)) - 1
    d_k = _each(lambda d: _dot(d.astype(BF16), expand(d)), d_k)
    yield
    for lvl in range(n_sq):
        d_e = _each(expand, d_k)
        t_m = _each(lambda t, d, de: t + d + _dot(t.astype(BF16), de), t_m, d_k, d_e)
        if lvl + 1 < n_sq:
            d_k = _each(lambda d, de: _dot(d.astype(BF16), de), d_k, d_e)
        yield
    t_mb = _each(bf, t_m)

    zero_blk = jnp.zeros((BS, 2 * PAIR), BF16)
    x_blocks = [[] for _ in a_ab]
    xs_lo = [[] for _ in a_ab]
    xs_hi = [[] for _ in a_ab]

    def stacked(lo_parts, hi_parts):
        rows = [lo_parts.get(c, zero_blk) for c in range(n_blk)] + [hi_parts.get(c, zero_blk) for c in range(n_blk)]
        return jnp.concatenate(rows, axis=0)

    for b in range(n_blk):
        r_b = [t[b * BS:(b + 1) * BS, :] for t in rhs0]
        if b > 0:
            r_b = _each(lambda rr, a, lo_p, hi_p: rr + _dot(a[b * BS:(b + 1) * BS, :].astype(BF16),
                                                            stacked(dict(enumerate(lo_p)), dict(enumerate(hi_p)))),
                        r_b, a_ab, xs_lo, xs_hi)
            yield
        x_b = _each(lambda rr, tm: rr + _dot(tm, stacked({b: jnp.where(lo2, rr, 0.0).astype(BF16)},
                                                        {b: jnp.where(lo2, 0.0, rr).astype(BF16)})),
                    r_b, t_mb)
        for i, xb in enumerate(x_b):
            x_blocks[i].append(xb)
            xs_lo[i].append(jnp.where(lo2, xb, 0.0).astype(BF16))
            xs_hi[i].append(jnp.where(lo2, 0.0, xb).astype(BF16))
        yield
    x = [jnp.concatenate(blks, axis=0) for blks in x_blocks]
    x_s = _each(lambda lo_p, hi_p: jnp.concatenate(lo_p + hi_p, axis=0), xs_lo, xs_hi)

    z = _each(_dot, a_rb, x_s)
    akv = _each(_dot, a_rk, v_s)
    w2 = _each(lambda a, t: (a + t[:, :PAIR]).astype(BF16), rb, z)
    y_loc = _each(lambda t, a: t[:, PAIR:] + a, z, akv)
    w1 = [t[:, :PAIR].astype(BF16) for t in x]
    u_loc = [t[:, PAIR:] for t in x]
    yield

    m_t = _each(lambda a, b: jnp.where(same_blk, _dot_tn(a, b.astype(BF16)), 0.0).astype(BF16), w1, bh)
    s_loc = _each(
        lambda u, vv, b, kk_: jnp.where(
            same_blk,
            _dot_tn(jnp.concatenate([u, vv], axis=0).astype(BF16), jnp.concatenate([b, kk_], axis=0).astype(BF16)),
            0.0),
        u_loc, v, bh, kh)
    return w2, y_loc, m_t, s_loc, g_last


def _rwkv_kernel(p_ref, wc_ref, w0_ref, a0_ref, kk_ref, ka_ref, rk_ref, lnw_ref, lnb_ref,
                 o_ref, s_scr, *, nb, width, n_lora, n_chunks, n_groups, stage_offset):
    L = CHUNK
    R = n_chunks * L
    c = pl.program_id(0)

    @pl.when(c == 0)
    def _():
        s_scr[...] = jnp.zeros_like(s_scr)

    n_pairs = width // PAIR
    ri = lax.broadcasted_iota(jnp.int32, (2 * L, 2 * L), 0)
    ci = lax.broadcasted_iota(jnp.int32, (2 * L, 2 * L), 1)
    same_blk = (ri >= L) == (ci >= L)
    ti = lax.broadcasted_iota(jnp.int32, (L, 2 * L), 0)
    si = lax.broadcasted_iota(jnp.int32, (L, 2 * L), 1) % L
    strict_c = si < ti
    incl_c = si <= ti
    rr = lax.broadcasted_iota(jnp.int32, (R, R), 0)
    rc = lax.broadcasted_iota(jnp.int32, (R, R), 1)
    tri = ((rr // L == rc // L) & (rr >= rc)).astype(BF16)
    lane_lo = lax.broadcasted_iota(jnp.int32, (L, PAIR), 1) < HEAD_DIM
    lane_lo_r = lax.broadcasted_iota(jnp.int32, (R, PAIR), 1) < HEAD_DIM
    lora_lane = lax.broadcasted_iota(jnp.int32, (R, 2 * LANES), 1)

    def head_sums(x):
        s0 = jnp.sum(jnp.where(lane_lo_r, x, 0.0), axis=-1, keepdims=True)
        s1 = jnp.sum(jnp.where(lane_lo_r, 0.0, x), axis=-1, keepdims=True)
        return jnp.where(lane_lo_r, s0, s1)

    def lora_up(pm):
        lora = pm[:, 3 * width:]
        act = jnp.where(lora_lane < n_lora[0], jnp.tanh(lora),
                        jnp.where(lora_lane < n_lora[1], lora,
                                  jnp.where(lora_lane < n_lora[2], _sigmoid(lora), 0.0)))
        return _dot(act.astype(BF16), wc_ref[...])

    def log_decay(up):
        t = w0_ref[...] + up[:, :width]
        return (-math.exp(-0.5) * math.log2(math.e)) / (1.0 + jnp.exp(-t))

    def running_sum(t):
        h1 = t.astype(BF16)
        r1 = t - h1.astype(F32)
        h2 = r1.astype(BF16)
        h3 = (r1 - h2.astype(F32)).astype(BF16)
        return _dot(tri, h1) + _dot(tri, h2) + _dot(tri, h3)

    tiles = [(b, pr) for b in range(nb) for pr in range(n_pairs)]
    n_t = len(tiles)
    lanes = lambda pr, part=0: slice(part * width + pr * PAIR, part * width + (pr + 1) * PAIR)
    sls = [lanes(pr) for _, pr in tiles]

    states = {0: [s_scr[i] for i in range(n_t)]}
    outs = {}

    def run_group(gi):
        g0 = gi * R
        pms = [p_ref[b, g0:g0 + R, :] for b in range(nb)]
        yield
        ups = _each(lora_up, pms)
        logw_b = _each(log_decay, ups)
        cum_b = _each(running_sum, logw_b)
        yield

        r = [pms[b][:, lanes(pr, 0)] for b, pr in tiles]
        k_raw = [pms[b][:, lanes(pr, 1)] for b, pr in tiles]
        v = [pms[b][:, lanes(pr, 2)] for b, pr in tiles]
        logw = [logw_b[b][:, lanes(pr)] for b, pr in tiles]
        cum = [cum_b[b][:, lanes(pr)] for b, pr in tiles]
        a = [_sigmoid(a0_ref[:, lanes(pr)] + ups[b][:, lanes(pr, 1)]) for b, pr in tiles]
        g = [ups[b][:, lanes(pr, 2)] for b, pr in tiles]

        kk = _each(lambda t, sl: t * kk_ref[:, sl], k_raw, sls)
        kk_ss = _each(lambda t: head_sums(t * t), kk)
        kkn = _each(lambda t, ss: t * lax.rsqrt(jnp.maximum(ss, 1e-24)), kk, kk_ss)
        k = _each(lambda t, aa, sl: t * (1.0 + (aa - 1.0) * ka_ref[:, sl]), k_raw, a, sls)
        bb = _each(jnp.multiply, kkn, a)
        bonus_dot = _each(lambda rr_, kk_, sl: head_sums(rr_ * kk_ * rk_ref[:, sl]), r, k, sls)
        yield

        def chunks(ts):
            return [t[ch * L:(ch + 1) * L] for ch in range(n_chunks) for t in ts]

        w2, y_loc, m_t, s_loc, g_last = yield from _rwkv_chunk_terms(
            chunks(r), chunks(k), chunks(v), chunks(logw), chunks(cum), chunks(kkn), chunks(bb),
            same_blk, strict_c, incl_c, lane_lo)
        yield

        s = states[gi]
        y_parts = []
        for ch in range(n_chunks):
            sel = slice(ch * n_t, (ch + 1) * n_t)
            s_b = _each(lambda t: t.astype(BF16), s)
            y_parts.append(_each(lambda a_, sb, yl: _dot_nt(a_, sb) + yl, w2[sel], s_b, y_loc[sel]))
            s = _each(lambda s0, gl, sb, m, sl: s0 * gl + _dot(sb, m) + sl, s, g_last[sel], s_b, m_t[sel], s_loc[sel])
        states[gi + 1] = s
        y = [jnp.concatenate([y_parts[ch][i] for ch in range(n_chunks)], axis=0) for i in range(n_t)]
        yield

        mu = _each(lambda t: head_sums(t) * (1.0 / HEAD_DIM), y)
        yc = _each(jnp.subtract, y, mu)
        var = _each(lambda t: head_sums(t * t) * (1.0 / HEAD_DIM), yc)
        yield
        res = []
        for i in range(n_t):
            sl = sls[i]
            yn = yc[i] * lax.rsqrt(var[i] + GN_EPS) * lnw_ref[:, sl] + lnb_ref[:, sl]
            res.append(((yn + bonus_dot[i] * v[i]) * g[i]).astype(o_ref.dtype))
        outs[gi] = res

    gens = [run_group(gi) for gi in range(n_groups)]
    live = [True] * n_groups
    tick = 0
    while any(live):
        for gi in range(n_groups):
            if live[gi] and tick >= gi * stage_offset:
                try:
                    next(gens[gi])
                except StopIteration:
                    live[gi] = False
        tick += 1

    for gi in range(n_groups):
        for i, (b, pr) in enumerate(tiles):
            o_ref[b, gi * R:(gi + 1) * R, sls[i]] = outs[gi][i]
    for i in range(n_t):
        s_scr[i] = states[n_groups][i]


def _rwkv(p_r, wc, w0, a0, k_k, k_a, r_k, ln_w, ln_b, width, n_lora, n_chunks=2, n_groups=4, stage_offset=5):
    B, S, C = p_r.shape
    L = CHUNK * n_chunks * n_groups
    n_state = B * (width // PAIR)
    vec = lambda n: pl.BlockSpec((1, n), lambda c: (0, 0))
    return pl.pallas_call(
        functools.partial(_rwkv_kernel, nb=B, width=width, n_lora=n_lora, n_chunks=n_chunks, n_groups=n_groups,
                          stage_offset=stage_offset),
        out_shape=jax.ShapeDtypeStruct((B, S, width), BF16),
        grid=(S // L,),
        in_specs=[pl.BlockSpec((B, L, C), lambda c: (0, c, 0)),
                  pl.BlockSpec(wc.shape, lambda c: (0, 0))] + [vec(width)] * 7,
        out_specs=pl.BlockSpec((B, L, width), lambda c: (0, c, 0)),
        scratch_shapes=[pltpu.VMEM((n_state, PAIR, PAIR), F32)],
        compiler_params=pltpu.CompilerParams(dimension_semantics=("arbitrary",), vmem_limit_bytes=VMEM_LIMIT),
        name="rwkv7",
    )(p_r, wc, w0, a0, k_k, k_a, r_k, ln_w, ln_b)


SB_ROWS = 2048
SB_QUARTER = SB_ROWS // 4
N_RES = 4


def _attn_segments(pi, g):
    if pi == 0:
        return [(r * SB_QUARTER + (ATTN_BLOCK // N_RES) * g, ATTN_BLOCK // N_RES, 1) for r in range(N_RES)]
    if pi == 1:
        return [((g % 4) * SB_QUARTER + (g // 4) * ATTN_BLOCK, ATTN_BLOCK, 1)]
    return [((g % 4) * SB_QUARTER + g // 4, ATTN_BLOCK, 4)]


def _attn_prev_tile(pi, g):
    if pi == 0:
        return (g + 15) % 16, g == 0
    if pi == 1:
        return ((g // 4 + 3) % 4) * 4 + g % 4, g < 4
    return g, True


def _attn_kernel(*refs, tiles_per_group):
    q_refs, k_refs, v_refs = refs[0:N_RES], refs[N_RES:2 * N_RES], refs[2 * N_RES:3 * N_RES]
    gain_ref, o_ref, q_scr, k_ring, v_ring, bias_scr, out_nat = refs[3 * N_RES:3 * N_RES + 7]
    stat_scr = refs[3 * N_RES + 7:]
    sb = pl.program_id(2)
    n_pat = len(DILATED_PATTERNS)
    assert DILATED_PATTERNS == ((128, 1), (512, 4), (2048, 16)) and ATTN_BLOCK == 128
    num_scr, m_scr, l_scr = stat_scr[:n_pat], stat_scr[n_pat:2 * n_pat], stat_scr[2 * n_pat:]
    Q = ATTN_BLOCK
    cur_base = (sb % 2) * SB_ROWS
    other_base = SB_ROWS - cur_base

    @pl.when(sb == 0)
    def _():
        k_ring[pl.ds(SB_ROWS, SB_ROWS), :] = jnp.zeros((SB_ROWS, PAIR), F32)
        v_ring[pl.ds(SB_ROWS, SB_ROWS), :] = jnp.zeros((SB_ROWS, PAIR), F32)

    for r in range(N_RES):
        q_scr[r * SB_QUARTER:(r + 1) * SB_QUARTER, :] = q_refs[r][...].astype(F32)
        k_ring[pl.ds(cur_base + r * SB_QUARTER, SB_QUARTER), :] = k_refs[r][...].astype(F32)
        v_ring[pl.ds(cur_base + r * SB_QUARTER, SB_QUARTER), :] = v_refs[r][...].astype(F32)

    @pl.when(sb == 0)
    def _():
        ii = lax.broadcasted_iota(jnp.int32, (2 * Q, 2 * Q), 0) % Q
        cj = lax.broadcasted_iota(jnp.int32, (2 * Q, 2 * Q), 1)
        per = Q // N_RES
        for kind in range(2):
            if kind == 0:
                qi = N_RES * (ii % per) + ii // per
                kj = N_RES * ((cj % Q) % per) + (cj % Q) // per + (cj // Q) * Q
            else:
                qi, kj = ii, cj
            band = (kj >= qi) & (kj <= qi + Q)
            bias_scr[kind, 1] = jnp.where(band, 0.0, NEG_BIG)
            bias_scr[kind, 0] = jnp.where(band & (cj >= Q), 0.0, NEG_BIG)

    def load_tile(ref, base, segs):
        parts = [ref[pl.ds(base + s, n, stride=st) if st > 1 else pl.ds(base + s, n), :] for s, n, st in segs]
        return parts[0] if len(parts) == 1 else jnp.concatenate(parts, axis=0)

    def store_tile(ref, segs, val):
        off = 0
        for s, n, st in segs:
            ref[pl.ds(s, n, stride=st) if st > 1 else pl.ds(s, n), :] = val[off:off + n]
            off += n

    lane_lo = lax.broadcasted_iota(jnp.int32, (Q, PAIR), 1) < HEAD_DIM
    ones_blk = jnp.ones((2 * Q, PAIR), BF16)

    n_blk = SB_ROWS // Q
    has_prev_sb = jnp.where(sb > 0, 1, 0)

    def scores(pi, gs):
        segs = [_attn_segments(pi, g) for g in gs]
        prev = [_attn_prev_tile(pi, g) for g in gs]
        prev_segs = [_attn_segments(pi, pg) for pg, _ in prev]
        prev_base = [other_base if other else cur_base for _, other in prev]
        q2 = [load_tile(q_scr, 0, sg) for sg in segs]
        q2 = [jnp.concatenate([jnp.where(lane_lo, t, 0.0), jnp.where(lane_lo, 0.0, t)], axis=0).astype(BF16)
              for t in q2]
        kcat = [jnp.concatenate([load_tile(k_ring, pb, psg), load_tile(k_ring, cur_base, sg)], axis=0).astype(BF16)
                for pb, psg, sg in zip(prev_base, prev_segs, segs)]
        vcat = [jnp.concatenate([load_tile(v_ring, pb, psg), load_tile(v_ring, cur_base, sg)], axis=0).astype(BF16)
                for pb, psg, sg in zip(prev_base, prev_segs, segs)]
        vext = [jnp.concatenate([t, ones_blk], axis=1) for t in vcat]
        has_prev = [has_prev_sb if other else 1 for _, other in prev]
        return dict(pi=pi, segs=segs, s=_each(_dot_nt, q2, kcat), vext=vext, has_prev=has_prev)

    def softmax(c):
        kind = 0 if c["pi"] == 0 else 1
        s = _each(lambda t, hp: t + bias_scr[kind, hp], c["s"], c["has_prev"])
        m = [jnp.max(t, axis=-1, keepdims=True) for t in s]
        p = _each(lambda t, mm: jnp.exp(t - mm).astype(BF16), s, m)
        return dict(pi=c["pi"], segs=c["segs"], vext=c["vext"], m=m, p=p)

    def weighted_values(c):
        pi, segs, m = c["pi"], c["segs"], c["m"]
        nl = _each(_dot, c["p"], c["vext"])
        for t in range(len(segs)):
            store_tile(num_scr[pi], segs[t], jnp.where(lane_lo, nl[t][:Q, :PAIR], nl[t][Q:, :PAIR]))
            store_tile(l_scr[pi], segs[t], jnp.where(lane_lo, nl[t][:Q, PAIR:], nl[t][Q:, PAIR:]))
            store_tile(m_scr[pi], segs[t], jnp.where(lane_lo, m[t][:Q], m[t][Q:]))

    groups = [(pi, list(range(g0, g0 + tiles_per_group)))
              for pi in range(n_pat) for g0 in range(0, n_blk, tiles_per_group)]
    n_grp = len(groups)
    sc = {0: scores(*groups[0])}
    if n_grp > 1:
        sc[1] = scores(*groups[1])
    sm = {0: softmax(sc.pop(0))}
    for k in range(n_grp):
        if k + 2 < n_grp:
            sc[k + 2] = scores(*groups[k + 2])
        if k + 1 < n_grp:
            sm[k + 1] = softmax(sc.pop(k + 1))
        weighted_values(sm.pop(k))

    ri = lax.broadcasted_iota(jnp.int32, (PAIR, PAIR), 0)
    ci = lax.broadcasted_iota(jnp.int32, (PAIR, PAIR), 1)
    seg_ones = ((ri >= HEAD_DIM) == (ci >= HEAD_DIM)).astype(BF16)
    gain = gain_ref[...]

    def merge(i, carry):
        rows = pl.ds(pl.multiple_of(i * Q, Q), Q)
        ms = [m_scr[pi][rows, :] for pi in range(n_pat)]
        m_all = functools.reduce(jnp.maximum, ms)
        num = 0.0
        den = 0.0
        for pi in range(n_pat):
            wgt = jnp.exp(ms[pi] - m_all)
            num = num + wgt * num_scr[pi][rows, :]
            den = den + wgt * l_scr[pi][rows, :]
        o = num / den
        ms_o = _split_dot(o * o, seg_ones) * (1.0 / HEAD_DIM)
        out_nat[pl.ds((i % 4) * SB_QUARTER + i // 4, Q, stride=N_RES), :] = o * lax.rsqrt(ms_o + NORM_EPS) * gain
        return carry

    lax.fori_loop(0, SB_ROWS // Q, merge, 0, unroll=4)
    o_ref[...] = out_nat[...].astype(o_ref.dtype)


def _attention(q, k, v, gain, tiles_per_group=4):
    B, S, W = q.shape
    n_pairs = W // PAIR
    n_pat = len(DILATED_PATTERNS)
    blk = pl.BlockSpec((None, SB_ROWS, PAIR), lambda b, p, s: (b, s, p))
    slabs = [pl.BlockSpec((None, SB_QUARTER, PAIR), lambda b, p, s, r=r: (b, s, r * n_pairs + p))
             for r in range(N_RES)]
    by_res = lambda t: t.reshape(B, S // N_RES, N_RES * W)
    tile = pltpu.VMEM((SB_ROWS, PAIR), F32)
    return pl.pallas_call(
        functools.partial(_attn_kernel, tiles_per_group=tiles_per_group),
        out_shape=jax.ShapeDtypeStruct((B, S, W), BF16),
        grid=(B, n_pairs, S // SB_ROWS),
        in_specs=slabs * 3 + [pl.BlockSpec((1, PAIR), lambda b, p, s: (0, p))],
        out_specs=blk,
        scratch_shapes=[tile,
                        pltpu.VMEM((2 * SB_ROWS, PAIR), F32),
                        pltpu.VMEM((2 * SB_ROWS, PAIR), F32),
                        pltpu.VMEM((2, 2, 2 * ATTN_BLOCK, 2 * ATTN_BLOCK), F32),
                        tile]
                       + [tile] * (3 * n_pat),
        compiler_params=pltpu.CompilerParams(dimension_semantics=("parallel", "parallel", "arbitrary"),
                                             vmem_limit_bytes=VMEM_LIMIT),
        name="dilated_attn",
    )(*([by_res(q)] * N_RES + [by_res(k)] * N_RES + [by_res(v)] * N_RES), gain)


def _rotary_tables(seq):
    half = ROT_DIM // 2
    inv_freq = ROPE_THETA ** (-jnp.arange(half, dtype=F32) * 2.0 / ROT_DIM)
    ang = jnp.arange(seq).astype(F32)[:, None] * inv_freq[None, :]
    cos, sin = jnp.cos(ang), jnp.sin(ang)
    rest = HEAD_DIM - ROT_DIM
    cos_h = jnp.concatenate([cos, cos, jnp.ones((seq, rest), F32)], axis=-1)
    sin_h = jnp.concatenate([-sin, sin, jnp.zeros((seq, rest), F32)], axis=-1)
    return jnp.tile(cos_h, (1, PAIR // HEAD_DIM)), jnp.tile(sin_h, (1, PAIR // HEAD_DIM))


FFN_HALO = 16
MIX_PIECES = 4


def _mix_ffn_kernel(x_ref, ya_ref, yb_ref, xh_ref, yah_ref, ybh_ref, wo_ref, g_ref, wu_ref, cw_ref,
                    cb_ref, wd_ref, fg_ref, o_ref, h_scr, hh_scr, *, tm, seq, row_chunks, apply_final):
    tf = wd_ref.shape[0]
    i = pl.program_id(0)
    j = pl.program_id(1)
    wa = ya_ref.shape[1]

    def mixed(x, ya, yb):
        return x + _dot(ya, wo_ref[:wa, :]) + _dot(yb, wo_ref[wa:, :])

    @pl.when(j == 0)
    def _():
        hh_scr[...] = _rmsnorm(mixed(xh_ref[...], yah_ref[...], ybh_ref[...]), g_ref[...]).astype(BF16)
        piece = tm // MIX_PIECES
        for c in range(MIX_PIECES):
            rows = slice(c * piece, (c + 1) * piece)
            x1 = mixed(x_ref[rows, :], ya_ref[rows, :], yb_ref[rows, :])
            o_ref[rows, :] = x1
            h_scr[rows, :] = _rmsnorm(x1, g_ref[...]).astype(BF16)

    seq_start = (i * tm) % seq == 0
    rc = tm // row_chunks
    row = lax.broadcasted_iota(jnp.int32, (rc, 1), 0)
    for c in range(row_chunks):
        rows = slice(c * rc, (c + 1) * rc)
        gv = _dot(h_scr[rows, :], wu_ref[...])
        gate, val = gv[:, :tf], gv[:, tf:]
        if c == 0:
            gate_h = jnp.where(seq_start, 0.0, _dot(hh_scr[...], wu_ref[:, :tf]))
        else:
            gate_h = _dot(h_scr[c * rc - FFN_HALO:c * rc, :], wu_ref[:, :tf])
        g1 = jnp.where(row == 0, gate_h[FFN_HALO - 1:FFN_HALO, :], pltpu.roll(gate, 1, 0))
        g2 = jnp.where(row == 0, gate_h[FFN_HALO - 2:FFN_HALO - 1, :],
                       jnp.where(row == 1, gate_h[FFN_HALO - 1:FFN_HALO, :], pltpu.roll(gate, 2, 0)))
        u = cw_ref[0:1, :] * g2 + cw_ref[1:2, :] * g1 + cw_ref[2:3, :] * gate + cb_ref[...]
        act = (u * _sigmoid(u) * val).astype(BF16)
        o_ref[rows, :] += _dot(act, wd_ref[...])

    if apply_final:
        @pl.when(j == pl.num_programs(1) - 1)
        def _():
            o_ref[...] = _rmsnorm(o_ref[...], fg_ref[...])


def _mix_ffn(x2d, y_a, y_b, w_o, gain, w_up, conv_w, conv_b, w_down, final_gain, seq, apply_final,
             tm=1024, n_ff_tiles=2, row_chunks=1):
    T, D = x2d.shape
    F = w_down.shape[0]
    tf = F // n_ff_tiles
    assert tf % LANES == 0 and seq % tm == 0 and w_up.shape[1] == 2 * F
    w_up = jnp.concatenate([w_up[:, half * F + j * tf: half * F + (j + 1) * tf]
                            for j in range(n_ff_tiles) for half in range(2)], axis=1)
    halo_blocks = tm // FFN_HALO
    row = lambda i, j: (i, 0)
    halo = lambda i, j: (jnp.maximum(i * halo_blocks - 1, 0), 0)
    fixed = lambda i, j: (0, 0)
    wa, wb = y_a.shape[1], y_b.shape[1]
    return pl.pallas_call(
        functools.partial(_mix_ffn_kernel, tm=tm, seq=seq, row_chunks=row_chunks, apply_final=apply_final),
        out_shape=jax.ShapeDtypeStruct((T, D), F32),
        grid=(T // tm, n_ff_tiles),
        in_specs=[pl.BlockSpec((tm, D), row), pl.BlockSpec((tm, wa), row), pl.BlockSpec((tm, wb), row),
                  pl.BlockSpec((FFN_HALO, D), halo), pl.BlockSpec((FFN_HALO, wa), halo),
                  pl.BlockSpec((FFN_HALO, wb), halo),
                  pl.BlockSpec(w_o.shape, fixed),
                  pl.BlockSpec((1, D), fixed),
                  pl.BlockSpec((D, 2 * tf), lambda i, j: (0, j)),
                  pl.BlockSpec((CONV_WIDTH, tf), lambda i, j: (0, j)),
                  pl.BlockSpec((1, tf), lambda i, j: (0, j)),
                  pl.BlockSpec((tf, D), lambda i, j: (j, 0)),
                  pl.BlockSpec((1, D), fixed)],
        out_specs=pl.BlockSpec((tm, D), row),
        scratch_shapes=[pltpu.VMEM((tm, D), BF16), pltpu.VMEM((FFN_HALO, D), BF16)],
        compiler_params=pltpu.CompilerParams(dimension_semantics=("parallel", "arbitrary"),
                                             vmem_limit_bytes=VMEM_LIMIT),
        name="mix_convglu_ffn",
    )(x2d, y_a, y_b, x2d, y_a, y_b, w_o, gain, w_up, conv_w, conv_b, w_down, final_gain)


def kernel(x, mix_norm_gain, w_in, rwkv_shift_mix, w0, w_lora_up, a0, a_lora_up, g_lora_up, k_k, k_a, r_k,
           ln_x_w, ln_x_b, attn_norm_gain, w_out, ffn_norm_gain, w_ffn_up, ffn_conv_w, ffn_conv_b,
           w_ffn_down, final_norm_gain):
    B, S, D = x.shape
    depth = w_in.shape[0]
    rw = w0.shape[1]
    aw = attn_norm_gain.shape[1]
    n_w, n_a, n_g = w_lora_up.shape[1], a_lora_up.shape[1], g_lora_up.shape[1]
    n_lora = n_w + n_a + n_g
    lora_pad = -(-n_lora // (2 * LANES)) * (2 * LANES)
    assert lora_pad == 2 * LANES and rw % PAIR == 0 and aw % PAIR == 0
    rwkv_cols = 3 * rw + n_lora
    cos_t, sin_t = _rotary_tables(S)

    x2d = x.reshape(B * S, D)
    for l in range(depth):
        w_r = w_in[l][:, :3 * rw + lora_pad].astype(BF16)
        w_a = w_in[l][:, rwkv_cols:].astype(BF16)
        mix =jnp.concatenate([rwkv_shift_mix[l], jnp.zeros((lora_pad - n_lora,), F32)])[None, :]
        wc = jnp.zeros((lora_pad, 3 * rw), F32)
        wc = wc.at[:n_w, :rw].set(w_lora_up[l])
        wc = wc.at[n_w:n_w + n_a, rw:2 * rw].set(a_lora_up[l])
        wc = wc.at[n_w + n_a:n_lora, 2 * rw:].set(g_lora_up[l]).astype(BF16)

        p_r, q, k, v = _inproj(x2d, mix_norm_gain[l][None, :], w_r, w_a, mix, cos_t, sin_t)
        y_rwkv = _rwkv(p_r.reshape(B, S, -1), wc, w0[l][None, :], a0[l][None, :], k_k[l][None, :],
                       k_a[l][None, :], r_k[l].reshape(1, rw), ln_x_w[l][None, :], ln_x_b[l][None, :],
                       rw, (n_w, n_w + n_a, n_lora))
        y_attn = _attention(q.reshape(B, S, aw), k.reshape(B, S, aw), v.reshape(B, S, aw),
                            attn_norm_gain[l][None, :])
        x2d = _mix_ffn(x2d, y_rwkv.reshape(B * S, rw), y_attn.reshape(B * S, aw), w_out[l].astype(BF16),
                       ffn_norm_gain[l][None, :], w_ffn_up[l].astype(BF16), ffn_conv_w[l],
                       ffn_conv_b[l][None, :], w_ffn_down[l].astype(BF16), final_norm_gain[None, :], S,
                       apply_final=(l == depth - 1))
    return x2d.reshape(B, S, D)
```

```python
import functools
import math

import jax
import jax.numpy as jnp
from jax import lax
from jax.experimental import pallas as pl
from jax.experimental.pallas import tpu as pltpu

F32 = jnp.float32
BF16 = jnp.bfloat16

LANES = 128
HEAD_DIM = 64
PAIR = 2 * HEAD_DIM
ROT_DIM = HEAD_DIM // 4
ROPE_THETA = 500000.0
NORM_EPS = 1e-6
GN_EPS = 64e-5
DILATED_PATTERNS = ((128, 1), (512, 4), (2048, 16))
ATTN_BLOCK = 128
CONV_WIDTH = 3
CHUNK = 64
SOLVE_BLOCK = 16
NEG_BIG = -1e30
VMEM_LIMIT = 56 * 1024 * 1024


def _dot(a, b):
    return jnp.dot(a, b, preferred_element_type=F32)


def _dot_nt(a, b):
    return lax.dot_general(a, b, (((1,), (1,)), ((), ())), preferred_element_type=F32)


def _dot_tn(a, b):
    return lax.dot_general(a, b, (((0,), (0,)), ((), ())), preferred_element_type=F32)


def _rmsnorm(x, gain):
    return x * lax.rsqrt(jnp.mean(x * x, axis=-1, keepdims=True) + NORM_EPS) * gain


def _sigmoid(x):
    return 1.0 / (1.0 + jnp.exp(-x))


def _split_dot(x, w):
    hi = x.astype(BF16)
    lo = (x - hi.astype(F32)).astype(BF16)
    return _dot(hi, w) + _dot(lo, w)


INPROJ_PIECES = 2


def _inproj_kernel(x_ref, g_ref, wr_ref, wa_ref, mix_ref, cos_ref, sin_ref, pr_ref, q_ref, k_ref, v_ref, carry_scr,
                   *, seq_tiles):
    tm = x_ref.shape[0]
    aw = q_ref.shape[1]
    rp = tm // INPROJ_PIECES

    @pl.when(pl.program_id(0) == 0)
    def _():
        carry_scr[...] = jnp.zeros_like(carry_scr)

    row = lax.broadcasted_iota(jnp.int32, (rp, 1), 0)
    lane = lax.broadcasted_iota(jnp.int32, (rp, PAIR), 1)
    first_half = (lane % HEAD_DIM) < (ROT_DIM // 2)
    prev_last = jnp.where(pl.program_id(0) % seq_tiles == 0, 0.0, carry_scr[7:8, :])

    for c in range(INPROJ_PIECES):
        rows = slice(c * rp, (c + 1) * rp)
        h = _rmsnorm(x_ref[rows, :], g_ref[...]).astype(BF16)
        pr = _dot(h, wr_ref[...])
        p = _dot(h, wa_ref[...])
        pr_prev = jnp.where(row == 0, prev_last, pltpu.roll(pr, 1, 0))
        prev_last = pr[rp - 1:rp, :]
        if c == INPROJ_PIECES - 1:
            carry_scr[...] = pr[rp - 8:rp, :]
        pr_ref[rows, :] = pr + (pr_prev - pr) * mix_ref[...]

        cos, sin = cos_ref[rows, :], sin_ref[rows, :]

        def rotary(x):
            partner = jnp.where(first_half, pltpu.roll(x, PAIR - ROT_DIM // 2, 1), pltpu.roll(x, ROT_DIM // 2, 1))
            return x * cos + partner * sin

        for t in range(aw // PAIR):
            lo = t * PAIR
            q_ref[rows, lo:lo + PAIR] = (rotary(p[:, lo:lo + PAIR]) * (1.0 / math.sqrt(HEAD_DIM))).astype(BF16)
            k_ref[rows, lo:lo + PAIR] = rotary(p[:, aw + lo:aw + lo + PAIR]).astype(BF16)
        v_ref[rows, :] = p[:, 2 * aw:].astype(BF16)


def _inproj(x2d, gain, w_r, w_a, mix, cos_t, sin_t, tm=1024):
    T, D = x2d.shape
    rw, aw = w_r.shape[1], w_a.shape[1] // 3
    seq_tiles = cos_t.shape[0] // tm
    row = lambda i: (i, 0)
    fixed = lambda i: (0, 0)
    tab = pl.BlockSpec((tm, PAIR), lambda i: (i % seq_tiles, 0))
    return pl.pallas_call(
        functools.partial(_inproj_kernel, seq_tiles=seq_tiles),
        out_shape=(jax.ShapeDtypeStruct((T, rw), F32),) + (jax.ShapeDtypeStruct((T, aw), BF16),) * 3,
        grid=(T // tm,),
        in_specs=[pl.BlockSpec((tm, D), row), pl.BlockSpec((1, D), fixed), pl.BlockSpec(w_r.shape, fixed),
                  pl.BlockSpec(w_a.shape, fixed), pl.BlockSpec((1, rw), fixed), tab, tab],
        out_specs=(pl.BlockSpec((tm, rw), row),) + (pl.BlockSpec((tm, aw), row),) * 3,
        scratch_shapes=[pltpu.VMEM((8, rw), F32)],
        compiler_params=pltpu.CompilerParams(dimension_semantics=("arbitrary",), vmem_limit_bytes=VMEM_LIMIT),
        name="inproj",
    )(x2d, gain, w_r, w_a, mix, cos_t, sin_t)


def _each(fn, *lists):
    return [fn(*args) for args in zip(*lists)]


def _rwkv_chunk_terms(r, k, v, logw, cum, kkn, bb, same_blk, strict_c, incl_c, lane_lo):
    L = CHUNK
    bf = lambda t: t.astype(BF16)
    cum_last = [t[L - 1:L, :] for t in cum]
    g_in = _each(jnp.exp2, cum)
    g_ex = _each(lambda t, w: jnp.exp2(t - w), cum, logw)
    g_inv = _each(lambda t: jnp.exp2(-t), cum)
    g_hat = _each(lambda tl, t: jnp.exp2(tl - t), cum_last, cum)
    g_last = _each(jnp.exp2, cum_last)
    yield

    def by_head_rows(x):
        lo = lane_lo if x.shape[1] == PAIR else jnp.concatenate([lane_lo] * (x.shape[1] // PAIR), axis=1)
        return jnp.concatenate([jnp.where(lo, x, 0.0), jnp.where(lo, 0.0, x)], axis=0).astype(BF16)

    al = _each(lambda t, g: -t * g, kkn, g_ex)
    rb = _each(jnp.multiply, r, g_in)
    bt = _each(jnp.multiply, bb, g_inv)
    kt = _each(jnp.multiply, k, g_inv)
    bh = _each(jnp.multiply, bb, g_hat)
    kh = _each(jnp.multiply, k, g_hat)
    lhs = _each(lambda a, b: jnp.concatenate([a, b], axis=0).astype(BF16), al, rb)
    rhs = _each(lambda a, b: jnp.concatenate([by_head_rows(a), by_head_rows(b)], axis=0), bt, kt)
    yield
    aq = _each(_dot_nt, lhs, rhs)
    a_ab = [jnp.where(strict_c, t[:L, :2 * L], 0.0) for t in aq]
    a_ak = [jnp.where(strict_c, t[:L, 2 * L:], 0.0).astype(BF16) for t in aq]
    a_rb = [jnp.where(incl_c, t[L:, :2 * L], 0.0).astype(BF16) for t in aq]
    a_rk = [jnp.where(incl_c, t[L:, 2 * L:], 0.0).astype(BF16) for t in aq]
    yield

    v_s = _each(by_head_rows, v)
    av = _each(_dot, a_ak, v_s)
    BS = SOLVE_BLOCK
    n_blk = L // BS
    rhs0 = _each(lambda a, b: jnp.concatenate([a, b], axis=1), al, av)
    lo2 = lax.broadcasted_iota(jnp.int32, (BS, 2 * PAIR), 1) % PAIR < HEAD_DIM
    lane = lax.broadcasted_iota(jnp.int32, (BS, PAIR), 1)
    br =lax.broadcasted_iota(jnp.int32, (PAIR, PAIR), 0)
    bc = lax.broadcasted_iota(jnp.int32, (PAIR, PAIR), 1)
    blk_diag = (br // BS) == (bc // BS)

    def expand(pack):
        return jnp.where(blk_diag, jnp.concatenate([pack] * (PAIR // BS), axis=0), 0.0).astype(BF16)

    d_k = [functools.reduce(jnp.add, [jnp.where((lane % HEAD_DIM) // BS == b, t[b * BS:(b + 1) * BS, :], 0.0)
                                      for b in range(n_blk)]) for t in a_ab]
    t_m = d_k
    yield
    n_sq = int(math.log2(BS)) - 1
    d_k = _each(lambda d: _dot(d.astype(BF16), expand(d)), d_k)
    yield
    for lvl in range(n_sq):
        d_e = _each(expand, d_k)
        t_m = _each(lambda t, d, de: t + d + _dot(t.astype(BF16), de), t_m, d_k, d_e)
        if lvl + 1 < n_sq:
            d_k = _each(lambda d, de: _dot(d.astype(BF16), de), d_k, d_e)
        yield
    t_mb = _each(bf, t_m)

    zero_blk = jnp.zeros((BS, 2 * PAIR), BF16)
    x_blocks = [[] for _ in a_ab]
    xs_lo = [[] for _ in a_ab]
    xs_hi = [[] for _ in a_ab]

    def stacked(lo_parts, hi_parts):
        rows = [lo_parts.get(c, zero_blk) for c in range(n_blk)] + [hi_parts.get(c, zero_blk) for c in range(n_blk)]
        return jnp.concatenate(rows, axis=0)

    for b in range(n_blk):
        r_b = [t[b * BS:(b + 1) * BS, :] for t in rhs0]
        if b > 0:
            r_b = _each(lambda rr, a, lo_p, hi_p: rr + _dot(a[b * BS:(b + 1) * BS, :].astype(BF16),
                                                            stacked(dict(enumerate(lo_p)), dict(enumerate(hi_p)))),
                        r_b, a_ab, xs_lo, xs_hi)
            yield
        x_b = _each(lambda rr, tm: rr + _dot(tm, stacked({b: jnp.where(lo2, rr, 0.0).astype(BF16)},
                                                        {b: jnp.where(lo2, 0.0, rr).astype(BF16)})),
                    r_b, t_mb)
        for i, xb in enumerate(x_b):
            x_blocks[i].append(xb)
            xs_lo[i].append(jnp.where(lo2, xb, 0.0).astype(BF16))
            xs_hi[i].append(jnp.where(lo2, 0.0, xb).astype(BF16))
        yield
    x = [jnp.concatenate(blks, axis=0) for blks in x_blocks]
    x_s = _each(lambda lo_p, hi_p: jnp.concatenate(lo_p + hi_p, axis=0), xs_lo, xs_hi)

    z = _each(_dot, a_rb, x_s)
    akv = _each(_dot, a_rk, v_s)
    w2 = _each(lambda a, t: (a + t[:, :PAIR]).astype(BF16), rb, z)
    y_loc = _each(lambda t, a: t[:, PAIR:] + a, z, akv)
    w1 = [t[:, :PAIR].astype(BF16) for t in x]
    u_loc = [t[:, PAIR:] for t in x]
    yield

    m_t = _each(lambda a, b: jnp.where(same_blk, _dot_tn(a, b.astype(BF16)), 0.0).astype(BF16), w1, bh)
    s_loc = _each(
        lambda u, vv, b, kk_: jnp.where(
            same_blk,
            _dot_tn(jnp.concatenate([u, vv], axis=0).astype(BF16), jnp.concatenate([b, kk_], axis=0).astype(BF16)),
            0.0),
        u_loc, v, bh, kh)
    return w2, y_loc, m_t, s_loc, g_last


def _rwkv_kernel(p_ref, wc_ref, w0_ref, a0_ref, kk_ref, ka_ref, rk_ref, lnw_ref, lnb_ref,
                 o_ref, s_scr, *, nb, width, n_lora, n_chunks, n_groups, stage_offset):
    L = CHUNK
    R = n_chunks * L
    c = pl.program_id(0)

    @pl.when(c == 0)
    def _():
        s_scr[...] = jnp.zeros_like(s_scr)

    n_pairs = width // PAIR
    ri = lax.broadcasted_iota(jnp.int32, (2 * L, 2 * L), 0)
    ci = lax.broadcasted_iota(jnp.int32, (2 * L, 2 * L), 1)
    same_blk = (ri >= L) == (ci >= L)
    ti = lax.broadcasted_iota(jnp.int32, (L, 2 * L), 0)
    si = lax.broadcasted_iota(jnp.int32, (L, 2 * L), 1) % L
    strict_c = si < ti
    incl_c = si <= ti
    rr = lax.broadcasted_iota(jnp.int32, (R, R), 0)
    rc = lax.broadcasted_iota(jnp.int32, (R, R), 1)
    tri = ((rr // L == rc // L) & (rr >= rc)).astype(BF16)
    lane_lo = lax.broadcasted_iota(jnp.int32, (L, PAIR), 1) < HEAD_DIM
    lane_lo_r = lax.broadcasted_iota(jnp.int32, (R, PAIR), 1) < HEAD_DIM
    lora_lane = lax.broadcasted_iota(jnp.int32, (R, 2 * LANES), 1)

    def head_sums(x):
        s0 = jnp.sum(jnp.where(lane_lo_r, x, 0.0), axis=-1, keepdims=True)
        s1 = jnp.sum(jnp.where(lane_lo_r, 0.0, x), axis=-1, keepdims=True)
        return jnp.where(lane_lo_r, s0, s1)

    def lora_up(pm):
        lora = pm[:, 3 * width:]
        act = jnp.where(lora_lane < n_lora[0], jnp.tanh(lora),
                        jnp.where(lora_lane < n_lora[1], lora,
                                  jnp.where(lora_lane < n_lora[2], _sigmoid(lora), 0.0)))
        return _dot(act.astype(BF16), wc_ref[...])

    def log_decay(up):
        t = w0_ref[...] + up[:, :width]
        return (-math.exp(-0.5) * math.log2(math.e)) / (1.0 + jnp.exp(-t))

    def running_sum(t):
        h1 = t.astype(BF16)
        r1 = t - h1.astype(F32)
        h2 = r1.astype(BF16)
        h3 = (r1 - h2.astype(F32)).astype(BF16)
        return _dot(tri, h1) + _dot(tri, h2) + _dot(tri, h3)

    tiles = [(b, pr) for b in range(nb) for pr in range(n_pairs)]
    n_t = len(tiles)
    lanes = lambda pr, part=0: slice(part * width + pr * PAIR, part * width + (pr + 1) * PAIR)
    sls = [lanes(pr) for _, pr in tiles]

    states = {0: [s_scr[i] for i in range(n_t)]}
    outs = {}

    def run_group(gi):
        g0 = gi * R
        pms = [p_ref[b, g0:g0 + R, :] for b in range(nb)]
        yield
        ups = _each(lora_up, pms)
        logw_b = _each(log_decay, ups)
        cum_b = _each(running_sum, logw_b)
        yield

        r = [pms[b][:, lanes(pr, 0)] for b, pr in tiles]
        k_raw = [pms[b][:, lanes(pr, 1)] for b, pr in tiles]
        v = [pms[b][:, lanes(pr, 2)] for b, pr in tiles]
        logw = [logw_b[b][:, lanes(pr)] for b, pr in tiles]
        cum = [cum_b[b][:, lanes(pr)] for b, pr in tiles]
        a = [_sigmoid(a0_ref[:, lanes(pr)] + ups[b][:, lanes(pr, 1)]) for b, pr in tiles]
        g = [ups[b][:, lanes(pr, 2)] for b, pr in tiles]

        kk = _each(lambda t, sl: t * kk_ref[:, sl], k_raw, sls)
        kk_ss = _each(lambda t: head_sums(t * t), kk)
        kkn = _each(lambda t, ss: t * lax.rsqrt(jnp.maximum(ss, 1e-24)), kk, kk_ss)
        k = _each(lambda t, aa, sl: t * (1.0 + (aa - 1.0) * ka_ref[:, sl]), k_raw, a, sls)
        bb = _each(jnp.multiply, kkn, a)
        bonus_dot = _each(lambda rr_, kk_, sl: head_sums(rr_ * kk_ * rk_ref[:, sl]), r, k, sls)
        yield

        def chunks(ts):
            return [t[ch * L:(ch + 1) * L] for ch in range(n_chunks) for t in ts]

        w2, y_loc, m_t, s_loc, g_last = yield from _rwkv_chunk_terms(
            chunks(r), chunks(k), chunks(v), chunks(logw), chunks(cum), chunks(kkn), chunks(bb),
            same_blk, strict_c, incl_c, lane_lo)
        yield

        s = states[gi]
        y_parts = []
        for ch in range(n_chunks):
            sel = slice(ch * n_t, (ch + 1) * n_t)
            s_b = _each(lambda t: t.astype(BF16), s)
            y_parts.append(_each(lambda a_, sb, yl: _dot_nt(a_, sb) + yl, w2[sel], s_b, y_loc[sel]))
            s = _each(lambda s0, gl, sb, m, sl: s0 * gl + _dot(sb, m) + sl, s, g_last[sel], s_b, m_t[sel], s_loc[sel])
        states[gi + 1] = s
        y = [jnp.concatenate([y_parts[ch][i] for ch in range(n_chunks)], axis=0) for i in range(n_t)]
        yield

        mu = _each(lambda t: head_sums(t) * (1.0 / HEAD_DIM), y)
        yc = _each(jnp.subtract, y, mu)
        var = _each(lambda t: head_sums(t * t) * (1.0 / HEAD_DIM), yc)
        yield
        res = []
        for i in range(n_t):
            sl = sls[i]
            yn = yc[i] * lax.rsqrt(var[i] + GN_EPS) * lnw_ref[:, sl] + lnb_ref[:, sl]
            res.append(((yn + bonus_dot[i] * v[i]) * g[i]).astype(o_ref.dtype))
        outs[gi] = res

    gens = [run_group(gi) for gi in range(n_groups)]
    live = [True] * n_groups
    tick = 0
    while any(live):
        for gi in range(n_groups):
            if live[gi] and tick >= gi * stage_offset:
                try:
                    next(gens[gi])
                except StopIteration:
                    live[gi] = False
        tick += 1

    for gi in range(n_groups):
        for i, (b, pr) in enumerate(tiles):
            o_ref[b, gi * R:(gi + 1) * R, sls[i]] = outs[gi][i]
    for i in range(n_t):
        s_scr[i] = states[n_groups][i]


def _rwkv(p_r, wc, w0, a0, k_k, k_a, r_k, ln_w, ln_b, width, n_lora, n_chunks=2, n_groups=4, stage_offset=5):
    B, S, C = p_r.shape
    L = CHUNK * n_chunks * n_groups
    n_state = B * (width // PAIR)
    vec = lambda n: pl.BlockSpec((1, n), lambda c: (0, 0))
    return pl.pallas_call(
        functools.partial(_rwkv_kernel, nb=B, width=width, n_lora=n_lora, n_chunks=n_chunks, n_groups=n_groups,
                          stage_offset=stage_offset),
        out_shape=jax.ShapeDtypeStruct((B, S, width), BF16),
        grid=(S // L,),
        in_specs=[pl.BlockSpec((B, L, C), lambda c: (0, c, 0)),
                  pl.BlockSpec(wc.shape, lambda c: (0, 0))] + [vec(width)] * 7,
        out_specs=pl.BlockSpec((B, L, width), lambda c: (0, c, 0)),
        scratch_shapes=[pltpu.VMEM((n_state, PAIR, PAIR), F32)],
        compiler_params=pltpu.CompilerParams(dimension_semantics=("arbitrary",), vmem_limit_bytes=VMEM_LIMIT),
        name="rwkv7",
    )(p_r, wc, w0, a0, k_k, k_a, r_k, ln_w, ln_b)


SB_ROWS = 2048
SB_QUARTER = SB_ROWS // 4
N_RES = 4


def _attn_segments(pi, g):
    if pi == 0:
        return [((g // 4) * SB_QUARTER + r * ATTN_BLOCK + (ATTN_BLOCK // N_RES) * (g % 4), ATTN_BLOCK // N_RES, 1)
                for r in range(N_RES)]
    if pi == 1:
        return [((g // 4) * SB_QUARTER + (g % 4) * ATTN_BLOCK, ATTN_BLOCK, 1)]
    return [(qq * SB_QUARTER + (g % 4) * ATTN_BLOCK + g // 4, ATTN_BLOCK // 4, 4) for qq in range(4)]


def _attn_prev_tile(pi, g):
    if pi == 0:
        return (g + 15) % 16, g == 0
    if pi == 1:
        return ((g // 4 + 3) % 4) * 4 + g % 4, g < 4
    return g, True


def _attn_kernel(q_ref, k_ref, v_ref, gain_ref, o_ref,
                 nat_q, nat_k, nat_v, q_scr, k_ring, v_ring, bias_scr, out_nat, *stat_scr, tiles_per_group):
    sb = pl.program_id(2)
    n_pat = len(DILATED_PATTERNS)
    assert DILATED_PATTERNS == ((128, 1), (512, 4), (2048, 16)) and ATTN_BLOCK == 128
    num_scr, m_scr, l_scr = stat_scr[:n_pat], stat_scr[n_pat:2 * n_pat], stat_scr[2 * n_pat:]
    Q = ATTN_BLOCK
    cur_base = (sb % 2) * SB_ROWS
    other_base = SB_ROWS - cur_base

    @pl.when(sb == 0)
    def _():
        k_ring[pl.ds(SB_ROWS, SB_ROWS), :] = jnp.zeros((SB_ROWS, PAIR), F32)
        v_ring[pl.ds(SB_ROWS, SB_ROWS), :] = jnp.zeros((SB_ROWS, PAIR), F32)

    nat_q[...] = q_ref[...].astype(F32)
    nat_k[...] = k_ref[...].astype(F32)
    nat_v[...] = v_ref[...].astype(F32)
    for qq in range(4):
        for r in range(N_RES):
            src = pl.ds(qq * SB_QUARTER + r, Q, stride=N_RES)
            dst = qq * SB_QUARTER + r * Q
            q_scr[dst:dst + Q, :] = nat_q[src, :]
            k_ring[pl.ds(cur_base + dst, Q), :] = nat_k[src, :]
            v_ring[pl.ds(cur_base + dst, Q), :] = nat_v[src, :]

    @pl.when(sb == 0)
    def _():
        ii = lax.broadcasted_iota(jnp.int32, (2 * Q, 2 * Q), 0) % Q
        cj = lax.broadcasted_iota(jnp.int32, (2 * Q, 2 * Q), 1)
        per = Q // N_RES
        for kind in range(2):
            if kind == 0:
                qi = N_RES * (ii % per) + ii // per
                kj = N_RES * ((cj % Q) % per) + (cj % Q) // per + (cj // Q) * Q
            else:
                qi, kj = ii, cj
            band = (kj >= qi) & (kj <= qi + Q)
            bias_scr[kind, 1] = jnp.where(band, 0.0, NEG_BIG)
            bias_scr[kind, 0] = jnp.where(band & (cj >= Q), 0.0, NEG_BIG)

    def load_tile(ref, base, segs):
        parts = [ref[pl.ds(base + s, n, stride=st) if st > 1 else pl.ds(base + s, n), :] for s, n, st in segs]
        return parts[0] if len(parts) == 1 else jnp.concatenate(parts, axis=0)

    def store_tile(ref, segs, val):
        off = 0
        for s, n, st in segs:
            ref[pl.ds(s, n, stride=st) if st > 1 else pl.ds(s, n), :] = val[off:off + n]
            off += n

    lane_lo = lax.broadcasted_iota(jnp.int32, (Q, PAIR), 1) < HEAD_DIM
    ones_blk = jnp.ones((2 * Q, PAIR), BF16)

    n_blk = SB_ROWS // Q
    has_prev_sb = jnp.where(sb > 0, 1, 0)

    def scores(pi, gs):
        segs = [_attn_segments(pi, g) for g in gs]
        prev = [_attn_prev_tile(pi, g) for g in gs]
        prev_segs = [_attn_segments(pi, pg) for pg, _ in prev]
        prev_base = [other_base if other else cur_base for _, other in prev]
        q2 = [load_tile(q_scr, 0, sg) for sg in segs]
        q2 = [jnp.concatenate([jnp.where(lane_lo, t, 0.0), jnp.where(lane_lo, 0.0, t)], axis=0).astype(BF16)
              for t in q2]
        kcat = [jnp.concatenate([load_tile(k_ring, pb, psg), load_tile(k_ring, cur_base, sg)], axis=0).astype(BF16)
                for pb, psg, sg in zip(prev_base, prev_segs, segs)]
        vcat = [jnp.concatenate([load_tile(v_ring, pb, psg), load_tile(v_ring, cur_base, sg)], axis=0).astype(BF16)
                for pb, psg, sg in zip(prev_base, prev_segs, segs)]
        vext = [jnp.concatenate([t, ones_blk], axis=1) for t in vcat]
        has_prev = [has_prev_sb if other else 1 for _, other in prev]
        return dict(pi=pi, segs=segs, s=_each(_dot_nt, q2, kcat), vext=vext, has_prev=has_prev)

    def softmax(c):
        kind = 0 if c["pi"] == 0 else 1
        s = _each(lambda t, hp: t + bias_scr[kind, hp], c["s"], c["has_prev"])
        m = [jnp.max(t, axis=-1, keepdims=True) for t in s]
        p = _each(lambda t, mm: jnp.exp(t - mm).astype(BF16), s, m)
        return dict(pi=c["pi"], segs=c["segs"], vext=c["vext"], m=m, p=p)

    def weighted_values(c):
        pi, segs, m = c["pi"], c["segs"], c["m"]
        nl = _each(_dot, c["p"], c["vext"])
        for t in range(len(segs)):
            store_tile(num_scr[pi], segs[t], jnp.where(lane_lo, nl[t][:Q, :PAIR], nl[t][Q:, :PAIR]))
            store_tile(l_scr[pi], segs[t], jnp.where(lane_lo, nl[t][:Q, PAIR:], nl[t][Q:, PAIR:]))
            store_tile(m_scr[pi], segs[t], jnp.where(lane_lo, m[t][:Q], m[t][Q:]))

    groups = [(pi, list(range(g0, g0 + tiles_per_group)))
              for pi in range(n_pat) for g0 in range(0, n_blk, tiles_per_group)]
    n_grp = len(groups)
    sc = {0: scores(*groups[0])}
    if n_grp > 1:
        sc[1] = scores(*groups[1])
    sm = {0: softmax(sc.pop(0))}
    for k in range(n_grp):
        if k + 2 < n_grp:
            sc[k + 2] = scores(*groups[k + 2])
        if k + 1 < n_grp:
            sm[k + 1] = softmax(sc.pop(k + 1))
        weighted_values(sm.pop(k))

    ri = lax.broadcasted_iota(jnp.int32, (PAIR, PAIR), 0)
    ci = lax.broadcasted_iota(jnp.int32, (PAIR, PAIR), 1)
    seg_ones = ((ri >= HEAD_DIM) == (ci >= HEAD_DIM)).astype(BF16)
    gain = gain_ref[...]

    def merge(i, carry):
        rows = pl.ds(pl.multiple_of(i * Q, Q), Q)
        ms = [m_scr[pi][rows, :] for pi in range(n_pat)]
        m_all = functools.reduce(jnp.maximum, ms)
        num = 0.0
        den = 0.0
        for pi in range(n_pat):
            wgt = jnp.exp(ms[pi] - m_all)
            num = num + wgt * num_scr[pi][rows, :]
            den = den + wgt * l_scr[pi][rows, :]
        o = num / den
        ms_o = _split_dot(o * o, seg_ones) * (1.0 / HEAD_DIM)
        out_nat[pl.ds((i // N_RES) * SB_QUARTER + i % N_RES, Q, stride=N_RES), :] = o * lax.rsqrt(ms_o + NORM_EPS) * gain
        return carry

    lax.fori_loop(0, SB_ROWS // Q, merge, 0, unroll=4)
    o_ref[...] = out_nat[...].astype(o_ref.dtype)


def _attention(q, k, v, gain, tiles_per_group=4):
    B, S, W = q.shape
    n_pairs = W // PAIR
    n_pat = len(DILATED_PATTERNS)
    blk = pl.BlockSpec((None, SB_ROWS, PAIR), lambda b, p, s: (b, s, p))
    tile = pltpu.VMEM((SB_ROWS, PAIR), F32)
    return pl.pallas_call(
        functools.partial(_attn_kernel, tiles_per_group=tiles_per_group),
        out_shape=jax.ShapeDtypeStruct((B, S, W), BF16),
        grid=(B, n_pairs, S // SB_ROWS),
        in_specs=[blk, blk, blk, pl.BlockSpec((1, PAIR), lambda b, p, s: (0, p))],
        out_specs=blk,
        scratch_shapes=[tile, tile, tile, tile,
                        pltpu.VMEM((2 * SB_ROWS, PAIR), F32),
                        pltpu.VMEM((2 * SB_ROWS, PAIR), F32),
                        pltpu.VMEM((2, 2, 2 * ATTN_BLOCK, 2 * ATTN_BLOCK), F32),
                        tile]
                       + [tile] * (3 * n_pat),
        compiler_params=pltpu.CompilerParams(dimension_semantics=("parallel", "parallel", "arbitrary"),
                                             vmem_limit_bytes=VMEM_LIMIT),
        name="dilated_attn",
    )(q, k, v, gain)


def _rotary_tables(seq):
    half = ROT_DIM // 2
    inv_freq = ROPE_THETA ** (-jnp.arange(half, dtype=F32) * 2.0 / ROT_DIM)
    ang = jnp.arange(seq).astype(F32)[:, None] * inv_freq[None, :]
    cos, sin = jnp.cos(ang), jnp.sin(ang)
    rest = HEAD_DIM - ROT_DIM
    cos_h = jnp.concatenate([cos, cos, jnp.ones((seq, rest), F32)], axis=-1)
    sin_h = jnp.concatenate([-sin, sin, jnp.zeros((seq, rest), F32)], axis=-1)
    return jnp.tile(cos_h, (1, PAIR // HEAD_DIM)), jnp.tile(sin_h, (1, PAIR // HEAD_DIM))


FFN_HALO = 16
MIX_PIECES = 4


def _mix_ffn_kernel(x_ref, ya_ref, yb_ref, xh_ref, yah_ref, ybh_ref, wo_ref, g_ref, wu_ref, cw_ref,
                    cb_ref, wd_ref, fg_ref, o_ref, h_scr, hh_scr, *, tm, seq, row_chunks, apply_final):
    tf = wd_ref.shape[0]
    i = pl.program_id(0)
    j = pl.program_id(1)
    wa = ya_ref.shape[1]

    def mixed(x, ya, yb):
        return x + _dot(ya, wo_ref[:wa, :]) + _dot(yb, wo_ref[wa:, :])

    @pl.when(j == 0)
    def _():
        hh_scr[...] = _rmsnorm(mixed(xh_ref[...], yah_ref[...], ybh_ref[...]), g_ref[...]).astype(BF16)
        piece = tm // MIX_PIECES
        for c in range(MIX_PIECES):
            rows = slice(c * piece, (c + 1) * piece)
            x1 = mixed(x_ref[rows, :], ya_ref[rows, :], yb_ref[rows, :])
            o_ref[rows, :] = x1
            h_scr[rows, :] = _rmsnorm(x1, g_ref[...]).astype(BF16)

    seq_start = (i * tm) % seq == 0
    rc = tm // row_chunks
    row = lax.broadcasted_iota(jnp.int32, (rc, 1), 0)
    for c in range(row_chunks):
        rows = slice(c * rc, (c + 1) * rc)
        gv = _dot(h_scr[rows, :], wu_ref[...])
        gate, val = gv[:, :tf], gv[:, tf:]
        if c == 0:
            gate_h = jnp.where(seq_start, 0.0, _dot(hh_scr[...], wu_ref[:, :tf]))
        else:
            gate_h = _dot(h_scr[c * rc - FFN_HALO:c * rc, :], wu_ref[:, :tf])
        g1 = jnp.where(row == 0, gate_h[FFN_HALO - 1:FFN_HALO, :], pltpu.roll(gate, 1, 0))
        g2 = jnp.where(row == 0, gate_h[FFN_HALO - 2:FFN_HALO - 1, :],
                       jnp.where(row == 1, gate_h[FFN_HALO - 1:FFN_HALO, :], pltpu.roll(gate, 2, 0)))
        u = cw_ref[0:1, :] * g2 + cw_ref[1:2, :] * g1 + cw_ref[2:3, :] * gate + cb_ref[...]
        act = (u * _sigmoid(u) * val).astype(BF16)
        o_ref[rows, :] += _dot(act, wd_ref[...])

    if apply_final:
        @pl.when(j == pl.num_programs(1) - 1)
        def _():
            o_ref[...] = _rmsnorm(o_ref[...], fg_ref[...])


def _mix_ffn(x2d, y_a, y_b, w_o, gain, w_up, conv_w, conv_b, w_down, final_gain, seq, apply_final,
             tm=1024, n_ff_tiles=2, row_chunks=1):
    T, D = x2d.shape
    F = w_down.shape[0]
    tf = F // n_ff_tiles
    assert tf % LANES == 0 and seq % tm == 0 and w_up.shape[1] == 2 * F
    w_up = jnp.concatenate([w_up[:, half * F + j * tf: half * F + (j + 1) * tf]
                            for j in range(n_ff_tiles) for half in range(2)], axis=1)
    halo_blocks = tm // FFN_HALO
    row = lambda i, j: (i, 0)
    halo = lambda i, j: (jnp.maximum(i * halo_blocks - 1, 0), 0)
    fixed = lambda i, j: (0, 0)
    wa, wb = y_a.shape[1], y_b.shape[1]
    return pl.pallas_call(
        functools.partial(_mix_ffn_kernel, tm=tm, seq=seq, row_chunks=row_chunks, apply_final=apply_final),
        out_shape=jax.ShapeDtypeStruct((T, D), F32),
        grid=(T // tm, n_ff_tiles),
        in_specs=[pl.BlockSpec((tm, D), row), pl.BlockSpec((tm, wa), row), pl.BlockSpec((tm, wb), row),
                  pl.BlockSpec((FFN_HALO, D), halo), pl.BlockSpec((FFN_HALO, wa), halo),
                  pl.BlockSpec((FFN_HALO, wb), halo),
                  pl.BlockSpec(w_o.shape, fixed),
                  pl.BlockSpec((1, D), fixed),
                  pl.BlockSpec((D, 2 * tf), lambda i, j: (0, j)),
                  pl.BlockSpec((CONV_WIDTH, tf), lambda i, j: (0, j)),
                  pl.BlockSpec((1, tf), lambda i, j: (0, j)),
                  pl.BlockSpec((tf, D), lambda i, j: (j, 0)),
                  pl.BlockSpec((1, D), fixed)],
        out_specs=pl.BlockSpec((tm, D), row),
        scratch_shapes=[pltpu.VMEM((tm, D), BF16), pltpu.VMEM((FFN_HALO, D), BF16)],
        compiler_params=pltpu.CompilerParams(dimension_semantics=("parallel", "arbitrary"),
                                             vmem_limit_bytes=VMEM_LIMIT),
        name="mix_convglu_ffn",
    )(x2d, y_a, y_b, x2d, y_a, y_b, w_o, gain, w_up, conv_w, conv_b, w_down, final_gain)


def kernel(x, mix_norm_gain, w_in, rwkv_shift_mix, w0, w_lora_up, a0, a_lora_up, g_lora_up, k_k, k_a, r_k,
           ln_x_w, ln_x_b, attn_norm_gain, w_out, ffn_norm_gain, w_ffn_up, ffn_conv_w, ffn_conv_b,
           w_ffn_down, final_norm_gain):
    B, S, D = x.shape
    depth = w_in.shape[0]
    rw = w0.shape[1]
    aw = attn_norm_gain.shape[1]
    n_w, n_a, n_g = w_lora_up.shape[1], a_lora_up.shape[1], g_lora_up.shape[1]
    n_lora = n_w + n_a + n_g
    lora_pad = -(-n_lora // (2 * LANES)) * (2 * LANES)
    assert lora_pad == 2 * LANES and rw % PAIR == 0 and aw % PAIR == 0
    rwkv_cols = 3 * rw + n_lora
    cos_t, sin_t = _rotary_tables(S)

    x2d = x.reshape(B * S, D)
    for l in range(depth):
        w_r = w_in[l][:, :3 * rw + lora_pad].astype(BF16)
        w_a = w_in[l][:, rwkv_cols:].astype(BF16)
        mix =jnp.concatenate([rwkv_shift_mix[l], jnp.zeros((lora_pad - n_lora,), F32)])[None, :]
        wc = jnp.zeros((lora_pad, 3 * rw), F32)
        wc = wc.at[:n_w, :rw].set(w_lora_up[l])
        wc = wc.at[n_w:n_w + n_a, rw:2 * rw].set(a_lora_up[l])
        wc = wc.at[n_w + n_a:n_lora, 2 * rw:].set(g_lora_up[l]).astype(BF16)

        p_r, q, k, v = _inproj(x2d, mix_norm_gain[l][None, :], w_r, w_a, mix, cos_t, sin_t)
        y_rwkv = _rwkv(p_r.reshape(B, S, -1), wc, w0[l][None, :], a0[l][None, :], k_k[l][None, :],
                       k_a[l][None, :], r_k[l].reshape(1, rw), ln_x_w[l][None, :], ln_x_b[l][None, :],
                       rw, (n_w, n_w + n_a, n_lora))
        y_attn = _attention(q.reshape(B, S, aw), k.reshape(B, S, aw), v.reshape(B, S, aw),
                            attn_norm_gain[l][None, :])
        x2d = _mix_ffn(x2d, y_rwkv.reshape(B * S, rw), y_attn.reshape(B * S, aw), w_out[l].astype(BF16),
                       ffn_norm_gain[l][None, :], w_ffn_up[l].astype(BF16), ffn_conv_w[l],
                       ffn_conv_b[l][None, :], w_ffn_down[l].astype(BF16), final_norm_gain[None, :], S,
                       apply_final=(l == depth - 1))
    return x2d.reshape(B, S, D)
```

```python
import functools
import math

import jax
import jax.numpy as jnp
from jax import lax
from jax.experimental import pallas as pl
from jax.experimental.pallas import tpu as pltpu

F32 = jnp.float32
BF16 = jnp.bfloat16

LANES = 128
HEAD_DIM = 64
PAIR = 2 * HEAD_DIM
ROT_DIM = HEAD_DIM // 4
ROPE_THETA = 500000.0
NORM_EPS = 1e-6
GN_EPS = 64e-5
DILATED_PATTERNS = ((128, 1), (512, 4), (2048, 16))
ATTN_BLOCK = 128
CONV_WIDTH = 3
CHUNK = 64
SOLVE_BLOCK = 16
NEG_BIG = -1e30
VMEM_LIMIT = 56 * 1024 * 1024


def _dot(a, b):
    return jnp.dot(a, b, preferred_element_type=F32)


def _dot_nt(a, b):
    return lax.dot_general(a, b, (((1,), (1,)), ((), ())), preferred_element_type=F32)


def _dot_tn(a, b):
    return lax.dot_general(a, b, (((0,), (0,)), ((), ())), preferred_element_type=F32)


def _rmsnorm(x, gain):
    return x * lax.rsqrt(jnp.mean(x * x, axis=-1, keepdims=True) + NORM_EPS) * gain


def _sigmoid(x):
    return 1.0 / (1.0 + jnp.exp(-x))


def _split_dot(x, w):
    hi = x.astype(BF16)
    lo = (x - hi.astype(F32)).astype(BF16)
    return _dot(hi, w) + _dot(lo, w)


INPROJ_PIECES = 2


def _inproj_kernel(x_ref, g_ref, wr_ref, wa_ref, mix_ref, cos_ref, sin_ref, pr_ref, q_ref, k_ref, v_ref, carry_scr,
                   *, seq_tiles):
    tm = x_ref.shape[0]
    aw = q_ref.shape[1]
    rp = tm // INPROJ_PIECES

    @pl.when(pl.program_id(0) == 0)
    def _():
        carry_scr[...] = jnp.zeros_like(carry_scr)

    row = lax.broadcasted_iota(jnp.int32, (rp, 1), 0)
    lane = lax.broadcasted_iota(jnp.int32, (rp, PAIR), 1)
    first_half = (lane % HEAD_DIM) < (ROT_DIM // 2)
    prev_last = jnp.where(pl.program_id(0) % seq_tiles == 0, 0.0, carry_scr[7:8, :])

    for c in range(INPROJ_PIECES):
        rows = slice(c * rp, (c + 1) * rp)
        h = _rmsnorm(x_ref[rows, :], g_ref[...]).astype(BF16)
        pr = _dot(h, wr_ref[...])
        p = _dot(h, wa_ref[...])
        pr_prev = jnp.where(row == 0, prev_last, pltpu.roll(pr, 1, 0))
        prev_last = pr[rp - 1:rp, :]
        if c == INPROJ_PIECES - 1:
            carry_scr[...] = pr[rp - 8:rp, :]
        pr_ref[rows, :] = pr + (pr_prev - pr) * mix_ref[...]

        cos, sin = cos_ref[rows, :], sin_ref[rows, :]

        def rotary(x):
            partner = jnp.where(first_half, pltpu.roll(x, PAIR - ROT_DIM // 2, 1), pltpu.roll(x, ROT_DIM // 2, 1))
            return x * cos + partner * sin

        for t in range(aw // PAIR):
            lo = t * PAIR
            q_ref[rows, lo:lo + PAIR] = (rotary(p[:, lo:lo + PAIR]) * (1.0 / math.sqrt(HEAD_DIM))).astype(BF16)
            k_ref[rows, lo:lo + PAIR] = rotary(p[:, aw + lo:aw + lo + PAIR]).astype(BF16)
        v_ref[rows, :] = p[:, 2 * aw:].astype(BF16)


def _inproj(x2d, gain, w_r, w_a, mix, cos_t, sin_t, tm=1024):
    T, D = x2d.shape
    rw, aw = w_r.shape[1], w_a.shape[1] // 3
    seq_tiles = cos_t.shape[0] // tm
    row = lambda i: (i, 0)
    fixed = lambda i: (0, 0)
    tab = pl.BlockSpec((tm, PAIR), lambda i: (i % seq_tiles, 0))
    return pl.pallas_call(
        functools.partial(_inproj_kernel, seq_tiles=seq_tiles),
        out_shape=(jax.ShapeDtypeStruct((T, rw), F32),) + (jax.ShapeDtypeStruct((T, aw), BF16),) * 3,
        grid=(T // tm,),
        in_specs=[pl.BlockSpec((tm, D), row), pl.BlockSpec((1, D), fixed), pl.BlockSpec(w_r.shape, fixed),
                  pl.BlockSpec(w_a.shape, fixed), pl.BlockSpec((1, rw), fixed), tab, tab],
        out_specs=(pl.BlockSpec((tm, rw), row),) + (pl.BlockSpec((tm, aw), row),) * 3,
        scratch_shapes=[pltpu.VMEM((8, rw), F32)],
        compiler_params=pltpu.CompilerParams(dimension_semantics=("arbitrary",), vmem_limit_bytes=VMEM_LIMIT),
        name="inproj",
    )(x2d, gain, w_r, w_a, mix, cos_t, sin_t)


def _each(fn, *lists):
    return [fn(*args) for args in zip(*lists)]


def _rwkv_chunk_terms(r, k, v, logw, cum, kkn, bb, same_blk, strict_c, incl_c, lane_lo):
    L = CHUNK
    bf = lambda t: t.astype(BF16)
    cum_last = [t[L - 1:L, :] for t in cum]
    g_in = _each(jnp.exp2, cum)
    g_ex = _each(lambda t, w: jnp.exp2(t - w), cum, logw)
    g_inv = _each(lambda t: jnp.exp2(-t), cum)
    g_hat = _each(lambda tl, t: jnp.exp2(tl - t), cum_last, cum)
    g_last = _each(jnp.exp2, cum_last)
    yield

    def by_head_rows(x):
        lo = lane_lo if x.shape[1] == PAIR else jnp.concatenate([lane_lo] * (x.shape[1] // PAIR), axis=1)
        return jnp.concatenate([jnp.where(lo, x, 0.0), jnp.where(lo, 0.0, x)], axis=0).astype(BF16)

    al = _each(lambda t, g: -t * g, kkn, g_ex)
    rb = _each(jnp.multiply, r, g_in)
    bt = _each(jnp.multiply, bb, g_inv)
    kt = _each(jnp.multiply, k, g_inv)
    bh = _each(jnp.multiply, bb, g_hat)
    kh = _each(jnp.multiply, k, g_hat)
    lhs = _each(lambda a, b: jnp.concatenate([a, b], axis=0).astype(BF16), al, rb)
    rhs = _each(lambda a, b: jnp.concatenate([by_head_rows(a), by_head_rows(b)], axis=0), bt, kt)
    yield
    aq = _each(_dot_nt, lhs, rhs)
    a_ab = [jnp.where(strict_c, t[:L, :2 * L], 0.0) for t in aq]
    a_ak = [jnp.where(strict_c, t[:L, 2 * L:], 0.0).astype(BF16) for t in aq]
    a_rb = [jnp.where(incl_c, t[L:, :2 * L], 0.0).astype(BF16) for t in aq]
    a_rk = [jnp.where(incl_c, t[L:, 2 * L:], 0.0).astype(BF16) for t in aq]
    yield

    v_s = _each(by_head_rows, v)
    av = _each(_dot, a_ak, v_s)
    BS = SOLVE_BLOCK
    n_blk = L // BS
    rhs0 = _each(lambda a, b: jnp.concatenate([a, b], axis=1), al, av)
    lo2 = lax.broadcasted_iota(jnp.int32, (BS, 2 * PAIR), 1) % PAIR < HEAD_DIM
    lane = lax.broadcasted_iota(jnp.int32, (BS, PAIR), 1)
    br =lax.broadcasted_iota(jnp.int32, (PAIR, PAIR), 0)
    bc = lax.broadcasted_iota(jnp.int32, (PAIR, PAIR), 1)
    blk_diag = (br // BS) == (bc // BS)

    def expand(pack):
        return jnp.where(blk_diag, jnp.concatenate([pack] * (PAIR // BS), axis=0), 0.0).astype(BF16)

    d_k = [functools.reduce(jnp.add, [jnp.where((lane % HEAD_DIM) // BS == b, t[b * BS:(b + 1) * BS, :], 0.0)
                                      for b in range(n_blk)]) for t in a_ab]
    t_m = d_k
    yield
    n_sq = int(math.log2(BS)) - 1
    d_k = _each(lambda d: _dot(d.astype(BF16), expand(d)), d_k)
    yield
    for lvl in range(n_sq):
        d_e = _each(expand, d_k)
        t_m = _each(lambda t, d, de: t + d + _dot(t.astype(BF16), de), t_m, d_k, d_e)
        if lvl + 1 < n_sq:
            d_k = _each(lambda d, de: _dot(d.astype(BF16), de), d_k, d_e)
        yield
    t_mb = _each(bf, t_m)

    zero_blk = jnp.zeros((BS, 2 * PAIR), BF16)
    x_blocks = [[] for _ in a_ab]
    xs_lo = [[] for _ in a_ab]
    xs_hi = [[] for _ in a_ab]

    def stacked(lo_parts, hi_parts):
        rows = [lo_parts.get(c, zero_blk) for c in range(n_blk)] + [hi_parts.get(c, zero_blk) for c in range(n_blk)]
        return jnp.concatenate(rows, axis=0)

    for b in range(n_blk):
        r_b = [t[b * BS:(b + 1) * BS, :] for t in rhs0]
        if b > 0:
            r_b = _each(lambda rr, a, lo_p, hi_p: rr + _dot(a[b * BS:(b + 1) * BS, :].astype(BF16),
                                                            stacked(dict(enumerate(lo_p)), dict(enumerate(hi_p)))),
                        r_b, a_ab, xs_lo, xs_hi)
            yield
        x_b = _each(lambda rr, tm: rr + _dot(tm, stacked({b: jnp.where(lo2, rr, 0.0).astype(BF16)},
                                                        {b: jnp.where(lo2, 0.0, rr).astype(BF16)})),
                    r_b, t_mb)
        for i, xb in enumerate(x_b):
            x_blocks[i].append(xb)
            xs_lo[i].append(jnp.where(lo2, xb, 0.0).astype(BF16))
            xs_hi[i].append(jnp.where(lo2, 0.0, xb).astype(BF16))
        yield
    x = [jnp.concatenate(blks, axis=0) for blks in x_blocks]
    x_s = _each(lambda lo_p, hi_p: jnp.concatenate(lo_p + hi_p, axis=0), xs_lo, xs_hi)

    z = _each(_dot, a_rb, x_s)
    akv = _each(_dot, a_rk, v_s)
    w2 = _each(lambda a, t: (a + t[:, :PAIR]).astype(BF16), rb, z)
    y_loc = _each(lambda t, a: t[:, PAIR:] + a, z, akv)
    w1 = [t[:, :PAIR].astype(BF16) for t in x]
    u_loc = [t[:, PAIR:] for t in x]
    yield

    m_t = _each(lambda a, b: jnp.where(same_blk, _dot_tn(a, b.astype(BF16)), 0.0).astype(BF16), w1, bh)
    s_loc = _each(
        lambda u, vv, b, kk_: jnp.where(
            same_blk,
            _dot_tn(jnp.concatenate([u, vv], axis=0).astype(BF16), jnp.concatenate([b, kk_], axis=0).astype(BF16)),
            0.0),
        u_loc, v, bh, kh)
    return w2, y_loc, m_t, s_loc, g_last


def _rwkv_kernel(p_ref, wc_ref, w0_ref, a0_ref, kk_ref, ka_ref, rk_ref, lnw_ref, lnb_ref,
                 o_ref, s_scr, *, nb, width, n_lora, n_chunks, n_groups, stage_offset):
    L = CHUNK
    R = n_chunks * L
    c = pl.program_id(0)

    @pl.when(c == 0)
    def _():
        s_scr[...] = jnp.zeros_like(s_scr)

    n_pairs = width // PAIR
    ri = lax.broadcasted_iota(jnp.int32, (2 * L, 2 * L), 0)
    ci = lax.broadcasted_iota(jnp.int32, (2 * L, 2 * L), 1)
    same_blk = (ri >= L) == (ci >= L)
    ti = lax.broadcasted_iota(jnp.int32, (L, 2 * L), 0)
    si = lax.broadcasted_iota(jnp.int32, (L, 2 * L), 1) % L
    strict_c = si < ti
    incl_c = si <= ti
    rr = lax.broadcasted_iota(jnp.int32, (R, R), 0)
    rc = lax.broadcasted_iota(jnp.int32, (R, R), 1)
    tri = ((rr // L == rc // L) & (rr >= rc)).astype(BF16)
    lane_lo = lax.broadcasted_iota(jnp.int32, (L, PAIR), 1) < HEAD_DIM
    lane_lo_r = lax.broadcasted_iota(jnp.int32, (R, PAIR), 1) < HEAD_DIM
    lora_lane = lax.broadcasted_iota(jnp.int32, (R, 2 * LANES), 1)

    def head_sums(x):
        s0 = jnp.sum(jnp.where(lane_lo_r, x, 0.0), axis=-1, keepdims=True)
        s1 = jnp.sum(jnp.where(lane_lo_r, 0.0, x), axis=-1, keepdims=True)
        return jnp.where(lane_lo_r, s0, s1)

    def lora_up(pm):
        lora = pm[:, 3 * width:]
        act = jnp.where(lora_lane < n_lora[0], jnp.tanh(lora),
                        jnp.where(lora_lane < n_lora[1], lora,
                                  jnp.where(lora_lane < n_lora[2], _sigmoid(lora), 0.0)))
        return _dot(act.astype(BF16), wc_ref[...])

    def log_decay(up):
        t = w0_ref[...] + up[:, :width]
        return (-math.exp(-0.5) * math.log2(math.e)) / (1.0 + jnp.exp(-t))

    def running_sum(t):
        h1 = t.astype(BF16)
        r1 = t - h1.astype(F32)
        h2 = r1.astype(BF16)
        h3 = (r1 - h2.astype(F32)).astype(BF16)
        return _dot(tri, h1) + _dot(tri, h2) + _dot(tri, h3)

    tiles = [(b, pr) for b in range(nb) for pr in range(n_pairs)]
    n_t = len(tiles)
    lanes = lambda pr, part=0: slice(part * width + pr * PAIR, part * width + (pr + 1) * PAIR)
    sls = [lanes(pr) for _, pr in tiles]

    states = {0: [s_scr[i] for i in range(n_t)]}

    def run_group(gi):
        g0 = gi * R
        pms = [p_ref[b, g0:g0 + R, :] for b in range(nb)]
        yield
        ups = _each(lora_up, pms)
        logw_b = _each(log_decay, ups)
        cum_b = _each(running_sum, logw_b)
        yield

        r = [pms[b][:, lanes(pr, 0)] for b, pr in tiles]
        k_raw = [pms[b][:, lanes(pr, 1)] for b, pr in tiles]
        v = [pms[b][:, lanes(pr, 2)] for b, pr in tiles]
        logw = [logw_b[b][:, lanes(pr)] for b, pr in tiles]
        cum = [cum_b[b][:, lanes(pr)] for b, pr in tiles]
        a = [_sigmoid(a0_ref[:, lanes(pr)] + ups[b][:, lanes(pr, 1)]) for b, pr in tiles]
        g = [ups[b][:, lanes(pr, 2)] for b, pr in tiles]

        kk = _each(lambda t, sl: t * kk_ref[:, sl], k_raw, sls)
        kk_ss = _each(lambda t: head_sums(t * t), kk)
        kkn = _each(lambda t, ss: t * lax.rsqrt(jnp.maximum(ss, 1e-24)), kk, kk_ss)
        k = _each(lambda t, aa, sl: t * (1.0 + (aa - 1.0) * ka_ref[:, sl]), k_raw, a, sls)
        bb = _each(jnp.multiply, kkn, a)
        bonus_dot = _each(lambda rr_, kk_, sl: head_sums(rr_ * kk_ * rk_ref[:, sl]), r, k, sls)
        yield

        def chunks(ts):
            return [t[ch * L:(ch + 1) * L] for ch in range(n_chunks) for t in ts]

        w2, y_loc, m_t, s_loc, g_last = yield from _rwkv_chunk_terms(
            chunks(r), chunks(k), chunks(v), chunks(logw), chunks(cum), chunks(kkn), chunks(bb),
            same_blk, strict_c, incl_c, lane_lo)
        yield

        s = states[gi]
        y_parts = []
        for ch in range(n_chunks):
            sel = slice(ch * n_t, (ch + 1) * n_t)
            s_b = _each(lambda t: t.astype(BF16), s)
            y_parts.append(_each(lambda a_, sb, yl: _dot_nt(a_, sb) + yl, w2[sel], s_b, y_loc[sel]))
            s = _each(lambda s0, gl, sb, m, sl: s0 * gl + _dot(sb, m) + sl, s, g_last[sel], s_b, m_t[sel], s_loc[sel])
        states[gi + 1] = s
        y = [jnp.concatenate([y_parts[ch][i] for ch in range(n_chunks)], axis=0) for i in range(n_t)]
        yield

        mu = _each(lambda t: head_sums(t) * (1.0 / HEAD_DIM), y)
        yc = _each(jnp.subtract, y, mu)
        var = _each(lambda t: head_sums(t * t) * (1.0 / HEAD_DIM), yc)
        yield
        for i, (b, pr) in enumerate(tiles):
            sl = sls[i]
            yn = yc[i] * lax.rsqrt(var[i] + GN_EPS) * lnw_ref[:, sl] + lnb_ref[:, sl]
            o_ref[b, g0:g0 + R, sl] = ((yn + bonus_dot[i] * v[i]) * g[i]).astype(o_ref.dtype)

    gens = [run_group(gi) for gi in range(n_groups)]
    live = [True] * n_groups
    tick = 0
    while any(live):
        for gi in range(n_groups):
            if live[gi] and tick >= gi * stage_offset:
                try:
                    next(gens[gi])
                except StopIteration:
                    live[gi] = False
        tick += 1

    for i in range(n_t):
        s_scr[i] = states[n_groups][i]


def _rwkv(p_r, wc, w0, a0, k_k, k_a, r_k, ln_w, ln_b, width, n_lora, n_chunks=2, n_groups=4, stage_offset=5):
    B, S, C = p_r.shape
    L = CHUNK * n_chunks * n_groups
    n_state = B * (width // PAIR)
    vec = lambda n: pl.BlockSpec((1, n), lambda c: (0, 0))
    return pl.pallas_call(
        functools.partial(_rwkv_kernel, nb=B, width=width, n_lora=n_lora, n_chunks=n_chunks, n_groups=n_groups,
                          stage_offset=stage_offset),
        out_shape=jax.ShapeDtypeStruct((B, S, width), BF16),
        grid=(S // L,),
        in_specs=[pl.BlockSpec((B, L, C), lambda c: (0, c, 0)),
                  pl.BlockSpec(wc.shape, lambda c: (0, 0))] + [vec(width)] * 7,
        out_specs=pl.BlockSpec((B, L, width), lambda c: (0, c, 0)),
        scratch_shapes=[pltpu.VMEM((n_state, PAIR, PAIR), F32)],
        compiler_params=pltpu.CompilerParams(dimension_semantics=("arbitrary",), vmem_limit_bytes=VMEM_LIMIT),
        name="rwkv7",
    )(p_r, wc, w0, a0, k_k, k_a, r_k, ln_w, ln_b)


SB_ROWS = 2048
SB_QUARTER = SB_ROWS // 4
N_RES = 4


def _attn_segments(pi, g):
    if pi == 0:
        return [((g // 4) * SB_QUARTER + r * ATTN_BLOCK + (ATTN_BLOCK // N_RES) * (g % 4), ATTN_BLOCK // N_RES, 1)
                for r in range(N_RES)]
    if pi == 1:
        return [((g // 4) * SB_QUARTER + (g % 4) * ATTN_BLOCK, ATTN_BLOCK, 1)]
    return [(qq * SB_QUARTER + (g % 4) * ATTN_BLOCK + g // 4, ATTN_BLOCK // 4, 4) for qq in range(4)]


def _attn_prev_tile(pi, g):
    if pi == 0:
        return (g + 15) % 16, g == 0
    if pi == 1:
        return ((g // 4 + 3) % 4) * 4 + g % 4, g < 4
    return g, True


def _attn_kernel(q_ref, k_ref, v_ref, gain_ref, o_ref,
                 nat_q, nat_k, nat_v, q_scr, k_ring, v_ring, bias_scr, out_nat, *stat_scr, tiles_per_group):
    sb = pl.program_id(2)
    n_pat = len(DILATED_PATTERNS)
    assert DILATED_PATTERNS == ((128, 1), (512, 4), (2048, 16)) and ATTN_BLOCK == 128
    num_scr, m_scr, l_scr = stat_scr[:n_pat], stat_scr[n_pat:2 * n_pat], stat_scr[2 * n_pat:]
    Q = ATTN_BLOCK
    cur_base = (sb % 2) * SB_ROWS
    other_base = SB_ROWS - cur_base

    @pl.when(sb == 0)
    def _():
        k_ring[pl.ds(SB_ROWS, SB_ROWS), :] = jnp.zeros((SB_ROWS, PAIR), F32)
        v_ring[pl.ds(SB_ROWS, SB_ROWS), :] = jnp.zeros((SB_ROWS, PAIR), F32)

    nat_q[...] = q_ref[...].astype(F32)
    nat_k[...] = k_ref[...].astype(F32)
    nat_v[...] = v_ref[...].astype(F32)
    for qq in range(4):
        for r in range(N_RES):
            src = pl.ds(qq * SB_QUARTER + r, Q, stride=N_RES)
            dst = qq * SB_QUARTER + r * Q
            q_scr[dst:dst + Q, :] = nat_q[src, :]
            k_ring[pl.ds(cur_base + dst, Q), :] = nat_k[src, :]
            v_ring[pl.ds(cur_base + dst, Q), :] = nat_v[src, :]

    @pl.when(sb == 0)
    def _():
        ii = lax.broadcasted_iota(jnp.int32, (2 * Q, 2 * Q), 0) % Q
        cj = lax.broadcasted_iota(jnp.int32, (2 * Q, 2 * Q), 1)
        per = Q // N_RES
        for kind in range(2):
            if kind == 0:
                qi = N_RES * (ii % per) + ii // per
                kj = N_RES * ((cj % Q) % per) + (cj % Q) // per + (cj // Q) * Q
            else:
                qi, kj = ii, cj
            band = (kj >= qi) & (kj <= qi + Q)
            bias_scr[kind, 1] = jnp.where(band, 0.0, NEG_BIG)
            bias_scr[kind, 0] = jnp.where(band & (cj >= Q), 0.0, NEG_BIG)

    def load_tile(ref, base, segs):
        parts = [ref[pl.ds(base + s, n, stride=st) if st > 1 else pl.ds(base + s, n), :] for s, n, st in segs]
        return parts[0] if len(parts) == 1 else jnp.concatenate(parts, axis=0)

    def store_tile(ref, segs, val):
        off = 0
        for s, n, st in segs:
            ref[pl.ds(s, n, stride=st) if st > 1 else pl.ds(s, n), :] = val[off:off + n]
            off += n

    lane_lo = lax.broadcasted_iota(jnp.int32, (Q, PAIR), 1) < HEAD_DIM
    ones_blk = jnp.ones((2 * Q, PAIR), BF16)

    n_blk = SB_ROWS // Q
    has_prev_sb = jnp.where(sb > 0, 1, 0)

    def scores(pi, gs):
        segs = [_attn_segments(pi, g) for g in gs]
        prev = [_attn_prev_tile(pi, g) for g in gs]
        prev_segs = [_attn_segments(pi, pg) for pg, _ in prev]
        prev_base = [other_base if other else cur_base for _, other in prev]
        q2 = [load_tile(q_scr, 0, sg) for sg in segs]
        q2 = [jnp.concatenate([jnp.where(lane_lo, t, 0.0), jnp.where(lane_lo, 0.0, t)], axis=0).astype(BF16)
              for t in q2]
        kcat = [jnp.concatenate([load_tile(k_ring, pb, psg), load_tile(k_ring, cur_base, sg)], axis=0).astype(BF16)
                for pb, psg, sg in zip(prev_base, prev_segs, segs)]
        vcat = [jnp.concatenate([load_tile(v_ring, pb, psg), load_tile(v_ring, cur_base, sg)], axis=0).astype(BF16)
                for pb, psg, sg in zip(prev_base, prev_segs, segs)]
        vext = [jnp.concatenate([t, ones_blk], axis=1) for t in vcat]
        has_prev = [has_prev_sb if other else 1 for _, other in prev]
        return dict(pi=pi, segs=segs, s=_each(_dot_nt, q2, kcat), vext=vext, has_prev=has_prev)

    def softmax(c):
        kind = 0 if c["pi"] == 0 else 1
        s = _each(lambda t, hp: t + bias_scr[kind, hp], c["s"], c["has_prev"])
        m = [jnp.max(t, axis=-1, keepdims=True) for t in s]
        p = _each(lambda t, mm: jnp.exp(t - mm).astype(BF16), s, m)
        return dict(pi=c["pi"], segs=c["segs"], vext=c["vext"], m=m, p=p)

    def weighted_values(c):
        pi, segs, m = c["pi"], c["segs"], c["m"]
        nl = _each(_dot, c["p"], c["vext"])
        for t in range(len(segs)):
            store_tile(num_scr[pi], segs[t], jnp.where(lane_lo, nl[t][:Q, :PAIR], nl[t][Q:, :PAIR]))
            store_tile(l_scr[pi], segs[t], jnp.where(lane_lo, nl[t][:Q, PAIR:], nl[t][Q:, PAIR:]))
            store_tile(m_scr[pi], segs[t], jnp.where(lane_lo, m[t][:Q], m[t][Q:]))

    groups = [(pi, list(range(g0, g0 + tiles_per_group)))
              for pi in range(n_pat) for g0 in range(0, n_blk, tiles_per_group)]
    n_grp = len(groups)
    sc = {0: scores(*groups[0])}
    if n_grp > 1:
        sc[1] = scores(*groups[1])
    sm = {0: softmax(sc.pop(0))}
    for k in range(n_grp):
        if k + 2 < n_grp:
            sc[k + 2] = scores(*groups[k + 2])
        if k + 1 < n_grp:
            sm[k + 1] = softmax(sc.pop(k + 1))
        weighted_values(sm.pop(k))

    ri = lax.broadcasted_iota(jnp.int32, (PAIR, PAIR), 0)
    ci = lax.broadcasted_iota(jnp.int32, (PAIR, PAIR), 1)
    seg_ones = ((ri >= HEAD_DIM) == (ci >= HEAD_DIM)).astype(BF16)
    gain = gain_ref[...]

    def merge(i, carry):
        rows = pl.ds(pl.multiple_of(i * Q, Q), Q)
        ms = [m_scr[pi][rows, :] for pi in range(n_pat)]
        m_all = functools.reduce(jnp.maximum, ms)
        num = 0.0
        den = 0.0
        for pi in range(n_pat):
            wgt = jnp.exp(ms[pi] - m_all)
            num = num + wgt * num_scr[pi][rows, :]
            den = den + wgt * l_scr[pi][rows, :]
        o = num / den
        ms_o = _split_dot(o * o, seg_ones) * (1.0 / HEAD_DIM)
        out_nat[pl.ds((i // N_RES) * SB_QUARTER + i % N_RES, Q, stride=N_RES), :] = o * lax.rsqrt(ms_o + NORM_EPS) * gain
        return carry

    lax.fori_loop(0, SB_ROWS // Q, merge, 0, unroll=4)
    o_ref[...] = out_nat[...].astype(o_ref.dtype)


def _attention(q, k, v, gain, tiles_per_group=4):
    B, S, W = q.shape
    n_pairs = W // PAIR
    n_pat = len(DILATED_PATTERNS)
    blk = pl.BlockSpec((None, SB_ROWS, PAIR), lambda b, p, s: (b, s, p))
    tile = pltpu.VMEM((SB_ROWS, PAIR), F32)
    return pl.pallas_call(
        functools.partial(_attn_kernel, tiles_per_group=tiles_per_group),
        out_shape=jax.ShapeDtypeStruct((B, S, W), BF16),
        grid=(B, n_pairs, S // SB_ROWS),
        in_specs=[blk, blk, blk, pl.BlockSpec((1, PAIR), lambda b, p, s: (0, p))],
        out_specs=blk,
        scratch_shapes=[tile, tile, tile, tile,
                        pltpu.VMEM((2 * SB_ROWS, PAIR), F32),
                        pltpu.VMEM((2 * SB_ROWS, PAIR), F32),
                        pltpu.VMEM((2, 2, 2 * ATTN_BLOCK, 2 * ATTN_BLOCK), F32),
                        tile]
                       + [tile] * (3 * n_pat),
        compiler_params=pltpu.CompilerParams(dimension_semantics=("parallel", "parallel", "arbitrary"),
                                             vmem_limit_bytes=VMEM_LIMIT),
        name="dilated_attn",
    )(q, k, v, gain)


def _rotary_tables(seq):
    half = ROT_DIM // 2
    inv_freq = ROPE_THETA ** (-jnp.arange(half, dtype=F32) * 2.0 / ROT_DIM)
    ang = jnp.arange(seq).astype(F32)[:, None] * inv_freq[None, :]
    cos, sin = jnp.cos(ang), jnp.sin(ang)
    rest = HEAD_DIM - ROT_DIM
    cos_h = jnp.concatenate([cos, cos, jnp.ones((seq, rest), F32)], axis=-1)
    sin_h = jnp.concatenate([-sin, sin, jnp.zeros((seq, rest), F32)], axis=-1)
    return jnp.tile(cos_h, (1, PAIR // HEAD_DIM)), jnp.tile(sin_h, (1, PAIR // HEAD_DIM))


FFN_HALO = 16
MIX_PIECES = 8


def _mix_ffn_kernel(x_ref, ya_ref, yb_ref, xh_ref, yah_ref, ybh_ref, wo_ref, g_ref, wu_ref, cw_ref,
                    cb_ref, wd_ref, fg_ref, o_ref, h_scr, hh_scr, *, tm, seq, row_chunks, apply_final):
    tf = wd_ref.shape[0]
    i = pl.program_id(0)
    j = pl.program_id(1)
    wa = ya_ref.shape[1]

    def mixed(x, ya, yb):
        return x + _dot(ya, wo_ref[:wa, :]) + _dot(yb, wo_ref[wa:, :])

    @pl.when(j == 0)
    def _():
        hh_scr[...] = _rmsnorm(mixed(xh_ref[...], yah_ref[...], ybh_ref[...]), g_ref[...]).astype(BF16)
        piece = tm // MIX_PIECES
        for c in range(MIX_PIECES):
            rows = slice(c * piece, (c + 1) * piece)
            x1 = mixed(x_ref[rows, :], ya_ref[rows, :], yb_ref[rows, :])
            o_ref[rows, :] = x1
            h_scr[rows, :] = _rmsnorm(x1, g_ref[...]).astype(BF16)

    seq_start = (i * tm) % seq == 0
    rc = tm // row_chunks
    row = lax.broadcasted_iota(jnp.int32, (rc, 1), 0)
    for c in range(row_chunks):
        rows = slice(c * rc, (c + 1) * rc)
        gv = _dot(h_scr[rows, :], wu_ref[...])
        gate, val = gv[:, :tf], gv[:, tf:]
        if c == 0:
            gate_h = jnp.where(seq_start, 0.0, _dot(hh_scr[...], wu_ref[:, :tf]))
        else:
            gate_h = _dot(h_scr[c * rc - FFN_HALO:c * rc, :], wu_ref[:, :tf])
        g1 = jnp.where(row == 0, gate_h[FFN_HALO - 1:FFN_HALO, :], pltpu.roll(gate, 1, 0))
        g2 = jnp.where(row == 0, gate_h[FFN_HALO - 2:FFN_HALO - 1, :],
                       jnp.where(row == 1, gate_h[FFN_HALO - 1:FFN_HALO, :], pltpu.roll(gate, 2, 0)))
        u = cw_ref[0:1, :] * g2 + cw_ref[1:2, :] * g1 + cw_ref[2:3, :] * gate + cb_ref[...]
        act = (u * _sigmoid(u) * val).astype(BF16)
        o_ref[rows, :] += _dot(act, wd_ref[...])

    if apply_final:
        @pl.when(j == pl.num_programs(1) - 1)
        def _():
            o_ref[...] = _rmsnorm(o_ref[...], fg_ref[...])


def _mix_ffn(x2d, y_a, y_b, w_o, gain, w_up, conv_w, conv_b, w_down, final_gain, seq, apply_final,
             tm=1024, n_ff_tiles=2, row_chunks=1):
    T, D = x2d.shape
    F = w_down.shape[0]
    tf = F // n_ff_tiles
    assert tf % LANES == 0 and seq % tm == 0 and w_up.shape[1] == 2 * F
    w_up = jnp.concatenate([w_up[:, half * F + j * tf: half * F + (j + 1) * tf]
                            for j in range(n_ff_tiles) for half in range(2)], axis=1)
    halo_blocks = tm // FFN_HALO
    row = lambda i, j: (i, 0)
    halo = lambda i, j: (jnp.maximum(i * halo_blocks - 1, 0), 0)
    fixed = lambda i, j: (0, 0)
    wa, wb = y_a.shape[1], y_b.shape[1]
    return pl.pallas_call(
        functools.partial(_mix_ffn_kernel, tm=tm, seq=seq, row_chunks=row_chunks, apply_final=apply_final),
        out_shape=jax.ShapeDtypeStruct((T, D), F32),
        grid=(T // tm, n_ff_tiles),
        in_specs=[pl.BlockSpec((tm, D), row), pl.BlockSpec((tm, wa), row), pl.BlockSpec((tm, wb), row),
                  pl.BlockSpec((FFN_HALO, D), halo), pl.BlockSpec((FFN_HALO, wa), halo),
                  pl.BlockSpec((FFN_HALO, wb), halo),
                  pl.BlockSpec(w_o.shape, fixed),
                  pl.BlockSpec((1, D), fixed),
                  pl.BlockSpec((D, 2 * tf), lambda i, j: (0, j)),
                  pl.BlockSpec((CONV_WIDTH, tf), lambda i, j: (0, j)),
                  pl.BlockSpec((1, tf), lambda i, j: (0, j)),
                  pl.BlockSpec((tf, D), lambda i, j: (j, 0)),
                  pl.BlockSpec((1, D), fixed)],
        out_specs=pl.BlockSpec((tm, D), row),
        scratch_shapes=[pltpu.VMEM((tm, D), BF16), pltpu.VMEM((FFN_HALO, D), BF16)],
        compiler_params=pltpu.CompilerParams(dimension_semantics=("parallel", "arbitrary"),
                                             vmem_limit_bytes=VMEM_LIMIT),
        name="mix_convglu_ffn",
    )(x2d, y_a, y_b, x2d, y_a, y_b, w_o, gain, w_up, conv_w, conv_b, w_down, final_gain)


def kernel(x, mix_norm_gain, w_in, rwkv_shift_mix, w0, w_lora_up, a0, a_lora_up, g_lora_up, k_k, k_a, r_k,
           ln_x_w, ln_x_b, attn_norm_gain, w_out, ffn_norm_gain, w_ffn_up, ffn_conv_w, ffn_conv_b,
           w_ffn_down, final_norm_gain):
    B, S, D = x.shape
    depth = w_in.shape[0]
    rw = w0.shape[1]
    aw = attn_norm_gain.shape[1]
    n_w, n_a, n_g = w_lora_up.shape[1], a_lora_up.shape[1], g_lora_up.shape[1]
    n_lora = n_w + n_a + n_g
    lora_pad = -(-n_lora // (2 * LANES)) * (2 * LANES)
    assert lora_pad == 2 * LANES and rw % PAIR == 0 and aw % PAIR == 0
    rwkv_cols = 3 * rw + n_lora
    cos_t, sin_t = _rotary_tables(S)

    x2d = x.reshape(B * S, D)
    for l in range(depth):
        w_r = w_in[l][:, :3 * rw + lora_pad].astype(BF16)
        w_a = w_in[l][:, rwkv_cols:].astype(BF16)
        mix =jnp.concatenate([rwkv_shift_mix[l], jnp.zeros((lora_pad - n_lora,), F32)])[None, :]
        wc = jnp.zeros((lora_pad, 3 * rw), F32)
        wc = wc.at[:n_w, :rw].set(w_lora_up[l])
        wc = wc.at[n_w:n_w + n_a, rw:2 * rw].set(a_lora_up[l])
        wc = wc.at[n_w + n_a:n_lora, 2 * rw:].set(g_lora_up[l]).astype(BF16)

        p_r, q, k, v = _inproj(x2d, mix_norm_gain[l][None, :], w_r, w_a, mix, cos_t, sin_t)
        y_rwkv = _rwkv(p_r.reshape(B, S, -1), wc, w0[l][None, :], a0[l][None, :], k_k[l][None, :],
                       k_a[l][None, :], r_k[l].reshape(1, rw), ln_x_w[l][None, :], ln_x_b[l][None, :],
                       rw, (n_w, n_w + n_a, n_lora))
        y_attn = _attention(q.reshape(B, S, aw), k.reshape(B, S, aw), v.reshape(B, S, aw),
                            attn_norm_gain[l][None, :])
        x2d = _mix_ffn(x2d, y_rwkv.reshape(B * S, rw), y_attn.reshape(B * S, aw), w_out[l].astype(BF16),
                       ffn_norm_gain[l][None, :], w_ffn_up[l].astype(BF16), ffn_conv_w[l],
                       ffn_conv_b[l][None, :], w_ffn_down[l].astype(BF16), final_norm_gain[None, :], S,
                       apply_final=(l == depth - 1))
    return x2d.reshape(B, S, D)
```

```python
import functools
import math

import jax
import jax.numpy as jnp
from jax import lax
from jax.experimental import pallas as pl
from jax.experimental.pallas import tpu as pltpu

F32 = jnp.float32
BF16 = jnp.bfloat16

LANES = 128
HEAD_DIM = 64
PAIR = 2 * HEAD_DIM
ROT_DIM = HEAD_DIM // 4
ROPE_THETA = 500000.0
NORM_EPS = 1e-6
GN_EPS = 64e-5
DILATED_PATTERNS = ((128, 1), (512, 4), (2048, 16))
ATTN_BLOCK = 128
CONV_WIDTH = 3
CHUNK = 64
SOLVE_BLOCK = 16
NEG_BIG = -1e30
VMEM_LIMIT = 56 * 1024 * 1024


def _dot(a, b):
    return jnp.dot(a, b, preferred_element_type=F32)


def _dot_nt(a, b):
    return lax.dot_general(a, b, (((1,), (1,)), ((), ())), preferred_element_type=F32)


def _dot_tn(a, b):
    return lax.dot_general(a, b, (((0,), (0,)), ((), ())), preferred_element_type=F32)


def _rmsnorm(x, gain):
    return x * lax.rsqrt(jnp.mean(x * x, axis=-1, keepdims=True) + NORM_EPS) * gain


def _sigmoid(x):
    return 1.0 / (1.0 + jnp.exp(-x))


def _split_dot(x, w):
    hi = x.astype(BF16)
    lo = (x - hi.astype(F32)).astype(BF16)
    return _dot(hi, w) + _dot(lo, w)


INPROJ_PIECES = 2


def _inproj_kernel(x_ref, g_ref, wr_ref, wa_ref, mix_ref, cos_ref, sin_ref, pr_ref, q_ref, k_ref, v_ref, carry_scr,
                   *, seq_tiles):
    tm = x_ref.shape[0]
    aw = q_ref.shape[1]
    rp = tm // INPROJ_PIECES

    @pl.when(pl.program_id(0) == 0)
    def _():
        carry_scr[...] = jnp.zeros_like(carry_scr)

    row = lax.broadcasted_iota(jnp.int32, (rp, 1), 0)
    lane = lax.broadcasted_iota(jnp.int32, (rp, PAIR), 1)
    first_half = (lane % HEAD_DIM) < (ROT_DIM // 2)
    prev_last = jnp.where(pl.program_id(0) % seq_tiles == 0, 0.0, carry_scr[7:8, :])

    for c in range(INPROJ_PIECES):
        rows = slice(c * rp, (c + 1) * rp)
        h = _rmsnorm(x_ref[rows, :], g_ref[...]).astype(BF16)
        pr = _dot(h, wr_ref[...])
        p = _dot(h, wa_ref[...])
        pr_prev = jnp.where(row == 0, prev_last, pltpu.roll(pr, 1, 0))
        prev_last = pr[rp - 1:rp, :]
        if c == INPROJ_PIECES - 1:
            carry_scr[...] = pr[rp - 8:rp, :]
        pr_ref[rows, :] = pr + (pr_prev - pr) * mix_ref[...]

        cos, sin = cos_ref[rows, :], sin_ref[rows, :]

        def rotary(x):
            partner = jnp.where(first_half, pltpu.roll(x, PAIR - ROT_DIM // 2, 1), pltpu.roll(x, ROT_DIM // 2, 1))
            return x * cos + partner * sin

        for t in range(aw // PAIR):
            lo = t * PAIR
            q_ref[rows, lo:lo + PAIR] = (rotary(p[:, lo:lo + PAIR]) * (1.0 / math.sqrt(HEAD_DIM))).astype(BF16)
            k_ref[rows, lo:lo + PAIR] = rotary(p[:, aw + lo:aw + lo + PAIR]).astype(BF16)
        v_ref[rows, :] = p[:, 2 * aw:].astype(BF16)


def _inproj(x2d, gain, w_r, w_a, mix, cos_t, sin_t, tm=1024):
    T, D = x2d.shape
    rw, aw = w_r.shape[1], w_a.shape[1] // 3
    seq_tiles = cos_t.shape[0] // tm
    row = lambda i: (i, 0)
    fixed = lambda i: (0, 0)
    tab = pl.BlockSpec((tm, PAIR), lambda i: (i % seq_tiles, 0))
    return pl.pallas_call(
        functools.partial(_inproj_kernel, seq_tiles=seq_tiles),
        out_shape=(jax.ShapeDtypeStruct((T, rw), F32),) + (jax.ShapeDtypeStruct((T, aw), BF16),) * 3,
        grid=(T // tm,),
        in_specs=[pl.BlockSpec((tm, D), row), pl.BlockSpec((1, D), fixed), pl.BlockSpec(w_r.shape, fixed),
                  pl.BlockSpec(w_a.shape, fixed), pl.BlockSpec((1, rw), fixed), tab, tab],
        out_specs=(pl.BlockSpec((tm, rw), row),) + (pl.BlockSpec((tm, aw), row),) * 3,
        scratch_shapes=[pltpu.VMEM((8, rw), F32)],
        compiler_params=pltpu.CompilerParams(dimension_semantics=("arbitrary",), vmem_limit_bytes=VMEM_LIMIT),
        name="inproj",
    )(x2d, gain, w_r, w_a, mix, cos_t, sin_t)


def _each(fn, *lists):
    return [fn(*args) for args in zip(*lists)]


def _rwkv_chunk_terms(r, k, v, logw, cum, kkn, bb, same_blk, strict_c, incl_c, lane_lo):
    L = CHUNK
    bf = lambda t: t.astype(BF16)
    cum_last = [t[L - 1:L, :] for t in cum]
    g_in = _each(jnp.exp2, cum)
    g_ex = _each(lambda t, w: jnp.exp2(t - w), cum, logw)
    g_inv = _each(lambda t: jnp.exp2(-t), cum)
    g_hat = _each(lambda tl, t: jnp.exp2(tl - t), cum_last, cum)
    g_last = _each(jnp.exp2, cum_last)
    yield

    def by_head_rows(x):
        lo = lane_lo if x.shape[1] == PAIR else jnp.concatenate([lane_lo] * (x.shape[1] // PAIR), axis=1)
        return jnp.concatenate([jnp.where(lo, x, 0.0), jnp.where(lo, 0.0, x)], axis=0).astype(BF16)

    al = _each(lambda t, g: -t * g, kkn, g_ex)
    rb = _each(jnp.multiply, r, g_in)
    bt = _each(jnp.multiply, bb, g_inv)
    kt = _each(jnp.multiply, k, g_inv)
    bh = _each(jnp.multiply, bb, g_hat)
    kh = _each(jnp.multiply, k, g_hat)
    lhs = _each(lambda a, b: jnp.concatenate([a, b], axis=0).astype(BF16), al, rb)
    rhs = _each(lambda a, b: jnp.concatenate([by_head_rows(a), by_head_rows(b)], axis=0), bt, kt)
    yield
    aq = _each(_dot_nt, lhs, rhs)
    a_ab = [jnp.where(strict_c, t[:L, :2 * L], 0.0) for t in aq]
    a_ak = [jnp.where(strict_c, t[:L, 2 * L:], 0.0).astype(BF16) for t in aq]
    a_rb = [jnp.where(incl_c, t[L:, :2 * L], 0.0).astype(BF16) for t in aq]
    a_rk = [jnp.where(incl_c, t[L:, 2 * L:], 0.0).astype(BF16) for t in aq]
    yield

    v_s = _each(by_head_rows, v)
    av = _each(_dot, a_ak, v_s)
    BS = SOLVE_BLOCK
    n_blk = L // BS
    rhs0 = _each(lambda a, b: jnp.concatenate([a, b], axis=1), al, av)
    lo2 = lax.broadcasted_iota(jnp.int32, (BS, 2 * PAIR), 1) % PAIR < HEAD_DIM
    lane = lax.broadcasted_iota(jnp.int32, (BS, PAIR), 1)
    br =lax.broadcasted_iota(jnp.int32, (PAIR, PAIR), 0)
    bc = lax.broadcasted_iota(jnp.int32, (PAIR, PAIR), 1)
    blk_diag = (br // BS) == (bc // BS)

    def expand(pack):
        return jnp.where(blk_diag, jnp.concatenate([pack] * (PAIR // BS), axis=0), 0.0).astype(BF16)

    d_k = [functools.reduce(jnp.add, [jnp.where((lane % HEAD_DIM) // BS == b, t[b * BS:(b + 1) * BS, :], 0.0)
                                      for b in range(n_blk)]) for t in a_ab]
    t_m = d_k
    yield
    n_sq = int(math.log2(BS)) - 1
    d_k = _each(lambda d: _dot(d.astype(BF16), expand(d)), d_k)
    yield
    for lvl in range(n_sq):
        d_e = _each(expand, d_k)
        t_m = _each(lambda t, d, de: t + d + _dot(t.astype(BF16), de), t_m, d_k, d_e)
        if lvl + 1 < n_sq:
            d_k = _each(lambda d, de: _dot(d.astype(BF16), de), d_k, d_e)
        yield
    t_mb = _each(bf, t_m)

    x_blocks = [[] for _ in a_ab]
    xs_lo = [[] for _ in a_ab]
    xs_hi = [[] for _ in a_ab]

    def stacked(lo_parts, hi_parts, width=2 * PAIR):
        zero_blk = jnp.zeros((BS, width), BF16)
        rows = [lo_parts.get(c, zero_blk) for c in range(n_blk)] + [hi_parts.get(c, zero_blk) for c in range(n_blk)]
        return jnp.concatenate(rows, axis=0)

    def left_solved(t, tm, b, lo):
        return t + _dot(tm, stacked({b: jnp.where(lo, t, 0.0).astype(BF16)},
                                    {b: jnp.where(lo, 0.0, t).astype(BF16)}, t.shape[1]))

    lo1 = lane < HEAD_DIM
    r_hat = [_each(lambda t, tm: left_solved(t[b * BS:(b + 1) * BS, :], tm, b, lo2), rhs0, t_mb) for b in range(n_blk)]
    a_hat = [None] + [_each(lambda a, tm: left_solved(a[b * BS:(b + 1) * BS, :], tm, b, lo1).astype(BF16), a_ab, t_mb)
                      for b in range(1, n_blk)]
    yield
    for b in range(n_blk):
        if b == 0:
            x_b = r_hat[0]
        else:
            x_b = _each(lambda rr, a, lo_p, hi_p: rr + _dot(a, stacked(dict(enumerate(lo_p)), dict(enumerate(hi_p)))),
                        r_hat[b], a_hat[b], xs_lo, xs_hi)
        for i, xb in enumerate(x_b):
            x_blocks[i].append(xb)
            xs_lo[i].append(jnp.where(lo2, xb, 0.0).astype(BF16))
            xs_hi[i].append(jnp.where(lo2, 0.0, xb).astype(BF16))
        if b > 0:
            yield
    x = [jnp.concatenate(blks, axis=0) for blks in x_blocks]
    x_s = _each(lambda lo_p, hi_p: jnp.concatenate(lo_p + hi_p, axis=0), xs_lo, xs_hi)

    z = _each(_dot, a_rb, x_s)
    akv = _each(_dot, a_rk, v_s)
    w2 = _each(lambda a, t: (a + t[:, :PAIR]).astype(BF16), rb, z)
    y_loc = _each(lambda t, a: t[:, PAIR:] + a, z, akv)
    w1 = [t[:, :PAIR].astype(BF16) for t in x]
    u_loc = [t[:, PAIR:] for t in x]
    yield

    m_t = _each(lambda a, b: jnp.where(same_blk, _dot_tn(a, b.astype(BF16)), 0.0).astype(BF16), w1, bh)
    s_loc = _each(
        lambda u, vv, b, kk_: jnp.where(
            same_blk,
            _dot_tn(jnp.concatenate([u, vv], axis=0).astype(BF16), jnp.concatenate([b, kk_], axis=0).astype(BF16)),
            0.0),
        u_loc, v, bh, kh)
    return w2, y_loc, m_t, s_loc, g_last


def _rwkv_kernel(p_ref, wc_ref, w0_ref, a0_ref, kk_ref, ka_ref, rk_ref, lnw_ref, lnb_ref,
                 o_ref, s_scr, *, nb, width, n_lora, n_chunks, n_groups, stage_offset):
    L = CHUNK
    R = n_chunks * L
    c = pl.program_id(0)

    @pl.when(c == 0)
    def _():
        s_scr[...] = jnp.zeros_like(s_scr)

    n_pairs = width // PAIR
    ri = lax.broadcasted_iota(jnp.int32, (2 * L, 2 * L), 0)
    ci = lax.broadcasted_iota(jnp.int32, (2 * L, 2 * L), 1)
    same_blk = (ri >= L) == (ci >= L)
    ti = lax.broadcasted_iota(jnp.int32, (L, 2 * L), 0)
    si = lax.broadcasted_iota(jnp.int32, (L, 2 * L), 1) % L
    strict_c = si < ti
    incl_c = si <= ti
    rr = lax.broadcasted_iota(jnp.int32, (R, R), 0)
    rc = lax.broadcasted_iota(jnp.int32, (R, R), 1)
    tri = ((rr // L == rc // L) & (rr >= rc)).astype(BF16)
    lane_lo = lax.broadcasted_iota(jnp.int32, (L, PAIR), 1) < HEAD_DIM
    lane_lo_r = lax.broadcasted_iota(jnp.int32, (R, PAIR), 1) < HEAD_DIM
    lora_lane = lax.broadcasted_iota(jnp.int32, (R, 2 * LANES), 1)

    def head_sums(x):
        s0 = jnp.sum(jnp.where(lane_lo_r, x, 0.0), axis=-1, keepdims=True)
        s1 = jnp.sum(jnp.where(lane_lo_r, 0.0, x), axis=-1, keepdims=True)
        return jnp.where(lane_lo_r, s0, s1)

    def lora_up(pm):
        lora = pm[:, 3 * width:]
        act = jnp.where(lora_lane < n_lora[0], jnp.tanh(lora),
                        jnp.where(lora_lane < n_lora[1], lora,
                                  jnp.where(lora_lane < n_lora[2], _sigmoid(lora), 0.0)))
        return _dot(act.astype(BF16), wc_ref[...])

    def log_decay(up):
        t = w0_ref[...] + up[:, :width]
        return (-math.exp(-0.5) * math.log2(math.e)) / (1.0 + jnp.exp(-t))

    def running_sum(t):
        h1 = t.astype(BF16)
        r1 = t - h1.astype(F32)
        h2 = r1.astype(BF16)
        h3 = (r1 - h2.astype(F32)).astype(BF16)
        return _dot(tri, h1) + _dot(tri, h2) + _dot(tri, h3)

    tiles = [(b, pr) for b in range(nb) for pr in range(n_pairs)]
    n_t = len(tiles)
    lanes = lambda pr, part=0: slice(part * width + pr * PAIR, part * width + (pr + 1) * PAIR)
    sls = [lanes(pr) for _, pr in tiles]

    states = {0: [s_scr[i] for i in range(n_t)]}
    outs = {}

    def run_group(gi):
        g0 = gi * R
        pms = [p_ref[b, g0:g0 + R, :] for b in range(nb)]
        yield
        ups = _each(lora_up, pms)
        logw_b = _each(log_decay, ups)
        cum_b = _each(running_sum, logw_b)
        yield

        r = [pms[b][:, lanes(pr, 0)] for b, pr in tiles]
        k_raw = [pms[b][:, lanes(pr, 1)] for b, pr in tiles]
        v = [pms[b][:, lanes(pr, 2)] for b, pr in tiles]
        logw = [logw_b[b][:, lanes(pr)] for b, pr in tiles]
        cum = [cum_b[b][:, lanes(pr)] for b, pr in tiles]
        a = [_sigmoid(a0_ref[:, lanes(pr)] + ups[b][:, lanes(pr, 1)]) for b, pr in tiles]
        g = [ups[b][:, lanes(pr, 2)] for b, pr in tiles]

        kk = _each(lambda t, sl: t * kk_ref[:, sl], k_raw, sls)
        kk_ss = _each(lambda t: head_sums(t * t), kk)
        kkn = _each(lambda t, ss: t * lax.rsqrt(jnp.maximum(ss, 1e-24)), kk, kk_ss)
        k = _each(lambda t, aa, sl: t * (1.0 + (aa - 1.0) * ka_ref[:, sl]), k_raw, a, sls)
        bb = _each(jnp.multiply, kkn, a)
        bonus_dot = _each(lambda rr_, kk_, sl: head_sums(rr_ * kk_ * rk_ref[:, sl]), r, k, sls)
        yield

        def chunks(ts):
            return [t[ch * L:(ch + 1) * L] for ch in range(n_chunks) for t in ts]

        w2, y_loc, m_t, s_loc, g_last = yield from _rwkv_chunk_terms(
            chunks(r), chunks(k), chunks(v), chunks(logw), chunks(cum), chunks(kkn), chunks(bb),
            same_blk, strict_c, incl_c, lane_lo)
        yield

        s = states[gi]
        y_parts = []
        for ch in range(n_chunks):
            sel = slice(ch * n_t, (ch + 1) * n_t)
            s_b = _each(lambda t: t.astype(BF16), s)
            y_parts.append(_each(lambda a_, sb, yl: _dot_nt(a_, sb) + yl, w2[sel], s_b, y_loc[sel]))
            s = _each(lambda s0, gl, sb, m, sl: s0 * gl + _dot(sb, m) + sl, s, g_last[sel], s_b, m_t[sel], s_loc[sel])
        states[gi + 1] = s
        y = [jnp.concatenate([y_parts[ch][i] for ch in range(n_chunks)], axis=0) for i in range(n_t)]
        yield

        mu = _each(lambda t: head_sums(t) * (1.0 / HEAD_DIM), y)
        yc = _each(jnp.subtract, y, mu)
        var = _each(lambda t: head_sums(t * t) * (1.0 / HEAD_DIM), yc)
        yield
        res = []
        for i in range(n_t):
            sl = sls[i]
            yn = yc[i] * lax.rsqrt(var[i] + GN_EPS) * lnw_ref[:, sl] + lnb_ref[:, sl]
            res.append(((yn + bonus_dot[i] * v[i]) * g[i]).astype(o_ref.dtype))
        outs[gi] = res

    gens = [run_group(gi) for gi in range(n_groups)]
    live = [True] * n_groups
    tick = 0
    while any(live):
        for gi in range(n_groups):
            if live[gi] and tick >= gi * stage_offset:
                try:
                    next(gens[gi])
                except StopIteration:
                    live[gi] = False
        tick += 1

    for gi in range(n_groups):
        for i, (b, pr) in enumerate(tiles):
            o_ref[b, gi * R:(gi + 1) * R, sls[i]] = outs[gi][i]
    for i in range(n_t):
        s_scr[i] = states[n_groups][i]


def _rwkv(p_r, wc, w0, a0, k_k, k_a, r_k, ln_w, ln_b, width, n_lora, n_chunks=2, n_groups=4, stage_offset=5):
    B, S, C = p_r.shape
    L = CHUNK * n_chunks * n_groups
    n_state = B * (width // PAIR)
    vec = lambda n: pl.BlockSpec((1, n), lambda c: (0, 0))
    return pl.pallas_call(
        functools.partial(_rwkv_kernel, nb=B, width=width, n_lora=n_lora, n_chunks=n_chunks, n_groups=n_groups,
                          stage_offset=stage_offset),
        out_shape=jax.ShapeDtypeStruct((B, S, width), BF16),
        grid=(S // L,),
        in_specs=[pl.BlockSpec((B, L, C), lambda c: (0, c, 0)),
                  pl.BlockSpec(wc.shape, lambda c: (0, 0))] + [vec(width)] * 7,
        out_specs=pl.BlockSpec((B, L, width), lambda c: (0, c, 0)),
        scratch_shapes=[pltpu.VMEM((n_state, PAIR, PAIR), F32)],
        compiler_params=pltpu.CompilerParams(dimension_semantics=("arbitrary",), vmem_limit_bytes=VMEM_LIMIT),
        name="rwkv7",
    )(p_r, wc, w0, a0, k_k, k_a, r_k, ln_w, ln_b)


SB_ROWS = 2048
SB_QUARTER = SB_ROWS // 4
N_RES = 4


def _attn_segments(pi, g):
    if pi == 0:
        return [((g // 4) * SB_QUARTER + r * ATTN_BLOCK + (ATTN_BLOCK // N_RES) * (g % 4), ATTN_BLOCK // N_RES, 1)
                for r in range(N_RES)]
    if pi == 1:
        return [((g // 4) * SB_QUARTER + (g % 4) * ATTN_BLOCK, ATTN_BLOCK, 1)]
    return [(qq * SB_QUARTER + (g % 4) * ATTN_BLOCK + g // 4, ATTN_BLOCK // 4, 4) for qq in range(4)]


def _attn_prev_tile(pi, g):
    if pi == 0:
        return (g + 15) % 16, g == 0
    if pi == 1:
        return ((g // 4 + 3) % 4) * 4 + g % 4, g < 4
    return g, True


def _attn_kernel(q_ref, k_ref, v_ref, gain_ref, o_ref,
                 nat_q, nat_k, nat_v, q_scr, k_ring, v_ring, bias_scr, out_nat, *stat_scr, tiles_per_group):
    sb = pl.program_id(2)
    n_pat = len(DILATED_PATTERNS)
    assert DILATED_PATTERNS == ((128, 1), (512, 4), (2048, 16)) and ATTN_BLOCK == 128
    num_scr, m_scr, l_scr = stat_scr[:n_pat], stat_scr[n_pat:2 * n_pat], stat_scr[2 * n_pat:]
    Q = ATTN_BLOCK
    cur_base = (sb % 2) * SB_ROWS
    other_base = SB_ROWS - cur_base

    @pl.when(sb == 0)
    def _():
        k_ring[pl.ds(SB_ROWS, SB_ROWS), :] = jnp.zeros((SB_ROWS, PAIR), F32)
        v_ring[pl.ds(SB_ROWS, SB_ROWS), :] = jnp.zeros((SB_ROWS, PAIR), F32)

    nat_q[...] = q_ref[...].astype(F32)
    nat_k[...] = k_ref[...].astype(F32)
    nat_v[...] = v_ref[...].astype(F32)
    for qq in range(4):
        for r in range(N_RES):
            src = pl.ds(qq * SB_QUARTER + r, Q, stride=N_RES)
            dst = qq * SB_QUARTER + r * Q
            q_scr[dst:dst + Q, :] = nat_q[src, :]
            k_ring[pl.ds(cur_base + dst, Q), :] = nat_k[src, :]
            v_ring[pl.ds(cur_base + dst, Q), :] = nat_v[src, :]

    @pl.when(sb == 0)
    def _():
        ii = lax.broadcasted_iota(jnp.int32, (2 * Q, 2 * Q), 0) % Q
        cj = lax.broadcasted_iota(jnp.int32, (2 * Q, 2 * Q), 1)
        per = Q // N_RES
        for kind in range(2):
            if kind == 0:
                qi = N_RES * (ii % per) + ii // per
                kj = N_RES * ((cj % Q) % per) + (cj % Q) // per + (cj // Q) * Q
            else:
                qi, kj = ii, cj
            band = (kj >= qi) & (kj <= qi + Q)
            bias_scr[kind, 1] = jnp.where(band, 0.0, NEG_BIG)
            bias_scr[kind, 0] = jnp.where(band & (cj >= Q), 0.0, NEG_BIG)

    def load_tile(ref, base, segs):
        parts = [ref[pl.ds(base + s, n, stride=st) if st > 1 else pl.ds(base + s, n), :] for s, n, st in segs]
        return parts[0] if len(parts) == 1 else jnp.concatenate(parts, axis=0)

    def store_tile(ref, segs, val):
        off = 0
        for s, n, st in segs:
            ref[pl.ds(s, n, stride=st) if st > 1 else pl.ds(s, n), :] = val[off:off + n]
            off += n

    lane_lo = lax.broadcasted_iota(jnp.int32, (Q, PAIR), 1) < HEAD_DIM
    ones_blk = jnp.ones((2 * Q, PAIR), BF16)

    n_blk = SB_ROWS // Q
    has_prev_sb = jnp.where(sb > 0, 1, 0)

    def scores(pi, gs):
        segs = [_attn_segments(pi, g) for g in gs]
        prev = [_attn_prev_tile(pi, g) for g in gs]
        prev_segs = [_attn_segments(pi, pg) for pg, _ in prev]
        prev_base = [other_base if other else cur_base for _, other in prev]
        q2 = [load_tile(q_scr, 0, sg) for sg in segs]
        q2 = [jnp.concatenate([jnp.where(lane_lo, t, 0.0), jnp.where(lane_lo, 0.0, t)], axis=0).astype(BF16)
              for t in q2]
        kcat = [jnp.concatenate([load_tile(k_ring, pb, psg), load_tile(k_ring, cur_base, sg)], axis=0).astype(BF16)
                for pb, psg, sg in zip(prev_base, prev_segs, segs)]
        vcat = [jnp.concatenate([load_tile(v_ring, pb, psg), load_tile(v_ring, cur_base, sg)], axis=0).astype(BF16)
                for pb, psg, sg in zip(prev_base, prev_segs, segs)]
        vext = [jnp.concatenate([t, ones_blk], axis=1) for t in vcat]
        has_prev = [has_prev_sb if other else 1 for _, other in prev]
        return dict(pi=pi, segs=segs, s=_each(_dot_nt, q2, kcat), vext=vext, has_prev=has_prev)

    def softmax(c):
        kind = 0 if c["pi"] == 0 else 1
        s = _each(lambda t, hp: t + bias_scr[kind, hp], c["s"], c["has_prev"])
        m = [jnp.max(t, axis=-1, keepdims=True) for t in s]
        p = _each(lambda t, mm: jnp.exp(t - mm).astype(BF16), s, m)
        return dict(pi=c["pi"], segs=c["segs"], vext=c["vext"], m=m, p=p)

    def weighted_values(c):
        pi, segs, m = c["pi"], c["segs"], c["m"]
        nl = _each(_dot, c["p"], c["vext"])
        for t in range(len(segs)):
            store_tile(num_scr[pi], segs[t], jnp.where(lane_lo, nl[t][:Q, :PAIR], nl[t][Q:, :PAIR]))
            store_tile(l_scr[pi], segs[t], jnp.where(lane_lo, nl[t][:Q, PAIR:], nl[t][Q:, PAIR:]))
            store_tile(m_scr[pi], segs[t], jnp.where(lane_lo, m[t][:Q], m[t][Q:]))

    groups = [(pi, list(range(g0, g0 + tiles_per_group)))
              for pi in range(n_pat) for g0 in range(0, n_blk, tiles_per_group)]
    n_grp = len(groups)
    sc = {0: scores(*groups[0])}
    if n_grp > 1:
        sc[1] = scores(*groups[1])
    sm = {0: softmax(sc.pop(0))}
    for k in range(n_grp):
        if k + 2 < n_grp:
            sc[k + 2] = scores(*groups[k + 2])
        if k + 1 < n_grp:
            sm[k + 1] = softmax(sc.pop(k + 1))
        weighted_values(sm.pop(k))

    ri = lax.broadcasted_iota(jnp.int32, (PAIR, PAIR), 0)
    ci = lax.broadcasted_iota(jnp.int32, (PAIR, PAIR), 1)
    seg_ones = ((ri >= HEAD_DIM) == (ci >= HEAD_DIM)).astype(BF16)
    gain = gain_ref[...]

    def merge(i, carry):
        rows = pl.ds(pl.multiple_of(i * Q, Q), Q)
        ms = [m_scr[pi][rows, :] for pi in range(n_pat)]
        m_all = functools.reduce(jnp.maximum, ms)
        num = 0.0
        den = 0.0
        for pi in range(n_pat):
            wgt = jnp.exp(ms[pi] - m_all)
            num = num + wgt * num_scr[pi][rows, :]
            den = den + wgt * l_scr[pi][rows, :]
        o = num / den
        ms_o = _split_dot(o * o, seg_ones) * (1.0 / HEAD_DIM)
        out_nat[pl.ds((i // N_RES) * SB_QUARTER + i % N_RES, Q, stride=N_RES), :] = o * lax.rsqrt(ms_o + NORM_EPS) * gain
        return carry

    lax.fori_loop(0, SB_ROWS // Q, merge, 0, unroll=4)
    o_ref[...] = out_nat[...].astype(o_ref.dtype)


def _attention(q, k, v, gain, tiles_per_group=4):
    B, S, W = q.shape
    n_pairs = W // PAIR
    n_pat = len(DILATED_PATTERNS)
    blk = pl.BlockSpec((None, SB_ROWS, PAIR), lambda b, p, s: (b, s, p))
    tile = pltpu.VMEM((SB_ROWS, PAIR), F32)
    return pl.pallas_call(
        functools.partial(_attn_kernel, tiles_per_group=tiles_per_group),
        out_shape=jax.ShapeDtypeStruct((B, S, W), BF16),
        grid=(B, n_pairs, S // SB_ROWS),
        in_specs=[blk, blk, blk, pl.BlockSpec((1, PAIR), lambda b, p, s: (0, p))],
        out_specs=blk,
        scratch_shapes=[tile, tile, tile, tile,
                        pltpu.VMEM((2 * SB_ROWS, PAIR), F32),
                        pltpu.VMEM((2 * SB_ROWS, PAIR), F32),
                        pltpu.VMEM((2, 2, 2 * ATTN_BLOCK, 2 * ATTN_BLOCK), F32),
                        tile]
                       + [tile] * (3 * n_pat),
        compiler_params=pltpu.CompilerParams(dimension_semantics=("parallel", "parallel", "arbitrary"),
                                             vmem_limit_bytes=VMEM_LIMIT),
        name="dilated_attn",
    )(q, k, v, gain)


def _rotary_tables(seq):
    half = ROT_DIM // 2
    inv_freq = ROPE_THETA ** (-jnp.arange(half, dtype=F32) * 2.0 / ROT_DIM)
    ang = jnp.arange(seq).astype(F32)[:, None] * inv_freq[None, :]
    cos, sin = jnp.cos(ang), jnp.sin(ang)
    rest = HEAD_DIM - ROT_DIM
    cos_h = jnp.concatenate([cos, cos, jnp.ones((seq, rest), F32)], axis=-1)
    sin_h = jnp.concatenate([-sin, sin, jnp.zeros((seq, rest), F32)], axis=-1)
    return jnp.tile(cos_h, (1, PAIR // HEAD_DIM)), jnp.tile(sin_h, (1, PAIR // HEAD_DIM))


FFN_HALO = 16
MIX_PIECES = 4


def _mix_ffn_kernel(x_ref, ya_ref, yb_ref, xh_ref, yah_ref, ybh_ref, wo_ref, g_ref, wu_ref, cw_ref,
                    cb_ref, wd_ref, fg_ref, o_ref, h_scr, hh_scr, *, tm, seq, row_chunks, apply_final):
    tf = wd_ref.shape[0]
    i = pl.program_id(0)
    j = pl.program_id(1)
    wa = ya_ref.shape[1]

    def mixed(x, ya, yb):
        return x + _dot(ya, wo_ref[:wa, :]) + _dot(yb, wo_ref[wa:, :])

    @pl.when(j == 0)
    def _():
        hh_scr[...] = _rmsnorm(mixed(xh_ref[...], yah_ref[...], ybh_ref[...]), g_ref[...]).astype(BF16)
        piece = tm // MIX_PIECES
        for c in range(MIX_PIECES):
            rows = slice(c * piece, (c + 1) * piece)
            x1 = mixed(x_ref[rows, :], ya_ref[rows, :], yb_ref[rows, :])
            o_ref[rows, :] = x1
            h_scr[rows, :] = _rmsnorm(x1, g_ref[...]).astype(BF16)

    seq_start = (i * tm) % seq == 0
    rc = tm // row_chunks
    row = lax.broadcasted_iota(jnp.int32, (rc, 1), 0)
    for c in range(row_chunks):
        rows = slice(c * rc, (c + 1) * rc)
        gv = _dot(h_scr[rows, :], wu_ref[...])
        gate, val = gv[:, :tf], gv[:, tf:]
        if c == 0:
            gate_h = jnp.where(seq_start, 0.0, _dot(hh_scr[...], wu_ref[:, :tf]))
        else:
            gate_h = _dot(h_scr[c * rc - FFN_HALO:c * rc, :], wu_ref[:, :tf])
        g1 = jnp.where(row == 0, gate_h[FFN_HALO - 1:FFN_HALO, :], pltpu.roll(gate, 1, 0))
        g2 = jnp.where(row == 0, gate_h[FFN_HALO - 2:FFN_HALO - 1, :],
                       jnp.where(row == 1, gate_h[FFN_HALO - 1:FFN_HALO, :], pltpu.roll(gate, 2, 0)))
        u = cw_ref[0:1, :] * g2 + cw_ref[1:2, :] * g1 + cw_ref[2:3, :] * gate + cb_ref[...]
        act = (u * _sigmoid(u) * val).astype(BF16)
        o_ref[rows, :] += _dot(act, wd_ref[...])

    if apply_final:
        @pl.when(j == pl.num_programs(1) - 1)
        def _():
            o_ref[...] = _rmsnorm(o_ref[...], fg_ref[...])


def _mix_ffn(x2d, y_a, y_b, w_o, gain, w_up, conv_w, conv_b, w_down, final_gain, seq, apply_final,
             tm=1024, n_ff_tiles=2, row_chunks=1):
    T, D = x2d.shape
    F = w_down.shape[0]
    tf = F // n_ff_tiles
    assert tf % LANES == 0 and seq % tm == 0 and w_up.shape[1] == 2 * F
    w_up = jnp.concatenate([w_up[:, half * F + j * tf: half * F + (j + 1) * tf]
                            for j in range(n_ff_tiles) for half in range(2)], axis=1)
    halo_blocks = tm // FFN_HALO
    row = lambda i, j: (i, 0)
    halo = lambda i, j: (jnp.maximum(i * halo_blocks - 1, 0), 0)
    fixed = lambda i, j: (0, 0)
    wa, wb = y_a.shape[1], y_b.shape[1]
    return pl.pallas_call(
        functools.partial(_mix_ffn_kernel, tm=tm, seq=seq, row_chunks=row_chunks, apply_final=apply_final),
        out_shape=jax.ShapeDtypeStruct((T, D), F32),
        grid=(T // tm, n_ff_tiles),
        in_specs=[pl.BlockSpec((tm, D), row), pl.BlockSpec((tm, wa), row), pl.BlockSpec((tm, wb), row),
                  pl.BlockSpec((FFN_HALO, D), halo), pl.BlockSpec((FFN_HALO, wa), halo),
                  pl.BlockSpec((FFN_HALO, wb), halo),
                  pl.BlockSpec(w_o.shape, fixed),
                  pl.BlockSpec((1, D), fixed),
                  pl.BlockSpec((D, 2 * tf), lambda i, j: (0, j)),
                  pl.BlockSpec((CONV_WIDTH, tf), lambda i, j: (0, j)),
                  pl.BlockSpec((1, tf), lambda i, j: (0, j)),
                  pl.BlockSpec((tf, D), lambda i, j: (j, 0)),
                  pl.BlockSpec((1, D), fixed)],
        out_specs=pl.BlockSpec((tm, D), row),
        scratch_shapes=[pltpu.VMEM((tm, D), BF16), pltpu.VMEM((FFN_HALO, D), BF16)],
        compiler_params=pltpu.CompilerParams(dimension_semantics=("parallel", "arbitrary"),
                                             vmem_limit_bytes=VMEM_LIMIT),
        name="mix_convglu_ffn",
    )(x2d, y_a, y_b, x2d, y_a, y_b, w_o, gain, w_up, conv_w, conv_b, w_down, final_gain)


def kernel(x, mix_norm_gain, w_in, rwkv_shift_mix, w0, w_lora_up, a0, a_lora_up, g_lora_up, k_k, k_a, r_k,
           ln_x_w, ln_x_b, attn_norm_gain, w_out, ffn_norm_gain, w_ffn_up, ffn_conv_w, ffn_conv_b,
           w_ffn_down, final_norm_gain):
    B, S, D = x.shape
    depth = w_in.shape[0]
    rw = w0.shape[1]
    aw = attn_norm_gain.shape[1]
    n_w, n_a, n_g = w_lora_up.shape[1], a_lora_up.shape[1], g_lora_up.shape[1]
    n_lora = n_w + n_a + n_g
    lora_pad = -(-n_lora // (2 * LANES)) * (2 * LANES)
    assert lora_pad == 2 * LANES and rw % PAIR == 0 and aw % PAIR == 0
    rwkv_cols = 3 * rw + n_lora
    cos_t, sin_t = _rotary_tables(S)

    x2d = x.reshape(B * S, D)
    for l in range(depth):
        w_r = w_in[l][:, :3 * rw + lora_pad].astype(BF16)
        w_a = w_in[l][:, rwkv_cols:].astype(BF16)
        mix =jnp.concatenate([rwkv_shift_mix[l], jnp.zeros((lora_pad - n_lora,), F32)])[None, :]
        wc = jnp.zeros((lora_pad, 3 * rw), F32)
        wc = wc.at[:n_w, :rw].set(w_lora_up[l])
        wc = wc.at[n_w:n_w + n_a, rw:2 * rw].set(a_lora_up[l])
        wc = wc.at[n_w + n_a:n_lora, 2 * rw:].set(g_lora_up[l]).astype(BF16)

        p_r, q, k, v = _inproj(x2d, mix_norm_gain[l][None, :], w_r, w_a, mix, cos_t, sin_t)
        y_rwkv = _rwkv(p_r.reshape(B, S, -1), wc, w0[l][None, :], a0[l][None, :], k_k[l][None, :],
                       k_a[l][None, :], r_k[l].reshape(1, rw), ln_x_w[l][None, :], ln_x_b[l][None, :],
                       rw, (n_w, n_w + n_a, n_lora))
        y_attn = _attention(q.reshape(B, S, aw), k.reshape(B, S, aw), v.reshape(B, S, aw),
                            attn_norm_gain[l][None, :])
        x2d = _mix_ffn(x2d, y_rwkv.reshape(B * S, rw), y_attn.reshape(B * S, aw), w_out[l].astype(BF16),
                       ffn_norm_gain[l][None, :], w_ffn_up[l].astype(BF16), ffn_conv_w[l],
                       ffn_conv_b[l][None, :], w_ffn_down[l].astype(BF16), final_norm_gain[None, :], S,
                       apply_final=(l == depth - 1))
    return x2d.reshape(B, S, D)
```

```python
import functools
import math

import jax
import jax.numpy as jnp
from jax import lax
from jax.experimental import pallas as pl
from jax.experimental.pallas import tpu as pltpu

F32 = jnp.float32
BF16 = jnp.bfloat16

LANES = 128
HEAD_DIM = 64
PAIR = 2 * HEAD_DIM
ROT_DIM = HEAD_DIM // 4
ROPE_THETA = 500000.0
NORM_EPS = 1e-6
GN_EPS = 64e-5
DILATED_PATTERNS = ((128, 1), (512, 4), (2048, 16))
ATTN_BLOCK = 128
CONV_WIDTH = 3
CHUNK = 64
SOLVE_BLOCK = 16
NEG_BIG = -1e30
VMEM_LIMIT = 56 * 1024 * 1024


def _dot(a, b):
    return jnp.dot(a, b, preferred_element_type=F32)


def _dot_nt(a, b):
    return lax.dot_general(a, b, (((1,), (1,)), ((), ())), preferred_element_type=F32)


def _dot_tn(a, b):
    return lax.dot_general(a, b, (((0,), (0,)), ((), ())), preferred_element_type=F32)


def _rmsnorm(x, gain):
    return x * lax.rsqrt(jnp.mean(x * x, axis=-1, keepdims=True) + NORM_EPS) * gain


def _sigmoid(x):
    return 1.0 / (1.0 + jnp.exp(-x))


def _split_dot(x, w):
    hi = x.astype(BF16)
    lo = (x - hi.astype(F32)).astype(BF16)
    return _dot(hi, w) + _dot(lo, w)


INPROJ_PIECES = 2


def _inproj_kernel(x_ref, g_ref, wr_ref, wa_ref, mix_ref, cos_ref, sin_ref, pr_ref, q_ref, k_ref, v_ref, carry_scr,
                   *, seq_tiles):
    tm = x_ref.shape[0]
    aw = q_ref.shape[1]
    rp = tm // INPROJ_PIECES

    @pl.when(pl.program_id(0) == 0)
    def _():
        carry_scr[...] = jnp.zeros_like(carry_scr)

    row = lax.broadcasted_iota(jnp.int32, (rp, 1), 0)
    lane = lax.broadcasted_iota(jnp.int32, (rp, PAIR), 1)
    first_half = (lane % HEAD_DIM) < (ROT_DIM // 2)
    prev_last = jnp.where(pl.program_id(0) % seq_tiles == 0, 0.0, carry_scr[7:8, :])

    for c in range(INPROJ_PIECES):
        rows = slice(c * rp, (c + 1) * rp)
        h = _rmsnorm(x_ref[rows, :], g_ref[...]).astype(BF16)
        pr = _dot(h, wr_ref[...])
        p = _dot(h, wa_ref[...])
        pr_prev = jnp.where(row == 0, prev_last, pltpu.roll(pr, 1, 0))
        prev_last = pr[rp - 1:rp, :]
        if c == INPROJ_PIECES - 1:
            carry_scr[...] = pr[rp - 8:rp, :]
        pr_ref[rows, :] = pr + (pr_prev - pr) * mix_ref[...]

        cos, sin = cos_ref[rows, :], sin_ref[rows, :]

        def rotary(x):
            partner = jnp.where(first_half, pltpu.roll(x, PAIR - ROT_DIM // 2, 1), pltpu.roll(x, ROT_DIM // 2, 1))
            return x * cos + partner * sin

        for t in range(aw // PAIR):
            lo = t * PAIR
            q_ref[rows, lo:lo + PAIR] = (rotary(p[:, lo:lo + PAIR]) * (1.0 / math.sqrt(HEAD_DIM))).astype(BF16)
            k_ref[rows, lo:lo + PAIR] = rotary(p[:, aw + lo:aw + lo + PAIR]).astype(BF16)
        v_ref[rows, :] = p[:, 2 * aw:].astype(BF16)


def _inproj(x2d, gain, w_r, w_a, mix, cos_t, sin_t, tm=1024):
    T, D = x2d.shape
    rw, aw = w_r.shape[1], w_a.shape[1] // 3
    seq_tiles = cos_t.shape[0] // tm
    row = lambda i: (i, 0)
    fixed = lambda i: (0, 0)
    tab = pl.BlockSpec((tm, PAIR), lambda i: (i % seq_tiles, 0))
    return pl.pallas_call(
        functools.partial(_inproj_kernel, seq_tiles=seq_tiles),
        out_shape=(jax.ShapeDtypeStruct((T, rw), F32),) + (jax.ShapeDtypeStruct((T, aw), BF16),) * 3,
        grid=(T // tm,),
        in_specs=[pl.BlockSpec((tm, D), row), pl.BlockSpec((1, D), fixed), pl.BlockSpec(w_r.shape, fixed),
                  pl.BlockSpec(w_a.shape, fixed), pl.BlockSpec((1, rw), fixed), tab, tab],
        out_specs=(pl.BlockSpec((tm, rw), row),) + (pl.BlockSpec((tm, aw), row),) * 3,
        scratch_shapes=[pltpu.VMEM((8, rw), F32)],
        compiler_params=pltpu.CompilerParams(dimension_semantics=("arbitrary",), vmem_limit_bytes=VMEM_LIMIT),
        name="inproj",
    )(x2d, gain, w_r, w_a, mix, cos_t, sin_t)


def _each(fn, *lists):
    return [fn(*args) for args in zip(*lists)]


def _rwkv_chunk_terms(r, k, v, logw, cum, kkn, bb, same_blk, strict_c, incl_c, lane_lo):
    L = CHUNK
    bf = lambda t: t.astype(BF16)
    cum_last = [t[L - 1:L, :] for t in cum]
    g_in = _each(jnp.exp2, cum)
    g_ex = _each(lambda t, w: jnp.exp2(t - w), cum, logw)
    g_inv = _each(lambda t: jnp.exp2(-t), cum)
    g_hat = _each(lambda tl, t: jnp.exp2(tl - t), cum_last, cum)
    g_last = _each(jnp.exp2, cum_last)
    yield

    def by_head_rows(x):
        lo = lane_lo if x.shape[1] == PAIR else jnp.concatenate([lane_lo] * (x.shape[1] // PAIR), axis=1)
        return jnp.concatenate([jnp.where(lo, x, 0.0), jnp.where(lo, 0.0, x)], axis=0).astype(BF16)

    al = _each(lambda t, g: -t * g, kkn, g_ex)
    rb = _each(jnp.multiply, r, g_in)
    bt = _each(jnp.multiply, bb, g_inv)
    kt = _each(jnp.multiply, k, g_inv)
    bh = _each(jnp.multiply, bb, g_hat)
    kh = _each(jnp.multiply, k, g_hat)
    lhs = _each(lambda a, b: jnp.concatenate([a, b], axis=0).astype(BF16), al, rb)
    rhs = _each(lambda a, b: jnp.concatenate([by_head_rows(a), by_head_rows(b)], axis=0), bt, kt)
    yield
    aq = _each(_dot_nt, lhs, rhs)
    a_ab = [jnp.where(strict_c, t[:L, :2 * L], 0.0) for t in aq]
    a_ak = [jnp.where(strict_c, t[:L, 2 * L:], 0.0).astype(BF16) for t in aq]
    a_rb = [jnp.where(incl_c, t[L:, :2 * L], 0.0).astype(BF16) for t in aq]
    a_rk = [jnp.where(incl_c, t[L:, 2 * L:], 0.0).astype(BF16) for t in aq]
    yield

    v_s = _each(by_head_rows, v)
    a_v = _each(lambda a, b, t: _dot(jnp.concatenate([a, b], axis=0), t), a_ak, a_rk, v_s)
    av = [t[:L] for t in a_v]
    akv = [t[L:] for t in a_v]
    BS = SOLVE_BLOCK
    n_blk = L // BS
    rhs0 = _each(lambda a, b: jnp.concatenate([a, b], axis=1), al, av)
    lo2 = lax.broadcasted_iota(jnp.int32, (BS, 2 * PAIR), 1) % PAIR < HEAD_DIM
    lane = lax.broadcasted_iota(jnp.int32, (BS, PAIR), 1)
    br =lax.broadcasted_iota(jnp.int32, (PAIR, PAIR), 0)
    bc = lax.broadcasted_iota(jnp.int32, (PAIR, PAIR), 1)
    blk_diag = (br // BS) == (bc // BS)

    def expand(pack):
        return jnp.where(blk_diag, jnp.concatenate([pack] * (PAIR // BS), axis=0), 0.0).astype(BF16)

    d_k = [functools.reduce(jnp.add, [jnp.where((lane % HEAD_DIM) // BS == b, t[b * BS:(b + 1) * BS, :], 0.0)
                                      for b in range(n_blk)]) for t in a_ab]
    t_m = d_k
    yield
    n_sq = int(math.log2(BS)) - 1
    d_k = _each(lambda d: _dot(d.astype(BF16), expand(d)), d_k)
    yield
    for lvl in range(n_sq):
        d_e = _each(expand, d_k)
        if lvl + 1 < n_sq:
            td = _each(lambda t, d, de: _dot(jnp.concatenate([t, d], axis=0).astype(BF16), de), t_m, d_k, d_e)
            t_m = _each(lambda t, d, p: t + d + p[:BS], t_m, d_k, td)
            d_k = [p[BS:] for p in td]
        else:
            t_m = _each(lambda t, d, de: t + d + _dot(t.astype(BF16), de), t_m, d_k, d_e)
        yield
    t_mb = _each(bf, t_m)

    zero_blk = jnp.zeros((BS, 2 * PAIR), BF16)
    x_blocks = [[] for _ in a_ab]
    xs_lo = [[] for _ in a_ab]
    xs_hi = [[] for _ in a_ab]

    def stacked(lo_parts, hi_parts):
        rows = [lo_parts.get(c, zero_blk) for c in range(n_blk)] + [hi_parts.get(c, zero_blk) for c in range(n_blk)]
        return jnp.concatenate(rows, axis=0)

    for b in range(n_blk):
        r_b = [t[b * BS:(b + 1) * BS, :] for t in rhs0]
        if b > 0:
            r_b = _each(lambda rr, a, lo_p, hi_p: rr + _dot(a[b * BS:(b + 1) * BS, :].astype(BF16),
                                                            stacked(dict(enumerate(lo_p)), dict(enumerate(hi_p)))),
                        r_b, a_ab, xs_lo, xs_hi)
            yield
        x_b = _each(lambda rr, tm: rr + _dot(tm, stacked({b: jnp.where(lo2, rr, 0.0).astype(BF16)},
                                                        {b: jnp.where(lo2, 0.0, rr).astype(BF16)})),
                    r_b, t_mb)
        for i, xb in enumerate(x_b):
            x_blocks[i].append(xb)
            xs_lo[i].append(jnp.where(lo2, xb, 0.0).astype(BF16))
            xs_hi[i].append(jnp.where(lo2, 0.0, xb).astype(BF16))
        yield
    x = [jnp.concatenate(blks, axis=0) for blks in x_blocks]
    x_s = _each(lambda lo_p, hi_p: jnp.concatenate(lo_p + hi_p, axis=0), xs_lo, xs_hi)

    z = _each(_dot, a_rb, x_s)
    w2 = _each(lambda a, t: (a + t[:, :PAIR]).astype(BF16), rb, z)
    y_loc = _each(lambda t, a: t[:, PAIR:] + a, z, akv)
    w1 = [t[:, :PAIR].astype(BF16) for t in x]
    u_loc = [t[:, PAIR:] for t in x]
    yield

    m_t = _each(lambda a, b: jnp.where(same_blk, _dot_tn(a, b.astype(BF16)), 0.0).astype(BF16), w1, bh)
    s_loc = _each(
        lambda u, vv, b, kk_: jnp.where(
            same_blk,
            _dot_tn(jnp.concatenate([u, vv], axis=0).astype(BF16), jnp.concatenate([b, kk_], axis=0).astype(BF16)),
            0.0),
        u_loc, v, bh, kh)
    return w2, y_loc, m_t, s_loc, g_last


def _rwkv_kernel(p_ref, wc_ref, w0_ref, a0_ref, kk_ref, ka_ref, rk_ref, lnw_ref, lnb_ref,
                 o_ref, s_scr, *, nb, width, n_lora, n_chunks, n_groups, stage_offset):
    L = CHUNK
    R = n_chunks * L
    c = pl.program_id(0)

    @pl.when(c == 0)
    def _():
        s_scr[...] = jnp.zeros_like(s_scr)

    n_pairs = width // PAIR
    ri = lax.broadcasted_iota(jnp.int32, (2 * L, 2 * L), 0)
    ci = lax.broadcasted_iota(jnp.int32, (2 * L, 2 * L), 1)
    same_blk = (ri >= L) == (ci >= L)
    ti = lax.broadcasted_iota(jnp.int32, (L, 2 * L), 0)
    si = lax.broadcasted_iota(jnp.int32, (L, 2 * L), 1) % L
    strict_c = si < ti
    incl_c = si <= ti
    rr = lax.broadcasted_iota(jnp.int32, (R, R), 0)
    rc = lax.broadcasted_iota(jnp.int32, (R, R), 1)
    tri = ((rr // L == rc // L) & (rr >= rc)).astype(BF16)
    lane_lo = lax.broadcasted_iota(jnp.int32, (L, PAIR), 1) < HEAD_DIM
    lane_lo_r = lax.broadcasted_iota(jnp.int32, (R, PAIR), 1) < HEAD_DIM
    lora_lane = lax.broadcasted_iota(jnp.int32, (R, 2 * LANES), 1)

    def head_sums(x):
        s0 = jnp.sum(jnp.where(lane_lo_r, x, 0.0), axis=-1, keepdims=True)
        s1 = jnp.sum(jnp.where(lane_lo_r, 0.0, x), axis=-1, keepdims=True)
        return jnp.where(lane_lo_r, s0, s1)

    def lora_up(pm):
        lora = pm[:, 3 * width:]
        act = jnp.where(lora_lane < n_lora[0], jnp.tanh(lora),
                        jnp.where(lora_lane < n_lora[1], lora,
                                  jnp.where(lora_lane < n_lora[2], _sigmoid(lora), 0.0)))
        return _dot(act.astype(BF16), wc_ref[...])

    def log_decay(up):
        t = w0_ref[...] + up[:, :width]
        return (-math.exp(-0.5) * math.log2(math.e)) / (1.0 + jnp.exp(-t))

    def running_sum(t):
        h1 = t.astype(BF16)
        r1 = t - h1.astype(F32)
        h2 = r1.astype(BF16)
        h3 = (r1 - h2.astype(F32)).astype(BF16)
        return _dot(tri, h1) + _dot(tri, h2) + _dot(tri, h3)

    tiles = [(b, pr) for b in range(nb) for pr in range(n_pairs)]
    n_t = len(tiles)
    lanes = lambda pr, part=0: slice(part * width + pr * PAIR, part * width + (pr + 1) * PAIR)
    sls = [lanes(pr) for _, pr in tiles]

    states = {0: [s_scr[i] for i in range(n_t)]}
    outs = {}

    def run_group(gi):
        g0 = gi * R
        pms = [p_ref[b, g0:g0 + R, :] for b in range(nb)]
        yield
        ups = _each(lora_up, pms)
        logw_b = _each(log_decay, ups)
        cum_b = _each(running_sum, logw_b)
        yield

        r = [pms[b][:, lanes(pr, 0)] for b, pr in tiles]
        k_raw = [pms[b][:, lanes(pr, 1)] for b, pr in tiles]
        v = [pms[b][:, lanes(pr, 2)] for b, pr in tiles]
        logw = [logw_b[b][:, lanes(pr)] for b, pr in tiles]
        cum = [cum_b[b][:, lanes(pr)] for b, pr in tiles]
        a = [_sigmoid(a0_ref[:, lanes(pr)] + ups[b][:, lanes(pr, 1)]) for b, pr in tiles]
        g = [ups[b][:, lanes(pr, 2)] for b, pr in tiles]

        kk = _each(lambda t, sl: t * kk_ref[:, sl], k_raw, sls)
        kk_ss = _each(lambda t: head_sums(t * t), kk)
        kkn = _each(lambda t, ss: t * lax.rsqrt(jnp.maximum(ss, 1e-24)), kk, kk_ss)
        k = _each(lambda t, aa, sl: t * (1.0 + (aa - 1.0) * ka_ref[:, sl]), k_raw, a, sls)
        bb = _each(jnp.multiply, kkn, a)
        bonus_dot = _each(lambda rr_, kk_, sl: head_sums(rr_ * kk_ * rk_ref[:, sl]), r, k, sls)
        yield

        def chunks(ts):
            return [t[ch * L:(ch + 1) * L] for ch in range(n_chunks) for t in ts]

        w2, y_loc, m_t, s_loc, g_last = yield from _rwkv_chunk_terms(
            chunks(r), chunks(k), chunks(v), chunks(logw), chunks(cum), chunks(kkn), chunks(bb),
            same_blk, strict_c, incl_c, lane_lo)
        yield

        s = states[gi]
        y_parts = []
        for ch in range(n_chunks):
            sel = slice(ch * n_t, (ch + 1) * n_t)
            s_b = _each(lambda t: t.astype(BF16), s)
            y_parts.append(_each(lambda a_, sb, yl: _dot_nt(a_, sb) + yl, w2[sel], s_b, y_loc[sel]))
            s = _each(lambda s0, gl, sb, m, sl: s0 * gl + _dot(sb, m) + sl, s, g_last[sel], s_b, m_t[sel], s_loc[sel])
        states[gi + 1] = s
        y = [jnp.concatenate([y_parts[ch][i] for ch in range(n_chunks)], axis=0) for i in range(n_t)]
        yield

        mu = _each(lambda t: head_sums(t) * (1.0 / HEAD_DIM), y)
        yc = _each(jnp.subtract, y, mu)
        var = _each(lambda t: head_sums(t * t) * (1.0 / HEAD_DIM), yc)
        yield
        res = []
        for i in range(n_t):
            sl = sls[i]
            yn = yc[i] * lax.rsqrt(var[i] + GN_EPS) * lnw_ref[:, sl] + lnb_ref[:, sl]
            res.append(((yn + bonus_dot[i] * v[i]) * g[i]).astype(o_ref.dtype))
        outs[gi] = res

    gens = [run_group(gi) for gi in range(n_groups)]
    live = [True] * n_groups
    tick = 0
    while any(live):
        for gi in range(n_groups):
            if live[gi] and tick >= gi * stage_offset:
                try:
                    next(gens[gi])
                except StopIteration:
                    live[gi] = False
        tick += 1

    for gi in range(n_groups):
        for i, (b, pr) in enumerate(tiles):
            o_ref[b, gi * R:(gi + 1) * R, sls[i]] = outs[gi][i]
    for i in range(n_t):
        s_scr[i] = states[n_groups][i]


def _rwkv(p_r, wc, w0, a0, k_k, k_a, r_k, ln_w, ln_b, width, n_lora, n_chunks=2, n_groups=4, stage_offset=5):
    B, S, C = p_r.shape
    L = CHUNK * n_chunks * n_groups
    n_state = B * (width // PAIR)
    vec = lambda n: pl.BlockSpec((1, n), lambda c: (0, 0))
    return pl.pallas_call(
        functools.partial(_rwkv_kernel, nb=B, width=width, n_lora=n_lora, n_chunks=n_chunks, n_groups=n_groups,
                          stage_offset=stage_offset),
        out_shape=jax.ShapeDtypeStruct((B, S, width), BF16),
        grid=(S // L,),
        in_specs=[pl.BlockSpec((B, L, C), lambda c: (0, c, 0)),
                  pl.BlockSpec(wc.shape, lambda c: (0, 0))] + [vec(width)] * 7,
        out_specs=pl.BlockSpec((B, L, width), lambda c: (0, c, 0)),
        scratch_shapes=[pltpu.VMEM((n_state, PAIR, PAIR), F32)],
        compiler_params=pltpu.CompilerParams(dimension_semantics=("arbitrary",), vmem_limit_bytes=VMEM_LIMIT),
        name="rwkv7",
    )(p_r, wc, w0, a0, k_k, k_a, r_k, ln_w, ln_b)


SB_ROWS = 2048
SB_QUARTER = SB_ROWS // 4
N_RES = 4


def _attn_segments(pi, g):
    if pi == 0:
        return [((g // 4) * SB_QUARTER + r * ATTN_BLOCK + (ATTN_BLOCK // N_RES) * (g % 4), ATTN_BLOCK // N_RES, 1)
                for r in range(N_RES)]
    if pi == 1:
        return [((g // 4) * SB_QUARTER + (g % 4) * ATTN_BLOCK, ATTN_BLOCK, 1)]
    return [(qq * SB_QUARTER + (g % 4) * ATTN_BLOCK + g // 4, ATTN_BLOCK // 4, 4) for qq in range(4)]


def _attn_prev_tile(pi, g):
    if pi == 0:
        return (g + 15) % 16, g == 0
    if pi == 1:
        return ((g // 4 + 3) % 4) * 4 + g % 4, g < 4
    return g, True


def _attn_kernel(q_ref, k_ref, v_ref, gain_ref, o_ref,
                 nat_q, nat_k, nat_v, q_scr, k_ring, v_ring, bias_scr, out_nat, *stat_scr, tiles_per_group):
    sb = pl.program_id(2)
    n_pat = len(DILATED_PATTERNS)
    assert DILATED_PATTERNS == ((128, 1), (512, 4), (2048, 16)) and ATTN_BLOCK == 128
    num_scr, m_scr, l_scr = stat_scr[:n_pat], stat_scr[n_pat:2 * n_pat], stat_scr[2 * n_pat:]
    Q = ATTN_BLOCK
    cur_base = (sb % 2) * SB_ROWS
    other_base = SB_ROWS - cur_base

    @pl.when(sb == 0)
    def _():
        k_ring[pl.ds(SB_ROWS, SB_ROWS), :] = jnp.zeros((SB_ROWS, PAIR), F32)
        v_ring[pl.ds(SB_ROWS, SB_ROWS), :] = jnp.zeros((SB_ROWS, PAIR), F32)

    nat_q[...] = q_ref[...].astype(F32)
    nat_k[...] = k_ref[...].astype(F32)
    nat_v[...] = v_ref[...].astype(F32)
    for qq in range(4):
        for r in range(N_RES):
            src = pl.ds(qq * SB_QUARTER + r, Q, stride=N_RES)
            dst = qq * SB_QUARTER + r * Q
            q_scr[dst:dst + Q, :] = nat_q[src, :]
            k_ring[pl.ds(cur_base + dst, Q), :] = nat_k[src, :]
            v_ring[pl.ds(cur_base + dst, Q), :] = nat_v[src, :]

    @pl.when(sb == 0)
    def _():
        ii = lax.broadcasted_iota(jnp.int32, (2 * Q, 2 * Q), 0) % Q
        cj = lax.broadcasted_iota(jnp.int32, (2 * Q, 2 * Q), 1)
        per = Q // N_RES
        for kind in range(2):
            if kind == 0:
                qi = N_RES * (ii % per) + ii // per
                kj = N_RES * ((cj % Q) % per) + (cj % Q) // per + (cj // Q) * Q
            else:
                qi, kj = ii, cj
            band = (kj >= qi) & (kj <= qi + Q)
            bias_scr[kind, 1] = jnp.where(band, 0.0, NEG_BIG)
            bias_scr[kind, 0] = jnp.where(band & (cj >= Q), 0.0, NEG_BIG)

    def load_tile(ref, base, segs):
        parts = [ref[pl.ds(base + s, n, stride=st) if st > 1 else pl.ds(base + s, n), :] for s, n, st in segs]
        return parts[0] if len(parts) == 1 else jnp.concatenate(parts, axis=0)

    def store_tile(ref, segs, val):
        off = 0
        for s, n, st in segs:
            ref[pl.ds(s, n, stride=st) if st > 1 else pl.ds(s, n), :] = val[off:off + n]
            off += n

    lane_lo = lax.broadcasted_iota(jnp.int32, (Q, PAIR), 1) < HEAD_DIM
    ones_blk = jnp.ones((2 * Q, PAIR), BF16)

    n_blk = SB_ROWS // Q
    has_prev_sb = jnp.where(sb > 0, 1, 0)

    def scores(pi, gs):
        segs = [_attn_segments(pi, g) for g in gs]
        prev = [_attn_prev_tile(pi, g) for g in gs]
        prev_segs = [_attn_segments(pi, pg) for pg, _ in prev]
        prev_base = [other_base if other else cur_base for _, other in prev]
        q2 = [load_tile(q_scr, 0, sg) for sg in segs]
        q2 = [jnp.concatenate([jnp.where(lane_lo, t, 0.0), jnp.where(lane_lo, 0.0, t)], axis=0).astype(BF16)
              for t in q2]
        kcat = [jnp.concatenate([load_tile(k_ring, pb, psg), load_tile(k_ring, cur_base, sg)], axis=0).astype(BF16)
                for pb, psg, sg in zip(prev_base, prev_segs, segs)]
        vcat = [jnp.concatenate([load_tile(v_ring, pb, psg), load_tile(v_ring, cur_base, sg)], axis=0).astype(BF16)
                for pb, psg, sg in zip(prev_base, prev_segs, segs)]
        vext = [jnp.concatenate([t, ones_blk], axis=1) for t in vcat]
        has_prev = [has_prev_sb if other else 1 for _, other in prev]
        return dict(pi=pi, segs=segs, s=_each(_dot_nt, q2, kcat), vext=vext, has_prev=has_prev)

    def softmax(c):
        kind = 0 if c["pi"] == 0 else 1
        s = _each(lambda t, hp: t + bias_scr[kind, hp], c["s"], c["has_prev"])
        m = [jnp.max(t, axis=-1, keepdims=True) for t in s]
        p = _each(lambda t, mm: jnp.exp(t - mm).astype(BF16), s, m)
        return dict(pi=c["pi"], segs=c["segs"], vext=c["vext"], m=m, p=p)

    def weighted_values(c):
        pi, segs, m = c["pi"], c["segs"], c["m"]
        nl = _each(_dot, c["p"], c["vext"])
        for t in range(len(segs)):
            store_tile(num_scr[pi], segs[t], jnp.where(lane_lo, nl[t][:Q, :PAIR], nl[t][Q:, :PAIR]))
            store_tile(l_scr[pi], segs[t], jnp.where(lane_lo, nl[t][:Q, PAIR:], nl[t][Q:, PAIR:]))
            store_tile(m_scr[pi], segs[t], jnp.where(lane_lo, m[t][:Q], m[t][Q:]))

    groups = [(pi, list(range(g0, g0 + tiles_per_group)))
              for pi in range(n_pat) for g0 in range(0, n_blk, tiles_per_group)]
    n_grp = len(groups)
    sc = {0: scores(*groups[0])}
    if n_grp > 1:
        sc[1] = scores(*groups[1])
    sm = {0: softmax(sc.pop(0))}
    for k in range(n_grp):
        if k + 2 < n_grp:
            sc[k + 2] = scores(*groups[k + 2])
        if k + 1 < n_grp:
            sm[k + 1] = softmax(sc.pop(k + 1))
        weighted_values(sm.pop(k))

    ri = lax.broadcasted_iota(jnp.int32, (PAIR, PAIR), 0)
    ci = lax.broadcasted_iota(jnp.int32, (PAIR, PAIR), 1)
    seg_ones = ((ri >= HEAD_DIM) == (ci >= HEAD_DIM)).astype(BF16)
    gain = gain_ref[...]

    def merge(i, carry):
        rows = pl.ds(pl.multiple_of(i * Q, Q), Q)
        ms = [m_scr[pi][rows, :] for pi in range(n_pat)]
        m_all = functools.reduce(jnp.maximum, ms)
        num = 0.0
        den = 0.0
        for pi in range(n_pat):
            wgt = jnp.exp(ms[pi] - m_all)
            num = num + wgt * num_scr[pi][rows, :]
            den = den + wgt * l_scr[pi][rows, :]
        o = num / den
        ms_o = _split_dot(o * o, seg_ones) * (1.0 / HEAD_DIM)
        out_nat[pl.ds((i // N_RES) * SB_QUARTER + i % N_RES, Q, stride=N_RES), :] = o * lax.rsqrt(ms_o + NORM_EPS) * gain
        return carry

    lax.fori_loop(0, SB_ROWS // Q, merge, 0, unroll=4)
    o_ref[...] = out_nat[...].astype(o_ref.dtype)


def _attention(q, k, v, gain, tiles_per_group=4):
    B, S, W = q.shape
    n_pairs = W // PAIR
    n_pat = len(DILATED_PATTERNS)
    blk = pl.BlockSpec((None, SB_ROWS, PAIR), lambda b, p, s: (b, s, p))
    tile = pltpu.VMEM((SB_ROWS, PAIR), F32)
    return pl.pallas_call(
        functools.partial(_attn_kernel, tiles_per_group=tiles_per_group),
        out_shape=jax.ShapeDtypeStruct((B, S, W), BF16),
        grid=(B, n_pairs, S // SB_ROWS),
        in_specs=[blk, blk, blk, pl.BlockSpec((1, PAIR), lambda b, p, s: (0, p))],
        out_specs=blk,
        scratch_shapes=[tile, tile, tile, tile,
                        pltpu.VMEM((2 * SB_ROWS, PAIR), F32),
                        pltpu.VMEM((2 * SB_ROWS, PAIR), F32),
                        pltpu.VMEM((2, 2, 2 * ATTN_BLOCK, 2 * ATTN_BLOCK), F32),
                        tile]
                       + [tile] * (3 * n_pat),
        compiler_params=pltpu.CompilerParams(dimension_semantics=("parallel", "parallel", "arbitrary"),
                                             vmem_limit_bytes=VMEM_LIMIT),
        name="dilated_attn",
    )(q, k, v, gain)


def _rotary_tables(seq):
    half = ROT_DIM // 2
    inv_freq = ROPE_THETA ** (-jnp.arange(half, dtype=F32) * 2.0 / ROT_DIM)
    ang = jnp.arange(seq).astype(F32)[:, None] * inv_freq[None, :]
    cos, sin = jnp.cos(ang), jnp.sin(ang)
    rest = HEAD_DIM - ROT_DIM
    cos_h = jnp.concatenate([cos, cos, jnp.ones((seq, rest), F32)], axis=-1)
    sin_h = jnp.concatenate([-sin, sin, jnp.zeros((seq, rest), F32)], axis=-1)
    return jnp.tile(cos_h, (1, PAIR // HEAD_DIM)), jnp.tile(sin_h, (1, PAIR // HEAD_DIM))


FFN_HALO = 16
MIX_PIECES = 4


def _mix_ffn_kernel(x_ref, ya_ref, yb_ref, xh_ref, yah_ref, ybh_ref, wo_ref, g_ref, wu_ref, cw_ref,
                    cb_ref, wd_ref, fg_ref, o_ref, h_scr, hh_scr, *, tm, seq, row_chunks, apply_final):
    tf = wd_ref.shape[0]
    i = pl.program_id(0)
    j = pl.program_id(1)
    wa = ya_ref.shape[1]

    def mixed(x, ya, yb):
        return x + _dot(ya, wo_ref[:wa, :]) + _dot(yb, wo_ref[wa:, :])

    @pl.when(j == 0)
    def _():
        hh_scr[...] = _rmsnorm(mixed(xh_ref[...], yah_ref[...], ybh_ref[...]), g_ref[...]).astype(BF16)
        piece = tm // MIX_PIECES
        for c in range(MIX_PIECES):
            rows = slice(c * piece, (c + 1) * piece)
            x1 = mixed(x_ref[rows, :], ya_ref[rows, :], yb_ref[rows, :])
            o_ref[rows, :] = x1
            h_scr[rows, :] = _rmsnorm(x1, g_ref[...]).astype(BF16)

    seq_start = (i * tm) % seq == 0
    rc = tm // row_chunks
    row = lax.broadcasted_iota(jnp.int32, (rc, 1), 0)
    for c in range(row_chunks):
        rows = slice(c * rc, (c + 1) * rc)
        gv = _dot(h_scr[rows, :], wu_ref[...])
        gate, val = gv[:, :tf], gv[:, tf:]
        if c == 0:
            gate_h = jnp.where(seq_start, 0.0, _dot(hh_scr[...], wu_ref[:, :tf]))
        else:
            gate_h = _dot(h_scr[c * rc - FFN_HALO:c * rc, :], wu_ref[:, :tf])
        g1 = jnp.where(row == 0, gate_h[FFN_HALO - 1:FFN_HALO, :], pltpu.roll(gate, 1, 0))
        g2 = jnp.where(row == 0, gate_h[FFN_HALO - 2:FFN_HALO - 1, :],
                       jnp.where(row == 1, gate_h[FFN_HALO - 1:FFN_HALO, :], pltpu.roll(gate, 2, 0)))
        u = cw_ref[0:1, :] * g2 + cw_ref[1:2, :] * g1 + cw_ref[2:3, :] * gate + cb_ref[...]
        act = (u * _sigmoid(u) * val).astype(BF16)
        o_ref[rows, :] += _dot(act, wd_ref[...])

    if apply_final:
        @pl.when(j == pl.num_programs(1) - 1)
        def _():
            o_ref[...] = _rmsnorm(o_ref[...], fg_ref[...])


def _mix_ffn(x2d, y_a, y_b, w_o, gain, w_up, conv_w, conv_b, w_down, final_gain, seq, apply_final,
             tm=1024, n_ff_tiles=2, row_chunks=1):
    T, D = x2d.shape
    F = w_down.shape[0]
    tf = F // n_ff_tiles
    assert tf % LANES == 0 and seq % tm == 0 and w_up.shape[1] == 2 * F
    w_up = jnp.concatenate([w_up[:, half * F + j * tf: half * F + (j + 1) * tf]
                            for j in range(n_ff_tiles) for half in range(2)], axis=1)
    halo_blocks = tm // FFN_HALO
    row = lambda i, j: (i, 0)
    halo = lambda i, j: (jnp.maximum(i * halo_blocks - 1, 0), 0)
    fixed = lambda i, j: (0, 0)
    wa, wb = y_a.shape[1], y_b.shape[1]
    return pl.pallas_call(
        functools.partial(_mix_ffn_kernel, tm=tm, seq=seq, row_chunks=row_chunks, apply_final=apply_final),
        out_shape=jax.ShapeDtypeStruct((T, D), F32),
        grid=(T // tm, n_ff_tiles),
        in_specs=[pl.BlockSpec((tm, D), row), pl.BlockSpec((tm, wa), row), pl.BlockSpec((tm, wb), row),
                  pl.BlockSpec((FFN_HALO, D), halo), pl.BlockSpec((FFN_HALO, wa), halo),
                  pl.BlockSpec((FFN_HALO, wb), halo),
                  pl.BlockSpec(w_o.shape, fixed),
                  pl.BlockSpec((1, D), fixed),
                  pl.BlockSpec((D, 2 * tf), lambda i, j: (0, j)),
                  pl.BlockSpec((CONV_WIDTH, tf), lambda i, j: (0, j)),
                  pl.BlockSpec((1, tf), lambda i, j: (0, j)),
                  pl.BlockSpec((tf, D), lambda i, j: (j, 0)),
                  pl.BlockSpec((1, D), fixed)],
        out_specs=pl.BlockSpec((tm, D), row),
        scratch_shapes=[pltpu.VMEM((tm, D), BF16), pltpu.VMEM((FFN_HALO, D), BF16)],
        compiler_params=pltpu.CompilerParams(dimension_semantics=("parallel", "arbitrary"),
                                             vmem_limit_bytes=VMEM_LIMIT),
        name="mix_convglu_ffn",
    )(x2d, y_a, y_b, x2d, y_a, y_b, w_o, gain, w_up, conv_w, conv_b, w_down, final_gain)


def kernel(x, mix_norm_gain, w_in, rwkv_shift_mix, w0, w_lora_up, a0, a_lora_up, g_lora_up, k_k, k_a, r_k,
           ln_x_w, ln_x_b, attn_norm_gain, w_out, ffn_norm_gain, w_ffn_up, ffn_conv_w, ffn_conv_b,
           w_ffn_down, final_norm_gain):
    B, S, D = x.shape
    depth = w_in.shape[0]
    rw = w0.shape[1]
    aw = attn_norm_gain.shape[1]
    n_w, n_a, n_g = w_lora_up.shape[1], a_lora_up.shape[1], g_lora_up.shape[1]
    n_lora = n_w + n_a + n_g
    lora_pad = -(-n_lora // (2 * LANES)) * (2 * LANES)
    assert lora_pad == 2 * LANES and rw % PAIR == 0 and aw % PAIR == 0
    rwkv_cols = 3 * rw + n_lora
    cos_t, sin_t = _rotary_tables(S)

    x2d = x.reshape(B * S, D)
    for l in range(depth):
        w_r = w_in[l][:, :3 * rw + lora_pad].astype(BF16)
        w_a = w_in[l][:, rwkv_cols:].astype(BF16)
        mix =jnp.concatenate([rwkv_shift_mix[l], jnp.zeros((lora_pad - n_lora,), F32)])[None, :]
        wc = jnp.zeros((lora_pad, 3 * rw), F32)
        wc = wc.at[:n_w, :rw].set(w_lora_up[l])
        wc = wc.at[n_w:n_w + n_a, rw:2 * rw].set(a_lora_up[l])
        wc = wc.at[n_w + n_a:n_lora, 2 * rw:].set(g_lora_up[l]).astype(BF16)

        p_r, q, k, v = _inproj(x2d, mix_norm_gain[l][None, :], w_r, w_a, mix, cos_t, sin_t)
        y_rwkv = _rwkv(p_r.reshape(B, S, -1), wc, w0[l][None, :], a0[l][None, :], k_k[l][None, :],
                       k_a[l][None, :], r_k[l].reshape(1, rw), ln_x_w[l][None, :], ln_x_b[l][None, :],
                       rw, (n_w, n_w + n_a, n_lora))
        y_attn = _attention(q.reshape(B, S, aw), k.reshape(B, S, aw), v.reshape(B, S, aw),
                            attn_norm_gain[l][None, :])
        x2d = _mix_ffn(x2d, y_rwkv.reshape(B * S, rw), y_attn.reshape(B * S, aw), w_out[l].astype(BF16),
                       ffn_norm_gain[l][None, :], w_ffn_up[l].astype(BF16), ffn_conv_w[l],
                       ffn_conv_b[l][None, :], w_ffn_down[l].astype(BF16), final_norm_gain[None, :], S,
                       apply_final=(l == depth - 1))
    return x2d.reshape(B, S, D)
```

```python
import functools
import math

import jax
import jax.numpy as jnp
from jax import lax
from jax.experimental import pallas as pl
from jax.experimental.pallas import tpu as pltpu

F32 = jnp.float32
BF16 = jnp.bfloat16

LANES = 128
HEAD_DIM = 64
PAIR = 2 * HEAD_DIM
ROT_DIM = HEAD_DIM // 4
ROPE_THETA = 500000.0
NORM_EPS = 1e-6
GN_EPS = 64e-5
DILATED_PATTERNS = ((128, 1), (512, 4), (2048, 16))
ATTN_BLOCK = 128
CONV_WIDTH = 3
CHUNK = 64
SOLVE_BLOCK = 16
NEG_BIG = -1e30
VMEM_LIMIT = 56 * 1024 * 1024


def _dot(a, b):
    return jnp.dot(a, b, preferred_element_type=F32)


def _dot_nt(a, b):
    return lax.dot_general(a, b, (((1,), (1,)), ((), ())), preferred_element_type=F32)


def _dot_tn(a, b):
    return lax.dot_general(a, b, (((0,), (0,)), ((), ())), preferred_element_type=F32)


def _rmsnorm(x, gain):
    return x * lax.rsqrt(jnp.mean(x * x, axis=-1, keepdims=True) + NORM_EPS) * gain


def _sigmoid(x):
    return 1.0 / (1.0 + jnp.exp(-x))


def _split_dot(x, w):
    hi = x.astype(BF16)
    lo = (x - hi.astype(F32)).astype(BF16)
    return _dot(hi, w) + _dot(lo, w)


INPROJ_PIECES = 2


def _inproj_kernel(x_ref, g_ref, wr_ref, wa_ref, mix_ref, cos_ref, sin_ref, pr_ref, q_ref, k_ref, v_ref, carry_scr,
                   *, seq_tiles):
    tm = x_ref.shape[0]
    aw = q_ref.shape[1]
    rp = tm // INPROJ_PIECES

    @pl.when(pl.program_id(0) == 0)
    def _():
        carry_scr[...] = jnp.zeros_like(carry_scr)

    row = lax.broadcasted_iota(jnp.int32, (rp, 1), 0)
    lane = lax.broadcasted_iota(jnp.int32, (rp, PAIR), 1)
    first_half = (lane % HEAD_DIM) < (ROT_DIM // 2)
    prev_last = jnp.where(pl.program_id(0) % seq_tiles == 0, 0.0, carry_scr[7:8, :])

    for c in range(INPROJ_PIECES):
        rows = slice(c * rp, (c + 1) * rp)
        h = _rmsnorm(x_ref[rows, :], g_ref[...]).astype(BF16)
        pr = _dot(h, wr_ref[...])
        p = _dot(h, wa_ref[...])
        pr_prev = jnp.where(row == 0, prev_last, pltpu.roll(pr, 1, 0))
        prev_last = pr[rp - 1:rp, :]
        if c == INPROJ_PIECES - 1:
            carry_scr[...] = pr[rp - 8:rp, :]
        pr_ref[rows, :] = pr + (pr_prev - pr) * mix_ref[...]

        cos, sin = cos_ref[rows, :], sin_ref[rows, :]

        def rotary(x):
            partner = jnp.where(first_half, pltpu.roll(x, PAIR - ROT_DIM // 2, 1), pltpu.roll(x, ROT_DIM // 2, 1))
            return x * cos + partner * sin

        for t in range(aw // PAIR):
            lo = t * PAIR
            q_ref[rows, lo:lo + PAIR] = (rotary(p[:, lo:lo + PAIR]) * (1.0 / math.sqrt(HEAD_DIM))).astype(BF16)
            k_ref[rows, lo:lo + PAIR] = rotary(p[:, aw + lo:aw + lo + PAIR]).astype(BF16)
        v_ref[rows, :] = p[:, 2 * aw:].astype(BF16)


def _inproj(x2d, gain, w_r, w_a, mix, cos_t, sin_t, tm=1024):
    T, D = x2d.shape
    rw, aw = w_r.shape[1], w_a.shape[1] // 3
    seq_tiles = cos_t.shape[0] // tm
    row = lambda i: (i, 0)
    fixed = lambda i: (0, 0)
    tab = pl.BlockSpec((tm, PAIR), lambda i: (i % seq_tiles, 0))
    return pl.pallas_call(
        functools.partial(_inproj_kernel, seq_tiles=seq_tiles),
        out_shape=(jax.ShapeDtypeStruct((T, rw), F32),) + (jax.ShapeDtypeStruct((T, aw), BF16),) * 3,
        grid=(T // tm,),
        in_specs=[pl.BlockSpec((tm, D), row), pl.BlockSpec((1, D), fixed), pl.BlockSpec(w_r.shape, fixed),
                  pl.BlockSpec(w_a.shape, fixed), pl.BlockSpec((1, rw), fixed), tab, tab],
        out_specs=(pl.BlockSpec((tm, rw), row),) + (pl.BlockSpec((tm, aw), row),) * 3,
        scratch_shapes=[pltpu.VMEM((8, rw), F32)],
        compiler_params=pltpu.CompilerParams(dimension_semantics=("arbitrary",), vmem_limit_bytes=VMEM_LIMIT),
        name="inproj",
    )(x2d, gain, w_r, w_a, mix, cos_t, sin_t)


def _each(fn, *lists):
    return [fn(*args) for args in zip(*lists)]


def _rwkv_chunk_terms(r, k, v, logw, cum, kkn, bb, same_blk, strict_c, incl_c, lane_lo):
    L = CHUNK
    bf = lambda t: t.astype(BF16)
    cum_last = [t[L - 1:L, :] for t in cum]
    g_in = _each(jnp.exp2, cum)
    g_ex = _each(lambda t, w: jnp.exp2(t - w), cum, logw)
    g_inv = _each(lambda t: jnp.exp2(-t), cum)
    g_hat = _each(lambda tl, t: jnp.exp2(tl - t), cum_last, cum)
    g_last = _each(jnp.exp2, cum_last)
    yield

    def by_head_rows(x):
        lo = lane_lo if x.shape[1] == PAIR else jnp.concatenate([lane_lo] * (x.shape[1] // PAIR), axis=1)
        return jnp.concatenate([jnp.where(lo, x, 0.0), jnp.where(lo, 0.0, x)], axis=0).astype(BF16)

    al = _each(lambda t, g: -t * g, kkn, g_ex)
    rb = _each(jnp.multiply, r, g_in)
    bt = _each(jnp.multiply, bb, g_inv)
    kt = _each(jnp.multiply, k, g_inv)
    bh = _each(jnp.multiply, bb, g_hat)
    kh = _each(jnp.multiply, k, g_hat)
    lhs = _each(lambda a, b: jnp.concatenate([a, b], axis=0).astype(BF16), al, rb)
    rhs = _each(lambda a, b: jnp.concatenate([by_head_rows(a), by_head_rows(b)], axis=0), bt, kt)
    yield
    aq = _each(_dot_nt, lhs, rhs)
    a_ab = [jnp.where(strict_c, t[:L, :2 * L], 0.0) for t in aq]
    a_ak = [jnp.where(strict_c, t[:L, 2 * L:], 0.0).astype(BF16) for t in aq]
    a_rb = [jnp.where(incl_c, t[L:, :2 * L], 0.0).astype(BF16) for t in aq]
    a_rk = [jnp.where(incl_c, t[L:, 2 * L:], 0.0).astype(BF16) for t in aq]
    yield

    v_s = _each(by_head_rows, v)
    a_v = _each(lambda a, b, t: _dot(jnp.concatenate([a, b], axis=0), t), a_ak, a_rk, v_s)
    av = [t[:L] for t in a_v]
    akv = [t[L:] for t in a_v]
    BS = SOLVE_BLOCK
    n_blk = L // BS
    rhs0 = _each(lambda a, b: jnp.concatenate([a, b], axis=1), al, av)
    lo2 = lax.broadcasted_iota(jnp.int32, (BS, 2 * PAIR), 1) % PAIR < HEAD_DIM
    lane = lax.broadcasted_iota(jnp.int32, (BS, PAIR), 1)
    br =lax.broadcasted_iota(jnp.int32, (PAIR, PAIR), 0)
    bc = lax.broadcasted_iota(jnp.int32, (PAIR, PAIR), 1)
    blk_diag = (br // BS) == (bc // BS)

    def expand(pack):
        return jnp.where(blk_diag, jnp.concatenate([pack] * (PAIR // BS), axis=0), 0.0).astype(BF16)

    d_k = [functools.reduce(jnp.add, [jnp.where((lane % HEAD_DIM) // BS == b, t[b * BS:(b + 1) * BS, :], 0.0)
                                      for b in range(n_blk)]) for t in a_ab]
    t_m = d_k
    yield
    n_sq = int(math.log2(BS)) - 1
    d_k = _each(lambda d: _dot(d.astype(BF16), expand(d)), d_k)
    yield
    for lvl in range(n_sq):
        d_e = _each(expand, d_k)
        if lvl + 1 < n_sq:
            td = _each(lambda t, d, de: _dot(jnp.concatenate([t, d], axis=0).astype(BF16), de), t_m, d_k, d_e)
            t_m = _each(lambda t, d, p: t + d + p[:BS], t_m, d_k, td)
            d_k = [p[BS:] for p in td]
        else:
            t_m = _each(lambda t, d, de: t + d + _dot(t.astype(BF16), de), t_m, d_k, d_e)
        yield
    t_mb = _each(bf, t_m)

    zero_blk = jnp.zeros((BS, 2 * PAIR), BF16)
    x_blocks = [[] for _ in a_ab]
    xs_lo = [[] for _ in a_ab]
    xs_hi = [[] for _ in a_ab]

    def stacked(lo_parts, hi_parts):
        rows = [lo_parts.get(c, zero_blk) for c in range(n_blk)] + [hi_parts.get(c, zero_blk) for c in range(n_blk)]
        return jnp.concatenate(rows, axis=0)

    for b in range(n_blk):
        r_b = [t[b * BS:(b + 1) * BS, :] for t in rhs0]
        if b > 0:
            r_b = _each(lambda rr, a, lo_p, hi_p: rr + _dot(a[b * BS:(b + 1) * BS, :].astype(BF16),
                                                            stacked(dict(enumerate(lo_p)), dict(enumerate(hi_p)))),
                        r_b, a_ab, xs_lo, xs_hi)
            yield
        x_b = _each(lambda rr, tm: rr + _dot(tm, stacked({b: jnp.where(lo2, rr, 0.0).astype(BF16)},
                                                        {b: jnp.where(lo2, 0.0, rr).astype(BF16)})),
                    r_b, t_mb)
        for i, xb in enumerate(x_b):
            x_blocks[i].append(xb)
            xs_lo[i].append(jnp.where(lo2, xb, 0.0).astype(BF16))
            xs_hi[i].append(jnp.where(lo2, 0.0, xb).astype(BF16))
        yield
    x = [jnp.concatenate(blks, axis=0) for blks in x_blocks]
    x_s = _each(lambda lo_p, hi_p: jnp.concatenate(lo_p + hi_p, axis=0), xs_lo, xs_hi)

    z = _each(_dot, a_rb, x_s)
    w2 = _each(lambda a, t: (a + t[:, :PAIR]).astype(BF16), rb, z)
    y_loc = _each(lambda t, a: t[:, PAIR:] + a, z, akv)
    w1 = [t[:, :PAIR].astype(BF16) for t in x]
    u_loc = [t[:, PAIR:] for t in x]
    yield

    m_t = _each(lambda a, b: jnp.where(same_blk, _dot_tn(a, b.astype(BF16)), 0.0).astype(BF16), w1, bh)
    s_loc = _each(
        lambda u, vv, b, kk_: jnp.where(
            same_blk,
            _dot_tn(jnp.concatenate([u, vv], axis=0).astype(BF16), jnp.concatenate([b, kk_], axis=0).astype(BF16)),
            0.0),
        u_loc, v, bh, kh)
    return w2, y_loc, m_t, s_loc, g_last


def _rwkv_kernel(p_ref, wc_ref, w0_ref, a0_ref, kk_ref, ka_ref, rk_ref, lnw_ref, lnb_ref,
                 o_ref, s_scr, *, nb, width, n_lora, n_chunks, n_groups, stage_offset):
    L = CHUNK
    R = n_chunks * L
    c = pl.program_id(0)

    @pl.when(c == 0)
    def _():
        s_scr[...] = jnp.zeros_like(s_scr)

    n_pairs = width // PAIR
    ri = lax.broadcasted_iota(jnp.int32, (2 * L, 2 * L), 0)
    ci = lax.broadcasted_iota(jnp.int32, (2 * L, 2 * L), 1)
    same_blk = (ri >= L) == (ci >= L)
    ti = lax.broadcasted_iota(jnp.int32, (L, 2 * L), 0)
    si = lax.broadcasted_iota(jnp.int32, (L, 2 * L), 1) % L
    strict_c = si < ti
    incl_c = si <= ti
    rr = lax.broadcasted_iota(jnp.int32, (R, R), 0)
    rc = lax.broadcasted_iota(jnp.int32, (R, R), 1)
    tri = ((rr // L == rc // L) & (rr >= rc)).astype(BF16)
    lane_lo = lax.broadcasted_iota(jnp.int32, (L, PAIR), 1) < HEAD_DIM
    lane_lo_r = lax.broadcasted_iota(jnp.int32, (R, PAIR), 1) < HEAD_DIM
    lora_lane = lax.broadcasted_iota(jnp.int32, (R, 2 * LANES), 1)

    def head_sums(x):
        s0 = jnp.sum(jnp.where(lane_lo_r, x, 0.0), axis=-1, keepdims=True)
        s1 = jnp.sum(jnp.where(lane_lo_r, 0.0, x), axis=-1, keepdims=True)
        return jnp.where(lane_lo_r, s0, s1)

    def lora_act(pm):
        lora = pm[:, 3 * width:]
        act = jnp.where(lora_lane < n_lora[0], jnp.tanh(lora),
                        jnp.where(lora_lane < n_lora[1], lora,
                                  jnp.where(lora_lane < n_lora[2], _sigmoid(lora), 0.0)))
        return act.astype(BF16)

    def lora_up(pms):
        up = _dot(jnp.concatenate(_each(lora_act, pms), axis=0), wc_ref[...])
        return [up[b * R:(b + 1) * R] for b in range(len(pms))]

    def log_decay(up):
        t = w0_ref[...] + up[:, :width]
        return (-math.exp(-0.5) * math.log2(math.e)) / (1.0 + jnp.exp(-t))

    def running_sum(t):
        h1 = t.astype(BF16)
        r1 = t - h1.astype(F32)
        h2 = r1.astype(BF16)
        h3 = (r1 - h2.astype(F32)).astype(BF16)
        return _dot(tri, h1) + _dot(tri, h2) + _dot(tri, h3)

    tiles = [(b, pr) for b in range(nb) for pr in range(n_pairs)]
    n_t = len(tiles)
    lanes = lambda pr, part=0: slice(part * width + pr * PAIR, part * width + (pr + 1) * PAIR)
    sls = [lanes(pr) for _, pr in tiles]

    states = {0: [s_scr[i] for i in range(n_t)]}
    outs = {}

    def run_group(gi):
        g0 = gi * R
        pms = [p_ref[b, g0:g0 + R, :] for b in range(nb)]
        yield
        ups = lora_up(pms)
        logw_b = _each(log_decay, ups)
        cum_b = _each(running_sum, logw_b)
        yield

        r = [pms[b][:, lanes(pr, 0)] for b, pr in tiles]
        k_raw = [pms[b][:, lanes(pr, 1)] for b, pr in tiles]
        v = [pms[b][:, lanes(pr, 2)] for b, pr in tiles]
        logw = [logw_b[b][:, lanes(pr)] for b, pr in tiles]
        cum = [cum_b[b][:, lanes(pr)] for b, pr in tiles]
        a = [_sigmoid(a0_ref[:, lanes(pr)] + ups[b][:, lanes(pr, 1)]) for b, pr in tiles]
        g = [ups[b][:, lanes(pr, 2)] for b, pr in tiles]

        kk = _each(lambda t, sl: t * kk_ref[:, sl], k_raw, sls)
        kk_ss = _each(lambda t: head_sums(t * t), kk)
        kkn = _each(lambda t, ss: t * lax.rsqrt(jnp.maximum(ss, 1e-24)), kk, kk_ss)
        k = _each(lambda t, aa, sl: t * (1.0 + (aa - 1.0) * ka_ref[:, sl]), k_raw, a, sls)
        bb = _each(jnp.multiply, kkn, a)
        bonus_dot = _each(lambda rr_, kk_, sl: head_sums(rr_ * kk_ * rk_ref[:, sl]), r, k, sls)
        yield

        def chunks(ts):
            return [t[ch * L:(ch + 1) * L] for ch in range(n_chunks) for t in ts]

        w2, y_loc, m_t, s_loc, g_last = yield from _rwkv_chunk_terms(
            chunks(r), chunks(k), chunks(v), chunks(logw), chunks(cum), chunks(kkn), chunks(bb),
            same_blk, strict_c, incl_c, lane_lo)
        yield

        s = states[gi]
        y_parts = []
        for ch in range(n_chunks):
            sel = slice(ch * n_t, (ch + 1) * n_t)
            s_b = _each(lambda t: t.astype(BF16), s)
            y_parts.append(_each(lambda a_, sb, yl: _dot_nt(a_, sb) + yl, w2[sel], s_b, y_loc[sel]))
            s = _each(lambda s0, gl, sb, m, sl: s0 * gl + _dot(sb, m) + sl, s, g_last[sel], s_b, m_t[sel], s_loc[sel])
        states[gi + 1] = s
        y = [jnp.concatenate([y_parts[ch][i] for ch in range(n_chunks)], axis=0) for i in range(n_t)]
        yield

        mu = _each(lambda t: head_sums(t) * (1.0 / HEAD_DIM), y)
        yc = _each(jnp.subtract, y, mu)
        var = _each(lambda t: head_sums(t * t) * (1.0 / HEAD_DIM), yc)
        yield
        res = []
        for i in range(n_t):
            sl = sls[i]
            yn = yc[i] * lax.rsqrt(var[i] + GN_EPS) * lnw_ref[:, sl] + lnb_ref[:, sl]
            res.append(((yn + bonus_dot[i] * v[i]) * g[i]).astype(o_ref.dtype))
        outs[gi] = res

    gens = [run_group(gi) for gi in range(n_groups)]
    live = [True] * n_groups
    tick = 0
    while any(live):
        for gi in range(n_groups):
            if live[gi] and tick >= gi * stage_offset:
                try:
                    next(gens[gi])
                except StopIteration:
                    live[gi] = False
        tick += 1

    for gi in range(n_groups):
        for i, (b, pr) in enumerate(tiles):
            o_ref[b, gi * R:(gi + 1) * R, sls[i]] = outs[gi][i]
    for i in range(n_t):
        s_scr[i] = states[n_groups][i]


def _rwkv(p_r, wc, w0, a0, k_k, k_a, r_k, ln_w, ln_b, width, n_lora, n_chunks=2, n_groups=4, stage_offset=5):
    B, S, C = p_r.shape
    L = CHUNK * n_chunks * n_groups
    n_state = B * (width // PAIR)
    vec = lambda n: pl.BlockSpec((1, n), lambda c: (0, 0))
    return pl.pallas_call(
        functools.partial(_rwkv_kernel, nb=B, width=width, n_lora=n_lora, n_chunks=n_chunks, n_groups=n_groups,
                          stage_offset=stage_offset),
        out_shape=jax.ShapeDtypeStruct((B, S, width), BF16),
        grid=(S // L,),
        in_specs=[pl.BlockSpec((B, L, C), lambda c: (0, c, 0)),
                  pl.BlockSpec(wc.shape, lambda c: (0, 0))] + [vec(width)] * 7,
        out_specs=pl.BlockSpec((B, L, width), lambda c: (0, c, 0)),
        scratch_shapes=[pltpu.VMEM((n_state, PAIR, PAIR), F32)],
        compiler_params=pltpu.CompilerParams(dimension_semantics=("arbitrary",), vmem_limit_bytes=VMEM_LIMIT),
        name="rwkv7",
    )(p_r, wc, w0, a0, k_k, k_a, r_k, ln_w, ln_b)


SB_ROWS = 2048
SB_QUARTER = SB_ROWS // 4
N_RES = 4


def _attn_segments(pi, g):
    if pi == 0:
        return [((g // 4) * SB_QUARTER + r * ATTN_BLOCK + (ATTN_BLOCK // N_RES) * (g % 4), ATTN_BLOCK // N_RES, 1)
                for r in range(N_RES)]
    if pi == 1:
        return [((g // 4) * SB_QUARTER + (g % 4) * ATTN_BLOCK, ATTN_BLOCK, 1)]
    return [(qq * SB_QUARTER + (g % 4) * ATTN_BLOCK + g // 4, ATTN_BLOCK // 4, 4) for qq in range(4)]


def _attn_prev_tile(pi, g):
    if pi == 0:
        return (g + 15) % 16, g == 0
    if pi == 1:
        return ((g // 4 + 3) % 4) * 4 + g % 4, g < 4
    return g, True


def _attn_kernel(q_ref, k_ref, v_ref, gain_ref, o_ref,
                 nat_q, nat_k, nat_v, q_scr, k_ring, v_ring, bias_scr, out_nat, *stat_scr, tiles_per_group):
    sb = pl.program_id(2)
    n_pat = len(DILATED_PATTERNS)
    assert DILATED_PATTERNS == ((128, 1), (512, 4), (2048, 16)) and ATTN_BLOCK == 128
    num_scr, m_scr, l_scr = stat_scr[:n_pat], stat_scr[n_pat:2 * n_pat], stat_scr[2 * n_pat:]
    Q = ATTN_BLOCK
    cur_base = (sb % 2) * SB_ROWS
    other_base = SB_ROWS - cur_base

    @pl.when(sb == 0)
    def _():
        k_ring[pl.ds(SB_ROWS, SB_ROWS), :] = jnp.zeros((SB_ROWS, PAIR), F32)
        v_ring[pl.ds(SB_ROWS, SB_ROWS), :] = jnp.zeros((SB_ROWS, PAIR), F32)

    nat_q[...] = q_ref[...].astype(F32)
    nat_k[...] = k_ref[...].astype(F32)
    nat_v[...] = v_ref[...].astype(F32)
    for qq in range(4):
        for r in range(N_RES):
            src = pl.ds(qq * SB_QUARTER + r, Q, stride=N_RES)
            dst = qq * SB_QUARTER + r * Q
            q_scr[dst:dst + Q, :] = nat_q[src, :]
            k_ring[pl.ds(cur_base + dst, Q), :] = nat_k[src, :]
            v_ring[pl.ds(cur_base + dst, Q), :] = nat_v[src, :]

    @pl.when(sb == 0)
    def _():
        ii = lax.broadcasted_iota(jnp.int32, (2 * Q, 2 * Q), 0) % Q
        cj = lax.broadcasted_iota(jnp.int32, (2 * Q, 2 * Q), 1)
        per = Q // N_RES
        for kind in range(2):
            if kind == 0:
                qi = N_RES * (ii % per) + ii // per
                kj = N_RES * ((cj % Q) % per) + (cj % Q) // per + (cj // Q) * Q
            else:
                qi, kj = ii, cj
            band = (kj >= qi) & (kj <= qi + Q)
            bias_scr[kind, 1] = jnp.where(band, 0.0, NEG_BIG)
            bias_scr[kind, 0] = jnp.where(band & (cj >= Q), 0.0, NEG_BIG)

    def load_tile(ref, base, segs):
        parts = [ref[pl.ds(base + s, n, stride=st) if st > 1 else pl.ds(base + s, n), :] for s, n, st in segs]
        return parts[0] if len(parts) == 1 else jnp.concatenate(parts, axis=0)

    def store_tile(ref, segs, val):
        off = 0
        for s, n, st in segs:
            ref[pl.ds(s, n, stride=st) if st > 1 else pl.ds(s, n), :] = val[off:off + n]
            off += n

    lane_lo = lax.broadcasted_iota(jnp.int32, (Q, PAIR), 1) < HEAD_DIM
    ones_blk = jnp.ones((2 * Q, PAIR), BF16)

    n_blk = SB_ROWS // Q
    has_prev_sb = jnp.where(sb > 0, 1, 0)

    def scores(pi, gs):
        segs = [_attn_segments(pi, g) for g in gs]
        prev = [_attn_prev_tile(pi, g) for g in gs]
        prev_segs = [_attn_segments(pi, pg) for pg, _ in prev]
        prev_base = [other_base if other else cur_base for _, other in prev]
        q2 = [load_tile(q_scr, 0, sg) for sg in segs]
        q2 = [jnp.concatenate([jnp.where(lane_lo, t, 0.0), jnp.where(lane_lo, 0.0, t)], axis=0).astype(BF16)
              for t in q2]
        kcat = [jnp.concatenate([load_tile(k_ring, pb, psg), load_tile(k_ring, cur_base, sg)], axis=0).astype(BF16)
                for pb, psg, sg in zip(prev_base, prev_segs, segs)]
        vcat = [jnp.concatenate([load_tile(v_ring, pb, psg), load_tile(v_ring, cur_base, sg)], axis=0).astype(BF16)
                for pb, psg, sg in zip(prev_base, prev_segs, segs)]
        vext = [jnp.concatenate([t, ones_blk], axis=1) for t in vcat]
        has_prev = [has_prev_sb if other else 1 for _, other in prev]
        return dict(pi=pi, segs=segs, s=_each(_dot_nt, q2, kcat), vext=vext, has_prev=has_prev)

    def softmax(c):
        kind = 0 if c["pi"] == 0 else 1
        s = _each(lambda t, hp: t + bias_scr[kind, hp], c["s"], c["has_prev"])
        m = [jnp.max(t, axis=-1, keepdims=True) for t in s]
        p = _each(lambda t, mm: jnp.exp(t - mm).astype(BF16), s, m)
        return dict(pi=c["pi"], segs=c["segs"], vext=c["vext"], m=m, p=p)

    def weighted_values(c):
        pi, segs, m = c["pi"], c["segs"], c["m"]
        nl = _each(_dot, c["p"], c["vext"])
        for t in range(len(segs)):
            store_tile(num_scr[pi], segs[t], jnp.where(lane_lo, nl[t][:Q, :PAIR], nl[t][Q:, :PAIR]))
            store_tile(l_scr[pi], segs[t], jnp.where(lane_lo, nl[t][:Q, PAIR:], nl[t][Q:, PAIR:]))
            store_tile(m_scr[pi], segs[t], jnp.where(lane_lo, m[t][:Q], m[t][Q:]))

    groups = [(pi, list(range(g0, g0 + tiles_per_group)))
              for pi in range(n_pat) for g0 in range(0, n_blk, tiles_per_group)]
    n_grp = len(groups)
    sc = {0: scores(*groups[0])}
    if n_grp > 1:
        sc[1] = scores(*groups[1])
    sm = {0: softmax(sc.pop(0))}
    for k in range(n_grp):
        if k + 2 < n_grp:
            sc[k + 2] = scores(*groups[k + 2])
        if k + 1 < n_grp:
            sm[k + 1] = softmax(sc.pop(k + 1))
        weighted_values(sm.pop(k))

    ri = lax.broadcasted_iota(jnp.int32, (PAIR, PAIR), 0)
    ci = lax.broadcasted_iota(jnp.int32, (PAIR, PAIR), 1)
    seg_ones = ((ri >= HEAD_DIM) == (ci >= HEAD_DIM)).astype(BF16)
    gain = gain_ref[...]

    def merge(i, carry):
        rows = pl.ds(pl.multiple_of(i * Q, Q), Q)
        ms = [m_scr[pi][rows, :] for pi in range(n_pat)]
        m_all = functools.reduce(jnp.maximum, ms)
        num = 0.0
        den = 0.0
        for pi in range(n_pat):
            wgt = jnp.exp(ms[pi] - m_all)
            num = num + wgt * num_scr[pi][rows, :]
            den = den + wgt * l_scr[pi][rows, :]
        o = num / den
        ms_o = _split_dot(o * o, seg_ones) * (1.0 / HEAD_DIM)
        out_nat[pl.ds((i // N_RES) * SB_QUARTER + i % N_RES, Q, stride=N_RES), :] = o * lax.rsqrt(ms_o + NORM_EPS) * gain
        return carry

    lax.fori_loop(0, SB_ROWS // Q, merge, 0, unroll=4)
    o_ref[...] = out_nat[...].astype(o_ref.dtype)


def _attention(q, k, v, gain, tiles_per_group=4):
    B, S, W = q.shape
    n_pairs = W // PAIR
    n_pat = len(DILATED_PATTERNS)
    blk = pl.BlockSpec((None, SB_ROWS, PAIR), lambda b, p, s: (b, s, p))
    tile = pltpu.VMEM((SB_ROWS, PAIR), F32)
    return pl.pallas_call(
        functools.partial(_attn_kernel, tiles_per_group=tiles_per_group),
        out_shape=jax.ShapeDtypeStruct((B, S, W), BF16),
        grid=(B, n_pairs, S // SB_ROWS),
        in_specs=[blk, blk, blk, pl.BlockSpec((1, PAIR), lambda b, p, s: (0, p))],
        out_specs=blk,
        scratch_shapes=[tile, tile, tile, tile,
                        pltpu.VMEM((2 * SB_ROWS, PAIR), F32),
                        pltpu.VMEM((2 * SB_ROWS, PAIR), F32),
                        pltpu.VMEM((2, 2, 2 * ATTN_BLOCK, 2 * ATTN_BLOCK), F32),
                        tile]
                       + [tile] * (3 * n_pat),
        compiler_params=pltpu.CompilerParams(dimension_semantics=("parallel", "parallel", "arbitrary"),
                                             vmem_limit_bytes=VMEM_LIMIT),
        name="dilated_attn",
    )(q, k, v, gain)


def _rotary_tables(seq):
    half = ROT_DIM // 2
    inv_freq = ROPE_THETA ** (-jnp.arange(half, dtype=F32) * 2.0 / ROT_DIM)
    ang = jnp.arange(seq).astype(F32)[:, None] * inv_freq[None, :]
    cos, sin = jnp.cos(ang), jnp.sin(ang)
    rest = HEAD_DIM - ROT_DIM
    cos_h = jnp.concatenate([cos, cos, jnp.ones((seq, rest), F32)], axis=-1)
    sin_h = jnp.concatenate([-sin, sin, jnp.zeros((seq, rest), F32)], axis=-1)
    return jnp.tile(cos_h, (1, PAIR // HEAD_DIM)), jnp.tile(sin_h, (1, PAIR // HEAD_DIM))


FFN_HALO = 16
MIX_PIECES = 4


def _mix_ffn_kernel(x_ref, ya_ref, yb_ref, xh_ref, yah_ref, ybh_ref, wo_ref, g_ref, wu_ref, cw_ref,
                    cb_ref, wd_ref, fg_ref, o_ref, h_scr, hh_scr, *, tm, seq, row_chunks, apply_final):
    tf = wd_ref.shape[0]
    i = pl.program_id(0)
    j = pl.program_id(1)
    wa = ya_ref.shape[1]

    def mixed(x, ya, yb):
        return x + _dot(ya, wo_ref[:wa, :]) + _dot(yb, wo_ref[wa:, :])

    @pl.when(j == 0)
    def _():
        hh_scr[...] = _rmsnorm(mixed(xh_ref[...], yah_ref[...], ybh_ref[...]), g_ref[...]).astype(BF16)
        piece = tm // MIX_PIECES
        for c in range(MIX_PIECES):
            rows = slice(c * piece, (c + 1) * piece)
            x1 = mixed(x_ref[rows, :], ya_ref[rows, :], yb_ref[rows, :])
            o_ref[rows, :] = x1
            h_scr[rows, :] = _rmsnorm(x1, g_ref[...]).astype(BF16)

    seq_start = (i * tm) % seq == 0
    rc = tm // row_chunks
    row = lax.broadcasted_iota(jnp.int32, (rc, 1), 0)
    for c in range(row_chunks):
        rows = slice(c * rc, (c + 1) * rc)
        gv = _dot(h_scr[rows, :], wu_ref[...])
        gate, val = gv[:, :tf], gv[:, tf:]
        if c == 0:
            gate_h = jnp.where(seq_start, 0.0, _dot(hh_scr[...], wu_ref[:, :tf]))
        else:
            gate_h = _dot(h_scr[c * rc - FFN_HALO:c * rc, :], wu_ref[:, :tf])
        g1 = jnp.where(row == 0, gate_h[FFN_HALO - 1:FFN_HALO, :], pltpu.roll(gate, 1, 0))
        g2 = jnp.where(row == 0, gate_h[FFN_HALO - 2:FFN_HALO - 1, :],
                       jnp.where(row == 1, gate_h[FFN_HALO - 1:FFN_HALO, :], pltpu.roll(gate, 2, 0)))
        u = cw_ref[0:1, :] * g2 + cw_ref[1:2, :] * g1 + cw_ref[2:3, :] * gate + cb_ref[...]
        act = (u * _sigmoid(u) * val).astype(BF16)
        o_ref[rows, :] += _dot(act, wd_ref[...])

    if apply_final:
        @pl.when(j == pl.num_programs(1) - 1)
        def _():
            o_ref[...] = _rmsnorm(o_ref[...], fg_ref[...])


def _mix_ffn(x2d, y_a, y_b, w_o, gain, w_up, conv_w, conv_b, w_down, final_gain, seq, apply_final,
             tm=1024, n_ff_tiles=2, row_chunks=1):
    T, D = x2d.shape
    F = w_down.shape[0]
    tf = F // n_ff_tiles
    assert tf % LANES == 0 and seq % tm == 0 and w_up.shape[1] == 2 * F
    w_up = jnp.concatenate([w_up[:, half * F + j * tf: half * F + (j + 1) * tf]
                            for j in range(n_ff_tiles) for half in range(2)], axis=1)
    halo_blocks = tm // FFN_HALO
    row = lambda i, j: (i, 0)
    halo = lambda i, j: (jnp.maximum(i * halo_blocks - 1, 0), 0)
    fixed = lambda i, j: (0, 0)
    wa, wb = y_a.shape[1], y_b.shape[1]
    return pl.pallas_call(
        functools.partial(_mix_ffn_kernel, tm=tm, seq=seq, row_chunks=row_chunks, apply_final=apply_final),
        out_shape=jax.ShapeDtypeStruct((T, D), F32),
        grid=(T // tm, n_ff_tiles),
        in_specs=[pl.BlockSpec((tm, D), row), pl.BlockSpec((tm, wa), row), pl.BlockSpec((tm, wb), row),
                  pl.BlockSpec((FFN_HALO, D), halo), pl.BlockSpec((FFN_HALO, wa), halo),
                  pl.BlockSpec((FFN_HALO, wb), halo),
                  pl.BlockSpec(w_o.shape, fixed),
                  pl.BlockSpec((1, D), fixed),
                  pl.BlockSpec((D, 2 * tf), lambda i, j: (0, j)),
                  pl.BlockSpec((CONV_WIDTH, tf), lambda i, j: (0, j)),
                  pl.BlockSpec((1, tf), lambda i, j: (0, j)),
                  pl.BlockSpec((tf, D), lambda i, j: (j, 0)),
                  pl.BlockSpec((1, D), fixed)],
        out_specs=pl.BlockSpec((tm, D), row),
        scratch_shapes=[pltpu.VMEM((tm, D), BF16), pltpu.VMEM((FFN_HALO, D), BF16)],
        compiler_params=pltpu.CompilerParams(dimension_semantics=("parallel", "arbitrary"),
                                             vmem_limit_bytes=VMEM_LIMIT),
        name="mix_convglu_ffn",
    )(x2d, y_a, y_b, x2d, y_a, y_b, w_o, gain, w_up, conv_w, conv_b, w_down, final_gain)


def kernel(x, mix_norm_gain, w_in, rwkv_shift_mix, w0, w_lora_up, a0, a_lora_up, g_lora_up, k_k, k_a, r_k,
           ln_x_w, ln_x_b, attn_norm_gain, w_out, ffn_norm_gain, w_ffn_up, ffn_conv_w, ffn_conv_b,
           w_ffn_down, final_norm_gain):
    B, S, D = x.shape
    depth = w_in.shape[0]
    rw = w0.shape[1]
    aw = attn_norm_gain.shape[1]
    n_w, n_a, n_g = w_lora_up.shape[1], a_lora_up.shape[1], g_lora_up.shape[1]
    n_lora = n_w + n_a + n_g
    lora_pad = -(-n_lora // (2 * LANES)) * (2 * LANES)
    assert lora_pad == 2 * LANES and rw % PAIR == 0 and aw % PAIR == 0
    rwkv_cols = 3 * rw + n_lora
    cos_t, sin_t = _rotary_tables(S)

    x2d = x.reshape(B * S, D)
    for l in range(depth):
        w_r = w_in[l][:, :3 * rw + lora_pad].astype(BF16)
        w_a = w_in[l][:, rwkv_cols:].astype(BF16)
        mix =jnp.concatenate([rwkv_shift_mix[l], jnp.zeros((lora_pad - n_lora,), F32)])[None, :]
        wc = jnp.zeros((lora_pad, 3 * rw), F32)
        wc = wc.at[:n_w, :rw].set(w_lora_up[l])
        wc = wc.at[n_w:n_w + n_a, rw:2 * rw].set(a_lora_up[l])
        wc = wc.at[n_w + n_a:n_lora, 2 * rw:].set(g_lora_up[l]).astype(BF16)

        p_r, q, k, v = _inproj(x2d, mix_norm_gain[l][None, :], w_r, w_a, mix, cos_t, sin_t)
        y_rwkv = _rwkv(p_r.reshape(B, S, -1), wc, w0[l][None, :], a0[l][None, :], k_k[l][None, :],
                       k_a[l][None, :], r_k[l].reshape(1, rw), ln_x_w[l][None, :], ln_x_b[l][None, :],
                       rw, (n_w, n_w + n_a, n_lora))
        y_attn = _attention(q.reshape(B, S, aw), k.reshape(B, S, aw), v.reshape(B, S, aw),
                            attn_norm_gain[l][None, :])
        x2d = _mix_ffn(x2d, y_rwkv.reshape(B * S, rw), y_attn.reshape(B * S, aw), w_out[l].astype(BF16),
                       ffn_norm_gain[l][None, :], w_ffn_up[l].astype(BF16), ffn_conv_w[l],
                       ffn_conv_b[l][None, :], w_ffn_down[l].astype(BF16), final_norm_gain[None, :], S,
                       apply_final=(l == depth - 1))
    return x2d.reshape(B, S, D)
```
